```python
import math, functools
import jax, jax.numpy as jnp
from jax import lax
import numpy as np

D_MODEL = 1024
BATCH = 4
SEQ = 4096
DEPTH = 4
DEC_BATCH = 128
DEC_SEQ = 4
PAST_LEN = 2048
PAGE_SIZE = 128

N_EVEN = (DEPTH + 1) // 2
N_ODD = DEPTH // 2
RET_H = 4
RET_DK = 128
RET_DV = 128
RET_CHUNK = 128
ROPE_BASE = 10000.0
ATT_H = 4
ATT_DH = 128
ATT_BLOCK = 128
IDX_H = 8
IDX_DIM = 64
TOPK_MAX = 256
REL_BUCKETS = 32
REL_MAX_DIST = 128
POOL_WINDOWS = (2, 4, 8, 16)
POOL_GROUPS = 4
POOL_DIM = D_MODEL // 2
POOL_GC = POOL_DIM // POOL_GROUPS
POOL_PAST = 15
D_INNER = D_MODEL // 2
SSM_P = 64
SSM_H = D_INNER // SSM_P
SSM_G = 2
SSM_N = 128
CONV_W = 4
CONV_DIM = D_INNER + 2 * SSM_G * SSM_N
SSD_CHUNK = 128
FF_DIM = -(-8 * D_MODEL // (3 * 256)) * 256
EPS = 1e-6

EVEN_SPLITS = (RET_H * RET_DK, RET_H * RET_DK, RET_H * RET_DV, RET_H * RET_DV,
               ATT_H * ATT_DH, ATT_H * ATT_DH, ATT_H * ATT_DH, IDX_H * IDX_DIM, IDX_DIM, IDX_H)
EVEN_PROJ = sum(EVEN_SPLITS)
ODD_SPLITS = (POOL_DIM, D_INNER, CONV_DIM, SSM_H)
ODD_PROJ = sum(ODD_SPLITS)
MIX_EVEN = RET_H * RET_DV + ATT_H * ATT_DH
MIX_ODD = POOL_DIM + D_INNER

kernel_name = 'hybrid_retention_dsa_pool_ssd_step'

f32 = jnp.float32


def _split(a, sizes):
    return jnp.split(a, np.cumsum(sizes)[:-1].tolist(), axis=-1)


def rmsnorm(x, g):
    xf = x.astype(f32)
    return xf * lax.rsqrt(jnp.mean(xf * xf, -1, keepdims=True) + EPS) * g


def swiglu(h, wg, wu, wd):
    return (jax.nn.silu(h @ wg) * (h @ wu)) @ wd


def rotary(x, pos):
    half = x.shape[-1] // 2
    inv = ROPE_BASE ** (-jnp.linspace(0.0, 1.0, half, dtype=f32))
    ang = pos.astype(f32)[:, None] * inv[None, :]
    cos = jnp.cos(ang)[None, :, None, :]
    sin = jnp.sin(ang)[None, :, None, :]
    x1, x2 = x[..., :half], x[..., half:]
    return jnp.concatenate([x1 * cos - x2 * sin, x1 * sin + x2 * cos], -1)


def _to_chunks(a, q):
    b, l = a.shape[:2]
    return a.reshape(b, l // q, q, *a.shape[2:]).swapaxes(0, 1)


def retention_chunked(q, k, v, R0):
    b, L = q.shape[:2]
    Q = math.gcd(L, RET_CHUNK)
    log_g = jnp.log1p(-jnp.exp2(-5.0 - jnp.arange(RET_H, dtype=f32)))
    idx = jnp.arange(Q, dtype=f32)
    diff = idx[:, None] - idx[None, :]
    dmask = jnp.where(diff[None] >= 0, jnp.exp(log_g[:, None, None] * jnp.maximum(diff, 0.0)[None]), 0.0)
    xi = jnp.exp(log_g[:, None] * (idx + 1.0)[None])
    zeta = jnp.exp(log_g[:, None] * (Q - 1.0 - idx)[None])
    g_chunk = jnp.exp(log_g * Q)

    def step(R, inp):
        qc, kc, vc = inp
        s = jnp.einsum('bihd,bjhd->bhij', qc, kc) * dmask[None]
        inner = jnp.einsum('bhij,bjhe->bihe', s, vc)
        cross = jnp.einsum('bihd,bhde->bihe', qc, R) * xi.T[None, :, :, None]
        R_new = R * g_chunk[None, :, None, None] + jnp.einsum('bjhd,bjhe,hj->bhde', kc, vc, zeta)
        return R_new, inner + cross

    R, o = lax.scan(step, R0, (_to_chunks(q, Q), _to_chunks(k, Q), _to_chunks(v, Q)))
    return o.swapaxes(0, 1).reshape(b, L, RET_H, RET_DV), R


def retention_branch(q, k, v, g, pos, R0):
    b, T = q.shape[:2]
    q = rotary(q.reshape(b, T, RET_H, RET_DK).astype(f32), pos)
    k = rotary(k.reshape(b, T, RET_H, RET_DK).astype(f32), pos) * RET_DK ** -0.5
    v = v.reshape(b, T, RET_H, RET_DV).astype(f32)
    o, R = retention_chunked(q, k, v, R0.astype(f32))
    o = o * lax.rsqrt(jnp.mean(o * o, -1, keepdims=True) + EPS)
    return jax.nn.silu(g.astype(f32)) * o.reshape(b, T, RET_H * RET_DV), R


def t5_bucket(rel):
    n = jnp.maximum(rel, 0)
    max_exact = REL_BUCKETS // 2
    nf = jnp.maximum(n, 1).astype(f32)
    large = max_exact + (jnp.log(nf / max_exact) / math.log(REL_MAX_DIST / max_exact)
                         * (REL_BUCKETS - max_exact)).astype(jnp.int32)
    large = jnp.minimum(large, REL_BUCKETS - 1)
    return jnp.where(n < max_exact, n, large)


def indexer_select(qi, wi, ki, qpos, topk):
    L = ki.shape[1]
    s = jax.nn.relu(jnp.einsum('bthd,bsd->bths', qi, ki))
    score = jnp.einsum('bths,bth->bts', s, wi)
    score = jnp.where(jnp.arange(L)[None, None, :] <= qpos[None, :, None], score, -jnp.inf)
    _, idx = lax.top_k(score, topk)
    valid = idx <= qpos[None, :, None]
    return idx, valid


def sparse_softmax(q, k_sel, v_sel, rel, valid, rel_bias):
    logits = jnp.einsum('bthd,btkhd->bhtk', q.astype(f32), k_sel.astype(f32)) * ATT_DH ** -0.5
    bias = rel_bias.astype(f32)[t5_bucket(rel)]
    logits = logits + jnp.moveaxis(bias, -1, 1)
    logits = jnp.where(valid[:, None], logits, -jnp.inf)
    p = jax.nn.softmax(logits, axis=-1)
    return jnp.einsum('bhtk,btkhd->bthd', p, v_sel.astype(f32))


def attend_prompt(q, k, v, qi, ki, wi, rel_bias):
    b, L = q.shape[:2]
    topk = min(TOPK_MAX, L // 4)
    nb = L // ATT_BLOCK
    pos = jnp.arange(L).reshape(nb, ATT_BLOCK)
    take = jax.vmap(lambda a, i: a[i])

    def one(inp):
        qb, qib, wib, pb = inp
        idx, valid = indexer_select(qib, wib, ki, pb, topk)
        return sparse_softmax(qb, take(k, idx), take(v, idx), pb[None, :, None] - idx, valid, rel_bias)

    o = lax.map(one, (_to_chunks(q, ATT_BLOCK), _to_chunks(qi, ATT_BLOCK), _to_chunks(wi, ATT_BLOCK), pos))
    return o.swapaxes(0, 1).reshape(b, L, ATT_H * ATT_DH)


def gather_paged(pool, page_table, new, idx):
    db, T, K = idx.shape
    n_past = page_table.shape[1] * PAGE_SIZE
    flat = pool.reshape(-1, *pool.shape[2:])
    pi = jnp.minimum(idx, n_past - 1)
    page = jnp.take_along_axis(page_table, (pi // PAGE_SIZE).reshape(db, T * K), axis=1).reshape(db, T, K)
    past = flat[page * PAGE_SIZE + pi % PAGE_SIZE]
    cur = jax.vmap(lambda a, i: a[i])(new, jnp.clip(idx - n_past, 0, new.shape[1] - 1))
    is_past = (idx < n_past).reshape(idx.shape + (1,) * (past.ndim - idx.ndim))
    return jnp.where(is_past, past, cur)


def attend_paged(q, k, v, qi, ki, wi, pool_k, pool_v, pool_ki, page_table, rel_bias):
    db, T = q.shape[:2]
    n_past = page_table.shape[1] * PAGE_SIZE
    L = n_past + T
    topk = min(TOPK_MAX, L // 4)
    qpos = n_past + jnp.arange(T)
    ki_past = pool_ki[page_table].reshape(db, n_past, IDX_DIM).astype(f32)
    ki_all = jnp.concatenate([ki_past, ki], axis=1)
    idx, valid = indexer_select(qi, wi, ki_all, qpos, topk)
    k_sel = gather_paged(pool_k, page_table, k, idx)
    v_sel = gather_paged(pool_v, page_table, v, idx)
    o = sparse_softmax(q, k_sel, v_sel, qpos[None, :, None] - idx, valid, rel_bias)
    return o.reshape(db, T, ATT_H * ATT_DH)


def even_mixer(h, pos, R0, w_in, w_out, q_gain, k_gain, attend):
    b, T = h.shape[:2]
    qr, kr, vr, gr, qa, ka, va, qi, ki, wi = _split(h @ w_in, EVEN_SPLITS)
    ret, R = retention_branch(qr, kr, vr, gr, pos, R0)
    qa = rmsnorm(qa.reshape(b, T, ATT_H, ATT_DH), q_gain)
    ka = rmsnorm(ka.reshape(b, T, ATT_H, ATT_DH), k_gain)
    va = va.reshape(b, T, ATT_H, ATT_DH).astype(f32)
    qi = qi.reshape(b, T, IDX_H, IDX_DIM).astype(f32) * IDX_DIM ** -0.5
    ki = ki.astype(f32)
    wi = wi.astype(f32) * IDX_H ** -0.5
    att = attend(qa, ka, va, qi, ki, wi)
    out = jnp.concatenate([ret, att], -1) @ w_out
    return out, R, ka, va, ki


def multiscale_pool(u, prev, pos):
    T = u.shape[1]
    full = jnp.concatenate([prev.astype(f32), u.astype(f32)], 1)
    cs = jnp.concatenate([jnp.zeros_like(full[:, :1]), jnp.cumsum(full, 1)], 1)
    end = cs[:, POOL_PAST + 1:]
    outs = []
    for g, w in enumerate(POOL_WINDOWS):
        ch = slice(g * POOL_GC, (g + 1) * POOL_GC)
        start = cs[:, POOL_PAST + 1 - w: POOL_PAST + 1 - w + T, ch]
        cnt = jnp.minimum(pos + 1, w).astype(f32)[None, :, None]
        outs.append((end[..., ch] - start) / cnt)
    return jnp.concatenate(outs, -1) - u.astype(f32)


def ssd_chunked(x, dt, A, bm, cm, h0):
    L = x.shape[1]
    Q = math.gcd(L, SSD_CHUNK)
    bh = jnp.repeat(bm, SSM_H // SSM_G, axis=2)
    chh = jnp.repeat(cm, SSM_H // SSM_G, axis=2)
    a = dt * A[None, None, :]
    causal = jnp.tril(jnp.ones((Q, Q), bool))

    def step(h, inp):
        xc, dtc, ac, bc, cc = inp
        cs = jnp.cumsum(ac, axis=1)
        seg = cs[:, :, None, :] - cs[:, None, :, :]
        lm = jnp.exp(jnp.where(causal[None, :, :, None], seg, -jnp.inf))
        sc = jnp.einsum('bihn,bjhn->bijh', cc, bc) * lm * dtc[:, None, :, :]
        y = jnp.einsum('bijh,bjhp->bihp', sc, xc) + jnp.einsum('bihn,bhpn->bihp', cc, h) * jnp.exp(cs)[..., None]
        w_end = jnp.exp(cs[:, -1:, :] - cs) * dtc
        h_new = h * jnp.exp(cs[:, -1, :])[:, :, None, None] + jnp.einsum('bjhn,bjhp,bjh->bhpn', bc, xc, w_end)
        return h_new, y

    hN, y = lax.scan(step, h0, tuple(_to_chunks(t, Q) for t in (x, dt, a, bh, chh)))
    return y.swapaxes(0, 1).reshape(x.shape), hN


def odd_mixer(h, pos, pool_prev, conv_prev, ssm_prev, w_in, w_out, pool_w, pool_scale,
              conv_w, conv_b, dt_bias, a_log, d_skip, ssm_norm):
    b, T = h.shape[:2]
    u, z, xbc, dt = _split(h @ w_in, ODD_SPLITS)
    d = multiscale_pool(u, pool_prev, pos)
    pool_out = jnp.einsum('btgc,gcd->btgd', d.reshape(b, T, POOL_GROUPS, POOL_GC), pool_w).reshape(b, T, POOL_DIM) * pool_scale
    pool_new = jnp.concatenate([pool_prev.astype(f32), u.astype(f32)], 1)[:, -POOL_PAST:]
    full = jnp.concatenate([conv_prev.astype(f32), xbc.astype(f32)], 1)
    conv = conv_b + sum(full[:, j:j + T] * conv_w[j] for j in range(CONV_W))
    conv_new = full[:, -(CONV_W - 1):]
    xs, bm, cm = _split(jax.nn.silu(conv), (D_INNER, SSM_G * SSM_N, SSM_G * SSM_N))
    xs = xs.reshape(b, T, SSM_H, SSM_P)
    dt = jax.nn.softplus(dt.astype(f32) + dt_bias)
    A = -jnp.exp(a_log.astype(f32))
    y, hN = ssd_chunked(xs, dt, A, bm.reshape(b, T, SSM_G, SSM_N), cm.reshape(b, T, SSM_G, SSM_N), ssm_prev.astype(f32))
    y = (y + d_skip[:, None] * xs).reshape(b, T, D_INNER) * jax.nn.silu(z.astype(f32))
    yg = y.reshape(b, T, SSM_G, D_INNER // SSM_G)
    y = (yg * lax.rsqrt(jnp.mean(yg * yg, -1, keepdims=True) + EPS)).reshape(b, T, D_INNER) * ssm_norm
    out = jnp.concatenate([pool_out, y], -1) @ w_out
    return out, pool_new, conv_new, hN


def setup_inputs(seed: int = 0) -> dict:
    key = jax.random.key(seed)
    ks = iter(jax.random.split(key, 48))

    def nrm(shape, scale):
        return jax.random.normal(next(ks), shape, f32) * scale

    n_pages = PAST_LEN // PAGE_SIZE
    n_pool = (DEC_BATCH * n_pages * 5) // 4
    page_table = jax.random.permutation(next(ks), n_pool)[:DEC_BATCH * n_pages].reshape(DEC_BATCH, n_pages).astype(jnp.int32)
    dt0 = jnp.exp(jax.random.uniform(next(ks), (N_ODD, SSM_H), f32, math.log(1e-3), math.log(1e-1)))
    dt_bias = dt0 + jnp.log(-jnp.expm1(-dt0))
    a_log = jnp.log(jax.random.uniform(next(ks), (N_ODD, SSM_H), f32, 1.0, 16.0))
    return {
        'x_prompt': nrm((BATCH, SEQ, D_MODEL), 1.0),
        'x_sample': nrm((DEC_BATCH, DEC_SEQ, D_MODEL), 1.0),
        'cache_k': nrm((N_EVEN, n_pool, PAGE_SIZE, ATT_H, ATT_DH), 1.0),
        'cache_v': nrm((N_EVEN, n_pool, PAGE_SIZE, ATT_H, ATT_DH), 1.0),
        'cache_kidx': nrm((N_EVEN, n_pool, PAGE_SIZE, IDX_DIM), 1.0),
        'state_ret': nrm((N_EVEN, DEC_BATCH, RET_H, RET_DK, RET_DV), 0.5),
        'state_pool': nrm((N_ODD, DEC_BATCH, POOL_PAST, POOL_DIM), 1.0),
        'state_conv': nrm((N_ODD, DEC_BATCH, CONV_W - 1, CONV_DIM), 1.0),
        'state_ssm': nrm((N_ODD, DEC_BATCH, SSM_H, SSM_P, SSM_N), 0.5),
        'page_table': page_table,
        'norm_mix': 1.0 + nrm((DEPTH, D_MODEL), 0.02),
        'norm_ffn': 1.0 + nrm((DEPTH, D_MODEL), 0.02),
        'w_in_even': nrm((N_EVEN, D_MODEL, EVEN_PROJ), D_MODEL ** -0.5),
        'w_out_even': nrm((N_EVEN, MIX_EVEN, D_MODEL), MIX_EVEN ** -0.5),
        'q_norm': 1.0 + nrm((N_EVEN, ATT_DH), 0.02),
        'k_norm': 1.0 + nrm((N_EVEN, ATT_DH), 0.02),
        'rel_bias': nrm((REL_BUCKETS, ATT_H), 0.2),
        'w_in_odd': nrm((N_ODD, D_MODEL, ODD_PROJ), D_MODEL ** -0.5),
        'w_out_odd': nrm((N_ODD, MIX_ODD, D_MODEL), MIX_ODD ** -0.5),
        'pool_w': nrm((N_ODD, POOL_GROUPS, POOL_GC, POOL_GC), POOL_GC ** -0.5),
        'pool_scale': 1.0 + nrm((N_ODD, POOL_DIM), 0.02),
        'conv_w': nrm((N_ODD, CONV_W, CONV_DIM), CONV_W ** -0.5),
        'conv_b': nrm((N_ODD, CONV_DIM), 0.02),
        'dt_bias': dt_bias,
        'a_log': a_log,
        'd_skip': 1.0 + nrm((N_ODD, SSM_H), 0.02),
        'ssm_norm': 1.0 + nrm((N_ODD, D_INNER), 0.02),
        'w_gate': nrm((DEPTH, D_MODEL, FF_DIM), D_MODEL ** -0.5),
        'w_up': nrm((DEPTH, D_MODEL, FF_DIM), D_MODEL ** -0.5),
        'w_down': nrm((DEPTH, FF_DIM, D_MODEL), FF_DIM ** -0.5),
    }


def reference(x_prompt, x_sample, cache_k, cache_v, cache_kidx, state_ret, state_pool, state_conv, state_ssm,
              page_table, norm_mix, norm_ffn, w_in_even, w_out_even, q_norm, k_norm, rel_bias,
              w_in_odd, w_out_odd, pool_w, pool_scale, conv_w, conv_b, dt_bias, a_log, d_skip, ssm_norm,
              w_gate, w_up, w_down):
    bp, sp = x_prompt.shape[:2]
    ts = x_sample.shape[1]
    n_past = page_table.shape[1] * PAGE_SIZE
    pos_p = jnp.arange(sp)
    pos_s = n_past + jnp.arange(ts)
    xp, xs = x_prompt, x_sample
    kp_l, vp_l, kip_l, rp_l, ks_l, vs_l, kis_l, rs_l = [], [], [], [], [], [], [], []
    pp_l, cp_l, hp_l, ps_l, cs_l, hs_l = [], [], [], [], [], []
    for l in range(DEPTH):
        np_ = rmsnorm(xp, norm_mix[l])
        ns_ = rmsnorm(xs, norm_mix[l])
        if l % 2 == 0:
            i = l // 2
            prm = (w_in_even[i], w_out_even[i], q_norm[i], k_norm[i])
            op, rp, kp, vp, kip = even_mixer(np_, pos_p, jnp.zeros((bp, RET_H, RET_DK, RET_DV), f32), *prm,
                                             functools.partial(attend_prompt, rel_bias=rel_bias))
            os_, rs, ks_, vs_, kis = even_mixer(ns_, pos_s, state_ret[i], *prm,
                                                functools.partial(attend_paged, pool_k=cache_k[i], pool_v=cache_v[i],
                                                                  pool_ki=cache_kidx[i], page_table=page_table,
                                                                  rel_bias=rel_bias))
            kp_l.append(kp); vp_l.append(vp); kip_l.append(kip); rp_l.append(rp)
            ks_l.append(ks_); vs_l.append(vs_); kis_l.append(kis); rs_l.append(rs)
        else:
            j = l // 2
            prm = (w_in_odd[j], w_out_odd[j], pool_w[j], pool_scale[j], conv_w[j], conv_b[j],
                   dt_bias[j], a_log[j], d_skip[j], ssm_norm[j])
            op, pp, cp, hp = odd_mixer(np_, pos_p, jnp.zeros((bp, POOL_PAST, POOL_DIM), f32),
                                       jnp.zeros((bp, CONV_W - 1, CONV_DIM), f32),
                                       jnp.zeros((bp, SSM_H, SSM_P, SSM_N), f32), *prm)
            os_, ps, cs, hs = odd_mixer(ns_, pos_s, state_pool[j], state_conv[j], state_ssm[j], *prm)
            pp_l.append(pp); cp_l.append(cp); hp_l.append(hp)
            ps_l.append(ps); cs_l.append(cs); hs_l.append(hs)
        xp = xp + op
        xs = xs + os_
        xp = xp + swiglu(rmsnorm(xp, norm_ffn[l]), w_gate[l], w_up[l], w_down[l])
        xs = xs + swiglu(rmsnorm(xs, norm_ffn[l]), w_gate[l], w_up[l], w_down[l])
    return (xp, xs,
            jnp.stack(kp_l), jnp.stack(vp_l), jnp.stack(kip_l), jnp.stack(rp_l),
            jnp.stack(pp_l), jnp.stack(cp_l), jnp.stack(hp_l),
            jnp.stack(ks_l), jnp.stack(vs_l), jnp.stack(kis_l), jnp.stack(rs_l),
            jnp.stack(ps_l), jnp.stack(cs_l), jnp.stack(hs_l))
```

```python
import functools
import math

import jax
import jax.numpy as jnp
import numpy as np
from jax import lax
from jax.experimental import pallas as pl
from jax.experimental.pallas import tpu as pltpu

f32 = jnp.float32
bf16 = jnp.bfloat16
i32 = jnp.int32

D_MODEL = 1024
DEPTH = 4
PAGE = 128
RET_H, RET_DK, RET_DV, RET_CHUNK = 4, 128, 128, 128
ROPE_BASE = 10000.0
ATT_H, ATT_DH, ATT_BLOCK = 4, 128, 128
IDX_H, IDX_DIM = 8, 64
TOPK_MAX = 256
REL_BUCKETS, REL_MAX_DIST = 32, 128
POOL_WINDOWS = (2, 4, 8, 16)
POOL_GROUPS = 4
POOL_DIM = D_MODEL // 2
POOL_GC = POOL_DIM // POOL_GROUPS
POOL_PAST = 15
D_INNER = D_MODEL // 2
SSM_P = 64
SSM_H = D_INNER // SSM_P
SSM_G = 2
SSM_N = 128
CONV_W = 4
CONV_DIM = D_INNER + 2 * SSM_G * SSM_N
SSD_CHUNK = 128
FF_DIM = -(-8 * D_MODEL // (3 * 256)) * 256
EPS = 1e-6

EVEN_PROJ = 4 * 512 + 3 * 512 + 512 + IDX_DIM + IDX_H
EVEN_PROJ_PAD = 4224
ODD_PROJ = POOL_DIM + D_INNER + CONV_DIM + SSM_H
ODD_PROJ_PAD = 2176

LANE = 128
INT_MIN = -(2 ** 31)
NEG_BIG = -1e30
VMEM_LIMIT = 56 * 1024 * 1024
TM = 256
FF_SPLIT = 2


def _cparams(sem):
    return pltpu.CompilerParams(dimension_semantics=sem, vmem_limit_bytes=VMEM_LIMIT)


def _mm(a, b):
    return jnp.dot(a, b, preferred_element_type=f32)


def _mm_nt(a, b):
    return lax.dot_general(a, b, (((1,), (1,)), ((), ())), preferred_element_type=f32)


def _mm_tn(a, b):
    return lax.dot_general(a, b, (((0,), (0,)), ((), ())), preferred_element_type=f32)


def _rms(x, g):
    return x * lax.rsqrt(jnp.mean(x * x, -1, keepdims=True) + EPS) * g


def _silu(x):
    return x / (1.0 + jnp.exp(-x))


def _const_spec(shape):
    nd = len(shape)
    return pl.BlockSpec(shape, lambda *a: (0,) * nd)


def _even_proj_kernel(x_ref, g_ref, w_ref, qg_ref, kg_ref,
                      ret_ref, qa_ref, ka_ref, kab_ref, va_ref, vab_ref, qi_ref, kiwi_ref):
    xb = _rms(x_ref[...], g_ref[...]).astype(bf16)
    ret_ref[...] = _mm(xb, w_ref[:, 0:2048])
    qa = _mm(xb, w_ref[:, 2048:2560])
    ka = _mm(xb, w_ref[:, 2560:3072])
    for h in range(ATT_H):
        sl = slice(h * ATT_DH, (h + 1) * ATT_DH)
        qa_ref[:, sl] = _rms(qa[:, sl], qg_ref[...]).astype(bf16)
        kn = _rms(ka[:, sl], kg_ref[...])
        ka_ref[:, sl] = kn
        kab_ref[:, sl] = kn.astype(bf16)
    va = _mm(xb, w_ref[:, 3072:3584])
    va_ref[...] = va
    vab_ref[...] = va.astype(bf16)
    qi_ref[...] = (_mm(xb, w_ref[:, 3584:4096]) * IDX_DIM ** -0.5).astype(bf16)
    kiwi_ref[...] = _mm(xb, w_ref[:, 4096:4224])


def _even_proj(x, g, w, qg, kg):
    n = x.shape[0]
    row = lambda c: pl.BlockSpec((TM, c), lambda i: (i, 0))
    outs = [(2048, f32), (512, bf16), (512, f32), (512, bf16), (512, f32), (512, bf16), (512, bf16), (LANE, f32)]
    return pl.pallas_call(
        _even_proj_kernel,
        grid=(n // TM,),
        in_specs=[row(D_MODEL), _const_spec((1, D_MODEL)), _const_spec((D_MODEL, EVEN_PROJ_PAD)),
                  _const_spec((1, ATT_DH)), _const_spec((1, ATT_DH))],
        out_specs=[row(c) for c, _ in outs],
        out_shape=[jax.ShapeDtypeStruct((n, c), dt) for c, dt in outs],
        compiler_params=_cparams(("parallel",)),
        name="even_proj",
    )(x, g, w, qg, kg)


def _odd_proj_kernel(x_ref, g_ref, w_ref, u_ref, z_ref, xbc_ref, dt_ref):
    xb = _rms(x_ref[...], g_ref[...]).astype(bf16)
    u_ref[...] = _mm(xb, w_ref[:, 0:512])
    z_ref[...] = _mm(xb, w_ref[:, 512:1024])
    xbc_ref[...] = _mm(xb, w_ref[:, 1024:2048])
    dt_ref[...] = _mm(xb, w_ref[:, 2048:2176])


def _odd_proj(x, g, w):
    n = x.shape[0]
    row = lambda c: pl.BlockSpec((TM, c), lambda i: (i, 0))
    outs = [512, 512, 1024, LANE]
    return pl.pallas_call(
        _odd_proj_kernel,
        grid=(n // TM,),
        in_specs=[row(D_MODEL), _const_spec((1, D_MODEL)), _const_spec((D_MODEL, ODD_PROJ_PAD))],
        out_specs=[row(c) for c in outs],
        out_shape=[jax.ShapeDtypeStruct((n, c), f32) for c in outs],
        compiler_params=_cparams(("parallel",)),
        name="odd_proj",
    )(x, g, w)


def _out_ffn_kernel(x_ref, a_ref, b_ref, wo_ref, g_ref, wg_ref, wu_ref, wd_ref, o_ref):
    half = wo_ref.shape[0] // 2
    x = x_ref[...] + _mm(a_ref[...], wo_ref[0:half, :]) + _mm(b_ref[...], wo_ref[half:, :])
    hb = _rms(x, g_ref[...]).astype(bf16)
    fc = FF_DIM // FF_SPLIT
    ff = None
    for c in range(FF_SPLIT):
        sl = slice(c * fc, (c + 1) * fc)
        act = (_silu(_mm(hb, wg_ref[:, sl])) * _mm(hb, wu_ref[:, sl])).astype(bf16)
        down = _mm(act, wd_ref[sl, :])
        ff = down if ff is None else ff + down
    o_ref[...] = x + ff


def _out_ffn(x, a, b, wo, g, wg, wu, wd):
    n = x.shape[0]
    row = lambda c: pl.BlockSpec((TM, c), lambda i: (i, 0))
    once = lambda shape: pl.BlockSpec(shape, lambda i: (0, 0), pipeline_mode=pl.Buffered(1))
    return pl.pallas_call(
        _out_ffn_kernel,
        grid=(n // TM,),
        in_specs=[row(D_MODEL), row(a.shape[1]), row(b.shape[1]), once(wo.shape), _const_spec((1, D_MODEL)),
                  once(wg.shape), once(wu.shape), once(wd.shape)],
        out_specs=row(D_MODEL),
        out_shape=jax.ShapeDtypeStruct((n, D_MODEL), f32),
        compiler_params=_cparams(("parallel",)),
        name="out_ffn",
    )(x, a, b, wo, g, wg, wu, wd)


def _rope_tables(pos):
    half = RET_DK // 2
    inv = ROPE_BASE ** (-jnp.linspace(0.0, 1.0, half, dtype=f32))
    ang = pos.astype(f32)[:, None] * inv[None, :]
    cos, sin = jnp.cos(ang), jnp.sin(ang)
    return jnp.concatenate([cos, cos], -1), jnp.concatenate([-sin, sin], -1)


def _ret_decay(q):
    log_g = jnp.log1p(-jnp.exp2(-5.0 - jnp.arange(RET_H, dtype=f32)))
    idx = jnp.arange(q, dtype=f32)
    diff = idx[:, None] - idx[None, :]
    dmask = jnp.where(diff[None] >= 0, jnp.exp(log_g[:, None, None] * jnp.maximum(diff, 0.0)[None]), 0.0)
    xi = jnp.exp(log_g[:, None] * (idx + 1.0)[None])
    zeta = jnp.exp(log_g[:, None] * (q - 1.0 - idx)[None])
    g_chunk = jnp.exp(log_g * q)
    return dmask, xi, zeta, g_chunk


def _t5_bucket(rel):
    n = jnp.maximum(rel, 0)
    max_exact = REL_BUCKETS // 2
    nf = jnp.maximum(n, 1).astype(f32)
    large = max_exact + (jnp.log(nf / max_exact) / math.log(REL_MAX_DIST / max_exact)
                         * (REL_BUCKETS - max_exact)).astype(i32)
    large = jnp.minimum(large, REL_BUCKETS - 1)
    return jnp.where(n < max_exact, n, large)


def _rotary(x, c, s):
    return x * c + pltpu.roll(x, RET_DK // 2, 1) * s


def _retention_kernel(q_ref, k_ref, v_ref, g_ref, r0_ref, c_ref, s_ref, dm_ref, xi_ref, zt_ref, gc_ref,
                      o_ref, r_ref):
    @pl.when(pl.program_id(1) == 0)
    def _():
        r_ref[...] = r0_ref[...]

    cos, sin = c_ref[...], s_ref[...]
    for h in range(RET_H):
        sl = slice(h * RET_DK, (h + 1) * RET_DK)
        qr = _rotary(q_ref[:, sl], cos, sin).astype(bf16)
        kr = _rotary(k_ref[:, sl], cos, sin) * RET_DK ** -0.5
        vb = v_ref[:, sl].astype(bf16)
        r = r_ref[0, h]
        s = _mm_nt(qr, kr.astype(bf16)) * dm_ref[h]
        o = _mm(s.astype(bf16), vb) + _mm(qr, r.astype(bf16)) * xi_ref[h]
        r_ref[0, h] = r * gc_ref[h, 0:1, :] + _mm_tn((kr * zt_ref[h]).astype(bf16), vb)
        o = o * lax.rsqrt(jnp.mean(o * o, -1, keepdims=True) + EPS)
        o_ref[:, sl] = (_silu(g_ref[:, sl]) * o).astype(bf16)


def _retention(ret, r0, pos0, nb, t):
    q = math.gcd(t, RET_CHUNK)
    nc = t // q
    cos, sin = _rope_tables(pos0 + jnp.arange(t))
    dmask, xi, zeta, g_chunk = _ret_decay(q)
    bcast = lambda a: jnp.broadcast_to(a[:, :, None], (RET_H, a.shape[1], LANE))
    if ret.ndim == 2:
        col = lambda j: pl.BlockSpec((q, 512), lambda b, c: (b * nc + c, j))
        o_shape = (nb * t, 512)
    else:
        assert nc == 1
        col = lambda j: pl.BlockSpec((None, q, 512), lambda b, c: (b, 0, j))
        o_shape = (nb, t, 512)
    tab = pl.BlockSpec((q, LANE), lambda b, c: (c, 0))
    state = pl.BlockSpec((1, RET_H, RET_DK, RET_DV), lambda b, c: (b, 0, 0, 0))
    return pl.pallas_call(
        _retention_kernel,
        grid=(nb, nc),
        in_specs=[col(0), col(1), col(2), col(3), state, tab, tab,
                  _const_spec((RET_H, q, q)), _const_spec((RET_H, q, LANE)), _const_spec((RET_H, q, LANE)),
                  _const_spec((RET_H, 8, LANE))],
        out_specs=[col(0), state],
        out_shape=[jax.ShapeDtypeStruct(o_shape, bf16),
                   jax.ShapeDtypeStruct((nb, RET_H, RET_DK, RET_DV), f32)],
        compiler_params=_cparams(("parallel", "arbitrary")),
        name="retention_t%d" % t,
    )(ret, ret, ret, ret, r0, cos, sin, dmask, bcast(xi), bcast(zeta),
      jnp.broadcast_to(g_chunk[:, None, None], (RET_H, 8, LANE)))


def _sort_key(score):
    bits = lax.bitcast_convert_type(score, i32)
    bits = jnp.where(bits == INT_MIN, 0, bits)
    return jnp.where(bits < 0, bits ^ 0x7FFFFFFF, bits)


def _count(key_ref, nblk, pred):
    def body(kb, acc):
        r0 = pl.multiple_of(kb * LANE, LANE)
        hit = jnp.where(pred(key_ref[pl.ds(r0, LANE), :], r0), 1, 0).astype(i32)
        return acc + jnp.sum(hit.reshape(LANE // 8, 8, LANE), axis=0)
    acc = lax.fori_loop(0, nblk, body, jnp.zeros((8, LANE), i32))
    return jnp.sum(acc, axis=0, keepdims=True)


def _select_topk(key_ref, nblk, topk, idx_bits):
    def bit_step(it, t):
        cand = t + jnp.left_shift(jnp.int32(1), 31 - it)
        cnt = _count(key_ref, nblk, lambda blk, r0: blk >= cand)
        return jnp.where(cnt >= topk, cand, t)

    t = lax.fori_loop(0, 32, bit_step, jnp.full((1, LANE), INT_MIN, i32))
    t = jnp.maximum(t, INT_MIN + 1)
    c_ge = _count(key_ref, nblk, lambda blk, r0: blk >= t)
    c_gt = _count(key_ref, nblk, lambda blk, r0: blk > t)
    surplus = c_ge > topk
    rows = lax.broadcasted_iota(i32, (LANE, LANE), 0)

    @pl.when(jnp.max(jnp.where(surplus, 1, 0)) > 0)
    def _():
        want = jnp.where(surplus, topk - c_gt, jnp.int32(2 ** 30))

        def idx_step(it, x):
            cand = x + jnp.left_shift(jnp.int32(1), idx_bits - 1 - it)
            cnt = _count(key_ref, nblk, lambda blk, r0: jnp.where(blk == t, rows + r0, cand) < cand)
            return jnp.where(cnt < want, cand, x)

        last = lax.fori_loop(0, idx_bits, idx_step, jnp.zeros((1, LANE), i32))

        def demote(kb, carry):
            r0 = pl.multiple_of(kb * LANE, LANE)
            blk = key_ref[pl.ds(r0, LANE), :]
            drop = jnp.where(blk == t, rows + r0, last) > last
            key_ref[pl.ds(r0, LANE), :] = jnp.where(drop, INT_MIN, blk)
            return carry

        lax.fori_loop(0, nblk, demote, 0)

    return t


def _dsa_prompt_kernel(qa_ref, qi_ref, kiwiq_ref, k_ref, vt_ref, kiwik_ref, bias_ref, o_ref, key_ref,
                       *, topk, idx_bits, nqb):
    qb = pl.program_id(1)
    nkb = qb + 1
    wit = jnp.transpose(kiwiq_ref[...])[IDX_DIM:IDX_DIM + IDX_H, :] * IDX_H ** -0.5
    qpos = qb * LANE + lax.broadcasted_iota(i32, (LANE, LANE), 1)
    rows = lax.broadcasted_iota(i32, (LANE, LANE), 0)

    def score_blk(kb, carry):
        r0 = pl.multiple_of(kb * LANE, LANE)
        kic = kiwik_ref[pl.ds(r0, LANE), 0:IDX_DIM].astype(bf16)
        acc = jnp.zeros((LANE, LANE), f32)
        for h in range(IDX_H):
            s = _mm_nt(kic, qi_ref[:, h * IDX_DIM:(h + 1) * IDX_DIM])
            acc = acc + jnp.maximum(s, 0.0) * wit[h:h + 1, :]
        key_ref[pl.ds(r0, LANE), :] = jnp.where(rows + r0 <= qpos, _sort_key(acc), INT_MIN)
        return carry

    lax.fori_loop(0, nkb, score_blk, 0)
    t = _select_topk(key_ref, nkb, topk, idx_bits)

    for h in range(ATT_H):
        sl = slice(h * ATT_DH, (h + 1) * ATT_DH)
        qh = qa_ref[:, sl]

        def att_blk(kb, carry):
            m, l, acc = carry
            r0 = pl.multiple_of(kb * LANE, LANE)
            s = _mm_nt(k_ref[pl.ds(r0, LANE), sl], qh) * ATT_DH ** -0.5 + bias_ref[jnp.minimum(qb - kb, 2), h]
            sel = key_ref[pl.ds(r0, LANE), :] >= t
            s = jnp.where(sel, s, NEG_BIG)
            m_new = jnp.maximum(m, jnp.max(s, axis=0, keepdims=True))
            alpha = jnp.exp(m - m_new)
            p = jnp.where(sel, jnp.exp(s - m_new), 0.0)
            l = l * alpha + jnp.sum(p, axis=0, keepdims=True)
            acc = acc * alpha + _mm(vt_ref[h * nqb + kb], p.astype(bf16))
            return m_new, l, acc

        init = (jnp.full((1, LANE), NEG_BIG, f32), jnp.zeros((1, LANE), f32), jnp.zeros((ATT_DH, LANE), f32))
        m, l, acc = lax.fori_loop(0, nkb, att_blk, init)
        o_ref[:, sl] = jnp.transpose(acc / l).astype(bf16)


def _bias_tiles(rel_bias, nd):
    j = jnp.arange(LANE)[:, None]
    i = jnp.arange(LANE)[None, :]
    rel = jnp.arange(nd)[:, None, None] * LANE + (i - j)[None]
    return jnp.moveaxis(rel_bias.astype(f32)[_t5_bucket(rel)], -1, 1)


def _dsa_prompt(qa, kab, vab, qi, kiwi, rel_bias, nb, t):
    nqb = t // LANE
    topk = min(TOPK_MAX, t // 4)
    assert REL_MAX_DIST <= LANE + 1
    bias = _bias_tiles(rel_bias, 3)
    vt = vab[:nb * t].reshape(nb, nqb, LANE, ATT_H, ATT_DH).transpose(0, 3, 1, 4, 2)
    vt = vt.reshape(nb * ATT_H * nqb, ATT_DH, LANE)
    qrow = lambda c: pl.BlockSpec((LANE, c), lambda b, q: (b * nqb + q, 0))
    seq = lambda c: pl.BlockSpec((t, c), lambda b, q: (b, 0))
    kern = functools.partial(_dsa_prompt_kernel, topk=topk, idx_bits=max(1, (t - 1).bit_length()), nqb=nqb)
    return pl.pallas_call(
        kern,
        grid=(nb, nqb),
        in_specs=[qrow(512), qrow(512), qrow(LANE), seq(512),
                  pl.BlockSpec((ATT_H * nqb, ATT_DH, LANE), lambda b, q: (b, 0, 0)), seq(LANE),
                  _const_spec((3, ATT_H, LANE, LANE))],
        out_specs=qrow(512),
        out_shape=jax.ShapeDtypeStruct((nb * t, 512), bf16),
        scratch_shapes=[pltpu.VMEM((t, LANE), i32)],
        compiler_params=_cparams(("parallel", "arbitrary")),
        name="dsa_prompt",
    )(qa, qi, kiwi, kab, vt, kiwi, bias)


def _softplus(x):
    return jnp.maximum(x, 0.0) + jnp.log1p(jnp.exp(-jnp.abs(x)))


def _cumsum_rows(tri, a):
    hi = a.astype(bf16)
    r1 = a - hi.astype(f32)
    mid = r1.astype(bf16)
    lo = (r1 - mid.astype(f32)).astype(bf16)
    return _mm(tri, hi) + _mm(tri, mid) + _mm(tri, lo)


def _odd_mixer_kernel(u_ref, z_ref, xbc_ref, dt_ref, pp_ref, cp_ref, h0_ref, pw_ref, ps_ref, cw_ref, cb_ref,
                      dtb_ref, alog_ref, dsk_ref, nrm_ref, tri_ref, po_ref, y_ref, h_ref, ubuf, xbuf, ybuf,
                      *, q, pos0):
    c = pl.program_id(1)

    @pl.when(c == 0)
    def _():
        ubuf[0:1, :] = jnp.zeros((1, POOL_DIM), f32)
        ubuf[1:16, :] = pp_ref[0]
        xbuf[0:8 - (CONV_W - 1), :] = jnp.zeros((8 - (CONV_W - 1), CONV_DIM), f32)
        xbuf[8 - (CONV_W - 1):8, :] = cp_ref[0]
        h_ref[...] = h0_ref[...]

    pos = pos0 + c * q + lax.broadcasted_iota(i32, (q, LANE), 0)
    causal = lax.broadcasted_iota(i32, (q, q), 0) >= lax.broadcasted_iota(i32, (q, q), 1)

    u = u_ref[...]
    ubuf[16:16 + q, :] = u
    for g, w in enumerate(POOL_WINDOWS):
        sl = slice(g * POOL_GC, (g + 1) * POOL_GC)
        acc = u[:, sl]
        for k in range(1, w):
            acc = acc + ubuf[16 - k:16 - k + q, sl]
        d = acc / jnp.minimum(pos + 1, w).astype(f32) - u[:, sl]
        po_ref[:, sl] = (_mm(d.astype(bf16), pw_ref[g]) * ps_ref[:, sl]).astype(bf16)
    ubuf[0:16, :] = ubuf[q:q + 16, :]

    xbuf[8:8 + q, :] = xbc_ref[...]
    conv = cb_ref[...]
    for j in range(CONV_W):
        off = 8 - (CONV_W - 1) + j
        conv = conv + xbuf[off:off + q, :] * cw_ref[j:j + 1, :]
    xbuf[0:8, :] = xbuf[q:q + 8, :]
    act = _silu(conv)
    xs = act[:, 0:D_INNER]

    dt = _softplus(dt_ref[...] + dtb_ref[...])
    a = dt * (-jnp.exp(alog_ref[...]))
    cs = _cumsum_rows(tri_ref[...], a)
    cs_t = jnp.transpose(cs)
    dt_t = jnp.transpose(dt)
    cs_last = cs[q - 1:q, :]
    w_end = jnp.exp(cs_last - cs) * dt
    ecs = jnp.exp(cs)
    hpg = SSM_H // SSM_G
    for g in range(SSM_G):
        bm = act[:, D_INNER + g * SSM_N:D_INNER + (g + 1) * SSM_N].astype(bf16)
        cm = act[:, D_INNER + (SSM_G + g) * SSM_N:D_INNER + (SSM_G + g + 1) * SSM_N].astype(bf16)
        cb = _mm_nt(cm, bm)
        for hh in range(hpg):
            h = g * hpg + hh
            psl = slice(h * SSM_P, (h + 1) * SSM_P)
            seg = cs[:, h:h + 1] - cs_t[h:h + 1, :]
            lm = jnp.exp(jnp.where(causal, seg, NEG_BIG))
            sc = cb * lm * dt_t[h:h + 1, :]
            xh = xs[:, psl]
            hs = h_ref[0, h]
            yh = _mm(sc.astype(bf16), xh.astype(bf16)) + _mm_nt(cm, hs.astype(bf16)) * ecs[:, h:h + 1]
            h_ref[0, h] = hs * jnp.exp(cs_last[:, h:h + 1]) + _mm_tn((xh * w_end[:, h:h + 1]).astype(bf16), bm)
            ybuf[:, psl] = yh
    y = (ybuf[...] + dsk_ref[...] * xs) * _silu(z_ref[...])
    gw = D_INNER // SSM_G
    for g in range(SSM_G):
        sl = slice(g * gw, (g + 1) * gw)
        yg = y[:, sl]
        y_ref[:, sl] = (yg * lax.rsqrt(jnp.mean(yg * yg, -1, keepdims=True) + EPS) * nrm_ref[:, sl]).astype(bf16)


def _pad_lanes(v):
    return jnp.pad(v.astype(f32), (0, LANE - v.shape[0]))[None, :]


def _odd_mixer(u, z, xbc, dt, pool_prev, conv_prev, h0, prm, pos0, nb, t):
    pool_w, pool_scale, conv_w, conv_b, dt_bias, a_log, d_skip, ssm_norm = prm
    q = math.gcd(t, SSD_CHUNK)
    nc = t // q
    if u.ndim == 2:
        row = lambda c: pl.BlockSpec((q, c), lambda b, i: (b * nc + i, 0))
        lead = (nb * t,)
    else:
        assert nc == 1
        row = lambda c: pl.BlockSpec((None, q, c), lambda b, i: (b, 0, 0))
        lead = (nb, t)
    per_seq = lambda *s: pl.BlockSpec((1,) + s, lambda b, i: (b,) + (0,) * len(s))
    tri = jnp.tril(jnp.ones((q, q), bf16))
    return pl.pallas_call(
        functools.partial(_odd_mixer_kernel, q=q, pos0=pos0),
        grid=(nb, nc),
        in_specs=[row(POOL_DIM), row(D_INNER), row(CONV_DIM), row(LANE),
                  per_seq(POOL_PAST, POOL_DIM), per_seq(CONV_W - 1, CONV_DIM), per_seq(SSM_H, SSM_P, SSM_N),
                  _const_spec((POOL_GROUPS, POOL_GC, POOL_GC)), _const_spec((1, POOL_DIM)),
                  _const_spec((CONV_W, CONV_DIM)), _const_spec((1, CONV_DIM)),
                  _const_spec((1, LANE)), _const_spec((1, LANE)),
                  _const_spec((1, D_INNER)), _const_spec((1, D_INNER)), _const_spec((q, q))],
        out_specs=[row(POOL_DIM), row(D_INNER), per_seq(SSM_H, SSM_P, SSM_N)],
        out_shape=[jax.ShapeDtypeStruct(lead + (POOL_DIM,), bf16), jax.ShapeDtypeStruct(lead + (D_INNER,), bf16),
                   jax.ShapeDtypeStruct((nb, SSM_H, SSM_P, SSM_N), f32)],
        scratch_shapes=[pltpu.VMEM((q + 16, POOL_DIM), f32), pltpu.VMEM((q + 8, CONV_DIM), f32),
                        pltpu.VMEM((q, D_INNER), f32)],
        compiler_params=_cparams(("parallel", "arbitrary")),
        name="odd_mixer_t%d" % t,
    )(u, z, xbc, dt, pool_prev, conv_prev, h0, pool_w.astype(bf16), pool_scale[None, :], conv_w, conv_b[None, :],
      _pad_lanes(dt_bias), _pad_lanes(a_log), jnp.repeat(d_skip, SSM_P)[None, :], ssm_norm[None, :], tri)


def _dsa_sample_score_kernel(pt_ref, qi_ref, kiwi_ref, *rest, npg):
    pages, o_ref = rest[:npg], rest[npg]
    ts = qi_ref.shape[0]
    kiwi = kiwi_ref[...]
    wi = kiwi[:, IDX_DIM:IDX_DIM + IDX_H] * IDX_H ** -0.5
    ki_new = jnp.concatenate([kiwi[:, 0:IDX_DIM], jnp.zeros((PAGE - ts, IDX_DIM), f32)], axis=0)
    ki = jnp.concatenate([p[...] for p in pages] + [ki_new], axis=0).astype(bf16)
    nk = ki.shape[0]
    acc = jnp.zeros((ts, nk), f32)
    for h in range(IDX_H):
        s = _mm_nt(qi_ref[:, h * IDX_DIM:(h + 1) * IDX_DIM], ki)
        acc = acc + jnp.maximum(s, 0.0) * wi[:, h:h + 1]
    col = lax.broadcasted_iota(i32, (ts, nk), 1)
    row = lax.broadcasted_iota(i32, (ts, nk), 0)
    o_ref[...] = jnp.where(col <= npg * PAGE + row, _sort_key(acc), INT_MIN)


def _select_kernel(k_ref, o_ref, key_ref, *, topk, idx_bits):
    key_ref[...] = k_ref[...]
    t = _select_topk(key_ref, key_ref.shape[0] // LANE, topk, idx_bits)
    o_ref[...] = jnp.where(key_ref[...] >= t, 1.0, 0.0)


def _dsa_sample_attn_kernel(pt_ref, qblk_ref, kn_ref, vn_ref, sel_ref, bias_ref, *rest, npg):
    kpages, vpages, o_ref = rest[:npg], rest[npg:2 * npg], rest[2 * npg]
    ts = kn_ref.shape[0]
    pad = jnp.zeros((PAGE - ts, ATT_H * ATT_DH), f32)
    kb = jnp.concatenate([p[...] for p in kpages] + [kn_ref[...], pad], axis=0).astype(bf16)
    vb = jnp.concatenate([p[...] for p in vpages] + [vn_ref[...], pad], axis=0).astype(bf16)
    s = _mm_nt(qblk_ref[...], kb) * ATT_DH ** -0.5 + bias_ref[...]
    sel = sel_ref[...] > 0.5
    s = jnp.where(sel, s, NEG_BIG)
    p = jnp.where(sel, jnp.exp(s - jnp.max(s, axis=-1, keepdims=True)), 0.0)
    o = _mm(p.astype(bf16), vb) / jnp.sum(p, axis=-1, keepdims=True)
    for h in range(ATT_H):
        sl = slice(h * ATT_DH, (h + 1) * ATT_DH)
        o_ref[:, sl] = o[h * ts:(h + 1) * ts, sl].astype(bf16)


def _dsa_sample(qa, ka, va, qi, kiwi, pool_k, pool_v, pool_ki, page_table, rel_bias):
    db, ts = qa.shape[:2]
    npg = page_table.shape[1]
    n_past = npg * PAGE
    nk = n_past + PAGE
    nq = db * ts
    topk = min(TOPK_MAX, (n_past + ts) // 4)
    hd = ATT_H * ATT_DH
    page = lambda c, j: pl.BlockSpec((None, PAGE, c), lambda b, pt: (pt[b, j], 0, 0))
    seq = lambda r, c: pl.BlockSpec((None, r, c), lambda b, pt: (b, 0, 0))

    keys = pl.pallas_call(
        functools.partial(_dsa_sample_score_kernel, npg=npg),
        grid_spec=pltpu.PrefetchScalarGridSpec(
            num_scalar_prefetch=1, grid=(db,),
            in_specs=[seq(ts, IDX_H * IDX_DIM), seq(ts, LANE)] + [page(IDX_DIM, j) for j in range(npg)],
            out_specs=seq(ts, nk)),
        out_shape=jax.ShapeDtypeStruct((db, ts, nk), i32),
        compiler_params=_cparams(("parallel",)),
        name="dsa_sample_score",
    )(page_table, qi, kiwi, *([pool_ki] * npg))

    col = pl.BlockSpec((nk, LANE), lambda i: (0, i))
    sel = pl.pallas_call(
        functools.partial(_select_kernel, topk=topk, idx_bits=max(1, (nk - 1).bit_length())),
        grid=(nq // LANE,),
        in_specs=[col],
        out_specs=col,
        out_shape=jax.ShapeDtypeStruct((nk, nq), f32),
        scratch_shapes=[pltpu.VMEM((nk, LANE), i32)],
        compiler_params=_cparams(("parallel",)),
        name="dsa_sample_select",
    )(keys.reshape(nq, nk).T)
    sel = jnp.tile(sel.T.reshape(db, ts, nk), (1, ATT_H, 1))

    head_of_lane = jnp.arange(hd)[None, :] // ATT_DH == jnp.arange(ATT_H)[:, None]
    qblk = jnp.where(head_of_lane[None, :, None, :], qa[:, None, :, :], 0).reshape(db, ATT_H * ts, hd)
    rel = n_past + jnp.arange(ts)[:, None] - jnp.arange(nk)[None, :]
    bias = jnp.moveaxis(rel_bias.astype(f32)[_t5_bucket(rel)], -1, 0).reshape(ATT_H * ts, nk)

    return pl.pallas_call(
        functools.partial(_dsa_sample_attn_kernel, npg=npg),
        grid_spec=pltpu.PrefetchScalarGridSpec(
            num_scalar_prefetch=1, grid=(db,),
            in_specs=[seq(ATT_H * ts, hd), seq(ts, hd), seq(ts, hd), seq(ATT_H * ts, nk),
                      pl.BlockSpec((ATT_H * ts, nk), lambda b, pt: (0, 0))]
                     + [page(hd, j) for j in range(npg)] * 2,
            out_specs=seq(ts, hd)),
        out_shape=jax.ShapeDtypeStruct((db, ts, hd), bf16),
        compiler_params=_cparams(("parallel",)),
        name="dsa_sample_attn",
    )(page_table, qblk, ka, va, sel, bias, *([pool_k] * npg), *([pool_v] * npg))


def kernel(x_prompt, x_sample, cache_k, cache_v, cache_kidx, state_ret, state_pool, state_conv, state_ssm,
           page_table, norm_mix, norm_ffn, w_in_even, w_out_even, q_norm, k_norm, rel_bias,
           w_in_odd, w_out_odd, pool_w, pool_scale, conv_w, conv_b, dt_bias, a_log, d_skip, ssm_norm,
           w_gate, w_up, w_down):
    bp, sp, d = x_prompt.shape
    db, ts, _ = x_sample.shape
    n_p, n_s = bp * sp, db * ts
    assert d == D_MODEL and (n_p + n_s) % TM == 0 and sp % LANE == 0
    n_past = page_table.shape[1] * PAGE
    hd = ATT_H * ATT_DH
    x = jnp.concatenate([x_prompt.reshape(n_p, d), x_sample.reshape(n_s, d)], axis=0)
    prompt = lambda a, *s: a[:n_p].reshape(bp, sp, *s)
    sample = lambda a, *s: a[n_p:].reshape(db, ts, *s)
    both = lambda p, s: jnp.concatenate([p.reshape(n_p, -1), s.reshape(n_s, -1)], axis=0)
    outs = [[] for _ in range(14)]
    for l in range(DEPTH):
        if l % 2 == 0:
            i = l // 2
            w_in = jnp.pad(w_in_even[i], ((0, 0), (0, EVEN_PROJ_PAD - EVEN_PROJ))).astype(bf16)
            ret, qa, ka, kab, va, vab, qi, kiwi = _even_proj(
                x, norm_mix[l][None], w_in, q_norm[i][None], k_norm[i][None])
            ret_p, rstate_p = _retention(ret, jnp.zeros((bp, RET_H, RET_DK, RET_DV), f32), 0, bp, sp)
            ret_s, rstate_s = _retention(sample(ret, 4 * 512), state_ret[i], n_past, db, ts)
            att_p = _dsa_prompt(qa, kab, vab, qi, kiwi, rel_bias, bp, sp)
            pages = cache_k.shape[1]
            att_s = _dsa_sample(sample(qa, hd), sample(ka, hd), sample(va, hd), sample(qi, IDX_H * IDX_DIM),
                                sample(kiwi, LANE), cache_k[i].reshape(pages, PAGE, hd),
                                cache_v[i].reshape(pages, PAGE, hd), cache_kidx[i], page_table, rel_bias)
            mix_a, mix_b = both(ret_p, ret_s), both(att_p, att_s)
            w_out = w_out_even[i].astype(bf16)
            new = [prompt(ka, ATT_H, ATT_DH), prompt(va, ATT_H, ATT_DH), prompt(kiwi, LANE)[..., :IDX_DIM], rstate_p,
                   None, None, None,
                   sample(ka, ATT_H, ATT_DH), sample(va, ATT_H, ATT_DH), sample(kiwi, LANE)[..., :IDX_DIM], rstate_s,
                   None, None, None]
        else:
            j = l // 2
            w_in = jnp.pad(w_in_odd[j], ((0, 0), (0, ODD_PROJ_PAD - ODD_PROJ))).astype(bf16)
            u, z, xbc, dt = _odd_proj(x, norm_mix[l][None], w_in)
            prm = (pool_w[j], pool_scale[j], conv_w[j], conv_b[j], dt_bias[j], a_log[j], d_skip[j], ssm_norm[j])
            po_p, y_p, h_p = _odd_mixer(u, z, xbc, dt, jnp.zeros((bp, POOL_PAST, POOL_DIM), f32),
                                        jnp.zeros((bp, CONV_W - 1, CONV_DIM), f32),
                                        jnp.zeros((bp, SSM_H, SSM_P, SSM_N), f32), prm, 0, bp, sp)
            u_s, xbc_s = sample(u, POOL_DIM), sample(xbc, CONV_DIM)
            po_s, y_s, h_s = _odd_mixer(u_s, sample(z, D_INNER), xbc_s, sample(dt, LANE),
                                        state_pool[j], state_conv[j], state_ssm[j], prm, n_past, db, ts)
            mix_a, mix_b = both(po_p, po_s), both(y_p, y_s)
            w_out = w_out_odd[j].astype(bf16)
            tail = lambda prev, cur, n: jnp.concatenate([prev.astype(f32), cur], axis=1)[:, -n:]
            new = [None, None, None, None,
                   prompt(u, POOL_DIM)[:, -POOL_PAST:], prompt(xbc, CONV_DIM)[:, -(CONV_W - 1):], h_p,
                   None, None, None, None,
                   tail(state_pool[j], u_s, POOL_PAST), tail(state_conv[j], xbc_s, CONV_W - 1), h_s]
        for acc, leaf in zip(outs, new):
            if leaf is not None:
                acc.append(leaf)
        x = _out_ffn(x, mix_a, mix_b, w_out, norm_ffn[l][None], w_gate[l].astype(bf16), w_up[l].astype(bf16),
                     w_down[l].astype(bf16))
    return (x[:n_p].reshape(bp, sp, d), x[n_p:].reshape(db, ts, d)) + tuple(jnp.stack(a) for a in outs)
```

```python
import functools
import math

import jax
import jax.numpy as jnp
import numpy as np
from jax import lax
from jax.experimental import pallas as pl
from jax.experimental.pallas import tpu as pltpu

f32 = jnp.float32
bf16 = jnp.bfloat16
i32 = jnp.int32

D_MODEL = 1024
DEPTH = 4
PAGE = 128
RET_H, RET_DK, RET_DV, RET_CHUNK = 4, 128, 128, 128
ROPE_BASE = 10000.0
ATT_H, ATT_DH, ATT_BLOCK = 4, 128, 128
IDX_H, IDX_DIM = 8, 64
TOPK_MAX = 256
REL_BUCKETS, REL_MAX_DIST = 32, 128
POOL_WINDOWS = (2, 4, 8, 16)
POOL_GROUPS = 4
POOL_DIM = D_MODEL // 2
POOL_GC = POOL_DIM // POOL_GROUPS
POOL_PAST = 15
D_INNER = D_MODEL // 2
SSM_P = 64
SSM_H = D_INNER // SSM_P
SSM_G = 2
SSM_N = 128
CONV_W = 4
CONV_DIM = D_INNER + 2 * SSM_G * SSM_N
SSD_CHUNK = 128
FF_DIM = -(-8 * D_MODEL // (3 * 256)) * 256
EPS = 1e-6

EVEN_PROJ = 4 * 512 + 3 * 512 + 512 + IDX_DIM + IDX_H
EVEN_PROJ_PAD = 4224
ODD_PROJ = POOL_DIM + D_INNER + CONV_DIM + SSM_H
ODD_PROJ_PAD = 2176

LANE = 128
INT_MIN = -(2 ** 31)
NEG_BIG = -1e30
VMEM_LIMIT = 56 * 1024 * 1024
TM = 256
FF_SPLIT = 2


def _cparams(sem):
    return pltpu.CompilerParams(dimension_semantics=sem, vmem_limit_bytes=VMEM_LIMIT)


def _mm(a, b):
    return jnp.dot(a, b, preferred_element_type=f32)


def _mm_nt(a, b):
    return lax.dot_general(a, b, (((1,), (1,)), ((), ())), preferred_element_type=f32)


def _mm_tn(a, b):
    return lax.dot_general(a, b, (((0,), (0,)), ((), ())), preferred_element_type=f32)


def _rms(x, g):
    return x * lax.rsqrt(jnp.mean(x * x, -1, keepdims=True) + EPS) * g


def _silu(x):
    return x / (1.0 + jnp.exp(-x))


def _const_spec(shape):
    nd = len(shape)
    return pl.BlockSpec(shape, lambda *a: (0,) * nd)


def _even_proj_kernel(x_ref, g_ref, w_ref, qg_ref, kg_ref,
                      ret_ref, qa_ref, ka_ref, kab_ref, va_ref, vab_ref, qi_ref, kiwi_ref):
    xb = _rms(x_ref[...], g_ref[...]).astype(bf16)
    ret_ref[...] = _mm(xb, w_ref[:, 0:2048])
    qa = _mm(xb, w_ref[:, 2048:2560])
    ka = _mm(xb, w_ref[:, 2560:3072])
    for h in range(ATT_H):
        sl = slice(h * ATT_DH, (h + 1) * ATT_DH)
        qa_ref[:, sl] = _rms(qa[:, sl], qg_ref[...]).astype(bf16)
        kn = _rms(ka[:, sl], kg_ref[...])
        ka_ref[:, sl] = kn
        kab_ref[:, sl] = kn.astype(bf16)
    va = _mm(xb, w_ref[:, 3072:3584])
    va_ref[...] = va
    vab_ref[...] = va.astype(bf16)
    qi_ref[...] = (_mm(xb, w_ref[:, 3584:4096]) * IDX_DIM ** -0.5).astype(bf16)
    kiwi_ref[...] = _mm(xb, w_ref[:, 4096:4224])


def _even_proj(x, g, w, qg, kg):
    n = x.shape[0]
    row = lambda c: pl.BlockSpec((TM, c), lambda i: (i, 0))
    outs = [(2048, f32), (512, bf16), (512, f32), (512, bf16), (512, f32), (512, bf16), (512, bf16), (LANE, f32)]
    return pl.pallas_call(
        _even_proj_kernel,
        grid=(n // TM,),
        in_specs=[row(D_MODEL), _const_spec((1, D_MODEL)), _const_spec((D_MODEL, EVEN_PROJ_PAD)),
                  _const_spec((1, ATT_DH)), _const_spec((1, ATT_DH))],
        out_specs=[row(c) for c, _ in outs],
        out_shape=[jax.ShapeDtypeStruct((n, c), dt) for c, dt in outs],
        compiler_params=_cparams(("parallel",)),
        name="even_proj",
    )(x, g, w, qg, kg)


def _odd_proj_kernel(x_ref, g_ref, w_ref, u_ref, z_ref, xbc_ref, dt_ref):
    xb = _rms(x_ref[...], g_ref[...]).astype(bf16)
    u_ref[...] = _mm(xb, w_ref[:, 0:512])
    z_ref[...] = _mm(xb, w_ref[:, 512:1024])
    xbc_ref[...] = _mm(xb, w_ref[:, 1024:2048])
    dt_ref[...] = _mm(xb, w_ref[:, 2048:2176])


def _odd_proj(x, g, w):
    n = x.shape[0]
    row = lambda c: pl.BlockSpec((TM, c), lambda i: (i, 0))
    outs = [512, 512, 1024, LANE]
    return pl.pallas_call(
        _odd_proj_kernel,
        grid=(n // TM,),
        in_specs=[row(D_MODEL), _const_spec((1, D_MODEL)), _const_spec((D_MODEL, ODD_PROJ_PAD))],
        out_specs=[row(c) for c in outs],
        out_shape=[jax.ShapeDtypeStruct((n, c), f32) for c in outs],
        compiler_params=_cparams(("parallel",)),
        name="odd_proj",
    )(x, g, w)


def _out_ffn_kernel(x_ref, a_ref, b_ref, wo_ref, g_ref, wg_ref, wu_ref, wd_ref, o_ref):
    half = wo_ref.shape[0] // 2
    x = x_ref[...] + _mm(a_ref[...], wo_ref[0:half, :]) + _mm(b_ref[...], wo_ref[half:, :])
    hb = _rms(x, g_ref[...]).astype(bf16)
    fc = FF_DIM // FF_SPLIT
    ff = None
    for c in range(FF_SPLIT):
        sl = slice(c * fc, (c + 1) * fc)
        act = (_silu(_mm(hb, wg_ref[:, sl])) * _mm(hb, wu_ref[:, sl])).astype(bf16)
        down = _mm(act, wd_ref[sl, :])
        ff = down if ff is None else ff + down
    o_ref[...] = x + ff


def _out_ffn(x, a, b, wo, g, wg, wu, wd):
    n = x.shape[0]
    row = lambda c: pl.BlockSpec((TM, c), lambda i: (i, 0))
    once = lambda shape: pl.BlockSpec(shape, lambda i: (0, 0), pipeline_mode=pl.Buffered(1))
    return pl.pallas_call(
        _out_ffn_kernel,
        grid=(n // TM,),
        in_specs=[row(D_MODEL), row(a.shape[1]), row(b.shape[1]), once(wo.shape), _const_spec((1, D_MODEL)),
                  once(wg.shape), once(wu.shape), once(wd.shape)],
        out_specs=row(D_MODEL),
        out_shape=jax.ShapeDtypeStruct((n, D_MODEL), f32),
        compiler_params=_cparams(("parallel",)),
        name="out_ffn",
    )(x, a, b, wo, g, wg, wu, wd)


def _rope_tables(pos):
    half = RET_DK // 2
    inv = ROPE_BASE ** (-jnp.linspace(0.0, 1.0, half, dtype=f32))
    ang = pos.astype(f32)[:, None] * inv[None, :]
    cos, sin = jnp.cos(ang), jnp.sin(ang)
    return jnp.concatenate([cos, cos], -1), jnp.concatenate([-sin, sin], -1)


def _ret_decay(q):
    log_g = jnp.log1p(-jnp.exp2(-5.0 - jnp.arange(RET_H, dtype=f32)))
    idx = jnp.arange(q, dtype=f32)
    diff = idx[:, None] - idx[None, :]
    dmask = jnp.where(diff[None] >= 0, jnp.exp(log_g[:, None, None] * jnp.maximum(diff, 0.0)[None]), 0.0)
    xi = jnp.exp(log_g[:, None] * (idx + 1.0)[None])
    zeta = jnp.exp(log_g[:, None] * (q - 1.0 - idx)[None])
    g_chunk = jnp.exp(log_g * q)
    return dmask, xi, zeta, g_chunk


def _t5_bucket(rel):
    n = jnp.maximum(rel, 0)
    max_exact = REL_BUCKETS // 2
    nf = jnp.maximum(n, 1).astype(f32)
    large = max_exact + (jnp.log(nf / max_exact) / math.log(REL_MAX_DIST / max_exact)
                         * (REL_BUCKETS - max_exact)).astype(i32)
    large = jnp.minimum(large, REL_BUCKETS - 1)
    return jnp.where(n < max_exact, n, large)


def _rotary(x, c, s):
    return x * c + pltpu.roll(x, RET_DK // 2, 1) * s


def _retention_kernel(q_ref, k_ref, v_ref, g_ref, r0_ref, c_ref, s_ref, dm_ref, xi_ref, zt_ref, gc_ref,
                      o_ref, r_ref):
    @pl.when(pl.program_id(1) == 0)
    def _():
        r_ref[...] = r0_ref[...]

    cos, sin = c_ref[...], s_ref[...]
    for h in range(RET_H):
        sl = slice(h * RET_DK, (h + 1) * RET_DK)
        qr = _rotary(q_ref[:, sl], cos, sin).astype(bf16)
        kr = _rotary(k_ref[:, sl], cos, sin) * RET_DK ** -0.5
        vb = v_ref[:, sl].astype(bf16)
        r = r_ref[0, h]
        s = _mm_nt(qr, kr.astype(bf16)) * dm_ref[h]
        o = _mm(s.astype(bf16), vb) + _mm(qr, r.astype(bf16)) * xi_ref[h]
        r_ref[0, h] = r * gc_ref[h, 0:1, :] + _mm_tn((kr * zt_ref[h]).astype(bf16), vb)
        o = o * lax.rsqrt(jnp.mean(o * o, -1, keepdims=True) + EPS)
        o_ref[:, sl] = (_silu(g_ref[:, sl]) * o).astype(bf16)


def _retention(ret, r0, pos0, nb, t):
    q = math.gcd(t, RET_CHUNK)
    nc = t // q
    cos, sin = _rope_tables(pos0 + jnp.arange(t))
    dmask, xi, zeta, g_chunk = _ret_decay(q)
    bcast = lambda a: jnp.broadcast_to(a[:, :, None], (RET_H, a.shape[1], LANE))
    if ret.ndim == 2:
        col = lambda j: pl.BlockSpec((q, 512), lambda b, c: (b * nc + c, j))
        o_shape = (nb * t, 512)
    else:
        assert nc == 1
        col = lambda j: pl.BlockSpec((None, q, 512), lambda b, c: (b, 0, j))
        o_shape = (nb, t, 512)
    tab = pl.BlockSpec((q, LANE), lambda b, c: (c, 0))
    state = pl.BlockSpec((1, RET_H, RET_DK, RET_DV), lambda b, c: (b, 0, 0, 0))
    return pl.pallas_call(
        _retention_kernel,
        grid=(nb, nc),
        in_specs=[col(0), col(1), col(2), col(3), state, tab, tab,
                  _const_spec((RET_H, q, q)), _const_spec((RET_H, q, LANE)), _const_spec((RET_H, q, LANE)),
                  _const_spec((RET_H, 8, LANE))],
        out_specs=[col(0), state],
        out_shape=[jax.ShapeDtypeStruct(o_shape, bf16),
                   jax.ShapeDtypeStruct((nb, RET_H, RET_DK, RET_DV), f32)],
        compiler_params=_cparams(("parallel", "arbitrary")),
        name="retention_t%d" % t,
    )(ret, ret, ret, ret, r0, cos, sin, dmask, bcast(xi), bcast(zeta),
      jnp.broadcast_to(g_chunk[:, None, None], (RET_H, 8, LANE)))


def _sort_key(score):
    bits = lax.bitcast_convert_type(score, i32)
    bits = jnp.where(bits == INT_MIN, 0, bits)
    return jnp.where(bits < 0, bits ^ 0x7FFFFFFF, bits)


def _count(key_ref, nblk, pred):
    def body(kb, acc):
        r0 = pl.multiple_of(kb * LANE, LANE)
        hit = jnp.where(pred(key_ref[pl.ds(r0, LANE), :], r0), 1, 0).astype(i32)
        return acc + jnp.sum(hit.reshape(LANE // 8, 8, LANE), axis=0)
    acc = lax.fori_loop(0, nblk, body, jnp.zeros((8, LANE), i32))
    return jnp.sum(acc, axis=0, keepdims=True)


def _select_topk(key_ref, nblk, topk, idx_bits):
    def bit_step(it, t):
        cand = t + jnp.left_shift(jnp.int32(1), 31 - it)
        cnt = _count(key_ref, nblk, lambda blk, r0: blk >= cand)
        return jnp.where(cnt >= topk, cand, t)

    t = lax.fori_loop(0, 32, bit_step, jnp.full((1, LANE), INT_MIN, i32))
    t = jnp.maximum(t, INT_MIN + 1)
    c_ge = _count(key_ref, nblk, lambda blk, r0: blk >= t)
    c_gt = _count(key_ref, nblk, lambda blk, r0: blk > t)
    surplus = c_ge > topk
    rows = lax.broadcasted_iota(i32, (LANE, LANE), 0)

    @pl.when(jnp.max(jnp.where(surplus, 1, 0)) > 0)
    def _():
        want = jnp.where(surplus, topk - c_gt, jnp.int32(2 ** 30))

        def idx_step(it, x):
            cand = x + jnp.left_shift(jnp.int32(1), idx_bits - 1 - it)
            cnt = _count(key_ref, nblk, lambda blk, r0: jnp.where(blk == t, rows + r0, cand) < cand)
            return jnp.where(cnt < want, cand, x)

        last = lax.fori_loop(0, idx_bits, idx_step, jnp.zeros((1, LANE), i32))

        def demote(kb, carry):
            r0 = pl.multiple_of(kb * LANE, LANE)
            blk = key_ref[pl.ds(r0, LANE), :]
            drop = jnp.where(blk == t, rows + r0, last) > last
            key_ref[pl.ds(r0, LANE), :] = jnp.where(drop, INT_MIN, blk)
            return carry

        lax.fori_loop(0, nblk, demote, 0)

    return t


def _dsa_prompt_kernel(qa_ref, qi_ref, kiwiq_ref, k_ref, vt_ref, kiwik_ref, bias_ref, o_ref, key_ref,
                       *, topk, idx_bits, nqb):
    qb = pl.program_id(1)
    nkb = qb + 1
    wit = jnp.transpose(kiwiq_ref[...])[IDX_DIM:IDX_DIM + IDX_H, :] * IDX_H ** -0.5
    qpos = qb * LANE + lax.broadcasted_iota(i32, (LANE, LANE), 1)
    rows = lax.broadcasted_iota(i32, (LANE, LANE), 0)

    def score_blk(kb, carry):
        r0 = pl.multiple_of(kb * LANE, LANE)
        kic = kiwik_ref[pl.ds(r0, LANE), 0:IDX_DIM].astype(bf16)
        acc = jnp.zeros((LANE, LANE), f32)
        for h in range(IDX_H):
            s = _mm_nt(kic, qi_ref[:, h * IDX_DIM:(h + 1) * IDX_DIM])
            acc = acc + jnp.maximum(s, 0.0) * wit[h:h + 1, :]
        key_ref[pl.ds(r0, LANE), :] = jnp.where(rows + r0 <= qpos, _sort_key(acc), INT_MIN)
        return carry

    lax.fori_loop(0, nkb, score_blk, 0)
    t = _select_topk(key_ref, nkb, topk, idx_bits)

    for h in range(ATT_H):
        sl = slice(h * ATT_DH, (h + 1) * ATT_DH)
        qh = qa_ref[:, sl]

        def att_blk(kb, carry):
            m, l, acc = carry
            r0 = pl.multiple_of(kb * LANE, LANE)
            s = _mm_nt(k_ref[pl.ds(r0, LANE), sl], qh) * ATT_DH ** -0.5 + bias_ref[jnp.minimum(qb - kb, 2), h]
            sel = key_ref[pl.ds(r0, LANE), :] >= t
            s = jnp.where(sel, s, NEG_BIG)
            m_new = jnp.maximum(m, jnp.max(s, axis=0, keepdims=True))
            alpha = jnp.exp(m - m_new)
            p = jnp.where(sel, jnp.exp(s - m_new), 0.0)
            l = l * alpha + jnp.sum(p, axis=0, keepdims=True)
            acc = acc * alpha + _mm(vt_ref[h * nqb + kb], p.astype(bf16))
            return m_new, l, acc

        init = (jnp.full((1, LANE), NEG_BIG, f32), jnp.zeros((1, LANE), f32), jnp.zeros((ATT_DH, LANE), f32))
        m, l, acc = lax.fori_loop(0, nkb, att_blk, init)
        o_ref[:, sl] = jnp.transpose(acc / l).astype(bf16)


def _bias_tiles(rel_bias, nd):
    j = jnp.arange(LANE)[:, None]
    i = jnp.arange(LANE)[None, :]
    rel = jnp.arange(nd)[:, None, None] * LANE + (i - j)[None]
    return jnp.moveaxis(rel_bias.astype(f32)[_t5_bucket(rel)], -1, 1)


def _dsa_prompt(qa, kab, vab, qi, kiwi, rel_bias, nb, t):
    nqb = t // LANE
    topk = min(TOPK_MAX, t // 4)
    assert REL_MAX_DIST <= LANE + 1
    bias = _bias_tiles(rel_bias, 3)
    vt = vab[:nb * t].reshape(nb, nqb, LANE, ATT_H, ATT_DH).transpose(0, 3, 1, 4, 2)
    vt = vt.reshape(nb * ATT_H * nqb, ATT_DH, LANE)
    qrow = lambda c: pl.BlockSpec((LANE, c), lambda b, q: (b * nqb + q, 0))
    seq = lambda c: pl.BlockSpec((t, c), lambda b, q: (b, 0))
    kern = functools.partial(_dsa_prompt_kernel, topk=topk, idx_bits=max(1, (t - 1).bit_length()), nqb=nqb)
    return pl.pallas_call(
        kern,
        grid=(nb, nqb),
        in_specs=[qrow(512), qrow(512), qrow(LANE), seq(512),
                  pl.BlockSpec((ATT_H * nqb, ATT_DH, LANE), lambda b, q: (b, 0, 0)), seq(LANE),
                  _const_spec((3, ATT_H, LANE, LANE))],
        out_specs=qrow(512),
        out_shape=jax.ShapeDtypeStruct((nb * t, 512), bf16),
        scratch_shapes=[pltpu.VMEM((t, LANE), i32)],
        compiler_params=_cparams(("parallel", "arbitrary")),
        name="dsa_prompt",
    )(qa, qi, kiwi, kab, vt, kiwi, bias)


def _softplus(x):
    return jnp.maximum(x, 0.0) + jnp.log1p(jnp.exp(-jnp.abs(x)))


def _cumsum_rows(tri, a):
    hi = a.astype(bf16)
    r1 = a - hi.astype(f32)
    mid = r1.astype(bf16)
    lo = (r1 - mid.astype(f32)).astype(bf16)
    return _mm(tri, hi) + _mm(tri, mid) + _mm(tri, lo)


def _odd_mixer_kernel(u_ref, z_ref, xbc_ref, dt_ref, pp_ref, cp_ref, h0_ref, pw_ref, ps_ref, cw_ref, cb_ref,
                      dtb_ref, alog_ref, dsk_ref, nrm_ref, tri_ref, po_ref, y_ref, h_ref, ubuf, xbuf, ybuf,
                      *, q, pos0):
    c = pl.program_id(1)

    @pl.when(c == 0)
    def _():
        ubuf[0:1, :] = jnp.zeros((1, POOL_DIM), f32)
        ubuf[1:16, :] = pp_ref[0]
        xbuf[0:8 - (CONV_W - 1), :] = jnp.zeros((8 - (CONV_W - 1), CONV_DIM), f32)
        xbuf[8 - (CONV_W - 1):8, :] = cp_ref[0]
        h_ref[...] = h0_ref[...]

    pos = pos0 + c * q + lax.broadcasted_iota(i32, (q, LANE), 0)
    causal = lax.broadcasted_iota(i32, (q, q), 0) >= lax.broadcasted_iota(i32, (q, q), 1)

    u = u_ref[...]
    ubuf[16:16 + q, :] = u
    for g, w in enumerate(POOL_WINDOWS):
        sl = slice(g * POOL_GC, (g + 1) * POOL_GC)
        acc = u[:, sl]
        for k in range(1, w):
            acc = acc + ubuf[16 - k:16 - k + q, sl]
        d = acc / jnp.minimum(pos + 1, w).astype(f32) - u[:, sl]
        po_ref[:, sl] = (_mm(d.astype(bf16), pw_ref[g]) * ps_ref[:, sl]).astype(bf16)
    ubuf[0:16, :] = ubuf[q:q + 16, :]

    xbuf[8:8 + q, :] = xbc_ref[...]
    conv = cb_ref[...]
    for j in range(CONV_W):
        off = 8 - (CONV_W - 1) + j
        conv = conv + xbuf[off:off + q, :] * cw_ref[j:j + 1, :]
    xbuf[0:8, :] = xbuf[q:q + 8, :]
    act = _silu(conv)
    xs = act[:, 0:D_INNER]

    dt = _softplus(dt_ref[...] + dtb_ref[...])
    a = dt * (-jnp.exp(alog_ref[...]))
    cs = _cumsum_rows(tri_ref[...], a)
    cs_t = jnp.transpose(cs)
    dt_t = jnp.transpose(dt)
    cs_last = cs[q - 1:q, :]
    w_end = jnp.exp(cs_last - cs) * dt
    ecs = jnp.exp(cs)
    hpg = SSM_H // SSM_G
    for g in range(SSM_G):
        bm = act[:, D_INNER + g * SSM_N:D_INNER + (g + 1) * SSM_N].astype(bf16)
        cm = act[:, D_INNER + (SSM_G + g) * SSM_N:D_INNER + (SSM_G + g + 1) * SSM_N].astype(bf16)
        cb = _mm_nt(cm, bm)
        for hh in range(hpg):
            h = g * hpg + hh
            psl = slice(h * SSM_P, (h + 1) * SSM_P)
            seg = cs[:, h:h + 1] - cs_t[h:h + 1, :]
            lm = jnp.exp(jnp.where(causal, seg, NEG_BIG))
            sc = cb * lm * dt_t[h:h + 1, :]
            xh = xs[:, psl]
            hs = h_ref[0, h]
            yh = _mm(sc.astype(bf16), xh.astype(bf16)) + _mm_nt(cm, hs.astype(bf16)) * ecs[:, h:h + 1]
            h_ref[0, h] = hs * jnp.exp(cs_last[:, h:h + 1]) + _mm_tn((xh * w_end[:, h:h + 1]).astype(bf16), bm)
            ybuf[:, psl] = yh
    y = (ybuf[...] + dsk_ref[...] * xs) * _silu(z_ref[...])
    gw = D_INNER // SSM_G
    for g in range(SSM_G):
        sl = slice(g * gw, (g + 1) * gw)
        yg = y[:, sl]
        y_ref[:, sl] = (yg * lax.rsqrt(jnp.mean(yg * yg, -1, keepdims=True) + EPS) * nrm_ref[:, sl]).astype(bf16)


def _pad_lanes(v):
    return jnp.pad(v.astype(f32), (0, LANE - v.shape[0]))[None, :]


def _odd_mixer(u, z, xbc, dt, pool_prev, conv_prev, h0, prm, pos0, nb, t):
    pool_w, pool_scale, conv_w, conv_b, dt_bias, a_log, d_skip, ssm_norm = prm
    q = math.gcd(t, SSD_CHUNK)
    nc = t // q
    if u.ndim == 2:
        row = lambda c: pl.BlockSpec((q, c), lambda b, i: (b * nc + i, 0))
        lead = (nb * t,)
    else:
        assert nc == 1
        row = lambda c: pl.BlockSpec((None, q, c), lambda b, i: (b, 0, 0))
        lead = (nb, t)
    per_seq = lambda *s: pl.BlockSpec((1,) + s, lambda b, i: (b,) + (0,) * len(s))
    tri = jnp.tril(jnp.ones((q, q), bf16))
    return pl.pallas_call(
        functools.partial(_odd_mixer_kernel, q=q, pos0=pos0),
        grid=(nb, nc),
        in_specs=[row(POOL_DIM), row(D_INNER), row(CONV_DIM), row(LANE),
                  per_seq(POOL_PAST, POOL_DIM), per_seq(CONV_W - 1, CONV_DIM), per_seq(SSM_H, SSM_P, SSM_N),
                  _const_spec((POOL_GROUPS, POOL_GC, POOL_GC)), _const_spec((1, POOL_DIM)),
                  _const_spec((CONV_W, CONV_DIM)), _const_spec((1, CONV_DIM)),
                  _const_spec((1, LANE)), _const_spec((1, LANE)),
                  _const_spec((1, D_INNER)), _const_spec((1, D_INNER)), _const_spec((q, q))],
        out_specs=[row(POOL_DIM), row(D_INNER), per_seq(SSM_H, SSM_P, SSM_N)],
        out_shape=[jax.ShapeDtypeStruct(lead + (POOL_DIM,), bf16), jax.ShapeDtypeStruct(lead + (D_INNER,), bf16),
                   jax.ShapeDtypeStruct((nb, SSM_H, SSM_P, SSM_N), f32)],
        scratch_shapes=[pltpu.VMEM((q + 16, POOL_DIM), f32), pltpu.VMEM((q + 8, CONV_DIM), f32),
                        pltpu.VMEM((q, D_INNER), f32)],
        compiler_params=_cparams(("parallel", "arbitrary")),
        name="odd_mixer_t%d" % t,
    )(u, z, xbc, dt, pool_prev, conv_prev, h0, pool_w.astype(bf16), pool_scale[None, :], conv_w, conv_b[None, :],
      _pad_lanes(dt_bias), _pad_lanes(a_log), jnp.repeat(d_skip, SSM_P)[None, :], ssm_norm[None, :], tri)


def _dsa_sample_score_kernel(pt_ref, qi_ref, kiwi_ref, *rest, npg):
    pages, o_ref = rest[:npg], rest[npg]
    ts = qi_ref.shape[0]
    kiwi = kiwi_ref[...]
    wi = kiwi[:, IDX_DIM:IDX_DIM + IDX_H] * IDX_H ** -0.5
    ki_new = jnp.concatenate([kiwi[:, 0:IDX_DIM], jnp.zeros((PAGE - ts, IDX_DIM), f32)], axis=0)
    ki = jnp.concatenate([p[...] for p in pages] + [ki_new], axis=0).astype(bf16)
    nk = ki.shape[0]
    acc = jnp.zeros((ts, nk), f32)
    for h in range(IDX_H):
        s = _mm_nt(qi_ref[:, h * IDX_DIM:(h + 1) * IDX_DIM], ki)
        acc = acc + jnp.maximum(s, 0.0) * wi[:, h:h + 1]
    col = lax.broadcasted_iota(i32, (ts, nk), 1)
    row = lax.broadcasted_iota(i32, (ts, nk), 0)
    o_ref[...] = jnp.where(col <= npg * PAGE + row, _sort_key(acc), INT_MIN)


def _select_kernel(k_ref, o_ref, key_ref, *, topk, idx_bits):
    key_ref[...] = k_ref[...]
    t = _select_topk(key_ref, key_ref.shape[0] // LANE, topk, idx_bits)
    o_ref[...] = jnp.where(key_ref[...] >= t, 1.0, 0.0)


def _dsa_sample_attn_kernel(pt_ref, q_ref, kn_ref, vn_ref, mb_ref, *rest, npg):
    kpages, vpages, o_ref = rest[:npg], rest[npg:2 * npg], rest[2 * npg]
    pad = jnp.zeros((PAGE * ATT_H - kn_ref.shape[0], ATT_DH), f32)
    kx = jnp.concatenate([p[...] for p in kpages] + [kn_ref[...], pad], axis=0).astype(bf16)
    vx = jnp.concatenate([p[...] for p in vpages] + [vn_ref[...], pad], axis=0).astype(bf16)
    s = _mm_nt(q_ref[...], kx) * ATT_DH ** -0.5 + mb_ref[...]
    p = jnp.exp(s - jnp.max(s, axis=-1, keepdims=True))
    o_ref[...] = (_mm(p.astype(bf16), vx) / jnp.sum(p, axis=-1, keepdims=True)).astype(bf16)


def _dsa_sample(qa, ka, va, qi, kiwi, cache_k, cache_v, cache_ki, layer, page_table, rel_bias):
    db, ts = qa.shape[:2]
    npg = page_table.shape[1]
    n_past = npg * PAGE
    nk = n_past + PAGE
    nq = db * ts
    topk = min(TOPK_MAX, (n_past + ts) // 4)
    hd = ATT_H * ATT_DH
    ki_page = lambda j: pl.BlockSpec((None, None, PAGE, IDX_DIM), lambda b, pt: (layer, pt[b, j], 0, 0))
    seq = lambda r, c: pl.BlockSpec((None, r, c), lambda b, pt: (b, 0, 0))

    keys = pl.pallas_call(
        functools.partial(_dsa_sample_score_kernel, npg=npg),
        grid_spec=pltpu.PrefetchScalarGridSpec(
            num_scalar_prefetch=1, grid=(db,),
            in_specs=[seq(ts, IDX_H * IDX_DIM), seq(ts, LANE)] + [ki_page(j) for j in range(npg)],
            out_specs=seq(ts, nk)),
        out_shape=jax.ShapeDtypeStruct((db, ts, nk), i32),
        compiler_params=_cparams(("parallel",)),
        name="dsa_sample_score",
    )(page_table, qi, kiwi, *([cache_ki] * npg))

    col = pl.BlockSpec((nk, LANE), lambda i: (0, i))
    sel = pl.pallas_call(
        functools.partial(_select_kernel, topk=topk, idx_bits=max(1, (nk - 1).bit_length())),
        grid=(nq // LANE,),
        in_specs=[col],
        out_specs=col,
        out_shape=jax.ShapeDtypeStruct((nk, nq), f32),
        scratch_shapes=[pltpu.VMEM((nk, LANE), i32)],
        compiler_params=_cparams(("parallel",)),
        name="dsa_sample_select",
    )(keys.reshape(nq, nk).T)
    sel = jnp.repeat(sel.T.reshape(db, ts, 1, nk), ATT_H, axis=-1) > 0.5
    rel = n_past + jnp.arange(ts)[:, None] - jnp.arange(nk)[None, :]
    bias = jnp.repeat(jnp.moveaxis(rel_bias.astype(f32)[_t5_bucket(rel)], -1, 1), ATT_H, axis=-1)
    same_head = jnp.arange(nk * ATT_H)[None, :] % ATT_H == jnp.arange(ATT_H)[:, None]
    mb = jnp.where(sel & same_head[None, None], bias[None], NEG_BIG).reshape(db, ts * ATT_H, nk * ATT_H)

    n_pages = cache_k.shape[1]
    rows = PAGE * ATT_H
    kv_page = lambda j: pl.BlockSpec((rows, ATT_DH), lambda b, pt: (layer * n_pages + pt[b, j], 0))
    as_rows = lambda a: a.reshape(db, ts * ATT_H, ATT_DH)
    out = pl.pallas_call(
        functools.partial(_dsa_sample_attn_kernel, npg=npg),
        grid_spec=pltpu.PrefetchScalarGridSpec(
            num_scalar_prefetch=1, grid=(db,),
            in_specs=[seq(ts * ATT_H, ATT_DH)] * 3 + [seq(ts * ATT_H, nk * ATT_H)]
                     + [kv_page(j) for j in range(npg)] * 2,
            out_specs=seq(ts * ATT_H, ATT_DH)),
        out_shape=jax.ShapeDtypeStruct((db, ts * ATT_H, ATT_DH), bf16),
        compiler_params=_cparams(("parallel",)),
        name="dsa_sample_attn",
    )(page_table, as_rows(qa), as_rows(ka), as_rows(va), mb,
      *([cache_k.reshape(-1, ATT_DH)] * npg), *([cache_v.reshape(-1, ATT_DH)] * npg))
    return out.reshape(db, ts, hd)


def kernel(x_prompt, x_sample, cache_k, cache_v, cache_kidx, state_ret, state_pool, state_conv, state_ssm,
           page_table, norm_mix, norm_ffn, w_in_even, w_out_even, q_norm, k_norm, rel_bias,
           w_in_odd, w_out_odd, pool_w, pool_scale, conv_w, conv_b, dt_bias, a_log, d_skip, ssm_norm,
           w_gate, w_up, w_down):
    bp, sp, d = x_prompt.shape
    db, ts, _ = x_sample.shape
    n_p, n_s = bp * sp, db * ts
    assert d == D_MODEL and (n_p + n_s) % TM == 0 and sp % LANE == 0
    n_past = page_table.shape[1] * PAGE
    hd = ATT_H * ATT_DH
    x = jnp.concatenate([x_prompt.reshape(n_p, d), x_sample.reshape(n_s, d)], axis=0)
    prompt = lambda a, *s: a[:n_p].reshape(bp, sp, *s)
    sample = lambda a, *s: a[n_p:].reshape(db, ts, *s)
    both = lambda p, s: jnp.concatenate([p.reshape(n_p, -1), s.reshape(n_s, -1)], axis=0)
    outs = [[] for _ in range(14)]
    for l in range(DEPTH):
        if l % 2 == 0:
            i = l // 2
            w_in = jnp.pad(w_in_even[i], ((0, 0), (0, EVEN_PROJ_PAD - EVEN_PROJ))).astype(bf16)
            ret, qa, ka, kab, va, vab, qi, kiwi = _even_proj(
                x, norm_mix[l][None], w_in, q_norm[i][None], k_norm[i][None])
            ret_p, rstate_p = _retention(ret, jnp.zeros((bp, RET_H, RET_DK, RET_DV), f32), 0, bp, sp)
            ret_s, rstate_s = _retention(sample(ret, 4 * 512), state_ret[i], n_past, db, ts)
            att_p = _dsa_prompt(qa, kab, vab, qi, kiwi, rel_bias, bp, sp)
            att_s = _dsa_sample(sample(qa, hd), sample(ka, hd), sample(va, hd), sample(qi, IDX_H * IDX_DIM),
                                sample(kiwi, LANE), cache_k, cache_v, cache_kidx, i, page_table, rel_bias)
            mix_a, mix_b = both(ret_p, ret_s), both(att_p, att_s)
            w_out = w_out_even[i].astype(bf16)
            new = [prompt(ka, ATT_H, ATT_DH), prompt(va, ATT_H, ATT_DH), prompt(kiwi, LANE)[..., :IDX_DIM], rstate_p,
                   None, None, None,
                   sample(ka, ATT_H, ATT_DH), sample(va, ATT_H, ATT_DH), sample(kiwi, LANE)[..., :IDX_DIM], rstate_s,
                   None, None, None]
        else:
            j = l // 2
            w_in = jnp.pad(w_in_odd[j], ((0, 0), (0, ODD_PROJ_PAD - ODD_PROJ))).astype(bf16)
            u, z, xbc, dt = _odd_proj(x, norm_mix[l][None], w_in)
            prm = (pool_w[j], pool_scale[j], conv_w[j], conv_b[j], dt_bias[j], a_log[j], d_skip[j], ssm_norm[j])
            po_p, y_p, h_p = _odd_mixer(u, z, xbc, dt, jnp.zeros((bp, POOL_PAST, POOL_DIM), f32),
                                        jnp.zeros((bp, CONV_W - 1, CONV_DIM), f32),
                                        jnp.zeros((bp, SSM_H, SSM_P, SSM_N), f32), prm, 0, bp, sp)
            u_s, xbc_s = sample(u, POOL_DIM), sample(xbc, CONV_DIM)
            po_s, y_s, h_s = _odd_mixer(u_s, sample(z, D_INNER), xbc_s, sample(dt, LANE),
                                        state_pool[j], state_conv[j], state_ssm[j], prm, n_past, db, ts)
            mix_a, mix_b = both(po_p, po_s), both(y_p, y_s)
            w_out = w_out_odd[j].astype(bf16)
            tail = lambda prev, cur, n: jnp.concatenate([prev.astype(f32), cur], axis=1)[:, -n:]
            new = [None, None, None, None,
                   prompt(u, POOL_DIM)[:, -POOL_PAST:], prompt(xbc, CONV_DIM)[:, -(CONV_W - 1):], h_p,
                   None, None, None, None,
                   tail(state_pool[j], u_s, POOL_PAST), tail(state_conv[j], xbc_s, CONV_W - 1), h_s]
        for acc, leaf in zip(outs, new):
            if leaf is not None:
                acc.append(leaf)
        x = _out_ffn(x, mix_a, mix_b, w_out, norm_ffn[l][None], w_gate[l].astype(bf16), w_up[l].astype(bf16),
                     w_down[l].astype(bf16))
    return (x[:n_p].reshape(bp, sp, d), x[n_p:].reshape(db, ts, d)) + tuple(jnp.stack(a) for a in outs)
```

```python
import functools
import math

import jax
import jax.numpy as jnp
import numpy as np
from jax import lax
from jax.experimental import pallas as pl
from jax.experimental.pallas import tpu as pltpu

f32 = jnp.float32
bf16 = jnp.bfloat16
i32 = jnp.int32

D_MODEL = 1024
DEPTH = 4
PAGE = 128
RET_H, RET_DK, RET_DV, RET_CHUNK = 4, 128, 128, 128
ROPE_BASE = 10000.0
ATT_H, ATT_DH, ATT_BLOCK = 4, 128, 128
IDX_H, IDX_DIM = 8, 64
TOPK_MAX = 256
REL_BUCKETS, REL_MAX_DIST = 32, 128
POOL_WINDOWS = (2, 4, 8, 16)
POOL_GROUPS = 4
POOL_DIM = D_MODEL // 2
POOL_GC = POOL_DIM // POOL_GROUPS
POOL_PAST = 15
D_INNER = D_MODEL // 2
SSM_P = 64
SSM_H = D_INNER // SSM_P
SSM_G = 2
SSM_N = 128
CONV_W = 4
CONV_DIM = D_INNER + 2 * SSM_G * SSM_N
SSD_CHUNK = 128
FF_DIM = -(-8 * D_MODEL // (3 * 256)) * 256
EPS = 1e-6

EVEN_PROJ = 4 * 512 + 3 * 512 + 512 + IDX_DIM + IDX_H
EVEN_PROJ_PAD = 4224
ODD_PROJ = POOL_DIM + D_INNER + CONV_DIM + SSM_H
ODD_PROJ_PAD = 2176

LANE = 128
INT_MIN = -(2 ** 31)
NEG_BIG = -1e30
VMEM_LIMIT = 56 * 1024 * 1024
TM = 256
FF_SPLIT = 2
DSA_CHUNK = 512


def _cparams(sem):
    return pltpu.CompilerParams(dimension_semantics=sem, vmem_limit_bytes=VMEM_LIMIT)


def _mm(a, b):
    return jnp.dot(a, b, preferred_element_type=f32)


def _mm_nt(a, b):
    return lax.dot_general(a, b, (((1,), (1,)), ((), ())), preferred_element_type=f32)


def _mm_tn(a, b):
    return lax.dot_general(a, b, (((0,), (0,)), ((), ())), preferred_element_type=f32)


def _rms(x, g):
    return x * lax.rsqrt(jnp.mean(x * x, -1, keepdims=True) + EPS) * g


def _silu(x):
    return x / (1.0 + jnp.exp(-x))


def _const_spec(shape):
    nd = len(shape)
    return pl.BlockSpec(shape, lambda *a: (0,) * nd)


def _even_proj_kernel(x_ref, g_ref, w_ref, qg_ref, kg_ref,
                      ret_ref, qa_ref, ka_ref, kab_ref, va_ref, vab_ref, qi_ref, kiwi_ref):
    xb = _rms(x_ref[...], g_ref[...]).astype(bf16)
    ret_ref[...] = _mm(xb, w_ref[:, 0:2048])
    qa = _mm(xb, w_ref[:, 2048:2560])
    ka = _mm(xb, w_ref[:, 2560:3072])
    for h in range(ATT_H):
        sl = slice(h * ATT_DH, (h + 1) * ATT_DH)
        qa_ref[:, sl] = _rms(qa[:, sl], qg_ref[...]).astype(bf16)
        kn = _rms(ka[:, sl], kg_ref[...])
        ka_ref[:, sl] = kn
        kab_ref[:, sl] = kn.astype(bf16)
    va = _mm(xb, w_ref[:, 3072:3584])
    va_ref[...] = va
    vab_ref[...] = va.astype(bf16)
    qi_ref[...] = (_mm(xb, w_ref[:, 3584:4096]) * IDX_DIM ** -0.5).astype(bf16)
    kiwi_ref[...] = _mm(xb, w_ref[:, 4096:4224])


def _even_proj(x, g, w, qg, kg):
    n = x.shape[0]
    row = lambda c: pl.BlockSpec((TM, c), lambda i: (i, 0))
    outs = [(2048, f32), (512, bf16), (512, f32), (512, bf16), (512, f32), (512, bf16), (512, bf16), (LANE, f32)]
    return pl.pallas_call(
        _even_proj_kernel,
        grid=(n // TM,),
        in_specs=[row(D_MODEL), _const_spec((1, D_MODEL)), _const_spec((D_MODEL, EVEN_PROJ_PAD)),
                  _const_spec((1, ATT_DH)), _const_spec((1, ATT_DH))],
        out_specs=[row(c) for c, _ in outs],
        out_shape=[jax.ShapeDtypeStruct((n, c), dt) for c, dt in outs],
        compiler_params=_cparams(("parallel",)),
        name="even_proj",
    )(x, g, w, qg, kg)


def _odd_proj_kernel(x_ref, g_ref, w_ref, u_ref, z_ref, xbc_ref, dt_ref):
    xb = _rms(x_ref[...], g_ref[...]).astype(bf16)
    u_ref[...] = _mm(xb, w_ref[:, 0:512])
    z_ref[...] = _mm(xb, w_ref[:, 512:1024])
    xbc_ref[...] = _mm(xb, w_ref[:, 1024:2048])
    dt_ref[...] = _mm(xb, w_ref[:, 2048:2176])


def _odd_proj(x, g, w):
    n = x.shape[0]
    row = lambda c: pl.BlockSpec((TM, c), lambda i: (i, 0))
    outs = [512, 512, 1024, LANE]
    return pl.pallas_call(
        _odd_proj_kernel,
        grid=(n // TM,),
        in_specs=[row(D_MODEL), _const_spec((1, D_MODEL)), _const_spec((D_MODEL, ODD_PROJ_PAD))],
        out_specs=[row(c) for c in outs],
        out_shape=[jax.ShapeDtypeStruct((n, c), f32) for c in outs],
        compiler_params=_cparams(("parallel",)),
        name="odd_proj",
    )(x, g, w)


def _out_ffn_kernel(x_ref, a_ref, b_ref, wo_ref, g_ref, wg_ref, wu_ref, wd_ref, o_ref):
    half = wo_ref.shape[0] // 2
    x = x_ref[...] + _mm(a_ref[...], wo_ref[0:half, :]) + _mm(b_ref[...], wo_ref[half:, :])
    hb = _rms(x, g_ref[...]).astype(bf16)
    fc = FF_DIM // FF_SPLIT
    ff = None
    for c in range(FF_SPLIT):
        sl = slice(c * fc, (c + 1) * fc)
        act = (_silu(_mm(hb, wg_ref[:, sl])) * _mm(hb, wu_ref[:, sl])).astype(bf16)
        down = _mm(act, wd_ref[sl, :])
        ff = down if ff is None else ff + down
    o_ref[...] = x + ff


def _out_ffn(x, a, b, wo, g, wg, wu, wd):
    n = x.shape[0]
    row = lambda c: pl.BlockSpec((TM, c), lambda i: (i, 0))
    once = lambda shape: pl.BlockSpec(shape, lambda i: (0, 0), pipeline_mode=pl.Buffered(1))
    return pl.pallas_call(
        _out_ffn_kernel,
        grid=(n // TM,),
        in_specs=[row(D_MODEL), row(a.shape[1]), row(b.shape[1]), once(wo.shape), _const_spec((1, D_MODEL)),
                  once(wg.shape), once(wu.shape), once(wd.shape)],
        out_specs=row(D_MODEL),
        out_shape=jax.ShapeDtypeStruct((n, D_MODEL), f32),
        compiler_params=_cparams(("parallel",)),
        name="out_ffn",
    )(x, a, b, wo, g, wg, wu, wd)


def _rope_tables(pos):
    half = RET_DK // 2
    inv = ROPE_BASE ** (-jnp.linspace(0.0, 1.0, half, dtype=f32))
    ang = pos.astype(f32)[:, None] * inv[None, :]
    cos, sin = jnp.cos(ang), jnp.sin(ang)
    return jnp.concatenate([cos, cos], -1), jnp.concatenate([-sin, sin], -1)


def _ret_decay(q):
    log_g = jnp.log1p(-jnp.exp2(-5.0 - jnp.arange(RET_H, dtype=f32)))
    idx = jnp.arange(q, dtype=f32)
    diff = idx[:, None] - idx[None, :]
    dmask = jnp.where(diff[None] >= 0, jnp.exp(log_g[:, None, None] * jnp.maximum(diff, 0.0)[None]), 0.0)
    xi = jnp.exp(log_g[:, None] * (idx + 1.0)[None])
    zeta = jnp.exp(log_g[:, None] * (q - 1.0 - idx)[None])
    g_chunk = jnp.exp(log_g * q)
    return dmask, xi, zeta, g_chunk


def _t5_bucket(rel):
    n = jnp.maximum(rel, 0)
    max_exact = REL_BUCKETS // 2
    nf = jnp.maximum(n, 1).astype(f32)
    large = max_exact + (jnp.log(nf / max_exact) / math.log(REL_MAX_DIST / max_exact)
                         * (REL_BUCKETS - max_exact)).astype(i32)
    large = jnp.minimum(large, REL_BUCKETS - 1)
    return jnp.where(n < max_exact, n, large)


def _bias_lookup(rel_bias, rel):
    onehot = jax.nn.one_hot(_t5_bucket(rel), REL_BUCKETS, dtype=f32)
    return jnp.einsum("...b,bh->...h", onehot, rel_bias.astype(f32), precision=lax.Precision.HIGHEST)


def _rotary(x, c, s):
    return x * c + pltpu.roll(x, RET_DK // 2, 1) * s


def _retention_kernel(q_ref, k_ref, v_ref, g_ref, r0_ref, c_ref, s_ref, dm_ref, xi_ref, zt_ref, gc_ref,
                      o_ref, r_ref):
    @pl.when(pl.program_id(1) == 0)
    def _():
        r_ref[...] = r0_ref[...]

    cos, sin = c_ref[...], s_ref[...]
    for h in range(RET_H):
        sl = slice(h * RET_DK, (h + 1) * RET_DK)
        qr = _rotary(q_ref[:, sl], cos, sin).astype(bf16)
        kr = _rotary(k_ref[:, sl], cos, sin) * RET_DK ** -0.5
        vb = v_ref[:, sl].astype(bf16)
        r = r_ref[0, h]
        s = _mm_nt(qr, kr.astype(bf16)) * dm_ref[h]
        o = _mm(s.astype(bf16), vb) + _mm(qr, r.astype(bf16)) * xi_ref[h]
        r_ref[0, h] = r * gc_ref[h, 0:1, :] + _mm_tn((kr * zt_ref[h]).astype(bf16), vb)
        o = o * lax.rsqrt(jnp.mean(o * o, -1, keepdims=True) + EPS)
        o_ref[:, sl] = (_silu(g_ref[:, sl]) * o).astype(bf16)


def _retention(ret, r0, pos0, nb, t):
    q = math.gcd(t, RET_CHUNK)
    nc = t // q
    cos, sin = _rope_tables(pos0 + jnp.arange(t))
    dmask, xi, zeta, g_chunk = _ret_decay(q)
    bcast = lambda a: jnp.broadcast_to(a[:, :, None], (RET_H, a.shape[1], LANE))
    if ret.ndim == 2:
        col = lambda j: pl.BlockSpec((q, 512), lambda b, c: (b * nc + c, j))
        o_shape = (nb * t, 512)
    else:
        assert nc == 1
        col = lambda j: pl.BlockSpec((None, q, 512), lambda b, c: (b, 0, j))
        o_shape = (nb, t, 512)
    tab = pl.BlockSpec((q, LANE), lambda b, c: (c, 0))
    state = pl.BlockSpec((1, RET_H, RET_DK, RET_DV), lambda b, c: (b, 0, 0, 0))
    return pl.pallas_call(
        _retention_kernel,
        grid=(nb, nc),
        in_specs=[col(0), col(1), col(2), col(3), state, tab, tab,
                  _const_spec((RET_H, q, q)), _const_spec((RET_H, q, LANE)), _const_spec((RET_H, q, LANE)),
                  _const_spec((RET_H, 8, LANE))],
        out_specs=[col(0), state],
        out_shape=[jax.ShapeDtypeStruct(o_shape, bf16),
                   jax.ShapeDtypeStruct((nb, RET_H, RET_DK, RET_DV), f32)],
        compiler_params=_cparams(("parallel", "arbitrary")),
        name="retention_t%d" % t,
    )(ret, ret, ret, ret, r0, cos, sin, dmask, bcast(xi), bcast(zeta),
      jnp.broadcast_to(g_chunk[:, None, None], (RET_H, 8, LANE)))


def _sort_key(score):
    bits = lax.bitcast_convert_type(score, i32)
    bits = jnp.where(bits == INT_MIN, 0, bits)
    return jnp.where(bits < 0, bits ^ 0x7FFFFFFF, bits)


def _count(key_ref, nchunk, ck, pred):
    def body(c, acc):
        r0 = pl.multiple_of(c * ck, ck)
        hit = jnp.where(pred(key_ref[pl.ds(r0, ck), :], r0), 1, 0).astype(i32)
        return acc + jnp.sum(hit.reshape(ck // 8, 8, LANE), axis=0)
    acc = lax.fori_loop(0, nchunk, body, jnp.zeros((8, LANE), i32))
    return jnp.sum(acc, axis=0, keepdims=True)


def _select_topk(key_ref, nchunk, ck, topk, idx_bits):
    def bit_step(it, t):
        cand = t + jnp.left_shift(jnp.int32(1), 31 - it)
        cnt = _count(key_ref, nchunk, ck, lambda blk, r0: blk >= cand)
        return jnp.where(cnt >= topk, cand, t)

    t = lax.fori_loop(0, 32, bit_step, jnp.full((1, LANE), INT_MIN, i32))
    t = jnp.maximum(t, INT_MIN + 1)
    c_ge = _count(key_ref, nchunk, ck, lambda blk, r0: blk >= t)
    c_gt = _count(key_ref, nchunk, ck, lambda blk, r0: blk > t)
    surplus = c_ge > topk
    rows = lax.broadcasted_iota(i32, (ck, LANE), 0)

    @pl.when(jnp.max(jnp.where(surplus, 1, 0)) > 0)
    def _():
        want = jnp.where(surplus, topk - c_gt, jnp.int32(2 ** 30))

        def idx_step(it, x):
            cand = x + jnp.left_shift(jnp.int32(1), idx_bits - 1 - it)
            cnt = _count(key_ref, nchunk, ck, lambda blk, r0: jnp.where(blk == t, rows + r0, cand) < cand)
            return jnp.where(cnt < want, cand, x)

        last = lax.fori_loop(0, idx_bits, idx_step, jnp.zeros((1, LANE), i32))

        def demote(c, carry):
            r0 = pl.multiple_of(c * ck, ck)
            blk = key_ref[pl.ds(r0, ck), :]
            drop = jnp.where(blk == t, rows + r0, last) > last
            key_ref[pl.ds(r0, ck), :] = jnp.where(drop, INT_MIN, blk)
            return carry

        lax.fori_loop(0, nchunk, demote, 0)

    return t


def _dsa_prompt_kernel(qa_ref, qit_ref, kiwiq_ref, k_ref, vt_ref, kiwik_ref, bias_ref, o_ref, key_ref, acc_ref,
                       *, topk, idx_bits, nch, ck):
    qb = pl.program_id(1)
    cb = ck // LANE
    nchunk = (qb + cb) // cb
    wit = jnp.transpose(kiwiq_ref[...])[IDX_DIM:IDX_DIM + IDX_H, :] * IDX_H ** -0.5
    qpos = qb * LANE + lax.broadcasted_iota(i32, (ck, LANE), 1)
    rows = lax.broadcasted_iota(i32, (ck, LANE), 0)

    def score_chunk(c, carry):
        r0 = pl.multiple_of(c * ck, ck)
        kic = kiwik_ref[pl.ds(r0, ck), 0:IDX_DIM].astype(bf16)
        acc = jnp.zeros((ck, LANE), f32)
        for h in range(IDX_H):
            s = _mm(kic, qit_ref[:, h * LANE:(h + 1) * LANE])
            acc = acc + jnp.maximum(s, 0.0) * wit[h:h + 1, :]
        key_ref[pl.ds(r0, ck), :] = jnp.where(rows + r0 <= qpos, _sort_key(acc), INT_MIN)
        return carry

    lax.fori_loop(0, nchunk, score_chunk, 0)
    t = _select_topk(key_ref, nchunk, ck, topk, idx_bits)

    acc_ref[...] = jnp.zeros_like(acc_ref)

    def att_chunk(c, carry):
        ms, ls = carry
        r0 = pl.multiple_of(c * ck, ck)
        negm = jnp.where(key_ref[pl.ds(r0, ck), :] >= t, 0.0, NEG_BIG)
        new_ms, new_ls = [], []
        for h in range(ATT_H):
            sl = slice(h * ATT_DH, (h + 1) * ATT_DH)
            bias = jnp.concatenate([bias_ref[jnp.clip(qb - (c * cb + j), 0, 2), h] for j in range(cb)], axis=0)
            s = _mm_nt(k_ref[pl.ds(r0, ck), sl], qa_ref[:, sl]) * ATT_DH ** -0.5 + bias + negm
            m_new = jnp.maximum(ms[h], jnp.max(s, axis=0, keepdims=True))
            alpha = jnp.exp(ms[h] - m_new)
            p = jnp.exp(s - m_new)
            new_ls.append(ls[h] * alpha + jnp.sum(p, axis=0, keepdims=True))
            acc_ref[h] = acc_ref[h] * alpha + _mm(vt_ref[h * nch + c], p.astype(bf16))
            new_ms.append(m_new)
        return tuple(new_ms), tuple(new_ls)

    init = ((jnp.full((1, LANE), NEG_BIG, f32),) * ATT_H, (jnp.zeros((1, LANE), f32),) * ATT_H)
    _, ls = lax.fori_loop(0, nchunk, att_chunk, init)
    for h in range(ATT_H):
        o_ref[:, h * ATT_DH:(h + 1) * ATT_DH] = jnp.transpose(acc_ref[h] / ls[h]).astype(bf16)


def _bias_tiles(rel_bias, nd):
    j = jnp.arange(LANE)[:, None]
    i = jnp.arange(LANE)[None, :]
    rel = jnp.arange(nd)[:, None, None] * LANE + (i - j)[None]
    return jnp.moveaxis(_bias_lookup(rel_bias, rel), -1, 1)


def _dsa_prompt(qa, kab, vab, qi, kiwi, rel_bias, nb, t):
    nqb = t // LANE
    topk = min(TOPK_MAX, t // 4)
    assert REL_MAX_DIST <= LANE + 1
    bias = _bias_tiles(rel_bias, 3)
    ck = math.gcd(t, DSA_CHUNK)
    nch = t // ck
    vt = vab[:nb * t].reshape(nb, nch, ck, ATT_H, ATT_DH).transpose(0, 3, 1, 4, 2)
    vt = vt.reshape(nb * ATT_H * nch, ATT_DH, ck)
    qit = qi[:nb * t].reshape(nb * nqb, LANE, IDX_H, IDX_DIM).transpose(0, 3, 2, 1)
    qit = qit.reshape(nb * nqb, IDX_DIM, IDX_H * LANE)
    qrow = lambda c: pl.BlockSpec((LANE, c), lambda b, q: (b * nqb + q, 0))
    seq = lambda c: pl.BlockSpec((t, c), lambda b, q: (b, 0))
    kern = functools.partial(_dsa_prompt_kernel, topk=topk, idx_bits=max(1, (t - 1).bit_length()), nch=nch, ck=ck)
    return pl.pallas_call(
        kern,
        grid=(nb, nqb),
        in_specs=[qrow(512), pl.BlockSpec((None, IDX_DIM, IDX_H * LANE), lambda b, q: (b * nqb + q, 0, 0)),
                  qrow(LANE), seq(512),
                  pl.BlockSpec((ATT_H * nch, ATT_DH, ck), lambda b, q: (b, 0, 0)), seq(LANE),
                  _const_spec((3, ATT_H, LANE, LANE))],
        out_specs=qrow(512),
        out_shape=jax.ShapeDtypeStruct((nb * t, 512), bf16),
        scratch_shapes=[pltpu.VMEM((t, LANE), i32), pltpu.VMEM((ATT_H, ATT_DH, LANE), f32)],
        compiler_params=_cparams(("parallel", "arbitrary")),
        name="dsa_prompt",
    )(qa, qit, kiwi, kab, vt, kiwi, bias)


def _softplus(x):
    return jnp.maximum(x, 0.0) + jnp.log1p(jnp.exp(-jnp.abs(x)))


def _cumsum_rows(tri, a):
    hi = a.astype(bf16)
    r1 = a - hi.astype(f32)
    mid = r1.astype(bf16)
    lo = (r1 - mid.astype(f32)).astype(bf16)
    return _mm(tri, hi) + _mm(tri, mid) + _mm(tri, lo)


def _odd_mixer_kernel(u_ref, z_ref, xbc_ref, dt_ref, pp_ref, cp_ref, h0_ref, pw_ref, ps_ref, cw_ref, cb_ref,
                      dtb_ref, alog_ref, dsk_ref, nrm_ref, tri_ref, po_ref, y_ref, h_ref, ubuf, xbuf, ybuf,
                      *, q, pos0):
    c = pl.program_id(1)

    @pl.when(c == 0)
    def _():
        ubuf[0:1, :] = jnp.zeros((1, POOL_DIM), f32)
        ubuf[1:16, :] = pp_ref[0]
        xbuf[0:8 - (CONV_W - 1), :] = jnp.zeros((8 - (CONV_W - 1), CONV_DIM), f32)
        xbuf[8 - (CONV_W - 1):8, :] = cp_ref[0]
        h_ref[...] = h0_ref[...]

    pos = pos0 + c * q + lax.broadcasted_iota(i32, (q, LANE), 0)
    causal = lax.broadcasted_iota(i32, (q, q), 0) >= lax.broadcasted_iota(i32, (q, q), 1)

    u = u_ref[...]
    ubuf[16:16 + q, :] = u
    for g, w in enumerate(POOL_WINDOWS):
        sl = slice(g * POOL_GC, (g + 1) * POOL_GC)
        acc = u[:, sl]
        for k in range(1, w):
            acc = acc + ubuf[16 - k:16 - k + q, sl]
        d = acc / jnp.minimum(pos + 1, w).astype(f32) - u[:, sl]
        po_ref[:, sl] = (_mm(d.astype(bf16), pw_ref[g]) * ps_ref[:, sl]).astype(bf16)
    ubuf[0:16, :] = ubuf[q:q + 16, :]

    xbuf[8:8 + q, :] = xbc_ref[...]
    conv = cb_ref[...]
    for j in range(CONV_W):
        off = 8 - (CONV_W - 1) + j
        conv = conv + xbuf[off:off + q, :] * cw_ref[j:j + 1, :]
    xbuf[0:8, :] = xbuf[q:q + 8, :]
    act = _silu(conv)
    xs = act[:, 0:D_INNER]

    dt = _softplus(dt_ref[...] + dtb_ref[...])
    a = dt * (-jnp.exp(alog_ref[...]))
    cs = _cumsum_rows(tri_ref[...], a)
    cs_t = jnp.transpose(cs)
    dt_t = jnp.transpose(dt)
    cs_last = cs[q - 1:q, :]
    w_end = jnp.exp(cs_last - cs) * dt
    ecs = jnp.exp(cs)
    hpg = SSM_H // SSM_G
    for g in range(SSM_G):
        bm = act[:, D_INNER + g * SSM_N:D_INNER + (g + 1) * SSM_N].astype(bf16)
        cm = act[:, D_INNER + (SSM_G + g) * SSM_N:D_INNER + (SSM_G + g + 1) * SSM_N].astype(bf16)
        cb = _mm_nt(cm, bm)
        for hh in range(hpg):
            h = g * hpg + hh
            psl = slice(h * SSM_P, (h + 1) * SSM_P)
            seg = cs[:, h:h + 1] - cs_t[h:h + 1, :]
            lm = jnp.exp(jnp.where(causal, seg, NEG_BIG))
            sc = cb * lm * dt_t[h:h + 1, :]
            xh = xs[:, psl]
            hs = h_ref[0, h]
            yh = _mm(sc.astype(bf16), xh.astype(bf16)) + _mm_nt(cm, hs.astype(bf16)) * ecs[:, h:h + 1]
            h_ref[0, h] = hs * jnp.exp(cs_last[:, h:h + 1]) + _mm_tn((xh * w_end[:, h:h + 1]).astype(bf16), bm)
            ybuf[:, psl] = yh
    y = (ybuf[...] + dsk_ref[...] * xs) * _silu(z_ref[...])
    gw = D_INNER // SSM_G
    for g in range(SSM_G):
        sl = slice(g * gw, (g + 1) * gw)
        yg = y[:, sl]
        y_ref[:, sl] = (yg * lax.rsqrt(jnp.mean(yg * yg, -1, keepdims=True) + EPS) * nrm_ref[:, sl]).astype(bf16)


def _pad_lanes(v):
    return jnp.pad(v.astype(f32), (0, LANE - v.shape[0]))[None, :]


def _odd_mixer(u, z, xbc, dt, pool_prev, conv_prev, h0, prm, pos0, nb, t):
    pool_w, pool_scale, conv_w, conv_b, dt_bias, a_log, d_skip, ssm_norm = prm
    q = math.gcd(t, SSD_CHUNK)
    nc = t // q
    if u.ndim == 2:
        row = lambda c: pl.BlockSpec((q, c), lambda b, i: (b * nc + i, 0))
        lead = (nb * t,)
    else:
        assert nc == 1
        row = lambda c: pl.BlockSpec((None, q, c), lambda b, i: (b, 0, 0))
        lead = (nb, t)
    per_seq = lambda *s: pl.BlockSpec((1,) + s, lambda b, i: (b,) + (0,) * len(s))
    tri = jnp.tril(jnp.ones((q, q), bf16))
    return pl.pallas_call(
        functools.partial(_odd_mixer_kernel, q=q, pos0=pos0),
        grid=(nb, nc),
        in_specs=[row(POOL_DIM), row(D_INNER), row(CONV_DIM), row(LANE),
                  per_seq(POOL_PAST, POOL_DIM), per_seq(CONV_W - 1, CONV_DIM), per_seq(SSM_H, SSM_P, SSM_N),
                  _const_spec((POOL_GROUPS, POOL_GC, POOL_GC)), _const_spec((1, POOL_DIM)),
                  _const_spec((CONV_W, CONV_DIM)), _const_spec((1, CONV_DIM)),
                  _const_spec((1, LANE)), _const_spec((1, LANE)),
                  _const_spec((1, D_INNER)), _const_spec((1, D_INNER)), _const_spec((q, q))],
        out_specs=[row(POOL_DIM), row(D_INNER), per_seq(SSM_H, SSM_P, SSM_N)],
        out_shape=[jax.ShapeDtypeStruct(lead + (POOL_DIM,), bf16), jax.ShapeDtypeStruct(lead + (D_INNER,), bf16),
                   jax.ShapeDtypeStruct((nb, SSM_H, SSM_P, SSM_N), f32)],
        scratch_shapes=[pltpu.VMEM((q + 16, POOL_DIM), f32), pltpu.VMEM((q + 8, CONV_DIM), f32),
                        pltpu.VMEM((q, D_INNER), f32)],
        compiler_params=_cparams(("parallel", "arbitrary")),
        name="odd_mixer_t%d" % t,
    )(u, z, xbc, dt, pool_prev, conv_prev, h0, pool_w.astype(bf16), pool_scale[None, :], conv_w, conv_b[None, :],
      _pad_lanes(dt_bias), _pad_lanes(a_log), jnp.repeat(d_skip, SSM_P)[None, :], ssm_norm[None, :], tri)


def _dsa_sample_score_kernel(pt_ref, qi_ref, kiwi_ref, *rest, npg):
    pages, o_ref = rest[:npg], rest[npg]
    ts = qi_ref.shape[0]
    kiwi = kiwi_ref[...]
    wi = kiwi[:, IDX_DIM:IDX_DIM + IDX_H] * IDX_H ** -0.5
    ki_new = jnp.concatenate([kiwi[:, 0:IDX_DIM], jnp.zeros((PAGE - ts, IDX_DIM), f32)], axis=0)
    ki = jnp.concatenate([p[...] for p in pages] + [ki_new], axis=0).astype(bf16)
    nk = ki.shape[0]
    acc = jnp.zeros((ts, nk), f32)
    for h in range(IDX_H):
        s = _mm_nt(qi_ref[:, h * IDX_DIM:(h + 1) * IDX_DIM], ki)
        acc = acc + jnp.maximum(s, 0.0) * wi[:, h:h + 1]
    col = lax.broadcasted_iota(i32, (ts, nk), 1)
    row = lax.broadcasted_iota(i32, (ts, nk), 0)
    o_ref[...] = jnp.where(col <= npg * PAGE + row, _sort_key(acc), INT_MIN)


def _select_kernel(k_ref, o_ref, key_ref, *, topk, idx_bits):
    key_ref[...] = k_ref[...]
    t = _select_topk(key_ref, 1, key_ref.shape[0], topk, idx_bits)
    o_ref[...] = jnp.where(key_ref[...] >= t, 1.0, 0.0)


def _dsa_sample_attn_kernel(pt_ref, q_ref, kn_ref, vn_ref, mb_ref, *rest, npg):
    kpages, vpages, o_ref = rest[:npg], rest[npg:2 * npg], rest[2 * npg]
    pad = jnp.zeros((PAGE * ATT_H - kn_ref.shape[0], ATT_DH), f32)
    kx = jnp.concatenate([p[...] for p in kpages] + [kn_ref[...], pad], axis=0).astype(bf16)
    vx = jnp.concatenate([p[...] for p in vpages] + [vn_ref[...], pad], axis=0).astype(bf16)
    s = _mm_nt(q_ref[...], kx) * ATT_DH ** -0.5 + mb_ref[...]
    p = jnp.exp(s - jnp.max(s, axis=-1, keepdims=True))
    o_ref[...] = (_mm(p.astype(bf16), vx) / jnp.sum(p, axis=-1, keepdims=True)).astype(bf16)


def _dsa_sample(qa, ka, va, qi, kiwi, cache_k, cache_v, cache_ki, layer, page_table, rel_bias):
    db, ts = qa.shape[:2]
    npg = page_table.shape[1]
    n_past = npg * PAGE
    nk = n_past + PAGE
    nq = db * ts
    topk = min(TOPK_MAX, (n_past + ts) // 4)
    hd = ATT_H * ATT_DH
    ki_page = lambda j: pl.BlockSpec((None, None, PAGE, IDX_DIM), lambda b, pt: (layer, pt[b, j], 0, 0))
    seq = lambda r, c: pl.BlockSpec((None, r, c), lambda b, pt: (b, 0, 0))

    keys = pl.pallas_call(
        functools.partial(_dsa_sample_score_kernel, npg=npg),
        grid_spec=pltpu.PrefetchScalarGridSpec(
            num_scalar_prefetch=1, grid=(db,),
            in_specs=[seq(ts, IDX_H * IDX_DIM), seq(ts, LANE)] + [ki_page(j) for j in range(npg)],
            out_specs=seq(ts, nk)),
        out_shape=jax.ShapeDtypeStruct((db, ts, nk), i32),
        compiler_params=_cparams(("parallel",)),
        name="dsa_sample_score",
    )(page_table, qi, kiwi, *([cache_ki] * npg))

    col = pl.BlockSpec((nk, LANE), lambda i: (0, i))
    sel = pl.pallas_call(
        functools.partial(_select_kernel, topk=topk, idx_bits=max(1, (nk - 1).bit_length())),
        grid=(nq // LANE,),
        in_specs=[col],
        out_specs=col,
        out_shape=jax.ShapeDtypeStruct((nk, nq), f32),
        scratch_shapes=[pltpu.VMEM((nk, LANE), i32)],
        compiler_params=_cparams(("parallel",)),
        name="dsa_sample_select",
    )(keys.reshape(nq, nk).T)
    sel = jnp.repeat(sel.T.reshape(db, ts, 1, nk), ATT_H, axis=-1) > 0.5
    rel = n_past + jnp.arange(ts)[:, None] - jnp.arange(nk)[None, :]
    bias = jnp.repeat(jnp.moveaxis(_bias_lookup(rel_bias, rel), -1, 1), ATT_H, axis=-1)
    same_head = jnp.arange(nk * ATT_H)[None, :] % ATT_H == jnp.arange(ATT_H)[:, None]
    mb = jnp.where(sel & same_head[None, None], bias[None], NEG_BIG).reshape(db, ts * ATT_H, nk * ATT_H)

    n_pages = cache_k.shape[1]
    rows = PAGE * ATT_H
    kv_page = lambda j: pl.BlockSpec((rows, ATT_DH), lambda b, pt: (layer * n_pages + pt[b, j], 0))
    as_rows = lambda a: a.reshape(db, ts * ATT_H, ATT_DH)
    out = pl.pallas_call(
        functools.partial(_dsa_sample_attn_kernel, npg=npg),
        grid_spec=pltpu.PrefetchScalarGridSpec(
            num_scalar_prefetch=1, grid=(db,),
            in_specs=[seq(ts * ATT_H, ATT_DH)] * 3 + [seq(ts * ATT_H, nk * ATT_H)]
                     + [kv_page(j) for j in range(npg)] * 2,
            out_specs=seq(ts * ATT_H, ATT_DH)),
        out_shape=jax.ShapeDtypeStruct((db, ts * ATT_H, ATT_DH), bf16),
        compiler_params=_cparams(("parallel",)),
        name="dsa_sample_attn",
    )(page_table, as_rows(qa), as_rows(ka), as_rows(va), mb,
      *([cache_k.reshape(-1, ATT_DH)] * npg), *([cache_v.reshape(-1, ATT_DH)] * npg))
    return out.reshape(db, ts, hd)


def kernel(x_prompt, x_sample, cache_k, cache_v, cache_kidx, state_ret, state_pool, state_conv, state_ssm,
           page_table, norm_mix, norm_ffn, w_in_even, w_out_even, q_norm, k_norm, rel_bias,
           w_in_odd, w_out_odd, pool_w, pool_scale, conv_w, conv_b, dt_bias, a_log, d_skip, ssm_norm,
           w_gate, w_up, w_down):
    bp, sp, d = x_prompt.shape
    db, ts, _ = x_sample.shape
    n_p, n_s = bp * sp, db * ts
    assert d == D_MODEL and (n_p + n_s) % TM == 0 and sp % LANE == 0
    n_past = page_table.shape[1] * PAGE
    hd = ATT_H * ATT_DH
    x = jnp.concatenate([x_prompt.reshape(n_p, d), x_sample.reshape(n_s, d)], axis=0)
    prompt = lambda a, *s: a[:n_p].reshape(bp, sp, *s)
    sample = lambda a, *s: a[n_p:].reshape(db, ts, *s)
    both = lambda p, s: jnp.concatenate([p.reshape(n_p, -1), s.reshape(n_s, -1)], axis=0)
    outs = [[] for _ in range(14)]
    for l in range(DEPTH):
        if l % 2 == 0:
            i = l // 2
            w_in = jnp.pad(w_in_even[i], ((0, 0), (0, EVEN_PROJ_PAD - EVEN_PROJ))).astype(bf16)
            ret, qa, ka, kab, va, vab, qi, kiwi = _even_proj(
                x, norm_mix[l][None], w_in, q_norm[i][None], k_norm[i][None])
            ret_p, rstate_p = _retention(ret, jnp.zeros((bp, RET_H, RET_DK, RET_DV), f32), 0, bp, sp)
            ret_s, rstate_s = _retention(sample(ret, 4 * 512), state_ret[i], n_past, db, ts)
            att_p = _dsa_prompt(qa, kab, vab, qi, kiwi, rel_bias, bp, sp)
            att_s = _dsa_sample(sample(qa, hd), sample(ka, hd), sample(va, hd), sample(qi, IDX_H * IDX_DIM),
                                sample(kiwi, LANE), cache_k, cache_v, cache_kidx, i, page_table, rel_bias)
            mix_a, mix_b = both(ret_p, ret_s), both(att_p, att_s)
            w_out = w_out_even[i].astype(bf16)
            new = [prompt(ka, ATT_H, ATT_DH), prompt(va, ATT_H, ATT_DH), prompt(kiwi, LANE)[..., :IDX_DIM], rstate_p,
                   None, None, None,
                   sample(ka, ATT_H, ATT_DH), sample(va, ATT_H, ATT_DH), sample(kiwi, LANE)[..., :IDX_DIM], rstate_s,
                   None, None, None]
        else:
            j = l // 2
            w_in = jnp.pad(w_in_odd[j], ((0, 0), (0, ODD_PROJ_PAD - ODD_PROJ))).astype(bf16)
            u, z, xbc, dt = _odd_proj(x, norm_mix[l][None], w_in)
            prm = (pool_w[j], pool_scale[j], conv_w[j], conv_b[j], dt_bias[j], a_log[j], d_skip[j], ssm_norm[j])
            po_p, y_p, h_p = _odd_mixer(u, z, xbc, dt, jnp.zeros((bp, POOL_PAST, POOL_DIM), f32),
                                        jnp.zeros((bp, CONV_W - 1, CONV_DIM), f32),
                                        jnp.zeros((bp, SSM_H, SSM_P, SSM_N), f32), prm, 0, bp, sp)
            u_s, xbc_s = sample(u, POOL_DIM), sample(xbc, CONV_DIM)
            po_s, y_s, h_s = _odd_mixer(u_s, sample(z, D_INNER), xbc_s, sample(dt, LANE),
                                        state_pool[j], state_conv[j], state_ssm[j], prm, n_past, db, ts)
            mix_a, mix_b = both(po_p, po_s), both(y_p, y_s)
            w_out = w_out_odd[j].astype(bf16)
            tail = lambda prev, cur, n: jnp.concatenate([prev.astype(f32), cur], axis=1)[:, -n:]
            new = [None, None, None, None,
                   prompt(u, POOL_DIM)[:, -POOL_PAST:], prompt(xbc, CONV_DIM)[:, -(CONV_W - 1):], h_p,
                   None, None, None, None,
                   tail(state_pool[j], u_s, POOL_PAST), tail(state_conv[j], xbc_s, CONV_W - 1), h_s]
        for acc, leaf in zip(outs, new):
            if leaf is not None:
                acc.append(leaf)
        x = _out_ffn(x, mix_a, mix_b, w_out, norm_ffn[l][None], w_gate[l].astype(bf16), w_up[l].astype(bf16),
                     w_down[l].astype(bf16))
    return (x[:n_p].reshape(bp, sp, d), x[n_p:].reshape(db, ts, d)) + tuple(jnp.stack(a) for a in outs)
```

```python
import functools
import math

import jax
import jax.numpy as jnp
import numpy as np
from jax import lax
from jax.experimental import pallas as pl
from jax.experimental.pallas import tpu as pltpu

f32 = jnp.float32
bf16 = jnp.bfloat16
i32 = jnp.int32

D_MODEL = 1024
DEPTH = 4
PAGE = 128
RET_H, RET_DK, RET_DV, RET_CHUNK = 4, 128, 128, 128
ROPE_BASE = 10000.0
ATT_H, ATT_DH, ATT_BLOCK = 4, 128, 128
IDX_H, IDX_DIM = 8, 64
TOPK_MAX = 256
REL_BUCKETS, REL_MAX_DIST = 32, 128
POOL_WINDOWS = (2, 4, 8, 16)
POOL_GROUPS = 4
POOL_DIM = D_MODEL // 2
POOL_GC = POOL_DIM // POOL_GROUPS
POOL_PAST = 15
D_INNER = D_MODEL // 2
SSM_P = 64
SSM_H = D_INNER // SSM_P
SSM_G = 2
SSM_N = 128
CONV_W = 4
CONV_DIM = D_INNER + 2 * SSM_G * SSM_N
SSD_CHUNK = 128
FF_DIM = -(-8 * D_MODEL // (3 * 256)) * 256
EPS = 1e-6

EVEN_PROJ = 4 * 512 + 3 * 512 + 512 + IDX_DIM + IDX_H
EVEN_PROJ_PAD = 4224
ODD_PROJ = POOL_DIM + D_INNER + CONV_DIM + SSM_H
ODD_PROJ_PAD = 2176

LANE = 128
INT_MIN = -(2 ** 31)
NEG_BIG = -1e30
VMEM_LIMIT = 56 * 1024 * 1024
TM = 256
FF_SPLIT = 2
DSA_CHUNK = 512


def _cparams(sem):
    return pltpu.CompilerParams(dimension_semantics=sem, vmem_limit_bytes=VMEM_LIMIT)


def _mm(a, b):
    return jnp.dot(a, b, preferred_element_type=f32)


def _mm_nt(a, b):
    return lax.dot_general(a, b, (((1,), (1,)), ((), ())), preferred_element_type=f32)


def _mm_tn(a, b):
    return lax.dot_general(a, b, (((0,), (0,)), ((), ())), preferred_element_type=f32)


def _rms(x, g):
    return x * lax.rsqrt(jnp.mean(x * x, -1, keepdims=True) + EPS) * g


def _silu(x):
    return x / (1.0 + jnp.exp(-x))


def _const_spec(shape):
    nd = len(shape)
    return pl.BlockSpec(shape, lambda *a: (0,) * nd)


def _even_proj_kernel(x_ref, g_ref, w_ref, qg_ref, kg_ref,
                      ret_ref, qa_ref, ka_ref, kab_ref, va_ref, vab_ref, qi_ref, kiwi_ref):
    xb = _rms(x_ref[...], g_ref[...]).astype(bf16)
    ret_ref[...] = _mm(xb, w_ref[:, 0:2048])
    qa = _mm(xb, w_ref[:, 2048:2560])
    ka = _mm(xb, w_ref[:, 2560:3072])
    for h in range(ATT_H):
        sl = slice(h * ATT_DH, (h + 1) * ATT_DH)
        qa_ref[:, sl] = _rms(qa[:, sl], qg_ref[...]).astype(bf16)
        kn = _rms(ka[:, sl], kg_ref[...])
        ka_ref[:, sl] = kn
        kab_ref[:, sl] = kn.astype(bf16)
    va = _mm(xb, w_ref[:, 3072:3584])
    va_ref[...] = va
    vab_ref[...] = va.astype(bf16)
    qi_ref[...] = (_mm(xb, w_ref[:, 3584:4096]) * IDX_DIM ** -0.5).astype(bf16)
    kiwi_ref[...] = _mm(xb, w_ref[:, 4096:4224])


def _even_proj(x, g, w, qg, kg):
    n = x.shape[0]
    row = lambda c: pl.BlockSpec((TM, c), lambda i: (i, 0))
    outs = [(2048, f32), (512, bf16), (512, f32), (512, bf16), (512, f32), (512, bf16), (512, bf16), (LANE, f32)]
    return pl.pallas_call(
        _even_proj_kernel,
        grid=(n // TM,),
        in_specs=[row(D_MODEL), _const_spec((1, D_MODEL)), _const_spec((D_MODEL, EVEN_PROJ_PAD)),
                  _const_spec((1, ATT_DH)), _const_spec((1, ATT_DH))],
        out_specs=[row(c) for c, _ in outs],
        out_shape=[jax.ShapeDtypeStruct((n, c), dt) for c, dt in outs],
        compiler_params=_cparams(("parallel",)),
        name="even_proj",
    )(x, g, w, qg, kg)


def _odd_proj_kernel(x_ref, g_ref, w_ref, u_ref, z_ref, xbc_ref, dt_ref):
    xb = _rms(x_ref[...], g_ref[...]).astype(bf16)
    u_ref[...] = _mm(xb, w_ref[:, 0:512])
    z_ref[...] = _mm(xb, w_ref[:, 512:1024])
    xbc_ref[...] = _mm(xb, w_ref[:, 1024:2048])
    dt_ref[...] = _mm(xb, w_ref[:, 2048:2176])


def _odd_proj(x, g, w):
    n = x.shape[0]
    row = lambda c: pl.BlockSpec((TM, c), lambda i: (i, 0))
    outs = [512, 512, 1024, LANE]
    return pl.pallas_call(
        _odd_proj_kernel,
        grid=(n // TM,),
        in_specs=[row(D_MODEL), _const_spec((1, D_MODEL)), _const_spec((D_MODEL, ODD_PROJ_PAD))],
        out_specs=[row(c) for c in outs],
        out_shape=[jax.ShapeDtypeStruct((n, c), f32) for c in outs],
        compiler_params=_cparams(("parallel",)),
        name="odd_proj",
    )(x, g, w)


def _out_ffn_kernel(x_ref, a_ref, b_ref, wo_ref, g_ref, wg_ref, wu_ref, wd_ref, o_ref):
    half = wo_ref.shape[0] // 2
    x = x_ref[...] + _mm(a_ref[...], wo_ref[0:half, :]) + _mm(b_ref[...], wo_ref[half:, :])
    hb = _rms(x, g_ref[...]).astype(bf16)
    fc = FF_DIM // FF_SPLIT
    ff = None
    for c in range(FF_SPLIT):
        sl = slice(c * fc, (c + 1) * fc)
        act = (_silu(_mm(hb, wg_ref[:, sl])) * _mm(hb, wu_ref[:, sl])).astype(bf16)
        down = _mm(act, wd_ref[sl, :])
        ff = down if ff is None else ff + down
    o_ref[...] = x + ff


def _out_ffn(x, a, b, wo, g, wg, wu, wd):
    n = x.shape[0]
    row = lambda c: pl.BlockSpec((TM, c), lambda i: (i, 0))
    once = lambda shape: pl.BlockSpec(shape, lambda i: (0, 0), pipeline_mode=pl.Buffered(1))
    return pl.pallas_call(
        _out_ffn_kernel,
        grid=(n // TM,),
        in_specs=[row(D_MODEL), row(a.shape[1]), row(b.shape[1]), once(wo.shape), _const_spec((1, D_MODEL)),
                  once(wg.shape), once(wu.shape), once(wd.shape)],
        out_specs=row(D_MODEL),
        out_shape=jax.ShapeDtypeStruct((n, D_MODEL), f32),
        compiler_params=_cparams(("parallel",)),
        name="out_ffn",
    )(x, a, b, wo, g, wg, wu, wd)


def _rope_tables(pos):
    half = RET_DK // 2
    inv = ROPE_BASE ** (-jnp.linspace(0.0, 1.0, half, dtype=f32))
    ang = pos.astype(f32)[:, None] * inv[None, :]
    cos, sin = jnp.cos(ang), jnp.sin(ang)
    return jnp.concatenate([cos, cos], -1), jnp.concatenate([-sin, sin], -1)


def _ret_decay(q):
    log_g = jnp.log1p(-jnp.exp2(-5.0 - jnp.arange(RET_H, dtype=f32)))
    idx = jnp.arange(q, dtype=f32)
    diff = idx[:, None] - idx[None, :]
    dmask = jnp.where(diff[None] >= 0, jnp.exp(log_g[:, None, None] * jnp.maximum(diff, 0.0)[None]), 0.0)
    xi = jnp.exp(log_g[:, None] * (idx + 1.0)[None])
    zeta = jnp.exp(log_g[:, None] * (q - 1.0 - idx)[None])
    g_chunk = jnp.exp(log_g * q)
    return dmask, xi, zeta, g_chunk


def _t5_bucket(rel):
    n = jnp.maximum(rel, 0)
    max_exact = REL_BUCKETS // 2
    nf = jnp.maximum(n, 1).astype(f32)
    large = max_exact + (jnp.log(nf / max_exact) / math.log(REL_MAX_DIST / max_exact)
                         * (REL_BUCKETS - max_exact)).astype(i32)
    large = jnp.minimum(large, REL_BUCKETS - 1)
    return jnp.where(n < max_exact, n, large)


def _bias_lookup(rel_bias, rel):
    onehot = jax.nn.one_hot(_t5_bucket(rel), REL_BUCKETS, dtype=f32)
    return jnp.einsum("...b,bh->...h", onehot, rel_bias.astype(f32), precision=lax.Precision.HIGHEST)


def _rotary(x, c, s):
    return x * c + pltpu.roll(x, RET_DK // 2, 1) * s


def _retention_kernel(q_ref, k_ref, v_ref, g_ref, r0_ref, c_ref, s_ref, dm_ref, xi_ref, zt_ref, gc_ref,
                      *rest):
    o_ref, r_ref = rest[-2:]

    @pl.when(pl.program_id(1) == 0)
    def _():
        r_ref[...] = r0_ref[...]

    cos, sin = c_ref[...], s_ref[...]
    for h in range(RET_H):
        sl = slice(h * RET_DK, (h + 1) * RET_DK)
        qr = _rotary(q_ref[:, sl], cos, sin).astype(bf16)
        kr = _rotary(k_ref[:, sl], cos, sin) * RET_DK ** -0.5
        vb = v_ref[:, sl].astype(bf16)
        r = r_ref[0, h]
        s = _mm_nt(qr, kr.astype(bf16)) * dm_ref[h]
        o = _mm(s.astype(bf16), vb) + _mm(qr, r.astype(bf16)) * xi_ref[h]
        r_ref[0, h] = r * gc_ref[h, 0:1, :] + _mm_tn((kr * zt_ref[h]).astype(bf16), vb)
        o = o * lax.rsqrt(jnp.mean(o * o, -1, keepdims=True) + EPS)
        o_ref[:, sl] = (_silu(g_ref[:, sl]) * o).astype(bf16)


def _layer_block(tail, layer):
    return pl.BlockSpec((None, 1) + tail, lambda b, c: (layer, b) + (0,) * len(tail))


def _stacked_out(prev, n_layers, nb, tail, n_inputs, out_index):
    shape = jax.ShapeDtypeStruct((n_layers, nb) + tail, f32)
    if prev is None:
        return shape, [], [], {}
    return shape, [prev], [pl.BlockSpec(memory_space=pl.ANY)], {n_inputs: out_index}


def _retention(ret, r0, layer_in, pos0, nb, t, prev, layer_out, n_layers):
    q = math.gcd(t, RET_CHUNK)
    nc = t // q
    cos, sin = _rope_tables(pos0 + jnp.arange(t))
    dmask, xi, zeta, g_chunk = _ret_decay(q)
    bcast = lambda a: jnp.broadcast_to(a[:, :, None], (RET_H, a.shape[1], LANE))
    if ret.ndim == 2:
        col = lambda j: pl.BlockSpec((q, 512), lambda b, c: (b * nc + c, j))
        o_shape = (nb * t, 512)
    else:
        assert nc == 1
        col = lambda j: pl.BlockSpec((None, q, 512), lambda b, c: (b, 0, j))
        o_shape = (nb, t, 512)
    tab = pl.BlockSpec((q, LANE), lambda b, c: (c, 0))
    tail = (RET_H, RET_DK, RET_DV)
    in_specs = [col(0), col(1), col(2), col(3), _layer_block(tail, layer_in), tab, tab,
                _const_spec((RET_H, q, q)), _const_spec((RET_H, q, LANE)), _const_spec((RET_H, q, LANE)),
                _const_spec((RET_H, 8, LANE))]
    state_shape, extra, extra_specs, aliases = _stacked_out(prev, n_layers, nb, tail, len(in_specs), 1)
    return pl.pallas_call(
        _retention_kernel,
        grid=(nb, nc),
        in_specs=in_specs + extra_specs,
        out_specs=[col(0), _layer_block(tail, layer_out)],
        out_shape=[jax.ShapeDtypeStruct(o_shape, bf16), state_shape],
        input_output_aliases=aliases,
        compiler_params=_cparams(("parallel", "arbitrary")),
        name="retention_t%d" % t,
    )(ret, ret, ret, ret, r0, cos, sin, dmask, bcast(xi), bcast(zeta),
      jnp.broadcast_to(g_chunk[:, None, None], (RET_H, 8, LANE)), *extra)


def _sort_key(score):
    bits = lax.bitcast_convert_type(score, i32)
    bits = jnp.where(bits == INT_MIN, 0, bits)
    return jnp.where(bits < 0, bits ^ 0x7FFFFFFF, bits)


def _count(key_ref, nchunk, ck, pred):
    def body(c, acc):
        r0 = pl.multiple_of(c * ck, ck)
        hit = jnp.where(pred(key_ref[pl.ds(r0, ck), :], r0), 1, 0).astype(i32)
        return acc + jnp.sum(hit.reshape(ck // 8, 8, LANE), axis=0)
    acc = lax.fori_loop(0, nchunk, body, jnp.zeros((8, LANE), i32))
    return jnp.sum(acc, axis=0, keepdims=True)


def _select_topk(key_ref, nchunk, ck, topk, idx_bits):
    def bit_step(it, t):
        cand = t + jnp.left_shift(jnp.int32(1), 31 - it)
        cnt = _count(key_ref, nchunk, ck, lambda blk, r0: blk >= cand)
        return jnp.where(cnt >= topk, cand, t)

    t = lax.fori_loop(0, 32, bit_step, jnp.full((1, LANE), INT_MIN, i32))
    t = jnp.maximum(t, INT_MIN + 1)
    c_ge = _count(key_ref, nchunk, ck, lambda blk, r0: blk >= t)
    c_gt = _count(key_ref, nchunk, ck, lambda blk, r0: blk > t)
    surplus = c_ge > topk
    rows = lax.broadcasted_iota(i32, (ck, LANE), 0)

    @pl.when(jnp.max(jnp.where(surplus, 1, 0)) > 0)
    def _():
        want = jnp.where(surplus, topk - c_gt, jnp.int32(2 ** 30))

        def idx_step(it, x):
            cand = x + jnp.left_shift(jnp.int32(1), idx_bits - 1 - it)
            cnt = _count(key_ref, nchunk, ck, lambda blk, r0: jnp.where(blk == t, rows + r0, cand) < cand)
            return jnp.where(cnt < want, cand, x)

        last = lax.fori_loop(0, idx_bits, idx_step, jnp.zeros((1, LANE), i32))

        def demote(c, carry):
            r0 = pl.multiple_of(c * ck, ck)
            blk = key_ref[pl.ds(r0, ck), :]
            drop = jnp.where(blk == t, rows + r0, last) > last
            key_ref[pl.ds(r0, ck), :] = jnp.where(drop, INT_MIN, blk)
            return carry

        lax.fori_loop(0, nchunk, demote, 0)

    return t


def _dsa_prompt_kernel(qa_ref, qit_ref, kiwiq_ref, k_ref, vt_ref, kiwik_ref, bias_ref, o_ref, key_ref, acc_ref,
                       *, topk, idx_bits, nch, ck):
    qb = pl.program_id(1)
    cb = ck // LANE
    nchunk = (qb + cb) // cb
    wit = jnp.transpose(kiwiq_ref[...])[IDX_DIM:IDX_DIM + IDX_H, :] * IDX_H ** -0.5
    qpos = qb * LANE + lax.broadcasted_iota(i32, (ck, LANE), 1)
    rows = lax.broadcasted_iota(i32, (ck, LANE), 0)

    def score_chunk(c, carry):
        r0 = pl.multiple_of(c * ck, ck)
        kic = kiwik_ref[pl.ds(r0, ck), 0:IDX_DIM].astype(bf16)
        acc = jnp.zeros((ck, LANE), f32)
        for h in range(IDX_H):
            s = _mm(kic, qit_ref[:, h * LANE:(h + 1) * LANE])
            acc = acc + jnp.maximum(s, 0.0) * wit[h:h + 1, :]
        key_ref[pl.ds(r0, ck), :] = jnp.where(rows + r0 <= qpos, _sort_key(acc), INT_MIN)
        return carry

    lax.fori_loop(0, nchunk, score_chunk, 0)
    t = _select_topk(key_ref, nchunk, ck, topk, idx_bits)

    acc_ref[...] = jnp.zeros_like(acc_ref)

    def att_chunk(c, carry):
        ms, ls = carry
        r0 = pl.multiple_of(c * ck, ck)
        negm = jnp.where(key_ref[pl.ds(r0, ck), :] >= t, 0.0, NEG_BIG)
        new_ms, new_ls = [], []
        for h in range(ATT_H):
            sl = slice(h * ATT_DH, (h + 1) * ATT_DH)
            bias = jnp.concatenate([bias_ref[jnp.clip(qb - (c * cb + j), 0, 2), h] for j in range(cb)], axis=0)
            s = _mm_nt(k_ref[pl.ds(r0, ck), sl], qa_ref[:, sl]) * ATT_DH ** -0.5 + bias + negm
            m_new = jnp.maximum(ms[h], jnp.max(s, axis=0, keepdims=True))
            alpha = jnp.exp(ms[h] - m_new)
            p = jnp.exp(s - m_new)
            new_ls.append(ls[h] * alpha + jnp.sum(p, axis=0, keepdims=True))
            acc_ref[h] = acc_ref[h] * alpha + _mm(vt_ref[h * nch + c], p.astype(bf16))
            new_ms.append(m_new)
        return tuple(new_ms), tuple(new_ls)

    init = ((jnp.full((1, LANE), NEG_BIG, f32),) * ATT_H, (jnp.zeros((1, LANE), f32),) * ATT_H)
    _, ls = lax.fori_loop(0, nchunk, att_chunk, init)
    for h in range(ATT_H):
        o_ref[:, h * ATT_DH:(h + 1) * ATT_DH] = jnp.transpose(acc_ref[h] / ls[h]).astype(bf16)


def _bias_tiles(rel_bias, nd):
    j = jnp.arange(LANE)[:, None]
    i = jnp.arange(LANE)[None, :]
    rel = jnp.arange(nd)[:, None, None] * LANE + (i - j)[None]
    return jnp.moveaxis(_bias_lookup(rel_bias, rel), -1, 1)


def _dsa_prompt(qa, kab, vab, qi, kiwi, rel_bias, nb, t):
    nqb = t // LANE
    topk = min(TOPK_MAX, t // 4)
    assert REL_MAX_DIST <= LANE + 1
    bias = _bias_tiles(rel_bias, 3)
    ck = math.gcd(t, DSA_CHUNK)
    nch = t // ck
    vt = vab[:nb * t].reshape(nb, nch, ck, ATT_H, ATT_DH).transpose(0, 3, 1, 4, 2)
    vt = vt.reshape(nb * ATT_H * nch, ATT_DH, ck)
    qit = qi[:nb * t].reshape(nb * nqb, LANE, IDX_H, IDX_DIM).transpose(0, 3, 2, 1)
    qit = qit.reshape(nb * nqb, IDX_DIM, IDX_H * LANE)
    qrow = lambda c: pl.BlockSpec((LANE, c), lambda b, q: (b * nqb + q, 0))
    seq = lambda c: pl.BlockSpec((t, c), lambda b, q: (b, 0))
    kern = functools.partial(_dsa_prompt_kernel, topk=topk, idx_bits=max(1, (t - 1).bit_length()), nch=nch, ck=ck)
    return pl.pallas_call(
        kern,
        grid=(nb, nqb),
        in_specs=[qrow(512), pl.BlockSpec((None, IDX_DIM, IDX_H * LANE), lambda b, q: (b * nqb + q, 0, 0)),
                  qrow(LANE), seq(512),
                  pl.BlockSpec((ATT_H * nch, ATT_DH, ck), lambda b, q: (b, 0, 0)), seq(LANE),
                  _const_spec((3, ATT_H, LANE, LANE))],
        out_specs=qrow(512),
        out_shape=jax.ShapeDtypeStruct((nb * t, 512), bf16),
        scratch_shapes=[pltpu.VMEM((t, LANE), i32), pltpu.VMEM((ATT_H, ATT_DH, LANE), f32)],
        compiler_params=_cparams(("parallel", "arbitrary")),
        name="dsa_prompt",
    )(qa, qit, kiwi, kab, vt, kiwi, bias)


def _softplus(x):
    return jnp.maximum(x, 0.0) + jnp.log1p(jnp.exp(-jnp.abs(x)))


def _cumsum_rows(tri, a):
    hi = a.astype(bf16)
    r1 = a - hi.astype(f32)
    mid = r1.astype(bf16)
    lo = (r1 - mid.astype(f32)).astype(bf16)
    return _mm(tri, hi) + _mm(tri, mid) + _mm(tri, lo)


def _odd_mixer_kernel(u_ref, z_ref, xbc_ref, dt_ref, pp_ref, cp_ref, h0_ref, pw_ref, ps_ref, cw_ref, cb_ref,
                      dtb_ref, alog_ref, dsk_ref, nrm_ref, tri_ref, *rest, q, pos0):
    po_ref, y_ref, h_ref, ubuf, xbuf, ybuf = rest[-6:]
    c = pl.program_id(1)

    @pl.when(c == 0)
    def _():
        ubuf[0:1, :] = jnp.zeros((1, POOL_DIM), f32)
        ubuf[1:16, :] = pp_ref[0]
        xbuf[0:8 - (CONV_W - 1), :] = jnp.zeros((8 - (CONV_W - 1), CONV_DIM), f32)
        xbuf[8 - (CONV_W - 1):8, :] = cp_ref[0]
        h_ref[...] = h0_ref[...]

    pos = pos0 + c * q + lax.broadcasted_iota(i32, (q, LANE), 0)
    causal = lax.broadcasted_iota(i32, (q, q), 0) >= lax.broadcasted_iota(i32, (q, q), 1)

    u = u_ref[...]
    ubuf[16:16 + q, :] = u
    for g, w in enumerate(POOL_WINDOWS):
        sl = slice(g * POOL_GC, (g + 1) * POOL_GC)
        acc = u[:, sl]
        for k in range(1, w):
            acc = acc + ubuf[16 - k:16 - k + q, sl]
        d = acc / jnp.minimum(pos + 1, w).astype(f32) - u[:, sl]
        po_ref[:, sl] = (_mm(d.astype(bf16), pw_ref[g]) * ps_ref[:, sl]).astype(bf16)
    ubuf[0:16, :] = ubuf[q:q + 16, :]

    xbuf[8:8 + q, :] = xbc_ref[...]
    conv = cb_ref[...]
    for j in range(CONV_W):
        off = 8 - (CONV_W - 1) + j
        conv = conv + xbuf[off:off + q, :] * cw_ref[j:j + 1, :]
    xbuf[0:8, :] = xbuf[q:q + 8, :]
    act = _silu(conv)
    xs = act[:, 0:D_INNER]

    dt = _softplus(dt_ref[...] + dtb_ref[...])
    a = dt * (-jnp.exp(alog_ref[...]))
    cs = _cumsum_rows(tri_ref[...], a)
    cs_t = jnp.transpose(cs)
    dt_t = jnp.transpose(dt)
    cs_last = cs[q - 1:q, :]
    w_end = jnp.exp(cs_last - cs) * dt
    ecs = jnp.exp(cs)
    hpg = SSM_H // SSM_G
    for g in range(SSM_G):
        bm = act[:, D_INNER + g * SSM_N:D_INNER + (g + 1) * SSM_N].astype(bf16)
        cm = act[:, D_INNER + (SSM_G + g) * SSM_N:D_INNER + (SSM_G + g + 1) * SSM_N].astype(bf16)
        cb = _mm_nt(cm, bm)
        for hh in range(hpg):
            h = g * hpg + hh
            psl = slice(h * SSM_P, (h + 1) * SSM_P)
            seg = cs[:, h:h + 1] - cs_t[h:h + 1, :]
            lm = jnp.exp(jnp.where(causal, seg, NEG_BIG))
            sc = cb * lm * dt_t[h:h + 1, :]
            xh = xs[:, psl]
            hs = h_ref[0, h]
            yh = _mm(sc.astype(bf16), xh.astype(bf16)) + _mm_nt(cm, hs.astype(bf16)) * ecs[:, h:h + 1]
            h_ref[0, h] = hs * jnp.exp(cs_last[:, h:h + 1]) + _mm_tn((xh * w_end[:, h:h + 1]).astype(bf16), bm)
            ybuf[:, psl] = yh
    y = (ybuf[...] + dsk_ref[...] * xs) * _silu(z_ref[...])
    gw = D_INNER // SSM_G
    for g in range(SSM_G):
        sl = slice(g * gw, (g + 1) * gw)
        yg = y[:, sl]
        y_ref[:, sl] = (yg * lax.rsqrt(jnp.mean(yg * yg, -1, keepdims=True) + EPS) * nrm_ref[:, sl]).astype(bf16)


def _pad_lanes(v):
    return jnp.pad(v.astype(f32), (0, LANE - v.shape[0]))[None, :]


def _odd_mixer(u, z, xbc, dt, pool_prev, conv_prev, h0, layer_in, prm, pos0, nb, t, prev, layer_out, n_layers):
    pool_w, pool_scale, conv_w, conv_b, dt_bias, a_log, d_skip, ssm_norm = prm
    q = math.gcd(t, SSD_CHUNK)
    nc = t // q
    if u.ndim == 2:
        row = lambda c: pl.BlockSpec((q, c), lambda b, i: (b * nc + i, 0))
        lead = (nb * t,)
    else:
        assert nc == 1
        row = lambda c: pl.BlockSpec((None, q, c), lambda b, i: (b, 0, 0))
        lead = (nb, t)
    tri = jnp.tril(jnp.ones((q, q), bf16))
    tail = (SSM_H, SSM_P, SSM_N)
    in_specs = [row(POOL_DIM), row(D_INNER), row(CONV_DIM), row(LANE),
                _layer_block((POOL_PAST, POOL_DIM), layer_in), _layer_block((CONV_W - 1, CONV_DIM), layer_in),
                _layer_block(tail, layer_in),
                _const_spec((POOL_GROUPS, POOL_GC, POOL_GC)), _const_spec((1, POOL_DIM)),
                _const_spec((CONV_W, CONV_DIM)), _const_spec((1, CONV_DIM)),
                _const_spec((1, LANE)), _const_spec((1, LANE)),
                _const_spec((1, D_INNER)), _const_spec((1, D_INNER)), _const_spec((q, q))]
    state_shape, extra, extra_specs, aliases = _stacked_out(prev, n_layers, nb, tail, len(in_specs), 2)
    return pl.pallas_call(
        functools.partial(_odd_mixer_kernel, q=q, pos0=pos0),
        grid=(nb, nc),
        in_specs=in_specs + extra_specs,
        out_specs=[row(POOL_DIM), row(D_INNER), _layer_block(tail, layer_out)],
        out_shape=[jax.ShapeDtypeStruct(lead + (POOL_DIM,), bf16), jax.ShapeDtypeStruct(lead + (D_INNER,), bf16),
                   state_shape],
        input_output_aliases=aliases,
        scratch_shapes=[pltpu.VMEM((q + 16, POOL_DIM), f32), pltpu.VMEM((q + 8, CONV_DIM), f32),
                        pltpu.VMEM((q, D_INNER), f32)],
        compiler_params=_cparams(("parallel", "arbitrary")),
        name="odd_mixer_t%d" % t,
    )(u, z, xbc, dt, pool_prev, conv_prev, h0, pool_w.astype(bf16), pool_scale[None, :], conv_w, conv_b[None, :],
      _pad_lanes(dt_bias), _pad_lanes(a_log), jnp.repeat(d_skip, SSM_P)[None, :], ssm_norm[None, :], tri, *extra)


def _dsa_sample_score_kernel(pt_ref, qi_ref, kiwi_ref, *rest, npg):
    pages, o_ref = rest[:npg], rest[npg]
    ts = qi_ref.shape[0]
    kiwi = kiwi_ref[...]
    wi = kiwi[:, IDX_DIM:IDX_DIM + IDX_H] * IDX_H ** -0.5
    ki_new = jnp.transpose(jnp.concatenate([kiwi, jnp.zeros((PAGE - ts, LANE), f32)], axis=0))[0:IDX_DIM, :]
    ki = jnp.concatenate([p[...] for p in pages] + [ki_new], axis=1).astype(bf16)
    nk = ki.shape[1]
    acc = jnp.zeros((ts, nk), f32)
    for h in range(IDX_H):
        s = _mm(qi_ref[:, h * IDX_DIM:(h + 1) * IDX_DIM], ki)
        acc = acc + jnp.maximum(s, 0.0) * wi[:, h:h + 1]
    col = lax.broadcasted_iota(i32, (ts, nk), 1)
    row = lax.broadcasted_iota(i32, (ts, nk), 0)
    o_ref[...] = jnp.where(col <= npg * PAGE + row, _sort_key(acc), INT_MIN)


def _select_kernel(k_ref, o_ref, key_ref, *, topk, idx_bits):
    key_ref[...] = k_ref[...]
    t = _select_topk(key_ref, 1, key_ref.shape[0], topk, idx_bits)
    o_ref[...] = jnp.where(key_ref[...] >= t, 1.0, 0.0)


def _dsa_sample_attn_kernel(pt_ref, q_ref, kn_ref, vn_ref, sel_ref, bias_ref, spread_ref, *rest, npg):
    kpages, vpages, o_ref = rest[:npg], rest[npg:2 * npg], rest[2 * npg]
    pad = jnp.zeros((PAGE * ATT_H - kn_ref.shape[0], ATT_DH), f32)
    kx = jnp.concatenate([p[...] for p in kpages] + [kn_ref[...], pad], axis=0).astype(bf16)
    vx = jnp.concatenate([p[...] for p in vpages] + [vn_ref[...], pad], axis=0).astype(bf16)
    selx = jnp.concatenate([_mm(sel_ref[:, j * PAGE:(j + 1) * PAGE], spread_ref[...]) for j in range(npg + 1)],
                           axis=1)
    s = _mm_nt(q_ref[...], kx) * ATT_DH ** -0.5 + jnp.where(selx > 0.5, bias_ref[...], NEG_BIG)
    p = jnp.exp(s - jnp.max(s, axis=-1, keepdims=True))
    o_ref[...] = (_mm(p.astype(bf16), vx) / jnp.sum(p, axis=-1, keepdims=True)).astype(bf16)


def _dsa_sample(qa, ka, va, qi, kiwi, cache_k, cache_v, cache_ki, layer, page_table, rel_bias):
    db, ts = qa.shape[:2]
    npg = page_table.shape[1]
    n_past = npg * PAGE
    nk = n_past + PAGE
    nq = db * ts
    topk = min(TOPK_MAX, (n_past + ts) // 4)
    hd = ATT_H * ATT_DH
    ki_page = lambda j: pl.BlockSpec((None, None, IDX_DIM, PAGE), lambda b, pt: (layer, pt[b, j], 0, 0))
    seq = lambda r, c: pl.BlockSpec((None, r, c), lambda b, pt: (b, 0, 0))

    keys = pl.pallas_call(
        functools.partial(_dsa_sample_score_kernel, npg=npg),
        grid_spec=pltpu.PrefetchScalarGridSpec(
            num_scalar_prefetch=1, grid=(db,),
            in_specs=[seq(ts, IDX_H * IDX_DIM), seq(ts, LANE)] + [ki_page(j) for j in range(npg)],
            out_specs=seq(ts, nk)),
        out_shape=jax.ShapeDtypeStruct((db, ts, nk), i32),
        compiler_params=_cparams(("parallel",)),
        name="dsa_sample_score",
    )(page_table, qi, kiwi, *([jnp.swapaxes(cache_ki, 2, 3)] * npg))

    col = pl.BlockSpec((nk, LANE), lambda i: (0, i))
    sel = pl.pallas_call(
        functools.partial(_select_kernel, topk=topk, idx_bits=max(1, (nk - 1).bit_length())),
        grid=(nq // LANE,),
        in_specs=[col],
        out_specs=col,
        out_shape=jax.ShapeDtypeStruct((nk, nq), f32),
        scratch_shapes=[pltpu.VMEM((nk, LANE), i32)],
        compiler_params=_cparams(("parallel",)),
        name="dsa_sample_select",
    )(keys.reshape(nq, nk).T)
    sel = jnp.repeat(sel.T.reshape(db, ts, nk), ATT_H, axis=1).astype(bf16)
    rel = n_past + jnp.arange(ts)[:, None] - jnp.arange(nk)[None, :]
    bias = jnp.repeat(jnp.moveaxis(_bias_lookup(rel_bias, rel), -1, 1), ATT_H, axis=-1)
    same_head = jnp.arange(nk * ATT_H)[None, :] % ATT_H == jnp.arange(ATT_H)[:, None]
    bias = jnp.where(same_head[None], bias, NEG_BIG).reshape(ts * ATT_H, nk * ATT_H)
    spread = (jnp.arange(PAGE * ATT_H)[None, :] // ATT_H == jnp.arange(PAGE)[:, None]).astype(bf16)

    n_pages = cache_k.shape[1]
    rows = PAGE * ATT_H
    kv_page = lambda j: pl.BlockSpec((rows, ATT_DH), lambda b, pt: (layer * n_pages + pt[b, j], 0))
    const = lambda a: pl.BlockSpec(a.shape, lambda b, pt: (0, 0))
    as_rows = lambda a: a.reshape(db, ts * ATT_H, ATT_DH)
    out = pl.pallas_call(
        functools.partial(_dsa_sample_attn_kernel, npg=npg),
        grid_spec=pltpu.PrefetchScalarGridSpec(
            num_scalar_prefetch=1, grid=(db,),
            in_specs=[seq(ts * ATT_H, ATT_DH)] * 3 + [seq(ts * ATT_H, nk), const(bias), const(spread)]
                     + [kv_page(j) for j in range(npg)] * 2,
            out_specs=seq(ts * ATT_H, ATT_DH)),
        out_shape=jax.ShapeDtypeStruct((db, ts * ATT_H, ATT_DH), bf16),
        compiler_params=_cparams(("parallel",)),
        name="dsa_sample_attn",
    )(page_table, as_rows(qa), as_rows(ka), as_rows(va), sel, bias, spread,
      *([cache_k.reshape(-1, ATT_DH)] * npg), *([cache_v.reshape(-1, ATT_DH)] * npg))
    return out.reshape(db, ts, hd)


def kernel(x_prompt, x_sample, cache_k, cache_v, cache_kidx, state_ret, state_pool, state_conv, state_ssm,
           page_table, norm_mix, norm_ffn, w_in_even, w_out_even, q_norm, k_norm, rel_bias,
           w_in_odd, w_out_odd, pool_w, pool_scale, conv_w, conv_b, dt_bias, a_log, d_skip, ssm_norm,
           w_gate, w_up, w_down):
    bp, sp, d = x_prompt.shape
    db, ts, _ = x_sample.shape
    n_p, n_s = bp * sp, db * ts
    assert d == D_MODEL and n_p % TM == 0 and n_s % TM == 0 and sp % LANE == 0 and sp >= POOL_PAST
    n_past = page_table.shape[1] * PAGE
    n_even, n_odd = (DEPTH + 1) // 2, DEPTH // 2
    hd = ATT_H * ATT_DH
    xp, xs = x_prompt.reshape(n_p, d), x_sample.reshape(n_s, d)
    seqs = lambda a: a.reshape(db, ts, a.shape[-1])
    last = lambda a, n: a.reshape(bp, sp, a.shape[-1])[:, sp - n:]
    tail = lambda prev, cur, n: jnp.concatenate([prev.astype(f32), cur], axis=1)[:, -n:]
    zeros = lambda *s: jnp.zeros((1, bp) + s, f32)
    outs = [[] for _ in range(14)]
    rstate_p = rstate_s = h_p = h_s = None
    for l in range(DEPTH):
        w_ffn = (norm_ffn[l][None], w_gate[l].astype(bf16), w_up[l].astype(bf16), w_down[l].astype(bf16))
        if l % 2 == 0:
            i = l // 2
            w_in = jnp.pad(w_in_even[i], ((0, 0), (0, EVEN_PROJ_PAD - EVEN_PROJ))).astype(bf16)
            prm = (norm_mix[l][None], w_in, q_norm[i][None], k_norm[i][None])
            ret_p, qa_p, ka_p, kab_p, va_p, vab_p, qi_p, kiwi_p = _even_proj(xp, *prm)
            ret_s, qa_s, ka_s, _, va_s, _, qi_s, kiwi_s = _even_proj(xs, *prm)
            mix_a_p, rstate_p = _retention(ret_p, zeros(RET_H, RET_DK, RET_DV), 0, 0, bp, sp, rstate_p, i, n_even)
            mix_a_s, rstate_s = _retention(seqs(ret_s), state_ret, i, n_past, db, ts, rstate_s, i, n_even)
            mix_b_p = _dsa_prompt(qa_p, kab_p, vab_p, qi_p, kiwi_p, rel_bias, bp, sp)
            mix_b_s = _dsa_sample(seqs(qa_s), seqs(ka_s), seqs(va_s), seqs(qi_s), seqs(kiwi_s),
                                  cache_k, cache_v, cache_kidx, i, page_table, rel_bias)
            w_out = w_out_even[i].astype(bf16)
            new = [ka_p.reshape(bp, sp, ATT_H, ATT_DH), va_p.reshape(bp, sp, ATT_H, ATT_DH),
                   kiwi_p.reshape(bp, sp, LANE)[..., :IDX_DIM], None, None, None, None,
                   ka_s.reshape(db, ts, ATT_H, ATT_DH), va_s.reshape(db, ts, ATT_H, ATT_DH),
                   seqs(kiwi_s)[..., :IDX_DIM], None, None, None, None]
        else:
            j = l // 2
            w_in = jnp.pad(w_in_odd[j], ((0, 0), (0, ODD_PROJ_PAD - ODD_PROJ))).astype(bf16)
            u_p, z_p, xbc_p, dt_p = _odd_proj(xp, norm_mix[l][None], w_in)
            u_s, z_s, xbc_s, dt_s = _odd_proj(xs, norm_mix[l][None], w_in)
            prm = (pool_w[j], pool_scale[j], conv_w[j], conv_b[j], dt_bias[j], a_log[j], d_skip[j], ssm_norm[j])
            mix_a_p, mix_b_p, h_p = _odd_mixer(
                u_p, z_p, xbc_p, dt_p, zeros(POOL_PAST, POOL_DIM), zeros(CONV_W - 1, CONV_DIM),
                zeros(SSM_H, SSM_P, SSM_N), 0, prm, 0, bp, sp, h_p, j, n_odd)
            mix_a_s, mix_b_s, h_s = _odd_mixer(
                seqs(u_s), seqs(z_s), seqs(xbc_s), seqs(dt_s), state_pool, state_conv, state_ssm, j,
                prm, n_past, db, ts, h_s, j, n_odd)
            w_out = w_out_odd[j].astype(bf16)
            new = [None, None, None, None, last(u_p, POOL_PAST), last(xbc_p, CONV_W - 1), None,
                   None, None, None, None,
                   tail(state_pool[j], seqs(u_s), POOL_PAST), tail(state_conv[j], seqs(xbc_s), CONV_W - 1), None]
        for acc, leaf in zip(outs, new):
            if leaf is not None:
                acc.append(leaf)
        xp = _out_ffn(xp, mix_a_p, mix_b_p, w_out, *w_ffn)
        xs = _out_ffn(xs, mix_a_s.reshape(n_s, -1), mix_b_s.reshape(n_s, -1), w_out, *w_ffn)
    leaves = [jnp.stack(a) if a else None for a in outs]
    leaves[3], leaves[6], leaves[10], leaves[13] = rstate_p, h_p, rstate_s, h_s
    return (xp.reshape(bp, sp, d), xs.reshape(db, ts, d)) + tuple(leaves)
```

```python
import functools
import math

import jax
import jax.numpy as jnp
import numpy as np
from jax import lax
from jax.experimental import pallas as pl
from jax.experimental.pallas import tpu as pltpu

f32 = jnp.float32
bf16 = jnp.bfloat16
i32 = jnp.int32

D_MODEL = 1024
DEPTH = 4
PAGE = 128
RET_H, RET_DK, RET_DV, RET_CHUNK = 4, 128, 128, 128
ROPE_BASE = 10000.0
ATT_H, ATT_DH, ATT_BLOCK = 4, 128, 128
IDX_H, IDX_DIM = 8, 64
TOPK_MAX = 256
REL_BUCKETS, REL_MAX_DIST = 32, 128
POOL_WINDOWS = (2, 4, 8, 16)
POOL_GROUPS = 4
POOL_DIM = D_MODEL // 2
POOL_GC = POOL_DIM // POOL_GROUPS
POOL_PAST = 15
D_INNER = D_MODEL // 2
SSM_P = 64
SSM_H = D_INNER // SSM_P
SSM_G = 2
SSM_N = 128
CONV_W = 4
CONV_DIM = D_INNER + 2 * SSM_G * SSM_N
SSD_CHUNK = 128
FF_DIM = -(-8 * D_MODEL // (3 * 256)) * 256
EPS = 1e-6

EVEN_PROJ = 4 * 512 + 3 * 512 + 512 + IDX_DIM + IDX_H
EVEN_PROJ_PAD = 4224
ODD_PROJ = POOL_DIM + D_INNER + CONV_DIM + SSM_H
ODD_PROJ_PAD = 2176

LANE = 128
INT_MIN = -(2 ** 31)
NEG_BIG = -1e30
VMEM_LIMIT = 56 * 1024 * 1024
TM = 256
FF_SPLIT = 2
DSA_CHUNK = 512
SEQ_GROUP = 8


def _cparams(sem):
    return pltpu.CompilerParams(dimension_semantics=sem, vmem_limit_bytes=VMEM_LIMIT)


def _mm(a, b):
    return jnp.dot(a, b, preferred_element_type=f32)


def _mm_nt(a, b):
    return lax.dot_general(a, b, (((1,), (1,)), ((), ())), preferred_element_type=f32)


def _mm_tn(a, b):
    return lax.dot_general(a, b, (((0,), (0,)), ((), ())), preferred_element_type=f32)


def _rms(x, g):
    return x * lax.rsqrt(jnp.mean(x * x, -1, keepdims=True) + EPS) * g


def _silu(x):
    return x / (1.0 + jnp.exp(-x))


def _const_spec(shape):
    nd = len(shape)
    return pl.BlockSpec(shape, lambda *a: (0,) * nd)


def _even_proj_kernel(x_ref, g_ref, w_ref, qg_ref, kg_ref,
                      ret_ref, qa_ref, ka_ref, kab_ref, va_ref, vab_ref, qi_ref, kiwi_ref):
    xb = _rms(x_ref[...], g_ref[...]).astype(bf16)
    ret_ref[...] = _mm(xb, w_ref[:, 0:2048])
    qa = _mm(xb, w_ref[:, 2048:2560])
    ka = _mm(xb, w_ref[:, 2560:3072])
    for h in range(ATT_H):
        sl = slice(h * ATT_DH, (h + 1) * ATT_DH)
        qa_ref[:, sl] = _rms(qa[:, sl], qg_ref[...]).astype(bf16)
        kn = _rms(ka[:, sl], kg_ref[...])
        ka_ref[:, sl] = kn
        kab_ref[:, sl] = kn.astype(bf16)
    va = _mm(xb, w_ref[:, 3072:3584])
    va_ref[...] = va
    vab_ref[...] = va.astype(bf16)
    qi_ref[...] = (_mm(xb, w_ref[:, 3584:4096]) * IDX_DIM ** -0.5).astype(bf16)
    kiwi_ref[...] = _mm(xb, w_ref[:, 4096:4224])


def _even_proj(x, g, w, qg, kg):
    n = x.shape[0]
    row = lambda c: pl.BlockSpec((TM, c), lambda i: (i, 0))
    outs = [(2048, f32), (512, bf16), (512, f32), (512, bf16), (512, f32), (512, bf16), (512, bf16), (LANE, f32)]
    return pl.pallas_call(
        _even_proj_kernel,
        grid=(n // TM,),
        in_specs=[row(D_MODEL), _const_spec((1, D_MODEL)), _const_spec((D_MODEL, EVEN_PROJ_PAD)),
                  _const_spec((1, ATT_DH)), _const_spec((1, ATT_DH))],
        out_specs=[row(c) for c, _ in outs],
        out_shape=[jax.ShapeDtypeStruct((n, c), dt) for c, dt in outs],
        compiler_params=_cparams(("parallel",)),
        name="even_proj",
    )(x, g, w, qg, kg)


def _odd_proj_kernel(x_ref, g_ref, w_ref, u_ref, z_ref, xbc_ref, dt_ref):
    xb = _rms(x_ref[...], g_ref[...]).astype(bf16)
    u_ref[...] = _mm(xb, w_ref[:, 0:512])
    z_ref[...] = _mm(xb, w_ref[:, 512:1024])
    xbc_ref[...] = _mm(xb, w_ref[:, 1024:2048])
    dt_ref[...] = _mm(xb, w_ref[:, 2048:2176])


def _odd_proj(x, g, w):
    n = x.shape[0]
    row = lambda c: pl.BlockSpec((TM, c), lambda i: (i, 0))
    outs = [512, 512, 1024, LANE]
    return pl.pallas_call(
        _odd_proj_kernel,
        grid=(n // TM,),
        in_specs=[row(D_MODEL), _const_spec((1, D_MODEL)), _const_spec((D_MODEL, ODD_PROJ_PAD))],
        out_specs=[row(c) for c in outs],
        out_shape=[jax.ShapeDtypeStruct((n, c), f32) for c in outs],
        compiler_params=_cparams(("parallel",)),
        name="odd_proj",
    )(x, g, w)


def _out_ffn_kernel(x_ref, a_ref, b_ref, wo_ref, g_ref, wg_ref, wu_ref, wd_ref, o_ref):
    half = wo_ref.shape[0] // 2
    x = x_ref[...] + _mm(a_ref[...], wo_ref[0:half, :]) + _mm(b_ref[...], wo_ref[half:, :])
    hb = _rms(x, g_ref[...]).astype(bf16)
    fc = FF_DIM // FF_SPLIT
    ff = None
    for c in range(FF_SPLIT):
        sl = slice(c * fc, (c + 1) * fc)
        act = (_silu(_mm(hb, wg_ref[:, sl])) * _mm(hb, wu_ref[:, sl])).astype(bf16)
        down = _mm(act, wd_ref[sl, :])
        ff = down if ff is None else ff + down
    o_ref[...] = x + ff


def _out_ffn(x, a, b, wo, g, wg, wu, wd):
    n = x.shape[0]
    row = lambda c: pl.BlockSpec((TM, c), lambda i: (i, 0))
    once = lambda shape: pl.BlockSpec(shape, lambda i: (0, 0), pipeline_mode=pl.Buffered(1))
    return pl.pallas_call(
        _out_ffn_kernel,
        grid=(n // TM,),
        in_specs=[row(D_MODEL), row(a.shape[1]), row(b.shape[1]), once(wo.shape), _const_spec((1, D_MODEL)),
                  once(wg.shape), once(wu.shape), once(wd.shape)],
        out_specs=row(D_MODEL),
        out_shape=jax.ShapeDtypeStruct((n, D_MODEL), f32),
        compiler_params=_cparams(("parallel",)),
        name="out_ffn",
    )(x, a, b, wo, g, wg, wu, wd)


def _rope_tables(pos):
    half = RET_DK // 2
    inv = ROPE_BASE ** (-jnp.linspace(0.0, 1.0, half, dtype=f32))
    ang = pos.astype(f32)[:, None] * inv[None, :]
    cos, sin = jnp.cos(ang), jnp.sin(ang)
    return jnp.concatenate([cos, cos], -1), jnp.concatenate([-sin, sin], -1)


def _ret_decay(q):
    log_g = jnp.log1p(-jnp.exp2(-5.0 - jnp.arange(RET_H, dtype=f32)))
    idx = jnp.arange(q, dtype=f32)
    diff = idx[:, None] - idx[None, :]
    dmask = jnp.where(diff[None] >= 0, jnp.exp(log_g[:, None, None] * jnp.maximum(diff, 0.0)[None]), 0.0)
    xi = jnp.exp(log_g[:, None] * (idx + 1.0)[None])
    zeta = jnp.exp(log_g[:, None] * (q - 1.0 - idx)[None])
    g_chunk = jnp.exp(log_g * q)
    return dmask, xi, zeta, g_chunk


def _t5_bucket(rel):
    n = jnp.maximum(rel, 0)
    max_exact = REL_BUCKETS // 2
    nf = jnp.maximum(n, 1).astype(f32)
    large = max_exact + (jnp.log(nf / max_exact) / math.log(REL_MAX_DIST / max_exact)
                         * (REL_BUCKETS - max_exact)).astype(i32)
    large = jnp.minimum(large, REL_BUCKETS - 1)
    return jnp.where(n < max_exact, n, large)


def _bias_lookup(rel_bias, rel):
    onehot = jax.nn.one_hot(_t5_bucket(rel), REL_BUCKETS, dtype=f32)
    return jnp.einsum("...b,bh->...h", onehot, rel_bias.astype(f32), precision=lax.Precision.HIGHEST)


def _rotary(x, c, s):
    return x * c + pltpu.roll(x, RET_DK // 2, 1) * s


def _retention_kernel(q_ref, k_ref, v_ref, g_ref, r0_ref, c_ref, s_ref, dm_ref, xi_ref, zt_ref, gc_ref,
                      *rest):
    o_ref, r_ref = rest[-2:]

    @pl.when(pl.program_id(1) == 0)
    def _():
        r_ref[...] = r0_ref[...]

    cos, sin = c_ref[...], s_ref[...]
    for b in range(q_ref.shape[0]):
        for h in range(RET_H):
            sl = slice(h * RET_DK, (h + 1) * RET_DK)
            qr = _rotary(q_ref[b, :, sl], cos, sin).astype(bf16)
            kr = _rotary(k_ref[b, :, sl], cos, sin) * RET_DK ** -0.5
            vb = v_ref[b, :, sl].astype(bf16)
            r = r_ref[b, h]
            s = _mm_nt(qr, kr.astype(bf16)) * dm_ref[h]
            o = _mm(s.astype(bf16), vb) + _mm(qr, r.astype(bf16)) * xi_ref[h]
            r_ref[b, h] = r * gc_ref[h, 0:1, :] + _mm_tn((kr * zt_ref[h]).astype(bf16), vb)
            o = o * lax.rsqrt(jnp.mean(o * o, -1, keepdims=True) + EPS)
            o_ref[b, :, sl] = (_silu(g_ref[b, :, sl]) * o).astype(bf16)


def _seq_group(nb):
    return math.gcd(nb, SEQ_GROUP)


def _layer_block(tail, layer, group):
    return pl.BlockSpec((None, group) + tail, lambda b, c: (layer, b) + (0,) * len(tail))


def _stacked_out(prev, n_layers, nb, tail, n_inputs, out_index):
    shape = jax.ShapeDtypeStruct((n_layers, nb) + tail, f32)
    if prev is None:
        return shape, [], [], {}
    return shape, [prev], [pl.BlockSpec(memory_space=pl.ANY)], {n_inputs: out_index}


def _retention(ret, r0, layer_in, pos0, prev, layer_out, n_layers):
    nb, t, _ = ret.shape
    q = math.gcd(t, RET_CHUNK)
    grp = _seq_group(nb)
    cos, sin = _rope_tables(pos0 + jnp.arange(t))
    dmask, xi, zeta, g_chunk = _ret_decay(q)
    bcast = lambda a: jnp.broadcast_to(a[:, :, None], (RET_H, a.shape[1], LANE))
    col = lambda j: pl.BlockSpec((grp, q, 512), lambda b, c: (b, c, j))
    tab = pl.BlockSpec((q, LANE), lambda b, c: (c, 0))
    tail = (RET_H, RET_DK, RET_DV)
    in_specs = [col(0), col(1), col(2), col(3), _layer_block(tail, layer_in, grp), tab, tab,
                _const_spec((RET_H, q, q)), _const_spec((RET_H, q, LANE)), _const_spec((RET_H, q, LANE)),
                _const_spec((RET_H, 8, LANE))]
    state_shape, extra, extra_specs, aliases = _stacked_out(prev, n_layers, nb, tail, len(in_specs), 1)
    return pl.pallas_call(
        _retention_kernel,
        grid=(nb // grp, t // q),
        in_specs=in_specs + extra_specs,
        out_specs=[col(0), _layer_block(tail, layer_out, grp)],
        out_shape=[jax.ShapeDtypeStruct((nb, t, 512), bf16), state_shape],
        input_output_aliases=aliases,
        compiler_params=_cparams(("parallel", "arbitrary")),
        name="retention_t%d" % t,
    )(ret, ret, ret, ret, r0, cos, sin, dmask, bcast(xi), bcast(zeta),
      jnp.broadcast_to(g_chunk[:, None, None], (RET_H, 8, LANE)), *extra)


def _sort_key(score):
    bits = lax.bitcast_convert_type(score, i32)
    bits = jnp.where(bits == INT_MIN, 0, bits)
    return jnp.where(bits < 0, bits ^ 0x7FFFFFFF, bits)


def _count(key_ref, nchunk, ck, pred):
    def body(c, acc):
        r0 = pl.multiple_of(c * ck, ck)
        hit = jnp.where(pred(key_ref[pl.ds(r0, ck), :], r0), 1, 0).astype(i32)
        return acc + jnp.sum(hit.reshape(ck // 8, 8, LANE), axis=0)
    acc = lax.fori_loop(0, nchunk, body, jnp.zeros((8, LANE), i32))
    return jnp.sum(acc, axis=0, keepdims=True)


def _select_topk(key_ref, nchunk, ck, topk, idx_bits):
    def bit_step(it, t):
        cand = t + jnp.left_shift(jnp.int32(1), 31 - it)
        cnt = _count(key_ref, nchunk, ck, lambda blk, r0: blk >= cand)
        return jnp.where(cnt >= topk, cand, t)

    t = lax.fori_loop(0, 32, bit_step, jnp.full((1, LANE), INT_MIN, i32))
    t = jnp.maximum(t, INT_MIN + 1)
    c_ge = _count(key_ref, nchunk, ck, lambda blk, r0: blk >= t)
    c_gt = _count(key_ref, nchunk, ck, lambda blk, r0: blk > t)
    surplus = c_ge > topk
    rows = lax.broadcasted_iota(i32, (ck, LANE), 0)

    @pl.when(jnp.max(jnp.where(surplus, 1, 0)) > 0)
    def _():
        want = jnp.where(surplus, topk - c_gt, jnp.int32(2 ** 30))

        def idx_step(it, x):
            cand = x + jnp.left_shift(jnp.int32(1), idx_bits - 1 - it)
            cnt = _count(key_ref, nchunk, ck, lambda blk, r0: jnp.where(blk == t, rows + r0, cand) < cand)
            return jnp.where(cnt < want, cand, x)

        last = lax.fori_loop(0, idx_bits, idx_step, jnp.zeros((1, LANE), i32))

        def demote(c, carry):
            r0 = pl.multiple_of(c * ck, ck)
            blk = key_ref[pl.ds(r0, ck), :]
            drop = jnp.where(blk == t, rows + r0, last) > last
            key_ref[pl.ds(r0, ck), :] = jnp.where(drop, INT_MIN, blk)
            return carry

        lax.fori_loop(0, nchunk, demote, 0)

    return t


def _dsa_prompt_kernel(qa_ref, qit_ref, kiwiq_ref, k_ref, vt_ref, kiwik_ref, bias_ref, o_ref, key_ref, acc_ref,
                       *, topk, idx_bits, nch, ck):
    qb = pl.program_id(1)
    cb = ck // LANE
    nchunk = (qb + cb) // cb
    wit = jnp.transpose(kiwiq_ref[...])[IDX_DIM:IDX_DIM + IDX_H, :] * IDX_H ** -0.5
    qpos = qb * LANE + lax.broadcasted_iota(i32, (ck, LANE), 1)
    rows = lax.broadcasted_iota(i32, (ck, LANE), 0)

    def score_chunk(c, carry):
        r0 = pl.multiple_of(c * ck, ck)
        kic = kiwik_ref[pl.ds(r0, ck), 0:IDX_DIM].astype(bf16)
        acc = jnp.zeros((ck, LANE), f32)
        for h in range(IDX_H):
            s = _mm(kic, qit_ref[:, h * LANE:(h + 1) * LANE])
            acc = acc + jnp.maximum(s, 0.0) * wit[h:h + 1, :]
        key_ref[pl.ds(r0, ck), :] = jnp.where(rows + r0 <= qpos, _sort_key(acc), INT_MIN)
        return carry

    lax.fori_loop(0, nchunk, score_chunk, 0)
    t = _select_topk(key_ref, nchunk, ck, topk, idx_bits)

    acc_ref[...] = jnp.zeros_like(acc_ref)

    def att_chunk(c, carry):
        ms, ls = carry
        r0 = pl.multiple_of(c * ck, ck)
        negm = jnp.where(key_ref[pl.ds(r0, ck), :] >= t, 0.0, NEG_BIG)
        new_ms, new_ls = [], []
        for h in range(ATT_H):
            sl = slice(h * ATT_DH, (h + 1) * ATT_DH)
            bias = jnp.concatenate([bias_ref[jnp.clip(qb - (c * cb + j), 0, 2), h] for j in range(cb)], axis=0)
            s = _mm_nt(k_ref[pl.ds(r0, ck), sl], qa_ref[:, sl]) * ATT_DH ** -0.5 + bias + negm
            m_new = jnp.maximum(ms[h], jnp.max(s, axis=0, keepdims=True))
            alpha = jnp.exp(ms[h] - m_new)
            p = jnp.exp(s - m_new)
            new_ls.append(ls[h] * alpha + jnp.sum(p, axis=0, keepdims=True))
            acc_ref[h] = acc_ref[h] * alpha + _mm(vt_ref[h * nch + c], p.astype(bf16))
            new_ms.append(m_new)
        return tuple(new_ms), tuple(new_ls)

    init = ((jnp.full((1, LANE), NEG_BIG, f32),) * ATT_H, (jnp.zeros((1, LANE), f32),) * ATT_H)
    _, ls = lax.fori_loop(0, nchunk, att_chunk, init)
    for h in range(ATT_H):
        o_ref[:, h * ATT_DH:(h + 1) * ATT_DH] = jnp.transpose(acc_ref[h] / ls[h]).astype(bf16)


def _bias_tiles(rel_bias, nd):
    j = jnp.arange(LANE)[:, None]
    i = jnp.arange(LANE)[None, :]
    rel = jnp.arange(nd)[:, None, None] * LANE + (i - j)[None]
    return jnp.moveaxis(_bias_lookup(rel_bias, rel), -1, 1)


def _dsa_prompt(qa, kab, vab, qi, kiwi, rel_bias, nb, t):
    nqb = t // LANE
    topk = min(TOPK_MAX, t // 4)
    assert REL_MAX_DIST <= LANE + 1
    bias = _bias_tiles(rel_bias, 3)
    ck = math.gcd(t, DSA_CHUNK)
    nch = t // ck
    vt = vab[:nb * t].reshape(nb, nch, ck, ATT_H, ATT_DH).transpose(0, 3, 1, 4, 2)
    vt = vt.reshape(nb * ATT_H * nch, ATT_DH, ck)
    qit = qi[:nb * t].reshape(nb * nqb, LANE, IDX_H, IDX_DIM).transpose(0, 3, 2, 1)
    qit = qit.reshape(nb * nqb, IDX_DIM, IDX_H * LANE)
    qrow = lambda c: pl.BlockSpec((LANE, c), lambda b, q: (b * nqb + q, 0))
    seq = lambda c: pl.BlockSpec((t, c), lambda b, q: (b, 0))
    kern = functools.partial(_dsa_prompt_kernel, topk=topk, idx_bits=max(1, (t - 1).bit_length()), nch=nch, ck=ck)
    return pl.pallas_call(
        kern,
        grid=(nb, nqb),
        in_specs=[qrow(512), pl.BlockSpec((None, IDX_DIM, IDX_H * LANE), lambda b, q: (b * nqb + q, 0, 0)),
                  qrow(LANE), seq(512),
                  pl.BlockSpec((ATT_H * nch, ATT_DH, ck), lambda b, q: (b, 0, 0)), seq(LANE),
                  _const_spec((3, ATT_H, LANE, LANE))],
        out_specs=qrow(512),
        out_shape=jax.ShapeDtypeStruct((nb * t, 512), bf16),
        scratch_shapes=[pltpu.VMEM((t, LANE), i32), pltpu.VMEM((ATT_H, ATT_DH, LANE), f32)],
        compiler_params=_cparams(("parallel", "arbitrary")),
        name="dsa_prompt",
    )(qa, qit, kiwi, kab, vt, kiwi, bias)


def _softplus(x):
    return jnp.maximum(x, 0.0) + jnp.log1p(jnp.exp(-jnp.abs(x)))


def _cumsum_rows(tri, a):
    hi = a.astype(bf16)
    r1 = a - hi.astype(f32)
    mid = r1.astype(bf16)
    lo = (r1 - mid.astype(f32)).astype(bf16)
    return _mm(tri, hi) + _mm(tri, mid) + _mm(tri, lo)


def _odd_mixer_kernel(*refs, q, pos0):
    seq_in, shared, seq_out = refs[:7], refs[7:16], refs[-6:]
    for b in range(seq_in[0].shape[0]):
        _odd_mixer_seq(*(r.at[b] for r in seq_in), *shared, *(r.at[b] for r in seq_out), q=q, pos0=pos0)


def _odd_mixer_seq(u_ref, z_ref, xbc_ref, dt_ref, pp_ref, cp_ref, h0_ref, pw_ref, ps_ref, cw_ref, cb_ref,
                   dtb_ref, alog_ref, dsk_ref, nrm_ref, tri_ref, po_ref, y_ref, h_ref, ubuf, xbuf, ybuf, *, q, pos0):
    c = pl.program_id(1)

    @pl.when(c == 0)
    def _():
        ubuf[0:1, :] = jnp.zeros((1, POOL_DIM), f32)
        ubuf[1:16, :] = pp_ref[...]
        xbuf[0:8 - (CONV_W - 1), :] = jnp.zeros((8 - (CONV_W - 1), CONV_DIM), f32)
        xbuf[8 - (CONV_W - 1):8, :] = cp_ref[...]
        h_ref[...] = h0_ref[...]

    pos = pos0 + c * q + lax.broadcasted_iota(i32, (q, LANE), 0)
    causal = lax.broadcasted_iota(i32, (q, q), 0) >= lax.broadcasted_iota(i32, (q, q), 1)

    u = u_ref[...]
    ubuf[16:16 + q, :] = u
    for g, w in enumerate(POOL_WINDOWS):
        sl = slice(g * POOL_GC, (g + 1) * POOL_GC)
        acc = u[:, sl]
        for k in range(1, w):
            acc = acc + ubuf[16 - k:16 - k + q, sl]
        d = acc / jnp.minimum(pos + 1, w).astype(f32) - u[:, sl]
        po_ref[:, sl] = (_mm(d.astype(bf16), pw_ref[g]) * ps_ref[:, sl]).astype(bf16)
    ubuf[0:16, :] = ubuf[q:q + 16, :]

    xbuf[8:8 + q, :] = xbc_ref[...]
    conv = cb_ref[...]
    for j in range(CONV_W):
        off = 8 - (CONV_W - 1) + j
        conv = conv + xbuf[off:off + q, :] * cw_ref[j:j + 1, :]
    xbuf[0:8, :] = xbuf[q:q + 8, :]
    act = _silu(conv)
    xs = act[:, 0:D_INNER]

    dt = _softplus(dt_ref[...] + dtb_ref[...])
    a = dt * (-jnp.exp(alog_ref[...]))
    cs = _cumsum_rows(tri_ref[...], a)
    cs_t = jnp.transpose(cs)
    dt_t = jnp.transpose(dt)
    cs_last = cs[q - 1:q, :]
    w_end = jnp.exp(cs_last - cs) * dt
    ecs = jnp.exp(cs)
    hpg = SSM_H // SSM_G
    for g in range(SSM_G):
        bm = act[:, D_INNER + g * SSM_N:D_INNER + (g + 1) * SSM_N].astype(bf16)
        cm = act[:, D_INNER + (SSM_G + g) * SSM_N:D_INNER + (SSM_G + g + 1) * SSM_N].astype(bf16)
        cb = _mm_nt(cm, bm)
        for hh in range(hpg):
            h = g * hpg + hh
            psl = slice(h * SSM_P, (h + 1) * SSM_P)
            seg = cs[:, h:h + 1] - cs_t[h:h + 1, :]
            lm = jnp.exp(jnp.where(causal, seg, NEG_BIG))
            sc = cb * lm * dt_t[h:h + 1, :]
            xh = xs[:, psl]
            hs = h_ref[h]
            yh = _mm(sc.astype(bf16), xh.astype(bf16)) + _mm_nt(cm, hs.astype(bf16)) * ecs[:, h:h + 1]
            h_ref[h] = hs * jnp.exp(cs_last[:, h:h + 1]) + _mm_tn((xh * w_end[:, h:h + 1]).astype(bf16), bm)
            ybuf[:, psl] = yh
    y = (ybuf[...] + dsk_ref[...] * xs) * _silu(z_ref[...])
    gw = D_INNER // SSM_G
    for g in range(SSM_G):
        sl = slice(g * gw, (g + 1) * gw)
        yg = y[:, sl]
        y_ref[:, sl] = (yg * lax.rsqrt(jnp.mean(yg * yg, -1, keepdims=True) + EPS) * nrm_ref[:, sl]).astype(bf16)


def _pad_lanes(v):
    return jnp.pad(v.astype(f32), (0, LANE - v.shape[0]))[None, :]


def _odd_mixer(u, z, xbc, dt, pool_prev, conv_prev, h0, layer_in, prm, pos0, prev, layer_out, n_layers):
    pool_w, pool_scale, conv_w, conv_b, dt_bias, a_log, d_skip, ssm_norm = prm
    nb, t, _ = u.shape
    q = math.gcd(t, SSD_CHUNK)
    nc = t // q
    grp = _seq_group(nb)
    row = lambda c: pl.BlockSpec((grp, q, c), lambda b, i: (b, i, 0))
    lead = (nb, t)
    tri = jnp.tril(jnp.ones((q, q), bf16))
    tail = (SSM_H, SSM_P, SSM_N)
    in_specs = [row(POOL_DIM), row(D_INNER), row(CONV_DIM), row(LANE),
                _layer_block((POOL_PAST, POOL_DIM), layer_in, grp),
                _layer_block((CONV_W - 1, CONV_DIM), layer_in, grp), _layer_block(tail, layer_in, grp),
                _const_spec((POOL_GROUPS, POOL_GC, POOL_GC)), _const_spec((1, POOL_DIM)),
                _const_spec((CONV_W, CONV_DIM)), _const_spec((1, CONV_DIM)),
                _const_spec((1, LANE)), _const_spec((1, LANE)),
                _const_spec((1, D_INNER)), _const_spec((1, D_INNER)), _const_spec((q, q))]
    state_shape, extra, extra_specs, aliases = _stacked_out(prev, n_layers, nb, tail, len(in_specs), 2)
    return pl.pallas_call(
        functools.partial(_odd_mixer_kernel, q=q, pos0=pos0),
        grid=(nb // grp, nc),
        in_specs=in_specs + extra_specs,
        out_specs=[row(POOL_DIM), row(D_INNER), _layer_block(tail, layer_out, grp)],
        out_shape=[jax.ShapeDtypeStruct(lead + (POOL_DIM,), bf16), jax.ShapeDtypeStruct(lead + (D_INNER,), bf16),
                   state_shape],
        input_output_aliases=aliases,
        scratch_shapes=[pltpu.VMEM((grp, q + 16, POOL_DIM), f32), pltpu.VMEM((grp, q + 8, CONV_DIM), f32),
                        pltpu.VMEM((grp, q, D_INNER), f32)],
        compiler_params=_cparams(("parallel", "arbitrary")),
        name="odd_mixer_t%d" % t,
    )(u, z, xbc, dt, pool_prev, conv_prev, h0, pool_w.astype(bf16), pool_scale[None, :], conv_w, conv_b[None, :],
      _pad_lanes(dt_bias), _pad_lanes(a_log), jnp.repeat(d_skip, SSM_P)[None, :], ssm_norm[None, :], tri, *extra)


def _dsa_sample_score_kernel(pt_ref, qi_ref, kiwi_ref, *rest, npg):
    pages, o_ref = rest[:npg], rest[npg]
    ts = qi_ref.shape[0]
    kiwi = kiwi_ref[...]
    wi = kiwi[:, IDX_DIM:IDX_DIM + IDX_H] * IDX_H ** -0.5
    ki_new = jnp.transpose(jnp.concatenate([kiwi, jnp.zeros((PAGE - ts, LANE), f32)], axis=0))[0:IDX_DIM, :]
    ki = jnp.concatenate([p[...] for p in pages] + [ki_new], axis=1).astype(bf16)
    nk = ki.shape[1]
    acc = jnp.zeros((ts, nk), f32)
    for h in range(IDX_H):
        s = _mm(qi_ref[:, h * IDX_DIM:(h + 1) * IDX_DIM], ki)
        acc = acc + jnp.maximum(s, 0.0) * wi[:, h:h + 1]
    col = lax.broadcasted_iota(i32, (ts, nk), 1)
    row = lax.broadcasted_iota(i32, (ts, nk), 0)
    o_ref[...] = jnp.where(col <= npg * PAGE + row, _sort_key(acc), INT_MIN)


def _select_kernel(k_ref, o_ref, key_ref, *, topk, idx_bits):
    key_ref[...] = k_ref[...]
    t = _select_topk(key_ref, 1, key_ref.shape[0], topk, idx_bits)
    o_ref[...] = jnp.where(key_ref[...] >= t, 1.0, 0.0)


def _dsa_sample_attn_kernel(pt_ref, q_ref, kn_ref, vn_ref, sel_ref, bias_ref, spread_ref, *rest, npg):
    kpages, vpages, o_ref = rest[:npg], rest[npg:2 * npg], rest[2 * npg]
    pad = jnp.zeros((PAGE * ATT_H - kn_ref.shape[0], ATT_DH), f32)
    kx = jnp.concatenate([p[...] for p in kpages] + [kn_ref[...], pad], axis=0).astype(bf16)
    vx = jnp.concatenate([p[...] for p in vpages] + [vn_ref[...], pad], axis=0).astype(bf16)
    selx = jnp.concatenate([_mm(sel_ref[:, j * PAGE:(j + 1) * PAGE], spread_ref[...]) for j in range(npg + 1)],
                           axis=1)
    s = _mm_nt(q_ref[...], kx) * ATT_DH ** -0.5 + jnp.where(selx > 0.5, bias_ref[...], NEG_BIG)
    p = jnp.exp(s - jnp.max(s, axis=-1, keepdims=True))
    o_ref[...] = (_mm(p.astype(bf16), vx) / jnp.sum(p, axis=-1, keepdims=True)).astype(bf16)


def _dsa_sample(qa, ka, va, qi, kiwi, cache_k, cache_v, cache_ki, layer, page_table, rel_bias):
    db, ts = qa.shape[:2]
    npg = page_table.shape[1]
    n_past = npg * PAGE
    nk = n_past + PAGE
    nq = db * ts
    topk = min(TOPK_MAX, (n_past + ts) // 4)
    hd = ATT_H * ATT_DH
    ki_page = lambda j: pl.BlockSpec((None, None, IDX_DIM, PAGE), lambda b, pt: (layer, pt[b, j], 0, 0))
    seq = lambda r, c: pl.BlockSpec((None, r, c), lambda b, pt: (b, 0, 0))

    keys = pl.pallas_call(
        functools.partial(_dsa_sample_score_kernel, npg=npg),
        grid_spec=pltpu.PrefetchScalarGridSpec(
            num_scalar_prefetch=1, grid=(db,),
            in_specs=[seq(ts, IDX_H * IDX_DIM), seq(ts, LANE)] + [ki_page(j) for j in range(npg)],
            out_specs=seq(ts, nk)),
        out_shape=jax.ShapeDtypeStruct((db, ts, nk), i32),
        compiler_params=_cparams(("parallel",)),
        name="dsa_sample_score",
    )(page_table, qi, kiwi, *([jnp.swapaxes(cache_ki, 2, 3)] * npg))

    col = pl.BlockSpec((nk, LANE), lambda i: (0, i))
    sel = pl.pallas_call(
        functools.partial(_select_kernel, topk=topk, idx_bits=max(1, (nk - 1).bit_length())),
        grid=(nq // LANE,),
        in_specs=[col],
        out_specs=col,
        out_shape=jax.ShapeDtypeStruct((nk, nq), f32),
        scratch_shapes=[pltpu.VMEM((nk, LANE), i32)],
        compiler_params=_cparams(("parallel",)),
        name="dsa_sample_select",
    )(keys.reshape(nq, nk).T)
    sel = jnp.repeat(sel.T.reshape(db, ts, nk), ATT_H, axis=1).astype(bf16)
    rel = n_past + jnp.arange(ts)[:, None] - jnp.arange(nk)[None, :]
    bias = jnp.repeat(jnp.moveaxis(_bias_lookup(rel_bias, rel), -1, 1), ATT_H, axis=-1)
    same_head = jnp.arange(nk * ATT_H)[None, :] % ATT_H == jnp.arange(ATT_H)[:, None]
    bias = jnp.where(same_head[None], bias, NEG_BIG).reshape(ts * ATT_H, nk * ATT_H)
    spread = (jnp.arange(PAGE * ATT_H)[None, :] // ATT_H == jnp.arange(PAGE)[:, None]).astype(bf16)

    n_pages = cache_k.shape[1]
    rows = PAGE * ATT_H
    kv_page = lambda j: pl.BlockSpec((rows, ATT_DH), lambda b, pt: (layer * n_pages + pt[b, j], 0))
    const = lambda a: pl.BlockSpec(a.shape, lambda b, pt: (0, 0))
    as_rows = lambda a: a.reshape(db, ts * ATT_H, ATT_DH)
    out = pl.pallas_call(
        functools.partial(_dsa_sample_attn_kernel, npg=npg),
        grid_spec=pltpu.PrefetchScalarGridSpec(
            num_scalar_prefetch=1, grid=(db,),
            in_specs=[seq(ts * ATT_H, ATT_DH)] * 3 + [seq(ts * ATT_H, nk), const(bias), const(spread)]
                     + [kv_page(j) for j in range(npg)] * 2,
            out_specs=seq(ts * ATT_H, ATT_DH)),
        out_shape=jax.ShapeDtypeStruct((db, ts * ATT_H, ATT_DH), bf16),
        compiler_params=_cparams(("parallel",)),
        name="dsa_sample_attn",
    )(page_table, as_rows(qa), as_rows(ka), as_rows(va), sel, bias, spread,
      *([cache_k.reshape(-1, ATT_DH)] * npg), *([cache_v.reshape(-1, ATT_DH)] * npg))
    return out.reshape(db, ts, hd)


def kernel(x_prompt, x_sample, cache_k, cache_v, cache_kidx, state_ret, state_pool, state_conv, state_ssm,
           page_table, norm_mix, norm_ffn, w_in_even, w_out_even, q_norm, k_norm, rel_bias,
           w_in_odd, w_out_odd, pool_w, pool_scale, conv_w, conv_b, dt_bias, a_log, d_skip, ssm_norm,
           w_gate, w_up, w_down):
    bp, sp, d = x_prompt.shape
    db, ts, _ = x_sample.shape
    n_p, n_s = bp * sp, db * ts
    assert d == D_MODEL and n_p % TM == 0 and n_s % TM == 0 and sp % LANE == 0 and sp >= POOL_PAST
    n_past = page_table.shape[1] * PAGE
    n_even, n_odd = (DEPTH + 1) // 2, DEPTH // 2
    hd = ATT_H * ATT_DH
    xp, xs = x_prompt.reshape(n_p, d), x_sample.reshape(n_s, d)
    seqs = lambda a: a.reshape(db, ts, a.shape[-1])
    pseqs = lambda a: a.reshape(bp, sp, a.shape[-1])
    last = lambda a, n: a.reshape(bp, sp, a.shape[-1])[:, sp - n:]
    tail = lambda prev, cur, n: jnp.concatenate([prev.astype(f32), cur], axis=1)[:, -n:]
    zeros = lambda *s: jnp.zeros((1, bp) + s, f32)
    outs = [[] for _ in range(14)]
    rstate_p = rstate_s = h_p = h_s = None
    for l in range(DEPTH):
        w_ffn = (norm_ffn[l][None], w_gate[l].astype(bf16), w_up[l].astype(bf16), w_down[l].astype(bf16))
        if l % 2 == 0:
            i = l // 2
            w_in = jnp.pad(w_in_even[i], ((0, 0), (0, EVEN_PROJ_PAD - EVEN_PROJ))).astype(bf16)
            prm = (norm_mix[l][None], w_in, q_norm[i][None], k_norm[i][None])
            ret_p, qa_p, ka_p, kab_p, va_p, vab_p, qi_p, kiwi_p = _even_proj(xp, *prm)
            ret_s, qa_s, ka_s, _, va_s, _, qi_s, kiwi_s = _even_proj(xs, *prm)
            mix_a_p, rstate_p = _retention(pseqs(ret_p), zeros(RET_H, RET_DK, RET_DV), 0, 0, rstate_p, i, n_even)
            mix_a_s, rstate_s = _retention(seqs(ret_s), state_ret, i, n_past, rstate_s, i, n_even)
            mix_b_p = _dsa_prompt(qa_p, kab_p, vab_p, qi_p, kiwi_p, rel_bias, bp, sp)
            mix_b_s = _dsa_sample(seqs(qa_s), seqs(ka_s), seqs(va_s), seqs(qi_s), seqs(kiwi_s),
                                  cache_k, cache_v, cache_kidx, i, page_table, rel_bias)
            w_out = w_out_even[i].astype(bf16)
            new = [ka_p.reshape(bp, sp, ATT_H, ATT_DH), va_p.reshape(bp, sp, ATT_H, ATT_DH),
                   kiwi_p.reshape(bp, sp, LANE)[..., :IDX_DIM], None, None, None, None,
                   ka_s.reshape(db, ts, ATT_H, ATT_DH), va_s.reshape(db, ts, ATT_H, ATT_DH),
                   seqs(kiwi_s)[..., :IDX_DIM], None, None, None, None]
        else:
            j = l // 2
            w_in = jnp.pad(w_in_odd[j], ((0, 0), (0, ODD_PROJ_PAD - ODD_PROJ))).astype(bf16)
            u_p, z_p, xbc_p, dt_p = _odd_proj(xp, norm_mix[l][None], w_in)
            u_s, z_s, xbc_s, dt_s = _odd_proj(xs, norm_mix[l][None], w_in)
            prm = (pool_w[j], pool_scale[j], conv_w[j], conv_b[j], dt_bias[j], a_log[j], d_skip[j], ssm_norm[j])
            mix_a_p, mix_b_p, h_p = _odd_mixer(
                pseqs(u_p), pseqs(z_p), pseqs(xbc_p), pseqs(dt_p), zeros(POOL_PAST, POOL_DIM),
                zeros(CONV_W - 1, CONV_DIM), zeros(SSM_H, SSM_P, SSM_N), 0, prm, 0, h_p, j, n_odd)
            mix_a_s, mix_b_s, h_s = _odd_mixer(
                seqs(u_s), seqs(z_s), seqs(xbc_s), seqs(dt_s), state_pool, state_conv, state_ssm, j,
                prm, n_past, h_s, j, n_odd)
            w_out = w_out_odd[j].astype(bf16)
            new = [None, None, None, None, last(u_p, POOL_PAST), last(xbc_p, CONV_W - 1), None,
                   None, None, None, None,
                   tail(state_pool[j], seqs(u_s), POOL_PAST), tail(state_conv[j], seqs(xbc_s), CONV_W - 1), None]
        for acc, leaf in zip(outs, new):
            if leaf is not None:
                acc.append(leaf)
        xp = _out_ffn(xp, mix_a_p.reshape(n_p, -1), mix_b_p.reshape(n_p, -1), w_out, *w_ffn)
        xs = _out_ffn(xs, mix_a_s.reshape(n_s, -1), mix_b_s.reshape(n_s, -1), w_out, *w_ffn)
    leaves = [jnp.stack(a) if a else None for a in outs]
    leaves[3], leaves[6], leaves[10], leaves[13] = rstate_p, h_p, rstate_s, h_s
    return (xp.reshape(bp, sp, d), xs.reshape(db, ts, d)) + tuple(leaves)
```

```python
import functools
import math

import jax
import jax.numpy as jnp
import numpy as np
from jax import lax
from jax.experimental import pallas as pl
from jax.experimental.pallas import tpu as pltpu

f32 = jnp.float32
bf16 = jnp.bfloat16
i32 = jnp.int32

D_MODEL = 1024
DEPTH = 4
PAGE = 128
RET_H, RET_DK, RET_DV, RET_CHUNK = 4, 128, 128, 128
ROPE_BASE = 10000.0
ATT_H, ATT_DH, ATT_BLOCK = 4, 128, 128
IDX_H, IDX_DIM = 8, 64
TOPK_MAX = 256
REL_BUCKETS, REL_MAX_DIST = 32, 128
POOL_WINDOWS = (2, 4, 8, 16)
POOL_GROUPS = 4
POOL_DIM = D_MODEL // 2
POOL_GC = POOL_DIM // POOL_GROUPS
POOL_PAST = 15
D_INNER = D_MODEL // 2
SSM_P = 64
SSM_H = D_INNER // SSM_P
SSM_G = 2
SSM_N = 128
CONV_W = 4
CONV_DIM = D_INNER + 2 * SSM_G * SSM_N
SSD_CHUNK = 128
FF_DIM = -(-8 * D_MODEL // (3 * 256)) * 256
EPS = 1e-6

EVEN_PROJ = 4 * 512 + 3 * 512 + 512 + IDX_DIM + IDX_H
EVEN_PROJ_PAD = 4224
ODD_PROJ = POOL_DIM + D_INNER + CONV_DIM + SSM_H
ODD_PROJ_PAD = 2176

LANE = 128
INT_MIN = -(2 ** 31)
NEG_BIG = -1e30
LOG2E = 1.4426950408889634
VMEM_LIMIT = 56 * 1024 * 1024
TM = 256
FF_SPLIT = 2
DSA_CHUNK = 512
SEQ_GROUP = 8


def _cparams(sem):
    return pltpu.CompilerParams(dimension_semantics=sem, vmem_limit_bytes=VMEM_LIMIT)


def _mm(a, b):
    return jnp.dot(a, b, preferred_element_type=f32)


def _mm_nt(a, b):
    return lax.dot_general(a, b, (((1,), (1,)), ((), ())), preferred_element_type=f32)


def _mm_tn(a, b):
    return lax.dot_general(a, b, (((0,), (0,)), ((), ())), preferred_element_type=f32)


def _rms(x, g):
    return x * lax.rsqrt(jnp.mean(x * x, -1, keepdims=True) + EPS) * g


def _silu(x):
    return x / (1.0 + jnp.exp(-x))


def _const_spec(shape):
    nd = len(shape)
    return pl.BlockSpec(shape, lambda *a: (0,) * nd)


def _even_proj_kernel(x_ref, g_ref, w_ref, qg_ref, kg_ref,
                      ret_ref, qa_ref, ka_ref, kab_ref, va_ref, vab_ref, qi_ref, kiwi_ref):
    xb = _rms(x_ref[...], g_ref[...]).astype(bf16)
    ret_ref[...] = _mm(xb, w_ref[:, 0:2048])
    qa = _mm(xb, w_ref[:, 2048:2560])
    ka = _mm(xb, w_ref[:, 2560:3072])
    for h in range(ATT_H):
        sl = slice(h * ATT_DH, (h + 1) * ATT_DH)
        qa_ref[:, sl] = _rms(qa[:, sl], qg_ref[...]).astype(bf16)
        kn = _rms(ka[:, sl], kg_ref[...])
        ka_ref[:, sl] = kn
        kab_ref[:, sl] = kn.astype(bf16)
    va = _mm(xb, w_ref[:, 3072:3584])
    va_ref[...] = va
    vab_ref[...] = va.astype(bf16)
    qi_ref[...] = (_mm(xb, w_ref[:, 3584:4096]) * IDX_DIM ** -0.5).astype(bf16)
    kiwi_ref[...] = _mm(xb, w_ref[:, 4096:4224])


def _even_proj(x, g, w, qg, kg):
    n = x.shape[0]
    row = lambda c: pl.BlockSpec((TM, c), lambda i: (i, 0))
    outs = [(2048, f32), (512, bf16), (512, f32), (512, bf16), (512, f32), (512, bf16), (512, bf16), (LANE, f32)]
    return pl.pallas_call(
        _even_proj_kernel,
        grid=(n // TM,),
        in_specs=[row(D_MODEL), _const_spec((1, D_MODEL)), _const_spec((D_MODEL, EVEN_PROJ_PAD)),
                  _const_spec((1, ATT_DH)), _const_spec((1, ATT_DH))],
        out_specs=[row(c) for c, _ in outs],
        out_shape=[jax.ShapeDtypeStruct((n, c), dt) for c, dt in outs],
        compiler_params=_cparams(("parallel",)),
        name="even_proj",
    )(x, g, w, qg, kg)


def _odd_proj_kernel(x_ref, g_ref, w_ref, u_ref, z_ref, xbc_ref, dt_ref):
    xb = _rms(x_ref[...], g_ref[...]).astype(bf16)
    u_ref[...] = _mm(xb, w_ref[:, 0:512])
    z_ref[...] = _mm(xb, w_ref[:, 512:1024])
    xbc_ref[...] = _mm(xb, w_ref[:, 1024:2048])
    dt_ref[...] = _mm(xb, w_ref[:, 2048:2176])


def _odd_proj(x, g, w):
    n = x.shape[0]
    row = lambda c: pl.BlockSpec((TM, c), lambda i: (i, 0))
    outs = [512, 512, 1024, LANE]
    return pl.pallas_call(
        _odd_proj_kernel,
        grid=(n // TM,),
        in_specs=[row(D_MODEL), _const_spec((1, D_MODEL)), _const_spec((D_MODEL, ODD_PROJ_PAD))],
        out_specs=[row(c) for c in outs],
        out_shape=[jax.ShapeDtypeStruct((n, c), f32) for c in outs],
        compiler_params=_cparams(("parallel",)),
        name="odd_proj",
    )(x, g, w)


def _out_ffn_kernel(x_ref, a_ref, b_ref, wo_ref, g_ref, wg_ref, wu_ref, wd_ref, o_ref):
    half = wo_ref.shape[0] // 2
    x = x_ref[...] + _mm(a_ref[...], wo_ref[0:half, :]) + _mm(b_ref[...], wo_ref[half:, :])
    hb = _rms(x, g_ref[...]).astype(bf16)
    fc = FF_DIM // FF_SPLIT
    ff = None
    for c in range(FF_SPLIT):
        sl = slice(c * fc, (c + 1) * fc)
        act = (_silu(_mm(hb, wg_ref[:, sl])) * _mm(hb, wu_ref[:, sl])).astype(bf16)
        down = _mm(act, wd_ref[sl, :])
        ff = down if ff is None else ff + down
    o_ref[...] = x + ff


def _out_ffn(x, a, b, wo, g, wg, wu, wd):
    n = x.shape[0]
    row = lambda c: pl.BlockSpec((TM, c), lambda i: (i, 0))
    once = lambda shape: pl.BlockSpec(shape, lambda i: (0, 0), pipeline_mode=pl.Buffered(1))
    return pl.pallas_call(
        _out_ffn_kernel,
        grid=(n // TM,),
        in_specs=[row(D_MODEL), row(a.shape[1]), row(b.shape[1]), once(wo.shape), _const_spec((1, D_MODEL)),
                  once(wg.shape), once(wu.shape), once(wd.shape)],
        out_specs=row(D_MODEL),
        out_shape=jax.ShapeDtypeStruct((n, D_MODEL), f32),
        compiler_params=_cparams(("parallel",)),
        name="out_ffn",
    )(x, a, b, wo, g, wg, wu, wd)


def _rope_tables(pos):
    half = RET_DK // 2
    inv = ROPE_BASE ** (-jnp.linspace(0.0, 1.0, half, dtype=f32))
    ang = pos.astype(f32)[:, None] * inv[None, :]
    cos, sin = jnp.cos(ang), jnp.sin(ang)
    return jnp.concatenate([cos, cos], -1), jnp.concatenate([-sin, sin], -1)


def _ret_decay(q):
    log_g = jnp.log1p(-jnp.exp2(-5.0 - jnp.arange(RET_H, dtype=f32)))
    idx = jnp.arange(q, dtype=f32)
    diff = idx[:, None] - idx[None, :]
    dmask = jnp.where(diff[None] >= 0, jnp.exp(log_g[:, None, None] * jnp.maximum(diff, 0.0)[None]), 0.0)
    xi = jnp.exp(log_g[:, None] * (idx + 1.0)[None])
    zeta = jnp.exp(log_g[:, None] * (q - 1.0 - idx)[None])
    g_chunk = jnp.exp(log_g * q)
    return dmask, xi, zeta, g_chunk


def _t5_bucket(rel):
    n = jnp.maximum(rel, 0)
    max_exact = REL_BUCKETS // 2
    nf = jnp.maximum(n, 1).astype(f32)
    large = max_exact + (jnp.log(nf / max_exact) / math.log(REL_MAX_DIST / max_exact)
                         * (REL_BUCKETS - max_exact)).astype(i32)
    large = jnp.minimum(large, REL_BUCKETS - 1)
    return jnp.where(n < max_exact, n, large)


def _bias_lookup(rel_bias, rel):
    onehot = jax.nn.one_hot(_t5_bucket(rel), REL_BUCKETS, dtype=f32)
    return jnp.einsum("...b,bh->...h", onehot, rel_bias.astype(f32), precision=lax.Precision.HIGHEST)


def _rotary(x, c, s):
    return x * c + pltpu.roll(x, RET_DK // 2, 1) * s


def _retention_kernel(q_ref, k_ref, v_ref, g_ref, r0_ref, c_ref, s_ref, dm_ref, xi_ref, zt_ref, gc_ref,
                      *rest):
    o_ref, r_ref = rest[-2:]

    @pl.when(pl.program_id(1) == 0)
    def _():
        r_ref[...] = r0_ref[...]

    cos, sin = c_ref[...], s_ref[...]
    for b in range(q_ref.shape[0]):
        for h in range(RET_H):
            sl = slice(h * RET_DK, (h + 1) * RET_DK)
            qr = _rotary(q_ref[b, :, sl], cos, sin).astype(bf16)
            kr = _rotary(k_ref[b, :, sl], cos, sin) * RET_DK ** -0.5
            vb = v_ref[b, :, sl].astype(bf16)
            r = r_ref[b, h]
            s = _mm_nt(qr, kr.astype(bf16)) * dm_ref[h]
            o = _mm(s.astype(bf16), vb) + _mm(qr, r.astype(bf16)) * xi_ref[h]
            r_ref[b, h] = r * gc_ref[h, 0:1, :] + _mm_tn((kr * zt_ref[h]).astype(bf16), vb)
            o = o * lax.rsqrt(jnp.mean(o * o, -1, keepdims=True) + EPS)
            o_ref[b, :, sl] = (_silu(g_ref[b, :, sl]) * o).astype(bf16)


def _seq_group(nb):
    return math.gcd(nb, SEQ_GROUP)


def _layer_block(tail, layer, group):
    return pl.BlockSpec((None, group) + tail, lambda b, c: (layer, b) + (0,) * len(tail))


def _stacked_out(prev, n_layers, nb, tail, n_inputs, out_index):
    shape = jax.ShapeDtypeStruct((n_layers, nb) + tail, f32)
    if prev is None:
        return shape, [], [], {}
    return shape, [prev], [pl.BlockSpec(memory_space=pl.ANY)], {n_inputs: out_index}


def _retention(ret, r0, layer_in, pos0, prev, layer_out, n_layers):
    nb, t, _ = ret.shape
    q = math.gcd(t, RET_CHUNK)
    grp = _seq_group(nb)
    cos, sin = _rope_tables(pos0 + jnp.arange(t))
    dmask, xi, zeta, g_chunk = _ret_decay(q)
    bcast = lambda a: jnp.broadcast_to(a[:, :, None], (RET_H, a.shape[1], LANE))
    col = lambda j: pl.BlockSpec((grp, q, 512), lambda b, c: (b, c, j))
    tab = pl.BlockSpec((q, LANE), lambda b, c: (c, 0))
    tail = (RET_H, RET_DK, RET_DV)
    in_specs = [col(0), col(1), col(2), col(3), _layer_block(tail, layer_in, grp), tab, tab,
                _const_spec((RET_H, q, q)), _const_spec((RET_H, q, LANE)), _const_spec((RET_H, q, LANE)),
                _const_spec((RET_H, 8, LANE))]
    state_shape, extra, extra_specs, aliases = _stacked_out(prev, n_layers, nb, tail, len(in_specs), 1)
    return pl.pallas_call(
        _retention_kernel,
        grid=(nb // grp, t // q),
        in_specs=in_specs + extra_specs,
        out_specs=[col(0), _layer_block(tail, layer_out, grp)],
        out_shape=[jax.ShapeDtypeStruct((nb, t, 512), bf16), state_shape],
        input_output_aliases=aliases,
        compiler_params=_cparams(("parallel", "arbitrary")),
        name="retention_t%d" % t,
    )(ret, ret, ret, ret, r0, cos, sin, dmask, bcast(xi), bcast(zeta),
      jnp.broadcast_to(g_chunk[:, None, None], (RET_H, 8, LANE)), *extra)


def _sort_key(score):
    bits = lax.bitcast_convert_type(score, i32)
    bits = jnp.where(bits == INT_MIN, 0, bits)
    return jnp.where(bits < 0, bits ^ 0x7FFFFFFF, bits)


def _count(key_ref, nchunk, ck, pred):
    def body(c, acc):
        r0 = pl.multiple_of(c * ck, ck)
        hit = jnp.where(pred(key_ref[pl.ds(r0, ck), :], r0), 1, 0).astype(i32)
        return acc + jnp.sum(hit.reshape(ck // 8, 8, LANE), axis=0)
    acc = lax.fori_loop(0, nchunk, body, jnp.zeros((8, LANE), i32))
    return jnp.sum(acc, axis=0, keepdims=True)


def _select_topk(key_ref, nchunk, ck, topk, idx_bits):
    def bit_step(it, t):
        cand = t + jnp.left_shift(jnp.int32(1), 31 - it)
        cnt = _count(key_ref, nchunk, ck, lambda blk, r0: blk >= cand)
        return jnp.where(cnt >= topk, cand, t)

    t = lax.fori_loop(0, 32, bit_step, jnp.full((1, LANE), INT_MIN, i32))
    t = jnp.maximum(t, INT_MIN + 1)
    c_ge = _count(key_ref, nchunk, ck, lambda blk, r0: blk >= t)
    c_gt = _count(key_ref, nchunk, ck, lambda blk, r0: blk > t)
    surplus = c_ge > topk
    rows = lax.broadcasted_iota(i32, (ck, LANE), 0)

    @pl.when(jnp.max(jnp.where(surplus, 1, 0)) > 0)
    def _():
        want = jnp.where(surplus, topk - c_gt, jnp.int32(2 ** 30))

        def idx_step(it, x):
            cand = x + jnp.left_shift(jnp.int32(1), idx_bits - 1 - it)
            cnt = _count(key_ref, nchunk, ck, lambda blk, r0: jnp.where(blk == t, rows + r0, cand) < cand)
            return jnp.where(cnt < want, cand, x)

        last = lax.fori_loop(0, idx_bits, idx_step, jnp.zeros((1, LANE), i32))

        def demote(c, carry):
            r0 = pl.multiple_of(c * ck, ck)
            blk = key_ref[pl.ds(r0, ck), :]
            drop = jnp.where(blk == t, rows + r0, last) > last
            key_ref[pl.ds(r0, ck), :] = jnp.where(drop, INT_MIN, blk)
            return carry

        lax.fori_loop(0, nchunk, demote, 0)

    return t


def _dsa_prompt_kernel(qa_ref, qit_ref, kiwiq_ref, k_ref, vt_ref, kiwik_ref, bias_ref, o_ref, key_ref, acc_ref,
                       qbd_ref, *, topk, idx_bits, nch, ck):
    qb = pl.program_id(1)
    cb = ck // LANE
    nchunk = (qb + cb) // cb
    wit = jnp.transpose(kiwiq_ref[...])[IDX_DIM:IDX_DIM + IDX_H, :] * IDX_H ** -0.5
    qpos = qb * LANE + lax.broadcasted_iota(i32, (ck, LANE), 1)
    rows = lax.broadcasted_iota(i32, (ck, LANE), 0)

    def score_chunk(c, carry):
        r0 = pl.multiple_of(c * ck, ck)
        kic = kiwik_ref[pl.ds(r0, ck), 0:IDX_DIM].astype(bf16)
        acc = jnp.zeros((ck, LANE), f32)
        for h in range(IDX_H):
            s = _mm(kic, qit_ref[:, h * LANE:(h + 1) * LANE])
            acc = acc + jnp.maximum(s, 0.0) * wit[h:h + 1, :]
        key_ref[pl.ds(r0, ck), :] = jnp.where(rows + r0 <= qpos, _sort_key(acc), INT_MIN)
        return carry

    lax.fori_loop(0, nchunk, score_chunk, 0)
    t = _select_topk(key_ref, nchunk, ck, topk, idx_bits)

    npair = ATT_H // 2
    for pr in range(npair):
        qbd_ref[pr] = jnp.zeros((2 * ATT_DH, 2 * LANE), bf16)
        for hh in range(2):
            sl = slice((2 * pr + hh) * ATT_DH, (2 * pr + hh + 1) * ATT_DH)
            qbd_ref[pr, hh * ATT_DH:(hh + 1) * ATT_DH, hh * LANE:(hh + 1) * LANE] = (
                jnp.transpose(qa_ref[:, sl].astype(f32)).astype(bf16))
    acc_ref[...] = jnp.zeros_like(acc_ref)
    nfar = jnp.maximum(qb - 1, 0) // cb

    pairs = range(npair)

    def att_chunks(near, unroll):
        def body(i, carry):
            ms, ls = list(carry[0]), list(carry[1])
            cs = [i * unroll + u for u in range(unroll)]
            r0s = [pl.multiple_of(c * ck, ck) for c in cs]
            qks = [[_mm(k_ref[pl.ds(r0, ck), pr * 2 * ATT_DH:(pr + 1) * 2 * ATT_DH], qbd_ref[pr]) for pr in pairs]
                   for r0 in r0s]
            for c, r0, qk in zip(cs, r0s, qks):
                neg1 = jnp.where(key_ref[pl.ds(r0, ck), :] >= t, 0.0, NEG_BIG)
                negm = jnp.concatenate([neg1, neg1], axis=1)
                alphas, ps = [], []
                for pr in pairs:
                    s = qk[pr] * (ATT_DH ** -0.5 * LOG2E) + negm
                    if near:
                        s = s + jnp.concatenate(
                            [bias_ref[jnp.clip(qb - (c * cb + j), 0, 2), pr] for j in range(cb)], axis=0)
                    m_new = jnp.maximum(ms[pr], jnp.max(s, axis=0, keepdims=True))
                    alpha = jnp.exp2(ms[pr] - m_new)
                    p = jnp.exp2(s - m_new)
                    ls[pr] = ls[pr] * alpha + jnp.sum(p, axis=0, keepdims=True)
                    ms[pr] = m_new
                    alphas.append(alpha)
                    ps.append(p.astype(bf16))
                pvs = [_mm(vt_ref[pr * nch + c], ps[pr]) for pr in pairs]
                for pr in pairs:
                    for hh in range(2):
                        d = slice(hh * LANE, (hh + 1) * LANE)
                        acc_ref[2 * pr + hh] = acc_ref[2 * pr + hh] * alphas[pr][:, d] + pvs[pr][d, d]
            return tuple(ms), tuple(ls)
        return body

    carry = ((jnp.full((1, 2 * LANE), NEG_BIG, f32),) * npair, (jnp.zeros((1, 2 * LANE), f32),) * npair)
    carry = lax.fori_loop(0, nfar // 2, att_chunks(False, 2), carry)
    carry = lax.fori_loop(nfar // 2 * 2, nfar, att_chunks(False, 1), carry)
    _, ls = lax.fori_loop(nfar, nchunk, att_chunks(True, 1), carry)
    for h in range(ATT_H):
        l = ls[h // 2][:, (h % 2) * LANE:(h % 2 + 1) * LANE]
        o_ref[:, h * ATT_DH:(h + 1) * ATT_DH] = jnp.transpose(acc_ref[h] / l).astype(bf16)


def _bias_tiles(rel_bias, nd):
    j = jnp.arange(LANE)[:, None]
    i = jnp.arange(LANE)[None, :]
    rel = jnp.arange(nd)[:, None, None] * LANE + (i - j)[None]
    return jnp.moveaxis(_bias_lookup(rel_bias, rel), -1, 1)


def _dsa_prompt(qa, kab, vab, qi, kiwi, rel_bias, nb, t):
    nqb = t // LANE
    topk = min(TOPK_MAX, t // 4)
    assert REL_MAX_DIST <= LANE + 1
    bias = _bias_tiles(rel_bias, 3)
    bias = (bias - bias[2:3]) * LOG2E
    npair = ATT_H // 2
    bias = bias.reshape(3, npair, 2, LANE, LANE).transpose(0, 1, 3, 2, 4).reshape(3, npair, LANE, 2 * LANE)
    ck = math.gcd(t, DSA_CHUNK)
    nch = t // ck
    vt = vab[:nb * t].reshape(nb, nch, ck, ATT_H * ATT_DH).transpose(0, 3, 1, 2)
    vt = vt.reshape(nb, npair, 2 * ATT_DH, nch, ck).transpose(0, 1, 3, 2, 4).reshape(nb * npair * nch, 2 * ATT_DH, ck)
    qit = qi[:nb * t].reshape(nb * nqb, LANE, IDX_H, IDX_DIM).transpose(0, 3, 2, 1)
    qit = qit.reshape(nb * nqb, IDX_DIM, IDX_H * LANE)
    qrow = lambda c: pl.BlockSpec((LANE, c), lambda b, q: (b * nqb + q, 0))
    seq = lambda c: pl.BlockSpec((t, c), lambda b, q: (b, 0))
    kern = functools.partial(_dsa_prompt_kernel, topk=topk, idx_bits=max(1, (t - 1).bit_length()), nch=nch, ck=ck)
    return pl.pallas_call(
        kern,
        grid=(nb, nqb),
        in_specs=[qrow(512), pl.BlockSpec((None, IDX_DIM, IDX_H * LANE), lambda b, q: (b * nqb + q, 0, 0)),
                  qrow(LANE), seq(512),
                  pl.BlockSpec((npair * nch, 2 * ATT_DH, ck), lambda b, q: (b, 0, 0)), seq(LANE),
                  _const_spec((3, npair, LANE, 2 * LANE))],
        out_specs=qrow(512),
        out_shape=jax.ShapeDtypeStruct((nb * t, 512), bf16),
        scratch_shapes=[pltpu.VMEM((t, LANE), i32), pltpu.VMEM((ATT_H, ATT_DH, LANE), f32),
                        pltpu.VMEM((npair, 2 * ATT_DH, 2 * LANE), bf16)],
        compiler_params=_cparams(("parallel", "arbitrary")),
        name="dsa_prompt",
    )(qa, qit, kiwi, kab, vt, kiwi, bias)


def _softplus(x):
    return jnp.maximum(x, 0.0) + jnp.log1p(jnp.exp(-jnp.abs(x)))


def _cumsum_rows(tri, a):
    hi = a.astype(bf16)
    r1 = a - hi.astype(f32)
    mid = r1.astype(bf16)
    lo = (r1 - mid.astype(f32)).astype(bf16)
    return _mm(tri, hi) + _mm(tri, mid) + _mm(tri, lo)


def _odd_mixer_kernel(*refs, q, pos0):
    seq_in, shared, seq_out = refs[:7], refs[7:16], refs[-6:]
    for b in range(seq_in[0].shape[0]):
        _odd_mixer_seq(*(r.at[b] for r in seq_in), *shared, *(r.at[b] for r in seq_out), q=q, pos0=pos0)


def _odd_mixer_seq(u_ref, z_ref, xbc_ref, dt_ref, pp_ref, cp_ref, h0_ref, pw_ref, ps_ref, cw_ref, cb_ref,
                   dtb_ref, alog_ref, dsk_ref, nrm_ref, tri_ref, po_ref, y_ref, h_ref, ubuf, xbuf, ybuf, *, q, pos0):
    c = pl.program_id(1)

    @pl.when(c == 0)
    def _():
        ubuf[0:1, :] = jnp.zeros((1, POOL_DIM), f32)
        ubuf[1:16, :] = pp_ref[...]
        xbuf[0:8 - (CONV_W - 1), :] = jnp.zeros((8 - (CONV_W - 1), CONV_DIM), f32)
        xbuf[8 - (CONV_W - 1):8, :] = cp_ref[...]
        h_ref[...] = h0_ref[...]

    pos = pos0 + c * q + lax.broadcasted_iota(i32, (q, LANE), 0)
    causal = lax.broadcasted_iota(i32, (q, q), 0) >= lax.broadcasted_iota(i32, (q, q), 1)

    u = u_ref[...]
    ubuf[16:16 + q, :] = u
    for g, w in enumerate(POOL_WINDOWS):
        sl = slice(g * POOL_GC, (g + 1) * POOL_GC)
        acc = u[:, sl]
        for k in range(1, w):
            acc = acc + ubuf[16 - k:16 - k + q, sl]
        d = acc / jnp.minimum(pos + 1, w).astype(f32) - u[:, sl]
        po_ref[:, sl] = (_mm(d.astype(bf16), pw_ref[g]) * ps_ref[:, sl]).astype(bf16)
    ubuf[0:16, :] = ubuf[q:q + 16, :]

    xbuf[8:8 + q, :] = xbc_ref[...]
    conv = cb_ref[...]
    for j in range(CONV_W):
        off = 8 - (CONV_W - 1) + j
        conv = conv + xbuf[off:off + q, :] * cw_ref[j:j + 1, :]
    xbuf[0:8, :] = xbuf[q:q + 8, :]
    act = _silu(conv)
    xs = act[:, 0:D_INNER]

    dt = _softplus(dt_ref[...] + dtb_ref[...])
    a = dt * (-jnp.exp(alog_ref[...]))
    cs = _cumsum_rows(tri_ref[...], a)
    cs_t = jnp.transpose(cs)
    dt_t = jnp.transpose(dt)
    cs_last = cs[q - 1:q, :]
    w_end = jnp.exp(cs_last - cs) * dt
    ecs = jnp.exp(cs)
    hpg = SSM_H // SSM_G
    for g in range(SSM_G):
        bm = act[:, D_INNER + g * SSM_N:D_INNER + (g + 1) * SSM_N].astype(bf16)
        cm = act[:, D_INNER + (SSM_G + g) * SSM_N:D_INNER + (SSM_G + g + 1) * SSM_N].astype(bf16)
        cb = _mm_nt(cm, bm)
        for hh in range(hpg):
            h = g * hpg + hh
            psl = slice(h * SSM_P, (h + 1) * SSM_P)
            seg = cs[:, h:h + 1] - cs_t[h:h + 1, :]
            lm = jnp.exp(jnp.where(causal, seg, NEG_BIG))
            sc = cb * lm * dt_t[h:h + 1, :]
            xh = xs[:, psl]
            hs = h_ref[h]
            yh = _mm(sc.astype(bf16), xh.astype(bf16)) + _mm_nt(cm, hs.astype(bf16)) * ecs[:, h:h + 1]
            h_ref[h] = hs * jnp.exp(cs_last[:, h:h + 1]) + _mm_tn((xh * w_end[:, h:h + 1]).astype(bf16), bm)
            ybuf[:, psl] = yh
    y = (ybuf[...] + dsk_ref[...] * xs) * _silu(z_ref[...])
    gw = D_INNER // SSM_G
    for g in range(SSM_G):
        sl = slice(g * gw, (g + 1) * gw)
        yg = y[:, sl]
        y_ref[:, sl] = (yg * lax.rsqrt(jnp.mean(yg * yg, -1, keepdims=True) + EPS) * nrm_ref[:, sl]).astype(bf16)


def _pad_lanes(v):
    return jnp.pad(v.astype(f32), (0, LANE - v.shape[0]))[None, :]


def _odd_mixer(u, z, xbc, dt, pool_prev, conv_prev, h0, layer_in, prm, pos0, prev, layer_out, n_layers):
    pool_w, pool_scale, conv_w, conv_b, dt_bias, a_log, d_skip, ssm_norm = prm
    nb, t, _ = u.shape
    q = math.gcd(t, SSD_CHUNK)
    nc = t // q
    grp = _seq_group(nb)
    row = lambda c: pl.BlockSpec((grp, q, c), lambda b, i: (b, i, 0))
    lead = (nb, t)
    tri = jnp.tril(jnp.ones((q, q), bf16))
    tail = (SSM_H, SSM_P, SSM_N)
    in_specs = [row(POOL_DIM), row(D_INNER), row(CONV_DIM), row(LANE),
                _layer_block((POOL_PAST, POOL_DIM), layer_in, grp),
                _layer_block((CONV_W - 1, CONV_DIM), layer_in, grp), _layer_block(tail, layer_in, grp),
                _const_spec((POOL_GROUPS, POOL_GC, POOL_GC)), _const_spec((1, POOL_DIM)),
                _const_spec((CONV_W, CONV_DIM)), _const_spec((1, CONV_DIM)),
                _const_spec((1, LANE)), _const_spec((1, LANE)),
                _const_spec((1, D_INNER)), _const_spec((1, D_INNER)), _const_spec((q, q))]
    state_shape, extra, extra_specs, aliases = _stacked_out(prev, n_layers, nb, tail, len(in_specs), 2)
    return pl.pallas_call(
        functools.partial(_odd_mixer_kernel, q=q, pos0=pos0),
        grid=(nb // grp, nc),
        in_specs=in_specs + extra_specs,
        out_specs=[row(POOL_DIM), row(D_INNER), _layer_block(tail, layer_out, grp)],
        out_shape=[jax.ShapeDtypeStruct(lead + (POOL_DIM,), bf16), jax.ShapeDtypeStruct(lead + (D_INNER,), bf16),
                   state_shape],
        input_output_aliases=aliases,
        scratch_shapes=[pltpu.VMEM((grp, q + 16, POOL_DIM), f32), pltpu.VMEM((grp, q + 8, CONV_DIM), f32),
                        pltpu.VMEM((grp, q, D_INNER), f32)],
        compiler_params=_cparams(("parallel", "arbitrary")),
        name="odd_mixer_t%d" % t,
    )(u, z, xbc, dt, pool_prev, conv_prev, h0, pool_w.astype(bf16), pool_scale[None, :], conv_w, conv_b[None, :],
      _pad_lanes(dt_bias), _pad_lanes(a_log), jnp.repeat(d_skip, SSM_P)[None, :], ssm_norm[None, :], tri, *extra)


def _dsa_sample_score_kernel(pt_ref, qi_ref, kiwi_ref, *rest, npg):
    pages, o_ref = rest[:npg], rest[npg]
    ts = qi_ref.shape[0]
    kiwi = kiwi_ref[...]
    wi = kiwi[:, IDX_DIM:IDX_DIM + IDX_H] * IDX_H ** -0.5
    ki_new = jnp.transpose(jnp.concatenate([kiwi, jnp.zeros((PAGE - ts, LANE), f32)], axis=0))[0:IDX_DIM, :]
    ki = jnp.concatenate([p[...] for p in pages] + [ki_new], axis=1).astype(bf16)
    nk = ki.shape[1]
    acc = jnp.zeros((ts, nk), f32)
    for h in range(IDX_H):
        s = _mm(qi_ref[:, h * IDX_DIM:(h + 1) * IDX_DIM], ki)
        acc = acc + jnp.maximum(s, 0.0) * wi[:, h:h + 1]
    col = lax.broadcasted_iota(i32, (ts, nk), 1)
    row = lax.broadcasted_iota(i32, (ts, nk), 0)
    o_ref[...] = jnp.where(col <= npg * PAGE + row, _sort_key(acc), INT_MIN)


def _select_kernel(k_ref, o_ref, key_ref, *, topk, idx_bits):
    key_ref[...] = k_ref[...]
    t = _select_topk(key_ref, 1, key_ref.shape[0], topk, idx_bits)
    o_ref[...] = jnp.where(key_ref[...] >= t, 1.0, 0.0)


def _dsa_sample_attn_kernel(pt_ref, q_ref, kn_ref, vn_ref, sel_ref, bias_ref, spread_ref, *rest, npg):
    kpages, vpages, o_ref = rest[:npg], rest[npg:2 * npg], rest[2 * npg]
    pad = jnp.zeros((PAGE * ATT_H - kn_ref.shape[0], ATT_DH), f32)
    kx = jnp.concatenate([p[...] for p in kpages] + [kn_ref[...], pad], axis=0).astype(bf16)
    vx = jnp.concatenate([p[...] for p in vpages] + [vn_ref[...], pad], axis=0).astype(bf16)
    selx = jnp.concatenate([_mm(sel_ref[:, j * PAGE:(j + 1) * PAGE], spread_ref[...]) for j in range(npg + 1)],
                           axis=1)
    s = _mm_nt(q_ref[...], kx) * ATT_DH ** -0.5 + jnp.where(selx > 0.5, bias_ref[...], NEG_BIG)
    p = jnp.exp(s - jnp.max(s, axis=-1, keepdims=True))
    o_ref[...] = (_mm(p.astype(bf16), vx) / jnp.sum(p, axis=-1, keepdims=True)).astype(bf16)


def _dsa_sample(qa, ka, va, qi, kiwi, cache_k, cache_v, cache_ki, layer, page_table, rel_bias):
    db, ts = qa.shape[:2]
    npg = page_table.shape[1]
    n_past = npg * PAGE
    nk = n_past + PAGE
    nq = db * ts
    topk = min(TOPK_MAX, (n_past + ts) // 4)
    hd = ATT_H * ATT_DH
    ki_page = lambda j: pl.BlockSpec((None, None, IDX_DIM, PAGE), lambda b, pt: (layer, pt[b, j], 0, 0))
    seq = lambda r, c: pl.BlockSpec((None, r, c), lambda b, pt: (b, 0, 0))

    keys = pl.pallas_call(
        functools.partial(_dsa_sample_score_kernel, npg=npg),
        grid_spec=pltpu.PrefetchScalarGridSpec(
            num_scalar_prefetch=1, grid=(db,),
            in_specs=[seq(ts, IDX_H * IDX_DIM), seq(ts, LANE)] + [ki_page(j) for j in range(npg)],
            out_specs=seq(ts, nk)),
        out_shape=jax.ShapeDtypeStruct((db, ts, nk), i32),
        compiler_params=_cparams(("parallel",)),
        name="dsa_sample_score",
    )(page_table, qi, kiwi, *([jnp.swapaxes(cache_ki, 2, 3)] * npg))

    col = pl.BlockSpec((nk, LANE), lambda i: (0, i))
    sel = pl.pallas_call(
        functools.partial(_select_kernel, topk=topk, idx_bits=max(1, (nk - 1).bit_length())),
        grid=(nq // LANE,),
        in_specs=[col],
        out_specs=col,
        out_shape=jax.ShapeDtypeStruct((nk, nq), f32),
        scratch_shapes=[pltpu.VMEM((nk, LANE), i32)],
        compiler_params=_cparams(("parallel",)),
        name="dsa_sample_select",
    )(keys.reshape(nq, nk).T)
    sel = jnp.repeat(sel.T.reshape(db, ts, nk), ATT_H, axis=1).astype(bf16)
    rel = n_past + jnp.arange(ts)[:, None] - jnp.arange(nk)[None, :]
    bias = jnp.repeat(jnp.moveaxis(_bias_lookup(rel_bias, rel), -1, 1), ATT_H, axis=-1)
    same_head = jnp.arange(nk * ATT_H)[None, :] % ATT_H == jnp.arange(ATT_H)[:, None]
    bias = jnp.where(same_head[None], bias, NEG_BIG).reshape(ts * ATT_H, nk * ATT_H)
    spread = (jnp.arange(PAGE * ATT_H)[None, :] // ATT_H == jnp.arange(PAGE)[:, None]).astype(bf16)

    n_pages = cache_k.shape[1]
    rows = PAGE * ATT_H
    kv_page = lambda j: pl.BlockSpec((rows, ATT_DH), lambda b, pt: (layer * n_pages + pt[b, j], 0))
    const = lambda a: pl.BlockSpec(a.shape, lambda b, pt: (0, 0))
    as_rows = lambda a: a.reshape(db, ts * ATT_H, ATT_DH)
    out = pl.pallas_call(
        functools.partial(_dsa_sample_attn_kernel, npg=npg),
        grid_spec=pltpu.PrefetchScalarGridSpec(
            num_scalar_prefetch=1, grid=(db,),
            in_specs=[seq(ts * ATT_H, ATT_DH)] * 3 + [seq(ts * ATT_H, nk), const(bias), const(spread)]
                     + [kv_page(j) for j in range(npg)] * 2,
            out_specs=seq(ts * ATT_H, ATT_DH)),
        out_shape=jax.ShapeDtypeStruct((db, ts * ATT_H, ATT_DH), bf16),
        compiler_params=_cparams(("parallel",)),
        name="dsa_sample_attn",
    )(page_table, as_rows(qa), as_rows(ka), as_rows(va), sel, bias, spread,
      *([cache_k.reshape(-1, ATT_DH)] * npg), *([cache_v.reshape(-1, ATT_DH)] * npg))
    return out.reshape(db, ts, hd)


def kernel(x_prompt, x_sample, cache_k, cache_v, cache_kidx, state_ret, state_pool, state_conv, state_ssm,
           page_table, norm_mix, norm_ffn, w_in_even, w_out_even, q_norm, k_norm, rel_bias,
           w_in_odd, w_out_odd, pool_w, pool_scale, conv_w, conv_b, dt_bias, a_log, d_skip, ssm_norm,
           w_gate, w_up, w_down):
    bp, sp, d = x_prompt.shape
    db, ts, _ = x_sample.shape
    n_p, n_s = bp * sp, db * ts
    assert d == D_MODEL and n_p % TM == 0 and n_s % TM == 0 and sp % LANE == 0 and sp >= POOL_PAST
    n_past = page_table.shape[1] * PAGE
    n_even, n_odd = (DEPTH + 1) // 2, DEPTH // 2
    hd = ATT_H * ATT_DH
    xp, xs = x_prompt.reshape(n_p, d), x_sample.reshape(n_s, d)
    seqs = lambda a: a.reshape(db, ts, a.shape[-1])
    pseqs = lambda a: a.reshape(bp, sp, a.shape[-1])
    last = lambda a, n: a.reshape(bp, sp, a.shape[-1])[:, sp - n:]
    tail = lambda prev, cur, n: jnp.concatenate([prev.astype(f32), cur], axis=1)[:, -n:]
    zeros = lambda *s: jnp.zeros((1, bp) + s, f32)
    outs = [[] for _ in range(14)]
    rstate_p = rstate_s = h_p = h_s = None
    for l in range(DEPTH):
        w_ffn = (norm_ffn[l][None], w_gate[l].astype(bf16), w_up[l].astype(bf16), w_down[l].astype(bf16))
        if l % 2 == 0:
            i = l // 2
            w_in = jnp.pad(w_in_even[i], ((0, 0), (0, EVEN_PROJ_PAD - EVEN_PROJ))).astype(bf16)
            prm = (norm_mix[l][None], w_in, q_norm[i][None], k_norm[i][None])
            ret_p, qa_p, ka_p, kab_p, va_p, vab_p, qi_p, kiwi_p = _even_proj(xp, *prm)
            ret_s, qa_s, ka_s, _, va_s, _, qi_s, kiwi_s = _even_proj(xs, *prm)
            mix_a_p, rstate_p = _retention(pseqs(ret_p), zeros(RET_H, RET_DK, RET_DV), 0, 0, rstate_p, i, n_even)
            mix_a_s, rstate_s = _retention(seqs(ret_s), state_ret, i, n_past, rstate_s, i, n_even)
            mix_b_p = _dsa_prompt(qa_p, kab_p, vab_p, qi_p, kiwi_p, rel_bias, bp, sp)
            mix_b_s = _dsa_sample(seqs(qa_s), seqs(ka_s), seqs(va_s), seqs(qi_s), seqs(kiwi_s),
                                  cache_k, cache_v, cache_kidx, i, page_table, rel_bias)
            w_out = w_out_even[i].astype(bf16)
            new = [ka_p.reshape(bp, sp, ATT_H, ATT_DH), va_p.reshape(bp, sp, ATT_H, ATT_DH),
                   kiwi_p.reshape(bp, sp, LANE)[..., :IDX_DIM], None, None, None, None,
                   ka_s.reshape(db, ts, ATT_H, ATT_DH), va_s.reshape(db, ts, ATT_H, ATT_DH),
                   seqs(kiwi_s)[..., :IDX_DIM], None, None, None, None]
        else:
            j = l // 2
            w_in = jnp.pad(w_in_odd[j], ((0, 0), (0, ODD_PROJ_PAD - ODD_PROJ))).astype(bf16)
            u_p, z_p, xbc_p, dt_p = _odd_proj(xp, norm_mix[l][None], w_in)
            u_s, z_s, xbc_s, dt_s = _odd_proj(xs, norm_mix[l][None], w_in)
            prm = (pool_w[j], pool_scale[j], conv_w[j], conv_b[j], dt_bias[j], a_log[j], d_skip[j], ssm_norm[j])
            mix_a_p, mix_b_p, h_p = _odd_mixer(
                pseqs(u_p), pseqs(z_p), pseqs(xbc_p), pseqs(dt_p), zeros(POOL_PAST, POOL_DIM),
                zeros(CONV_W - 1, CONV_DIM), zeros(SSM_H, SSM_P, SSM_N), 0, prm, 0, h_p, j, n_odd)
            mix_a_s, mix_b_s, h_s = _odd_mixer(
                seqs(u_s), seqs(z_s), seqs(xbc_s), seqs(dt_s), state_pool, state_conv, state_ssm, j,
                prm, n_past, h_s, j, n_odd)
            w_out = w_out_odd[j].astype(bf16)
            new = [None, None, None, None, last(u_p, POOL_PAST), last(xbc_p, CONV_W - 1), None,
                   None, None, None, None,
                   tail(state_pool[j], seqs(u_s), POOL_PAST), tail(state_conv[j], seqs(xbc_s), CONV_W - 1), None]
        for acc, leaf in zip(outs, new):
            if leaf is not None:
                acc.append(leaf)
        xp = _out_ffn(xp, mix_a_p.reshape(n_p, -1), mix_b_p.reshape(n_p, -1), w_out, *w_ffn)
        xs = _out_ffn(xs, mix_a_s.reshape(n_s, -1), mix_b_s.reshape(n_s, -1), w_out, *w_ffn)
    leaves = [jnp.stack(a) if a else None for a in outs]
    leaves[3], leaves[6], leaves[10], leaves[13] = rstate_p, h_p, rstate_s, h_s
    return (xp.reshape(bp, sp, d), xs.reshape(db, ts, d)) + tuple(leaves)
```

```python
import functools
import math

import jax
import jax.numpy as jnp
import numpy as np
from jax import lax
from jax.experimental import pallas as pl
from jax.experimental.pallas import tpu as pltpu

f32 = jnp.float32
bf16 = jnp.bfloat16
i32 = jnp.int32

D_MODEL = 1024
DEPTH = 4
PAGE = 128
RET_H, RET_DK, RET_DV, RET_CHUNK = 4, 128, 128, 128
ROPE_BASE = 10000.0
ATT_H, ATT_DH, ATT_BLOCK = 4, 128, 128
IDX_H, IDX_DIM = 8, 64
TOPK_MAX = 256
REL_BUCKETS, REL_MAX_DIST = 32, 128
POOL_WINDOWS = (2, 4, 8, 16)
POOL_GROUPS = 4
POOL_DIM = D_MODEL // 2
POOL_GC = POOL_DIM // POOL_GROUPS
POOL_PAST = 15
D_INNER = D_MODEL // 2
SSM_P = 64
SSM_H = D_INNER // SSM_P
SSM_G = 2
SSM_N = 128
CONV_W = 4
CONV_DIM = D_INNER + 2 * SSM_G * SSM_N
SSD_CHUNK = 128
FF_DIM = -(-8 * D_MODEL // (3 * 256)) * 256
EPS = 1e-6

EVEN_PROJ = 4 * 512 + 3 * 512 + 512 + IDX_DIM + IDX_H
EVEN_PROJ_PAD = 4224
ODD_PROJ = POOL_DIM + D_INNER + CONV_DIM + SSM_H
ODD_PROJ_PAD = 2176

LANE = 128
INT_MIN = -(2 ** 31)
NEG_BIG = -1e30
LOG2E = 1.4426950408889634
VMEM_LIMIT = 56 * 1024 * 1024
TM = 256
FF_SPLIT = 2
DSA_CHUNK = 512
SEQ_GROUP = 8


def _cparams(sem):
    return pltpu.CompilerParams(dimension_semantics=sem, vmem_limit_bytes=VMEM_LIMIT)


def _mm(a, b):
    return jnp.dot(a, b, preferred_element_type=f32)


def _mm_nt(a, b):
    return lax.dot_general(a, b, (((1,), (1,)), ((), ())), preferred_element_type=f32)


def _mm_tn(a, b):
    return lax.dot_general(a, b, (((0,), (0,)), ((), ())), preferred_element_type=f32)


def _rms(x, g):
    return x * lax.rsqrt(jnp.mean(x * x, -1, keepdims=True) + EPS) * g


def _silu(x):
    return x / (1.0 + jnp.exp(-x))


def _const_spec(shape):
    nd = len(shape)
    return pl.BlockSpec(shape, lambda *a: (0,) * nd)


def _even_proj_kernel(x_ref, g_ref, w_ref, qg_ref, kg_ref,
                      ret_ref, qa_ref, ka_ref, kab_ref, va_ref, vab_ref, qi_ref, kiwi_ref):
    xb = _rms(x_ref[...], g_ref[...]).astype(bf16)
    ret_ref[...] = _mm(xb, w_ref[:, 0:2048])
    qa = _mm(xb, w_ref[:, 2048:2560])
    ka = _mm(xb, w_ref[:, 2560:3072])
    for h in range(ATT_H):
        sl = slice(h * ATT_DH, (h + 1) * ATT_DH)
        qa_ref[:, sl] = _rms(qa[:, sl], qg_ref[...]).astype(bf16)
        kn = _rms(ka[:, sl], kg_ref[...])
        ka_ref[:, sl] = kn
        kab_ref[:, sl] = kn.astype(bf16)
    va = _mm(xb, w_ref[:, 3072:3584])
    va_ref[...] = va
    vab_ref[...] = va.astype(bf16)
    qi_ref[...] = (_mm(xb, w_ref[:, 3584:4096]) * IDX_DIM ** -0.5).astype(bf16)
    kiwi_ref[...] = _mm(xb, w_ref[:, 4096:4224])


def _even_proj(x, g, w, qg, kg):
    n = x.shape[0]
    row = lambda c: pl.BlockSpec((TM, c), lambda i: (i, 0))
    outs = [(2048, f32), (512, bf16), (512, f32), (512, bf16), (512, f32), (512, bf16), (512, bf16), (LANE, f32)]
    return pl.pallas_call(
        _even_proj_kernel,
        grid=(n // TM,),
        in_specs=[row(D_MODEL), _const_spec((1, D_MODEL)), _const_spec((D_MODEL, EVEN_PROJ_PAD)),
                  _const_spec((1, ATT_DH)), _const_spec((1, ATT_DH))],
        out_specs=[row(c) for c, _ in outs],
        out_shape=[jax.ShapeDtypeStruct((n, c), dt) for c, dt in outs],
        compiler_params=_cparams(("parallel",)),
        name="even_proj",
    )(x, g, w, qg, kg)


def _odd_proj_kernel(x_ref, g_ref, w_ref, u_ref, z_ref, xbc_ref, dt_ref):
    xb = _rms(x_ref[...], g_ref[...]).astype(bf16)
    u_ref[...] = _mm(xb, w_ref[:, 0:512])
    z_ref[...] = _mm(xb, w_ref[:, 512:1024])
    xbc_ref[...] = _mm(xb, w_ref[:, 1024:2048])
    dt_ref[...] = _mm(xb, w_ref[:, 2048:2176])


def _odd_proj(x, g, w):
    n = x.shape[0]
    row = lambda c: pl.BlockSpec((TM, c), lambda i: (i, 0))
    outs = [512, 512, 1024, LANE]
    return pl.pallas_call(
        _odd_proj_kernel,
        grid=(n // TM,),
        in_specs=[row(D_MODEL), _const_spec((1, D_MODEL)), _const_spec((D_MODEL, ODD_PROJ_PAD))],
        out_specs=[row(c) for c in outs],
        out_shape=[jax.ShapeDtypeStruct((n, c), f32) for c in outs],
        compiler_params=_cparams(("parallel",)),
        name="odd_proj",
    )(x, g, w)


def _out_ffn_kernel(x_ref, a_ref, b_ref, wo_ref, g_ref, wg_ref, wu_ref, wd_ref, o_ref):
    half = wo_ref.shape[0] // 2
    x = x_ref[...] + _mm(a_ref[...], wo_ref[0:half, :]) + _mm(b_ref[...], wo_ref[half:, :])
    hb = _rms(x, g_ref[...]).astype(bf16)
    fc = FF_DIM // FF_SPLIT
    ff = None
    for c in range(FF_SPLIT):
        sl = slice(c * fc, (c + 1) * fc)
        act = (_silu(_mm(hb, wg_ref[:, sl])) * _mm(hb, wu_ref[:, sl])).astype(bf16)
        down = _mm(act, wd_ref[sl, :])
        ff = down if ff is None else ff + down
    o_ref[...] = x + ff


def _out_ffn(x, a, b, wo, g, wg, wu, wd):
    n = x.shape[0]
    row = lambda c: pl.BlockSpec((TM, c), lambda i: (i, 0))
    once = lambda shape: pl.BlockSpec(shape, lambda i: (0, 0), pipeline_mode=pl.Buffered(1))
    return pl.pallas_call(
        _out_ffn_kernel,
        grid=(n // TM,),
        in_specs=[row(D_MODEL), row(a.shape[1]), row(b.shape[1]), once(wo.shape), _const_spec((1, D_MODEL)),
                  once(wg.shape), once(wu.shape), once(wd.shape)],
        out_specs=row(D_MODEL),
        out_shape=jax.ShapeDtypeStruct((n, D_MODEL), f32),
        compiler_params=_cparams(("parallel",)),
        name="out_ffn",
    )(x, a, b, wo, g, wg, wu, wd)


def _rope_tables(pos):
    half = RET_DK // 2
    inv = ROPE_BASE ** (-jnp.linspace(0.0, 1.0, half, dtype=f32))
    ang = pos.astype(f32)[:, None] * inv[None, :]
    cos, sin = jnp.cos(ang), jnp.sin(ang)
    return jnp.concatenate([cos, cos], -1), jnp.concatenate([-sin, sin], -1)


def _ret_decay(q):
    log_g = jnp.log1p(-jnp.exp2(-5.0 - jnp.arange(RET_H, dtype=f32)))
    idx = jnp.arange(q, dtype=f32)
    diff = idx[:, None] - idx[None, :]
    dmask = jnp.where(diff[None] >= 0, jnp.exp(log_g[:, None, None] * jnp.maximum(diff, 0.0)[None]), 0.0)
    xi = jnp.exp(log_g[:, None] * (idx + 1.0)[None])
    zeta = jnp.exp(log_g[:, None] * (q - 1.0 - idx)[None])
    g_chunk = jnp.exp(log_g * q)
    return dmask, xi, zeta, g_chunk


def _t5_bucket(rel):
    n = jnp.maximum(rel, 0)
    max_exact = REL_BUCKETS // 2
    nf = jnp.maximum(n, 1).astype(f32)
    large = max_exact + (jnp.log(nf / max_exact) / math.log(REL_MAX_DIST / max_exact)
                         * (REL_BUCKETS - max_exact)).astype(i32)
    large = jnp.minimum(large, REL_BUCKETS - 1)
    return jnp.where(n < max_exact, n, large)


def _bias_lookup(rel_bias, rel):
    onehot = jax.nn.one_hot(_t5_bucket(rel), REL_BUCKETS, dtype=f32)
    return jnp.einsum("...b,bh->...h", onehot, rel_bias.astype(f32), precision=lax.Precision.HIGHEST)


def _rotary(x, c, s):
    return x * c + pltpu.roll(x, RET_DK // 2, 1) * s


def _retention_kernel(q_ref, k_ref, v_ref, g_ref, r0_ref, c_ref, s_ref, dm_ref, xi_ref, zt_ref, gc_ref,
                      *rest):
    o_ref, r_ref = rest[-2:]

    @pl.when(pl.program_id(1) == 0)
    def _():
        r_ref[...] = r0_ref[...]

    cos, sin = c_ref[...], s_ref[...]
    for b in range(q_ref.shape[0]):
        for h in range(RET_H):
            sl = slice(h * RET_DK, (h + 1) * RET_DK)
            qr = _rotary(q_ref[b, :, sl], cos, sin).astype(bf16)
            kr = _rotary(k_ref[b, :, sl], cos, sin) * RET_DK ** -0.5
            vb = v_ref[b, :, sl].astype(bf16)
            r = r_ref[b, h]
            s = _mm_nt(qr, kr.astype(bf16)) * dm_ref[h]
            o = _mm(s.astype(bf16), vb) + _mm(qr, r.astype(bf16)) * xi_ref[h]
            r_ref[b, h] = r * gc_ref[h, 0:1, :] + _mm_tn((kr * zt_ref[h]).astype(bf16), vb)
            o = o * lax.rsqrt(jnp.mean(o * o, -1, keepdims=True) + EPS)
            o_ref[b, :, sl] = (_silu(g_ref[b, :, sl]) * o).astype(bf16)


def _seq_group(nb):
    return math.gcd(nb, SEQ_GROUP)


def _layer_block(tail, layer, group):
    return pl.BlockSpec((None, group) + tail, lambda b, c: (layer, b) + (0,) * len(tail))


def _stacked_out(prev, n_layers, nb, tail, n_inputs, out_index):
    shape = jax.ShapeDtypeStruct((n_layers, nb) + tail, f32)
    if prev is None:
        return shape, [], [], {}
    return shape, [prev], [pl.BlockSpec(memory_space=pl.ANY)], {n_inputs: out_index}


def _retention(ret, r0, layer_in, pos0, prev, layer_out, n_layers):
    nb, t, _ = ret.shape
    q = math.gcd(t, RET_CHUNK)
    grp = _seq_group(nb)
    cos, sin = _rope_tables(pos0 + jnp.arange(t))
    dmask, xi, zeta, g_chunk = _ret_decay(q)
    bcast = lambda a: jnp.broadcast_to(a[:, :, None], (RET_H, a.shape[1], LANE))
    col = lambda j: pl.BlockSpec((grp, q, 512), lambda b, c: (b, c, j))
    tab = pl.BlockSpec((q, LANE), lambda b, c: (c, 0))
    tail = (RET_H, RET_DK, RET_DV)
    in_specs = [col(0), col(1), col(2), col(3), _layer_block(tail, layer_in, grp), tab, tab,
                _const_spec((RET_H, q, q)), _const_spec((RET_H, q, LANE)), _const_spec((RET_H, q, LANE)),
                _const_spec((RET_H, 8, LANE))]
    state_shape, extra, extra_specs, aliases = _stacked_out(prev, n_layers, nb, tail, len(in_specs), 1)
    return pl.pallas_call(
        _retention_kernel,
        grid=(nb // grp, t // q),
        in_specs=in_specs + extra_specs,
        out_specs=[col(0), _layer_block(tail, layer_out, grp)],
        out_shape=[jax.ShapeDtypeStruct((nb, t, 512), bf16), state_shape],
        input_output_aliases=aliases,
        compiler_params=_cparams(("parallel", "arbitrary")),
        name="retention_t%d" % t,
    )(ret, ret, ret, ret, r0, cos, sin, dmask, bcast(xi), bcast(zeta),
      jnp.broadcast_to(g_chunk[:, None, None], (RET_H, 8, LANE)), *extra)


def _sort_key(score):
    bits = lax.bitcast_convert_type(score, i32)
    bits = jnp.where(bits == INT_MIN, 0, bits)
    return jnp.where(bits < 0, bits ^ 0x7FFFFFFF, bits)


def _count(key_ref, nchunk, ck, pred):
    def body(c, acc):
        r0 = pl.multiple_of(c * ck, ck)
        hit = jnp.where(pred(key_ref[pl.ds(r0, ck), :], r0), 1, 0).astype(i32)
        return acc + jnp.sum(hit.reshape(ck // 8, 8, LANE), axis=0)
    acc = lax.fori_loop(0, nchunk, body, jnp.zeros((8, LANE), i32))
    return jnp.sum(acc, axis=0, keepdims=True)


def _select_topk(key_ref, nchunk, ck, topk, idx_bits):
    def bit_step(it, t):
        cand = t + jnp.left_shift(jnp.int32(1), 31 - it)
        cnt = _count(key_ref, nchunk, ck, lambda blk, r0: blk >= cand)
        return jnp.where(cnt >= topk, cand, t)

    t = lax.fori_loop(0, 32, bit_step, jnp.full((1, LANE), INT_MIN, i32))
    t = jnp.maximum(t, INT_MIN + 1)
    c_ge = _count(key_ref, nchunk, ck, lambda blk, r0: blk >= t)
    c_gt = _count(key_ref, nchunk, ck, lambda blk, r0: blk > t)
    _cut_ties(key_ref, nchunk, ck, t, c_ge > topk, topk - c_gt, idx_bits)
    return t


def _cut_ties(key_ref, nchunk, ck, t, surplus, keep, idx_bits):
    rows = lax.broadcasted_iota(i32, (ck, LANE), 0)

    @pl.when(jnp.max(jnp.where(surplus, 1, 0)) > 0)
    def _():
        want = jnp.where(surplus, keep, jnp.int32(2 ** 30))

        def idx_step(it, x):
            cand = x + jnp.left_shift(jnp.int32(1), idx_bits - 1 - it)
            cnt = _count(key_ref, nchunk, ck, lambda blk, r0: jnp.where(blk == t, rows + r0, cand) < cand)
            return jnp.where(cnt < want, cand, x)

        last = lax.fori_loop(0, idx_bits, idx_step, jnp.zeros((1, LANE), i32))

        def demote(c, carry):
            r0 = pl.multiple_of(c * ck, ck)
            blk = key_ref[pl.ds(r0, ck), :]
            drop = jnp.where(blk == t, rows + r0, last) > last
            key_ref[pl.ds(r0, ck), :] = jnp.where(drop, INT_MIN, blk)
            return carry

        lax.fori_loop(0, nchunk, demote, 0)


PLANE_ROWS = 256
_SWAP_STEPS = ((16, 0x0000FFFF), (8, 0x00FF00FF), (4, 0x0F0F0F0F), (2, 0x33333333), (1, 0x55555555))


def _bit_planes(words):
    a = list(words)
    for j, m in _SWAP_STEPS:
        for k in range(32):
            if k & j == 0:
                t = ((a[k] >> j) ^ a[k + j]) & m
                a[k] = a[k] ^ (t << j)
                a[k + j] = a[k + j] ^ t
    return a


def _select_topk_planes(key_ref, planes_ref, nchunk, ck, topk, idx_bits):
    u32 = jnp.uint32
    per_chunk = ck // PLANE_ROWS
    ngroups = nchunk * per_chunk

    def to_planes(g, carry):
        r0 = pl.multiple_of(g * PLANE_ROWS, PLANE_ROWS)
        words = [lax.bitcast_convert_type(key_ref[pl.ds(r0 + 8 * i, 8), :], u32) for i in range(32)]
        planes = _bit_planes(words)
        planes[31] = ~planes[31]
        for b in range(32):
            planes_ref[g, b] = lax.bitcast_convert_type(planes[b], i32)
        return carry

    lax.fori_loop(0, ngroups, to_planes, 0)

    gmax = planes_ref.shape[0]
    full = jnp.full((8, LANE), 0xFFFFFFFF, u32)
    none = jnp.zeros((8, LANE), u32)
    alive0 = tuple(jnp.where(g < ngroups, full, none) for g in range(gmax))

    def bit_step(it, carry):
        t, need, alive = carry
        b = 31 - it
        ones = [a & lax.bitcast_convert_type(planes_ref[g, b], u32) for g, a in enumerate(alive)]
        cnt = functools.reduce(lambda x, y: x + y, [lax.population_count(o) for o in ones])
        c = jnp.sum(cnt.astype(i32), axis=0, keepdims=True)
        take = c >= need
        t = t | jnp.where(take, jnp.left_shift(jnp.int32(1), b), 0)
        need = jnp.where(take, need, need - c)
        return t, need, tuple(jnp.where(take, o, a ^ o) for o, a in zip(ones, alive))

    init = (jnp.zeros((1, LANE), i32), jnp.full((1, LANE), topk, i32), alive0)
    t, need, alive = lax.fori_loop(0, 32, bit_step, init)
    t = t ^ INT_MIN
    equal = functools.reduce(lambda x, y: x + y, [lax.population_count(a) for a in alive])
    n_equal = jnp.sum(equal.astype(i32), axis=0, keepdims=True)
    surplus = (n_equal > need) & (t > INT_MIN)
    t = jnp.maximum(t, INT_MIN + 1)
    _cut_ties(key_ref, nchunk, ck, t, surplus, need, idx_bits)
    return t


def _dsa_prompt_kernel(qa_ref, qit_ref, kiwiq_ref, k_ref, vt_ref, kiwik_ref, bias_ref, o_ref, key_ref, acc_ref,
                       qbd_ref, planes_ref, *, topk, idx_bits, nch, ck):
    qb = pl.program_id(1)
    cb = ck // LANE
    nchunk = (qb + cb) // cb
    wit = jnp.transpose(kiwiq_ref[...])[IDX_DIM:IDX_DIM + IDX_H, :] * IDX_H ** -0.5
    qpos = qb * LANE + lax.broadcasted_iota(i32, (ck, LANE), 1)
    rows = lax.broadcasted_iota(i32, (ck, LANE), 0)

    def score_chunk(c, carry):
        r0 = pl.multiple_of(c * ck, ck)
        kic = kiwik_ref[pl.ds(r0, ck), 0:IDX_DIM].astype(bf16)
        acc = jnp.zeros((ck, LANE), f32)
        for h in range(IDX_H):
            s = _mm(kic, qit_ref[:, h * LANE:(h + 1) * LANE])
            acc = acc + jnp.maximum(s, 0.0) * wit[h:h + 1, :]
        key_ref[pl.ds(r0, ck), :] = jnp.where(rows + r0 <= qpos, _sort_key(acc), INT_MIN)
        return carry

    lax.fori_loop(0, nchunk, score_chunk, 0)
    t = _select_topk_planes(key_ref, planes_ref, nchunk, ck, topk, idx_bits)

    npair = ATT_H // 2
    for pr in range(npair):
        qbd_ref[pr] = jnp.zeros((2 * ATT_DH, 2 * LANE), bf16)
        for hh in range(2):
            sl = slice((2 * pr + hh) * ATT_DH, (2 * pr + hh + 1) * ATT_DH)
            qbd_ref[pr, hh * ATT_DH:(hh + 1) * ATT_DH, hh * LANE:(hh + 1) * LANE] = (
                jnp.transpose(qa_ref[:, sl].astype(f32)).astype(bf16))
    acc_ref[...] = jnp.zeros_like(acc_ref)
    nfar = jnp.maximum(qb - 1, 0) // cb

    pairs = range(npair)

    def att_chunks(near, unroll):
        def body(i, carry):
            ms, ls = list(carry[0]), list(carry[1])
            cs = [i * unroll + u for u in range(unroll)]
            r0s = [pl.multiple_of(c * ck, ck) for c in cs]
            qks = [[_mm(k_ref[pl.ds(r0, ck), pr * 2 * ATT_DH:(pr + 1) * 2 * ATT_DH], qbd_ref[pr]) for pr in pairs]
                   for r0 in r0s]
            for c, r0, qk in zip(cs, r0s, qks):
                neg1 = jnp.where(key_ref[pl.ds(r0, ck), :] >= t, 0.0, NEG_BIG)
                negm = jnp.concatenate([neg1, neg1], axis=1)
                alphas, ps = [], []
                for pr in pairs:
                    s = qk[pr] * (ATT_DH ** -0.5 * LOG2E) + negm
                    if near:
                        s = s + jnp.concatenate(
                            [bias_ref[jnp.clip(qb - (c * cb + j), 0, 2), pr] for j in range(cb)], axis=0)
                    m_new = jnp.maximum(ms[pr], jnp.max(s, axis=0, keepdims=True))
                    alpha = jnp.exp2(ms[pr] - m_new)
                    p = jnp.exp2(s - m_new)
                    ls[pr] = ls[pr] * alpha + jnp.sum(p, axis=0, keepdims=True)
                    ms[pr] = m_new
                    alphas.append(alpha)
                    ps.append(p.astype(bf16))
                pvs = [_mm(vt_ref[pr * nch + c], ps[pr]) for pr in pairs]
                for pr in pairs:
                    for hh in range(2):
                        d = slice(hh * LANE, (hh + 1) * LANE)
                        acc_ref[2 * pr + hh] = acc_ref[2 * pr + hh] * alphas[pr][:, d] + pvs[pr][d, d]
            return tuple(ms), tuple(ls)
        return body

    carry = ((jnp.full((1, 2 * LANE), NEG_BIG, f32),) * npair, (jnp.zeros((1, 2 * LANE), f32),) * npair)
    carry = lax.fori_loop(0, nfar // 2, att_chunks(False, 2), carry)
    carry = lax.fori_loop(nfar // 2 * 2, nfar, att_chunks(False, 1), carry)
    _, ls = lax.fori_loop(nfar, nchunk, att_chunks(True, 1), carry)
    for h in range(ATT_H):
        l = ls[h // 2][:, (h % 2) * LANE:(h % 2 + 1) * LANE]
        o_ref[:, h * ATT_DH:(h + 1) * ATT_DH] = jnp.transpose(acc_ref[h] / l).astype(bf16)


def _bias_tiles(rel_bias, nd):
    j = jnp.arange(LANE)[:, None]
    i = jnp.arange(LANE)[None, :]
    rel = jnp.arange(nd)[:, None, None] * LANE + (i - j)[None]
    return jnp.moveaxis(_bias_lookup(rel_bias, rel), -1, 1)


def _dsa_prompt(qa, kab, vab, qi, kiwi, rel_bias, nb, t):
    nqb = t // LANE
    topk = min(TOPK_MAX, t // 4)
    assert REL_MAX_DIST <= LANE + 1
    bias = _bias_tiles(rel_bias, 3)
    bias = (bias - bias[2:3]) * LOG2E
    npair = ATT_H // 2
    bias = bias.reshape(3, npair, 2, LANE, LANE).transpose(0, 1, 3, 2, 4).reshape(3, npair, LANE, 2 * LANE)
    ck = math.gcd(t, DSA_CHUNK)
    nch = t // ck
    vt = vab[:nb * t].reshape(nb, nch, ck, ATT_H * ATT_DH).transpose(0, 3, 1, 2)
    vt = vt.reshape(nb, npair, 2 * ATT_DH, nch, ck).transpose(0, 1, 3, 2, 4).reshape(nb * npair * nch, 2 * ATT_DH, ck)
    qit = qi[:nb * t].reshape(nb * nqb, LANE, IDX_H, IDX_DIM).transpose(0, 3, 2, 1)
    qit = qit.reshape(nb * nqb, IDX_DIM, IDX_H * LANE)
    qrow = lambda c: pl.BlockSpec((LANE, c), lambda b, q: (b * nqb + q, 0))
    seq = lambda c: pl.BlockSpec((t, c), lambda b, q: (b, 0))
    kern = functools.partial(_dsa_prompt_kernel, topk=topk, idx_bits=max(1, (t - 1).bit_length()), nch=nch, ck=ck)
    return pl.pallas_call(
        kern,
        grid=(nb, nqb),
        in_specs=[qrow(512), pl.BlockSpec((None, IDX_DIM, IDX_H * LANE), lambda b, q: (b * nqb + q, 0, 0)),
                  qrow(LANE), seq(512),
                  pl.BlockSpec((npair * nch, 2 * ATT_DH, ck), lambda b, q: (b, 0, 0)), seq(LANE),
                  _const_spec((3, npair, LANE, 2 * LANE))],
        out_specs=qrow(512),
        out_shape=jax.ShapeDtypeStruct((nb * t, 512), bf16),
        scratch_shapes=[pltpu.VMEM((t, LANE), i32), pltpu.VMEM((ATT_H, ATT_DH, LANE), f32),
                        pltpu.VMEM((npair, 2 * ATT_DH, 2 * LANE), bf16),
                        pltpu.VMEM((t // PLANE_ROWS, 32, 8, LANE), i32)],
        compiler_params=_cparams(("parallel", "arbitrary")),
        name="dsa_prompt",
    )(qa, qit, kiwi, kab, vt, kiwi, bias)


def _softplus(x):
    return jnp.maximum(x, 0.0) + jnp.log1p(jnp.exp(-jnp.abs(x)))


def _cumsum_rows(tri, a):
    hi = a.astype(bf16)
    r1 = a - hi.astype(f32)
    mid = r1.astype(bf16)
    lo = (r1 - mid.astype(f32)).astype(bf16)
    return _mm(tri, hi) + _mm(tri, mid) + _mm(tri, lo)


def _odd_mixer_kernel(*refs, q, pos0):
    seq_in, shared, seq_out = refs[:7], refs[7:16], refs[-6:]
    for b in range(seq_in[0].shape[0]):
        _odd_mixer_seq(*(r.at[b] for r in seq_in), *shared, *(r.at[b] for r in seq_out), q=q, pos0=pos0)


def _odd_mixer_seq(u_ref, z_ref, xbc_ref, dt_ref, pp_ref, cp_ref, h0_ref, pw_ref, ps_ref, cw_ref, cb_ref,
                   dtb_ref, alog_ref, dsk_ref, nrm_ref, tri_ref, po_ref, y_ref, h_ref, ubuf, xbuf, ybuf, *, q, pos0):
    c = pl.program_id(1)

    @pl.when(c == 0)
    def _():
        ubuf[0:1, :] = jnp.zeros((1, POOL_DIM), f32)
        ubuf[1:16, :] = pp_ref[...]
        xbuf[0:8 - (CONV_W - 1), :] = jnp.zeros((8 - (CONV_W - 1), CONV_DIM), f32)
        xbuf[8 - (CONV_W - 1):8, :] = cp_ref[...]
        h_ref[...] = h0_ref[...]

    pos = pos0 + c * q + lax.broadcasted_iota(i32, (q, LANE), 0)
    causal = lax.broadcasted_iota(i32, (q, q), 0) >= lax.broadcasted_iota(i32, (q, q), 1)

    u = u_ref[...]
    ubuf[16:16 + q, :] = u
    for g, w in enumerate(POOL_WINDOWS):
        sl = slice(g * POOL_GC, (g + 1) * POOL_GC)
        acc = u[:, sl]
        for k in range(1, w):
            acc = acc + ubuf[16 - k:16 - k + q, sl]
        d = acc / jnp.minimum(pos + 1, w).astype(f32) - u[:, sl]
        po_ref[:, sl] = (_mm(d.astype(bf16), pw_ref[g]) * ps_ref[:, sl]).astype(bf16)
    ubuf[0:16, :] = ubuf[q:q + 16, :]

    xbuf[8:8 + q, :] = xbc_ref[...]
    conv = cb_ref[...]
    for j in range(CONV_W):
        off = 8 - (CONV_W - 1) + j
        conv = conv + xbuf[off:off + q, :] * cw_ref[j:j + 1, :]
    xbuf[0:8, :] = xbuf[q:q + 8, :]
    act = _silu(conv)
    xs = act[:, 0:D_INNER]

    dt = _softplus(dt_ref[...] + dtb_ref[...])
    a = dt * (-jnp.exp(alog_ref[...]))
    cs = _cumsum_rows(tri_ref[...], a)
    cs_t = jnp.transpose(cs)
    dt_t = jnp.transpose(dt)
    cs_last = cs[q - 1:q, :]
    w_end = jnp.exp(cs_last - cs) * dt
    ecs = jnp.exp(cs)
    hpg = SSM_H // SSM_G
    for g in range(SSM_G):
        bm = act[:, D_INNER + g * SSM_N:D_INNER + (g + 1) * SSM_N].astype(bf16)
        cm = act[:, D_INNER + (SSM_G + g) * SSM_N:D_INNER + (SSM_G + g + 1) * SSM_N].astype(bf16)
        cb = _mm_nt(cm, bm)
        for hh in range(hpg):
            h = g * hpg + hh
            psl = slice(h * SSM_P, (h + 1) * SSM_P)
            seg = cs[:, h:h + 1] - cs_t[h:h + 1, :]
            lm = jnp.exp(jnp.where(causal, seg, NEG_BIG))
            sc = cb * lm * dt_t[h:h + 1, :]
            xh = xs[:, psl]
            hs = h_ref[h]
            yh = _mm(sc.astype(bf16), xh.astype(bf16)) + _mm_nt(cm, hs.astype(bf16)) * ecs[:, h:h + 1]
            h_ref[h] = hs * jnp.exp(cs_last[:, h:h + 1]) + _mm_tn((xh * w_end[:, h:h + 1]).astype(bf16), bm)
            ybuf[:, psl] = yh
    y = (ybuf[...] + dsk_ref[...] * xs) * _silu(z_ref[...])
    gw = D_INNER // SSM_G
    for g in range(SSM_G):
        sl = slice(g * gw, (g + 1) * gw)
        yg = y[:, sl]
        y_ref[:, sl] = (yg * lax.rsqrt(jnp.mean(yg * yg, -1, keepdims=True) + EPS) * nrm_ref[:, sl]).astype(bf16)


def _pad_lanes(v):
    return jnp.pad(v.astype(f32), (0, LANE - v.shape[0]))[None, :]


def _odd_mixer(u, z, xbc, dt, pool_prev, conv_prev, h0, layer_in, prm, pos0, prev, layer_out, n_layers):
    pool_w, pool_scale, conv_w, conv_b, dt_bias, a_log, d_skip, ssm_norm = prm
    nb, t, _ = u.shape
    q = math.gcd(t, SSD_CHUNK)
    nc = t // q
    grp = _seq_group(nb)
    row = lambda c: pl.BlockSpec((grp, q, c), lambda b, i: (b, i, 0))
    lead = (nb, t)
    tri = jnp.tril(jnp.ones((q, q), bf16))
    tail = (SSM_H, SSM_P, SSM_N)
    in_specs = [row(POOL_DIM), row(D_INNER), row(CONV_DIM), row(LANE),
                _layer_block((POOL_PAST, POOL_DIM), layer_in, grp),
                _layer_block((CONV_W - 1, CONV_DIM), layer_in, grp), _layer_block(tail, layer_in, grp),
                _const_spec((POOL_GROUPS, POOL_GC, POOL_GC)), _const_spec((1, POOL_DIM)),
                _const_spec((CONV_W, CONV_DIM)), _const_spec((1, CONV_DIM)),
                _const_spec((1, LANE)), _const_spec((1, LANE)),
                _const_spec((1, D_INNER)), _const_spec((1, D_INNER)), _const_spec((q, q))]
    state_shape, extra, extra_specs, aliases = _stacked_out(prev, n_layers, nb, tail, len(in_specs), 2)
    return pl.pallas_call(
        functools.partial(_odd_mixer_kernel, q=q, pos0=pos0),
        grid=(nb // grp, nc),
        in_specs=in_specs + extra_specs,
        out_specs=[row(POOL_DIM), row(D_INNER), _layer_block(tail, layer_out, grp)],
        out_shape=[jax.ShapeDtypeStruct(lead + (POOL_DIM,), bf16), jax.ShapeDtypeStruct(lead + (D_INNER,), bf16),
                   state_shape],
        input_output_aliases=aliases,
        scratch_shapes=[pltpu.VMEM((grp, q + 16, POOL_DIM), f32), pltpu.VMEM((grp, q + 8, CONV_DIM), f32),
                        pltpu.VMEM((grp, q, D_INNER), f32)],
        compiler_params=_cparams(("parallel", "arbitrary")),
        name="odd_mixer_t%d" % t,
    )(u, z, xbc, dt, pool_prev, conv_prev, h0, pool_w.astype(bf16), pool_scale[None, :], conv_w, conv_b[None, :],
      _pad_lanes(dt_bias), _pad_lanes(a_log), jnp.repeat(d_skip, SSM_P)[None, :], ssm_norm[None, :], tri, *extra)


def _dsa_sample_score_kernel(pt_ref, qi_ref, kiwi_ref, *rest, npg):
    pages, o_ref = rest[:npg], rest[npg]
    ts = qi_ref.shape[0]
    kiwi = kiwi_ref[...]
    wi = kiwi[:, IDX_DIM:IDX_DIM + IDX_H] * IDX_H ** -0.5
    ki_new = jnp.transpose(jnp.concatenate([kiwi, jnp.zeros((PAGE - ts, LANE), f32)], axis=0))[0:IDX_DIM, :]
    ki = jnp.concatenate([p[...] for p in pages] + [ki_new], axis=1).astype(bf16)
    nk = ki.shape[1]
    acc = jnp.zeros((ts, nk), f32)
    for h in range(IDX_H):
        s = _mm(qi_ref[:, h * IDX_DIM:(h + 1) * IDX_DIM], ki)
        acc = acc + jnp.maximum(s, 0.0) * wi[:, h:h + 1]
    col = lax.broadcasted_iota(i32, (ts, nk), 1)
    row = lax.broadcasted_iota(i32, (ts, nk), 0)
    o_ref[...] = jnp.where(col <= npg * PAGE + row, _sort_key(acc), INT_MIN)


def _select_kernel(k_ref, o_ref, key_ref, *, topk, idx_bits):
    key_ref[...] = k_ref[...]
    t = _select_topk(key_ref, 1, key_ref.shape[0], topk, idx_bits)
    o_ref[...] = jnp.where(key_ref[...] >= t, 1.0, 0.0)


def _dsa_sample_attn_kernel(pt_ref, q_ref, kn_ref, vn_ref, sel_ref, bias_ref, spread_ref, *rest, npg):
    kpages, vpages, o_ref = rest[:npg], rest[npg:2 * npg], rest[2 * npg]
    pad = jnp.zeros((PAGE * ATT_H - kn_ref.shape[0], ATT_DH), f32)
    kx = jnp.concatenate([p[...] for p in kpages] + [kn_ref[...], pad], axis=0).astype(bf16)
    vx = jnp.concatenate([p[...] for p in vpages] + [vn_ref[...], pad], axis=0).astype(bf16)
    selx = jnp.concatenate([_mm(sel_ref[:, j * PAGE:(j + 1) * PAGE], spread_ref[...]) for j in range(npg + 1)],
                           axis=1)
    s = _mm_nt(q_ref[...], kx) * ATT_DH ** -0.5 + jnp.where(selx > 0.5, bias_ref[...], NEG_BIG)
    p = jnp.exp(s - jnp.max(s, axis=-1, keepdims=True))
    o_ref[...] = (_mm(p.astype(bf16), vx) / jnp.sum(p, axis=-1, keepdims=True)).astype(bf16)


def _dsa_sample(qa, ka, va, qi, kiwi, cache_k, cache_v, cache_ki, layer, page_table, rel_bias):
    db, ts = qa.shape[:2]
    npg = page_table.shape[1]
    n_past = npg * PAGE
    nk = n_past + PAGE
    nq = db * ts
    topk = min(TOPK_MAX, (n_past + ts) // 4)
    hd = ATT_H * ATT_DH
    ki_page = lambda j: pl.BlockSpec((None, None, IDX_DIM, PAGE), lambda b, pt: (layer, pt[b, j], 0, 0))
    seq = lambda r, c: pl.BlockSpec((None, r, c), lambda b, pt: (b, 0, 0))

    keys = pl.pallas_call(
        functools.partial(_dsa_sample_score_kernel, npg=npg),
        grid_spec=pltpu.PrefetchScalarGridSpec(
            num_scalar_prefetch=1, grid=(db,),
            in_specs=[seq(ts, IDX_H * IDX_DIM), seq(ts, LANE)] + [ki_page(j) for j in range(npg)],
            out_specs=seq(ts, nk)),
        out_shape=jax.ShapeDtypeStruct((db, ts, nk), i32),
        compiler_params=_cparams(("parallel",)),
        name="dsa_sample_score",
    )(page_table, qi, kiwi, *([jnp.swapaxes(cache_ki, 2, 3)] * npg))

    col = pl.BlockSpec((nk, LANE), lambda i: (0, i))
    sel = pl.pallas_call(
        functools.partial(_select_kernel, topk=topk, idx_bits=max(1, (nk - 1).bit_length())),
        grid=(nq // LANE,),
        in_specs=[col],
        out_specs=col,
        out_shape=jax.ShapeDtypeStruct((nk, nq), f32),
        scratch_shapes=[pltpu.VMEM((nk, LANE), i32)],
        compiler_params=_cparams(("parallel",)),
        name="dsa_sample_select",
    )(keys.reshape(nq, nk).T)
    sel = jnp.repeat(sel.T.reshape(db, ts, nk), ATT_H, axis=1).astype(bf16)
    rel = n_past + jnp.arange(ts)[:, None] - jnp.arange(nk)[None, :]
    bias = jnp.repeat(jnp.moveaxis(_bias_lookup(rel_bias, rel), -1, 1), ATT_H, axis=-1)
    same_head = jnp.arange(nk * ATT_H)[None, :] % ATT_H == jnp.arange(ATT_H)[:, None]
    bias = jnp.where(same_head[None], bias, NEG_BIG).reshape(ts * ATT_H, nk * ATT_H)
    spread = (jnp.arange(PAGE * ATT_H)[None, :] // ATT_H == jnp.arange(PAGE)[:, None]).astype(bf16)

    n_pages = cache_k.shape[1]
    rows = PAGE * ATT_H
    kv_page = lambda j: pl.BlockSpec((rows, ATT_DH), lambda b, pt: (layer * n_pages + pt[b, j], 0))
    const = lambda a: pl.BlockSpec(a.shape, lambda b, pt: (0, 0))
    as_rows = lambda a: a.reshape(db, ts * ATT_H, ATT_DH)
    out = pl.pallas_call(
        functools.partial(_dsa_sample_attn_kernel, npg=npg),
        grid_spec=pltpu.PrefetchScalarGridSpec(
            num_scalar_prefetch=1, grid=(db,),
            in_specs=[seq(ts * ATT_H, ATT_DH)] * 3 + [seq(ts * ATT_H, nk), const(bias), const(spread)]
                     + [kv_page(j) for j in range(npg)] * 2,
            out_specs=seq(ts * ATT_H, ATT_DH)),
        out_shape=jax.ShapeDtypeStruct((db, ts * ATT_H, ATT_DH), bf16),
        compiler_params=_cparams(("parallel",)),
        name="dsa_sample_attn",
    )(page_table, as_rows(qa), as_rows(ka), as_rows(va), sel, bias, spread,
      *([cache_k.reshape(-1, ATT_DH)] * npg), *([cache_v.reshape(-1, ATT_DH)] * npg))
    return out.reshape(db, ts, hd)


def kernel(x_prompt, x_sample, cache_k, cache_v, cache_kidx, state_ret, state_pool, state_conv, state_ssm,
           page_table, norm_mix, norm_ffn, w_in_even, w_out_even, q_norm, k_norm, rel_bias,
           w_in_odd, w_out_odd, pool_w, pool_scale, conv_w, conv_b, dt_bias, a_log, d_skip, ssm_norm,
           w_gate, w_up, w_down):
    bp, sp, d = x_prompt.shape
    db, ts, _ = x_sample.shape
    n_p, n_s = bp * sp, db * ts
    assert d == D_MODEL and n_p % TM == 0 and n_s % TM == 0 and sp % LANE == 0 and sp >= POOL_PAST
    n_past = page_table.shape[1] * PAGE
    n_even, n_odd = (DEPTH + 1) // 2, DEPTH // 2
    hd = ATT_H * ATT_DH
    xp, xs = x_prompt.reshape(n_p, d), x_sample.reshape(n_s, d)
    seqs = lambda a: a.reshape(db, ts, a.shape[-1])
    pseqs = lambda a: a.reshape(bp, sp, a.shape[-1])
    last = lambda a, n: a.reshape(bp, sp, a.shape[-1])[:, sp - n:]
    tail = lambda prev, cur, n: jnp.concatenate([prev.astype(f32), cur], axis=1)[:, -n:]
    zeros = lambda *s: jnp.zeros((1, bp) + s, f32)
    outs = [[] for _ in range(14)]
    rstate_p = rstate_s = h_p = h_s = None
    for l in range(DEPTH):
        w_ffn = (norm_ffn[l][None], w_gate[l].astype(bf16), w_up[l].astype(bf16), w_down[l].astype(bf16))
        if l % 2 == 0:
            i = l // 2
            w_in = jnp.pad(w_in_even[i], ((0, 0), (0, EVEN_PROJ_PAD - EVEN_PROJ))).astype(bf16)
            prm = (norm_mix[l][None], w_in, q_norm[i][None], k_norm[i][None])
            ret_p, qa_p, ka_p, kab_p, va_p, vab_p, qi_p, kiwi_p = _even_proj(xp, *prm)
            ret_s, qa_s, ka_s, _, va_s, _, qi_s, kiwi_s = _even_proj(xs, *prm)
            mix_a_p, rstate_p = _retention(pseqs(ret_p), zeros(RET_H, RET_DK, RET_DV), 0, 0, rstate_p, i, n_even)
            mix_a_s, rstate_s = _retention(seqs(ret_s), state_ret, i, n_past, rstate_s, i, n_even)
            mix_b_p = _dsa_prompt(qa_p, kab_p, vab_p, qi_p, kiwi_p, rel_bias, bp, sp)
            mix_b_s = _dsa_sample(seqs(qa_s), seqs(ka_s), seqs(va_s), seqs(qi_s), seqs(kiwi_s),
                                  cache_k, cache_v, cache_kidx, i, page_table, rel_bias)
            w_out = w_out_even[i].astype(bf16)
            new = [ka_p.reshape(bp, sp, ATT_H, ATT_DH), va_p.reshape(bp, sp, ATT_H, ATT_DH),
                   kiwi_p.reshape(bp, sp, LANE)[..., :IDX_DIM], None, None, None, None,
                   ka_s.reshape(db, ts, ATT_H, ATT_DH), va_s.reshape(db, ts, ATT_H, ATT_DH),
                   seqs(kiwi_s)[..., :IDX_DIM], None, None, None, None]
        else:
            j = l // 2
            w_in = jnp.pad(w_in_odd[j], ((0, 0), (0, ODD_PROJ_PAD - ODD_PROJ))).astype(bf16)
            u_p, z_p, xbc_p, dt_p = _odd_proj(xp, norm_mix[l][None], w_in)
            u_s, z_s, xbc_s, dt_s = _odd_proj(xs, norm_mix[l][None], w_in)
            prm = (pool_w[j], pool_scale[j], conv_w[j], conv_b[j], dt_bias[j], a_log[j], d_skip[j], ssm_norm[j])
            mix_a_p, mix_b_p, h_p = _odd_mixer(
                pseqs(u_p), pseqs(z_p), pseqs(xbc_p), pseqs(dt_p), zeros(POOL_PAST, POOL_DIM),
                zeros(CONV_W - 1, CONV_DIM), zeros(SSM_H, SSM_P, SSM_N), 0, prm, 0, h_p, j, n_odd)
            mix_a_s, mix_b_s, h_s = _odd_mixer(
                seqs(u_s), seqs(z_s), seqs(xbc_s), seqs(dt_s), state_pool, state_conv, state_ssm, j,
                prm, n_past, h_s, j, n_odd)
            w_out = w_out_odd[j].astype(bf16)
            new = [None, None, None, None, last(u_p, POOL_PAST), last(xbc_p, CONV_W - 1), None,
                   None, None, None, None,
                   tail(state_pool[j], seqs(u_s), POOL_PAST), tail(state_conv[j], seqs(xbc_s), CONV_W - 1), None]
        for acc, leaf in zip(outs, new):
            if leaf is not None:
                acc.append(leaf)
        xp = _out_ffn(xp, mix_a_p.reshape(n_p, -1), mix_b_p.reshape(n_p, -1), w_out, *w_ffn)
        xs = _out_ffn(xs, mix_a_s.reshape(n_s, -1), mix_b_s.reshape(n_s, -1), w_out, *w_ffn)
    leaves = [jnp.stack(a) if a else None for a in outs]
    leaves[3], leaves[6], leaves[10], leaves[13] = rstate_p, h_p, rstate_s, h_s
    return (xp.reshape(bp, sp, d), xs.reshape(db, ts, d)) + tuple(leaves)
```

```python
import functools
import math

import jax
import jax.numpy as jnp
import numpy as np
from jax import lax
from jax.experimental import pallas as pl
from jax.experimental.pallas import tpu as pltpu

f32 = jnp.float32
bf16 = jnp.bfloat16
i32 = jnp.int32

D_MODEL = 1024
DEPTH = 4
PAGE = 128
RET_H, RET_DK, RET_DV, RET_CHUNK = 4, 128, 128, 128
ROPE_BASE = 10000.0
ATT_H, ATT_DH, ATT_BLOCK = 4, 128, 128
IDX_H, IDX_DIM = 8, 64
TOPK_MAX = 256
REL_BUCKETS, REL_MAX_DIST = 32, 128
POOL_WINDOWS = (2, 4, 8, 16)
POOL_GROUPS = 4
POOL_DIM = D_MODEL // 2
POOL_GC = POOL_DIM // POOL_GROUPS
POOL_PAST = 15
D_INNER = D_MODEL // 2
SSM_P = 64
SSM_H = D_INNER // SSM_P
SSM_G = 2
SSM_N = 128
CONV_W = 4
CONV_DIM = D_INNER + 2 * SSM_G * SSM_N
SSD_CHUNK = 128
FF_DIM = -(-8 * D_MODEL // (3 * 256)) * 256
EPS = 1e-6

EVEN_PROJ = 4 * 512 + 3 * 512 + 512 + IDX_DIM + IDX_H
EVEN_PROJ_PAD = 4224
ODD_PROJ = POOL_DIM + D_INNER + CONV_DIM + SSM_H
ODD_PROJ_PAD = 2176

LANE = 128
INT_MIN = -(2 ** 31)
NEG_BIG = -1e30
LOG2E = 1.4426950408889634
VMEM_LIMIT = 56 * 1024 * 1024
TM = 256
FF_SPLIT = 2
DSA_CHUNK = 512
SEQ_GROUP = 8


def _cparams(sem):
    return pltpu.CompilerParams(dimension_semantics=sem, vmem_limit_bytes=VMEM_LIMIT)


def _mm(a, b):
    return jnp.dot(a, b, preferred_element_type=f32)


def _mm_nt(a, b):
    return lax.dot_general(a, b, (((1,), (1,)), ((), ())), preferred_element_type=f32)


def _mm_tn(a, b):
    return lax.dot_general(a, b, (((0,), (0,)), ((), ())), preferred_element_type=f32)


def _rms(x, g):
    return x * lax.rsqrt(jnp.mean(x * x, -1, keepdims=True) + EPS) * g


def _silu(x):
    return x / (1.0 + jnp.exp(-x))


def _const_spec(shape):
    nd = len(shape)
    return pl.BlockSpec(shape, lambda *a: (0,) * nd)


def _even_proj_kernel(x_ref, g_ref, w_ref, qg_ref, kg_ref, *refs, for_prompt, n_aliased):
    ret_ref, qa_ref, ka_ref, kab_ref, va_ref, kiwi_ref, *idx_refs = refs[n_aliased:]
    xb = _rms(x_ref[...], g_ref[...]).astype(bf16)
    ret_ref[...] = _mm(xb, w_ref[:, 0:2048])
    qa = _mm(xb, w_ref[:, 2048:2560])
    ka = _mm(xb, w_ref[:, 2560:3072])
    va = _mm(xb, w_ref[:, 3072:3584])
    for h in range(ATT_H):
        sl = slice(h * ATT_DH, (h + 1) * ATT_DH)
        qa_ref[:, sl] = _rms(qa[:, sl], qg_ref[...]).astype(bf16)
        kn = _rms(ka[:, sl], kg_ref[...])
        ka_ref[:, h, :] = kn
        kab_ref[:, sl] = kn.astype(bf16)
        va_ref[:, h, :] = va[:, sl]
    qi = _mm(xb, w_ref[:, 3584:4096]) * IDX_DIM ** -0.5
    kiwi_ref[...] = _mm(xb, w_ref[:, 4096:4224])
    if not for_prompt:
        idx_refs[0][...] = qi.astype(bf16)
        return
    qit_ref, vt_ref = idx_refs
    for blk in range(TM // LANE):
        for h in range(IDX_H):
            part = qi[blk * LANE:(blk + 1) * LANE, h * IDX_DIM:(h + 1) * IDX_DIM]
            qit_ref[blk, :, h * LANE:(h + 1) * LANE] = jnp.transpose(part).astype(bf16)
    for pr in range(ATT_H // 2):
        vt_ref[pr] = jnp.transpose(va[:, pr * 2 * ATT_DH:(pr + 1) * 2 * ATT_DH]).astype(bf16)


def _even_proj(x, g, w, qg, kg, layer, n_layers, prev_kv, prompt_len=None):
    n = x.shape[0]
    row = lambda c: pl.BlockSpec((TM, c), lambda i: (i, 0))
    heads = pl.BlockSpec((None, TM, ATT_H, ATT_DH), lambda i: (layer, i, 0, 0))
    stacked = jax.ShapeDtypeStruct((n_layers, n, ATT_H, ATT_DH), f32)
    out_specs = [row(2048), row(512), heads, row(512), heads, row(LANE)]
    out_shape = [jax.ShapeDtypeStruct((n, 2048), f32), jax.ShapeDtypeStruct((n, 512), bf16), stacked,
                 jax.ShapeDtypeStruct((n, 512), bf16), stacked, jax.ShapeDtypeStruct((n, LANE), f32)]
    extra = [] if prev_kv is None else list(prev_kv)
    aliases = {} if prev_kv is None else {5: 2, 6: 4}
    if prompt_len is None:
        out_specs.append(row(512))
        out_shape.append(jax.ShapeDtypeStruct((n, 512), bf16))
    else:
        ck = math.gcd(prompt_len, DSA_CHUNK)
        assert ck % TM == 0 and prompt_len % ck == 0
        per_chunk, per_seq, npair = ck // TM, prompt_len // TM, ATT_H // 2
        out_specs += [pl.BlockSpec((TM // LANE, IDX_DIM, IDX_H * LANE), lambda i: (i, 0, 0)),
                      pl.BlockSpec((None, npair, None, 2 * ATT_DH, TM),
                                   lambda i: (i // per_seq, 0, (i % per_seq) // per_chunk, 0, i % per_chunk))]
        out_shape += [jax.ShapeDtypeStruct((n // LANE, IDX_DIM, IDX_H * LANE), bf16),
                      jax.ShapeDtypeStruct((n // prompt_len, npair, prompt_len // ck, 2 * ATT_DH, ck), bf16)]
    return pl.pallas_call(
        functools.partial(_even_proj_kernel, for_prompt=prompt_len is not None, n_aliased=len(extra)),
        grid=(n // TM,),
        in_specs=[row(D_MODEL), _const_spec((1, D_MODEL)), _layer_weight(w, layer),
                  _const_spec((1, ATT_DH)), _const_spec((1, ATT_DH))]
                 + [pl.BlockSpec(memory_space=pl.ANY)] * len(extra),
        out_specs=out_specs,
        out_shape=out_shape,
        input_output_aliases=aliases,
        compiler_params=_cparams(("parallel",)),
        name="even_proj",
    )(x, g, w, qg, kg, *extra)


def _odd_proj_kernel(x_ref, g_ref, w_ref, u_ref, z_ref, xbc_ref, dt_ref):
    xb = _rms(x_ref[...], g_ref[...]).astype(bf16)
    u_ref[...] = _mm(xb, w_ref[:, 0:512])
    z_ref[...] = _mm(xb, w_ref[:, 512:1024])
    xbc_ref[...] = _mm(xb, w_ref[:, 1024:2048])
    dt_ref[...] = _mm(xb, w_ref[:, 2048:2176])


def _odd_proj(x, g, w, layer):
    n = x.shape[0]
    row = lambda c: pl.BlockSpec((TM, c), lambda i: (i, 0))
    outs = [512, 512, 1024, LANE]
    return pl.pallas_call(
        _odd_proj_kernel,
        grid=(n // TM,),
        in_specs=[row(D_MODEL), _const_spec((1, D_MODEL)), _layer_weight(w, layer)],
        out_specs=[row(c) for c in outs],
        out_shape=[jax.ShapeDtypeStruct((n, c), f32) for c in outs],
        compiler_params=_cparams(("parallel",)),
        name="odd_proj",
    )(x, g, w)


def _out_ffn_kernel(x_ref, a_ref, b_ref, wo_ref, g_ref, wg_ref, wu_ref, wd_ref, o_ref):
    half = wo_ref.shape[0] // 2
    x = x_ref[...] + _mm(a_ref[...], wo_ref[0:half, :]) + _mm(b_ref[...], wo_ref[half:, :])
    hb = _rms(x, g_ref[...]).astype(bf16)
    fc = FF_DIM // FF_SPLIT
    ff = None
    for c in range(FF_SPLIT):
        sl = slice(c * fc, (c + 1) * fc)
        act = (_silu(_mm(hb, wg_ref[:, sl])) * _mm(hb, wu_ref[:, sl])).astype(bf16)
        down = _mm(act, wd_ref[sl, :])
        ff = down if ff is None else ff + down
    o_ref[...] = x + ff


def _layer_weight(w, layer, single_buffer=False):
    mode = dict(pipeline_mode=pl.Buffered(1)) if single_buffer else {}
    return pl.BlockSpec((None,) + w.shape[1:], lambda i: (layer,) + (0,) * (w.ndim - 1), **mode)


def _out_ffn(x, a, b, wo, mix_layer, g, wg, wu, wd, layer):
    n = x.shape[0]
    row = lambda c: pl.BlockSpec((TM, c), lambda i: (i, 0))
    return pl.pallas_call(
        _out_ffn_kernel,
        grid=(n // TM,),
        in_specs=[row(D_MODEL), row(a.shape[1]), row(b.shape[1]), _layer_weight(wo, mix_layer, True),
                  _const_spec((1, D_MODEL)), _layer_weight(wg, layer, True), _layer_weight(wu, layer, True),
                  _layer_weight(wd, layer, True)],
        out_specs=row(D_MODEL),
        out_shape=jax.ShapeDtypeStruct((n, D_MODEL), f32),
        compiler_params=_cparams(("parallel",)),
        name="out_ffn",
    )(x, a, b, wo, g, wg, wu, wd)


def _rope_tables(pos):
    half = RET_DK // 2
    inv = ROPE_BASE ** (-jnp.linspace(0.0, 1.0, half, dtype=f32))
    ang = pos.astype(f32)[:, None] * inv[None, :]
    cos, sin = jnp.cos(ang), jnp.sin(ang)
    return jnp.concatenate([cos, cos], -1), jnp.concatenate([-sin, sin], -1)


def _ret_decay(q):
    log_g = jnp.log1p(-jnp.exp2(-5.0 - jnp.arange(RET_H, dtype=f32)))
    idx = jnp.arange(q, dtype=f32)
    diff = idx[:, None] - idx[None, :]
    dmask = jnp.where(diff[None] >= 0, jnp.exp(log_g[:, None, None] * jnp.maximum(diff, 0.0)[None]), 0.0)
    xi = jnp.exp(log_g[:, None] * (idx + 1.0)[None])
    zeta = jnp.exp(log_g[:, None] * (q - 1.0 - idx)[None])
    g_chunk = jnp.exp(log_g * q)
    return dmask, xi, zeta, g_chunk


def _t5_bucket(rel):
    n = jnp.maximum(rel, 0)
    max_exact = REL_BUCKETS // 2
    nf = jnp.maximum(n, 1).astype(f32)
    large = max_exact + (jnp.log(nf / max_exact) / math.log(REL_MAX_DIST / max_exact)
                         * (REL_BUCKETS - max_exact)).astype(i32)
    large = jnp.minimum(large, REL_BUCKETS - 1)
    return jnp.where(n < max_exact, n, large)


def _bias_lookup(rel_bias, rel):
    onehot = jax.nn.one_hot(_t5_bucket(rel), REL_BUCKETS, dtype=f32)
    return jnp.einsum("...b,bh->...h", onehot, rel_bias.astype(f32), precision=lax.Precision.HIGHEST)


def _rotary(x, c, s):
    return x * c + pltpu.roll(x, RET_DK // 2, 1) * s


def _retention_kernel(q_ref, k_ref, v_ref, g_ref, r0_ref, c_ref, s_ref, dm_ref, xi_ref, zt_ref, gc_ref,
                      *rest):
    o_ref, r_ref = rest[-2:]

    @pl.when(pl.program_id(1) == 0)
    def _():
        r_ref[...] = r0_ref[...]

    cos, sin = c_ref[...], s_ref[...]
    for b in range(q_ref.shape[0]):
        for h in range(RET_H):
            sl = slice(h * RET_DK, (h + 1) * RET_DK)
            qr = _rotary(q_ref[b, :, sl], cos, sin).astype(bf16)
            kr = _rotary(k_ref[b, :, sl], cos, sin) * RET_DK ** -0.5
            vb = v_ref[b, :, sl].astype(bf16)
            r = r_ref[b, h]
            s = _mm_nt(qr, kr.astype(bf16)) * dm_ref[h]
            o = _mm(s.astype(bf16), vb) + _mm(qr, r.astype(bf16)) * xi_ref[h]
            r_ref[b, h] = r * gc_ref[h, 0:1, :] + _mm_tn((kr * zt_ref[h]).astype(bf16), vb)
            o = o * lax.rsqrt(jnp.mean(o * o, -1, keepdims=True) + EPS)
            o_ref[b, :, sl] = (_silu(g_ref[b, :, sl]) * o).astype(bf16)


def _seq_group(nb):
    return math.gcd(nb, SEQ_GROUP)


def _layer_block(tail, layer, group):
    return pl.BlockSpec((None, group) + tail, lambda b, c: (layer, b) + (0,) * len(tail))


def _stacked_out(prev, n_layers, nb, tail, n_inputs, out_index):
    shape = jax.ShapeDtypeStruct((n_layers, nb) + tail, f32)
    if prev is None:
        return shape, [], [], {}
    return shape, [prev], [pl.BlockSpec(memory_space=pl.ANY)], {n_inputs: out_index}


def _retention(ret, r0, layer_in, pos0, prev, layer_out, n_layers):
    nb, t, _ = ret.shape
    q = math.gcd(t, RET_CHUNK)
    grp = _seq_group(nb)
    cos, sin = _rope_tables(pos0 + jnp.arange(t))
    dmask, xi, zeta, g_chunk = _ret_decay(q)
    bcast = lambda a: jnp.broadcast_to(a[:, :, None], (RET_H, a.shape[1], LANE))
    col = lambda j: pl.BlockSpec((grp, q, 512), lambda b, c: (b, c, j))
    tab = pl.BlockSpec((q, LANE), lambda b, c: (c, 0))
    tail = (RET_H, RET_DK, RET_DV)
    in_specs = [col(0), col(1), col(2), col(3), _layer_block(tail, layer_in, grp), tab, tab,
                _const_spec((RET_H, q, q)), _const_spec((RET_H, q, LANE)), _const_spec((RET_H, q, LANE)),
                _const_spec((RET_H, 8, LANE))]
    state_shape, extra, extra_specs, aliases = _stacked_out(prev, n_layers, nb, tail, len(in_specs), 1)
    return pl.pallas_call(
        _retention_kernel,
        grid=(nb // grp, t // q),
        in_specs=in_specs + extra_specs,
        out_specs=[col(0), _layer_block(tail, layer_out, grp)],
        out_shape=[jax.ShapeDtypeStruct((nb, t, 512), bf16), state_shape],
        input_output_aliases=aliases,
        compiler_params=_cparams(("parallel", "arbitrary")),
        name="retention_t%d" % t,
    )(ret, ret, ret, ret, r0, cos, sin, dmask, bcast(xi), bcast(zeta),
      jnp.broadcast_to(g_chunk[:, None, None], (RET_H, 8, LANE)), *extra)


def _sort_key(score):
    bits = lax.bitcast_convert_type(score, i32)
    bits = jnp.where(bits == INT_MIN, 0, bits)
    return jnp.where(bits < 0, bits ^ 0x7FFFFFFF, bits)


def _count(key_ref, nchunk, ck, pred):
    def body(c, acc):
        r0 = pl.multiple_of(c * ck, ck)
        hit = jnp.where(pred(key_ref[pl.ds(r0, ck), :], r0), 1, 0).astype(i32)
        return acc + jnp.sum(hit.reshape(ck // 8, 8, LANE), axis=0)
    acc = lax.fori_loop(0, nchunk, body, jnp.zeros((8, LANE), i32))
    return jnp.sum(acc, axis=0, keepdims=True)


def _select_topk(key_ref, nchunk, ck, topk, idx_bits):
    def bit_step(it, t):
        cand = t + jnp.left_shift(jnp.int32(1), 31 - it)
        cnt = _count(key_ref, nchunk, ck, lambda blk, r0: blk >= cand)
        return jnp.where(cnt >= topk, cand, t)

    t = lax.fori_loop(0, 32, bit_step, jnp.full((1, LANE), INT_MIN, i32))
    t = jnp.maximum(t, INT_MIN + 1)
    c_ge = _count(key_ref, nchunk, ck, lambda blk, r0: blk >= t)
    c_gt = _count(key_ref, nchunk, ck, lambda blk, r0: blk > t)
    _cut_ties(key_ref, nchunk, ck, t, c_ge > topk, topk - c_gt, idx_bits)
    return t


def _cut_ties(key_ref, nchunk, ck, t, surplus, keep, idx_bits):
    rows = lax.broadcasted_iota(i32, (ck, LANE), 0)

    @pl.when(jnp.max(jnp.where(surplus, 1, 0)) > 0)
    def _():
        want = jnp.where(surplus, keep, jnp.int32(2 ** 30))

        def idx_step(it, x):
            cand = x + jnp.left_shift(jnp.int32(1), idx_bits - 1 - it)
            cnt = _count(key_ref, nchunk, ck, lambda blk, r0: jnp.where(blk == t, rows + r0, cand) < cand)
            return jnp.where(cnt < want, cand, x)

        last = lax.fori_loop(0, idx_bits, idx_step, jnp.zeros((1, LANE), i32))

        def demote(c, carry):
            r0 = pl.multiple_of(c * ck, ck)
            blk = key_ref[pl.ds(r0, ck), :]
            drop = jnp.where(blk == t, rows + r0, last) > last
            key_ref[pl.ds(r0, ck), :] = jnp.where(drop, INT_MIN, blk)
            return carry

        lax.fori_loop(0, nchunk, demote, 0)


PLANE_ROWS = 256
_SWAP_STEPS = ((16, 0x0000FFFF), (8, 0x00FF00FF), (4, 0x0F0F0F0F), (2, 0x33333333), (1, 0x55555555))


def _bit_planes(words):
    a = list(words)
    for j, m in _SWAP_STEPS:
        for k in range(32):
            if k & j == 0:
                t = ((a[k] >> j) ^ a[k + j]) & m
                a[k] = a[k] ^ (t << j)
                a[k + j] = a[k + j] ^ t
    return a


def _select_topk_planes(key_ref, planes_ref, nchunk, ck, topk, idx_bits):
    u32 = jnp.uint32
    per_chunk = ck // PLANE_ROWS
    ngroups = nchunk * per_chunk

    def to_planes(g, carry):
        r0 = pl.multiple_of(g * PLANE_ROWS, PLANE_ROWS)
        words = [lax.bitcast_convert_type(key_ref[pl.ds(r0 + 8 * i, 8), :], u32) for i in range(32)]
        planes = _bit_planes(words)
        planes[31] = ~planes[31]
        for b in range(32):
            planes_ref[g, b] = lax.bitcast_convert_type(planes[b], i32)
        return carry

    lax.fori_loop(0, ngroups, to_planes, 0)

    gmax = planes_ref.shape[0]
    full = jnp.full((8, LANE), 0xFFFFFFFF, u32)
    none = jnp.zeros((8, LANE), u32)
    alive0 = tuple(jnp.where(g < ngroups, full, none) for g in range(gmax))

    def bit_step(it, carry):
        t, need, alive = carry
        b = 31 - it
        ones = [a & lax.bitcast_convert_type(planes_ref[g, b], u32) for g, a in enumerate(alive)]
        cnt = functools.reduce(lambda x, y: x + y, [lax.population_count(o) for o in ones])
        c = jnp.sum(cnt.astype(i32), axis=0, keepdims=True)
        take = c >= need
        t = t | jnp.where(take, jnp.left_shift(jnp.int32(1), b), 0)
        need = jnp.where(take, need, need - c)
        return t, need, tuple(jnp.where(take, o, a ^ o) for o, a in zip(ones, alive))

    init = (jnp.zeros((1, LANE), i32), jnp.full((1, LANE), topk, i32), alive0)
    t, need, alive = lax.fori_loop(0, 32, bit_step, init)
    t = t ^ INT_MIN
    equal = functools.reduce(lambda x, y: x + y, [lax.population_count(a) for a in alive])
    n_equal = jnp.sum(equal.astype(i32), axis=0, keepdims=True)
    surplus = (n_equal > need) & (t > INT_MIN)
    t = jnp.maximum(t, INT_MIN + 1)
    _cut_ties(key_ref, nchunk, ck, t, surplus, need, idx_bits)
    return t


def _dsa_prompt_kernel(qa_ref, qit_ref, kiwiq_ref, k_ref, vt_ref, kiwik_ref, bias_ref, o_ref, key_ref, acc_ref,
                       qbd_ref, planes_ref, *, topk, idx_bits, nch, ck):
    qb = pl.program_id(1)
    cb = ck // LANE
    nchunk = (qb + cb) // cb
    wit = jnp.transpose(kiwiq_ref[...])[IDX_DIM:IDX_DIM + IDX_H, :] * IDX_H ** -0.5
    qpos = qb * LANE + lax.broadcasted_iota(i32, (ck, LANE), 1)
    rows = lax.broadcasted_iota(i32, (ck, LANE), 0)

    def score_chunk(c, carry):
        r0 = pl.multiple_of(c * ck, ck)
        kic = kiwik_ref[pl.ds(r0, ck), 0:IDX_DIM].astype(bf16)
        acc = jnp.zeros((ck, LANE), f32)
        for h in range(IDX_H):
            s = _mm(kic, qit_ref[:, h * LANE:(h + 1) * LANE])
            acc = acc + jnp.maximum(s, 0.0) * wit[h:h + 1, :]
        key_ref[pl.ds(r0, ck), :] = jnp.where(rows + r0 <= qpos, _sort_key(acc), INT_MIN)
        return carry

    lax.fori_loop(0, nchunk, score_chunk, 0)
    t = _select_topk_planes(key_ref, planes_ref, nchunk, ck, topk, idx_bits)

    npair = ATT_H // 2
    for pr in range(npair):
        qbd_ref[pr] = jnp.zeros((2 * ATT_DH, 2 * LANE), bf16)
        for hh in range(2):
            sl = slice((2 * pr + hh) * ATT_DH, (2 * pr + hh + 1) * ATT_DH)
            qbd_ref[pr, hh * ATT_DH:(hh + 1) * ATT_DH, hh * LANE:(hh + 1) * LANE] = (
                jnp.transpose(qa_ref[:, sl].astype(f32)).astype(bf16))
    acc_ref[...] = jnp.zeros_like(acc_ref)
    nfar = jnp.maximum(qb - 1, 0) // cb

    pairs = range(npair)

    def att_chunks(near, unroll):
        def body(i, carry):
            ms, ls = list(carry[0]), list(carry[1])
            cs = [i * unroll + u for u in range(unroll)]
            r0s = [pl.multiple_of(c * ck, ck) for c in cs]
            qks = [[_mm(k_ref[pl.ds(r0, ck), pr * 2 * ATT_DH:(pr + 1) * 2 * ATT_DH], qbd_ref[pr]) for pr in pairs]
                   for r0 in r0s]
            for c, r0, qk in zip(cs, r0s, qks):
                neg1 = jnp.where(key_ref[pl.ds(r0, ck), :] >= t, 0.0, NEG_BIG)
                negm = jnp.concatenate([neg1, neg1], axis=1)
                alphas, ps = [], []
                for pr in pairs:
                    s = qk[pr] * (ATT_DH ** -0.5 * LOG2E) + negm
                    if near:
                        s = s + jnp.concatenate(
                            [bias_ref[jnp.clip(qb - (c * cb + j), 0, 2), pr] for j in range(cb)], axis=0)
                    m_new = jnp.maximum(ms[pr], jnp.max(s, axis=0, keepdims=True))
                    alpha = jnp.exp2(ms[pr] - m_new)
                    p = jnp.exp2(s - m_new)
                    ls[pr] = ls[pr] * alpha + jnp.sum(p, axis=0, keepdims=True)
                    ms[pr] = m_new
                    alphas.append(alpha)
                    ps.append(p.astype(bf16))
                pvs = [_mm(vt_ref[pr * nch + c], ps[pr]) for pr in pairs]
                for pr in pairs:
                    for hh in range(2):
                        d = slice(hh * LANE, (hh + 1) * LANE)
                        acc_ref[2 * pr + hh] = acc_ref[2 * pr + hh] * alphas[pr][:, d] + pvs[pr][d, d]
            return tuple(ms), tuple(ls)
        return body

    carry = ((jnp.full((1, 2 * LANE), NEG_BIG, f32),) * npair, (jnp.zeros((1, 2 * LANE), f32),) * npair)
    carry = lax.fori_loop(0, nfar // 2, att_chunks(False, 2), carry)
    carry = lax.fori_loop(nfar // 2 * 2, nfar, att_chunks(False, 1), carry)
    _, ls = lax.fori_loop(nfar, nchunk, att_chunks(True, 1), carry)
    for h in range(ATT_H):
        l = ls[h // 2][:, (h % 2) * LANE:(h % 2 + 1) * LANE]
        o_ref[:, h * ATT_DH:(h + 1) * ATT_DH] = jnp.transpose(acc_ref[h] / l).astype(bf16)


def _bias_tiles(rel_bias, nd):
    j = jnp.arange(LANE)[:, None]
    i = jnp.arange(LANE)[None, :]
    rel = jnp.arange(nd)[:, None, None] * LANE + (i - j)[None]
    return jnp.moveaxis(_bias_lookup(rel_bias, rel), -1, 1)


def _dsa_prompt(qa, kab, vt, qit, kiwi, rel_bias, nb, t):
    nqb = t // LANE
    topk = min(TOPK_MAX, t // 4)
    assert REL_MAX_DIST <= LANE + 1
    bias = _bias_tiles(rel_bias, 3)
    bias = (bias - bias[2:3]) * LOG2E
    npair = ATT_H // 2
    bias = bias.reshape(3, npair, 2, LANE, LANE).transpose(0, 1, 3, 2, 4).reshape(3, npair, LANE, 2 * LANE)
    ck = math.gcd(t, DSA_CHUNK)
    nch = t // ck
    vt = vt.reshape(nb * npair * nch, 2 * ATT_DH, ck)
    qrow = lambda c: pl.BlockSpec((LANE, c), lambda b, q: (b * nqb + q, 0))
    seq = lambda c: pl.BlockSpec((t, c), lambda b, q: (b, 0))
    kern = functools.partial(_dsa_prompt_kernel, topk=topk, idx_bits=max(1, (t - 1).bit_length()), nch=nch, ck=ck)
    return pl.pallas_call(
        kern,
        grid=(nb, nqb),
        in_specs=[qrow(512), pl.BlockSpec((None, IDX_DIM, IDX_H * LANE), lambda b, q: (b * nqb + q, 0, 0)),
                  qrow(LANE), seq(512),
                  pl.BlockSpec((npair * nch, 2 * ATT_DH, ck), lambda b, q: (b, 0, 0)), seq(LANE),
                  _const_spec((3, npair, LANE, 2 * LANE))],
        out_specs=qrow(512),
        out_shape=jax.ShapeDtypeStruct((nb * t, 512), bf16),
        scratch_shapes=[pltpu.VMEM((t, LANE), i32), pltpu.VMEM((ATT_H, ATT_DH, LANE), f32),
                        pltpu.VMEM((npair, 2 * ATT_DH, 2 * LANE), bf16),
                        pltpu.VMEM((t // PLANE_ROWS, 32, 8, LANE), i32)],
        compiler_params=_cparams(("parallel", "arbitrary")),
        name="dsa_prompt",
    )(qa, qit, kiwi, kab, vt, kiwi, bias)


def _softplus(x):
    return jnp.maximum(x, 0.0) + jnp.log1p(jnp.exp(-jnp.abs(x)))


def _cumsum_rows(tri, a):
    hi = a.astype(bf16)
    r1 = a - hi.astype(f32)
    mid = r1.astype(bf16)
    lo = (r1 - mid.astype(f32)).astype(bf16)
    return _mm(tri, hi) + _mm(tri, mid) + _mm(tri, lo)


def _odd_mixer_kernel(*refs, q, pos0):
    seq_in, shared, seq_out = refs[:7], refs[7:16], refs[-6:]
    for b in range(seq_in[0].shape[0]):
        _odd_mixer_seq(*(r.at[b] for r in seq_in), *shared, *(r.at[b] for r in seq_out), q=q, pos0=pos0)


def _odd_mixer_seq(u_ref, z_ref, xbc_ref, dt_ref, pp_ref, cp_ref, h0_ref, pw_ref, ps_ref, cw_ref, cb_ref,
                   dtb_ref, alog_ref, dsk_ref, nrm_ref, tri_ref, po_ref, y_ref, h_ref, ubuf, xbuf, ybuf, *, q, pos0):
    c = pl.program_id(1)

    @pl.when(c == 0)
    def _():
        ubuf[0:1, :] = jnp.zeros((1, POOL_DIM), f32)
        ubuf[1:16, :] = pp_ref[...]
        xbuf[0:8 - (CONV_W - 1), :] = jnp.zeros((8 - (CONV_W - 1), CONV_DIM), f32)
        xbuf[8 - (CONV_W - 1):8, :] = cp_ref[...]
        h_ref[...] = h0_ref[...]

    pos = pos0 + c * q + lax.broadcasted_iota(i32, (q, LANE), 0)
    causal = lax.broadcasted_iota(i32, (q, q), 0) >= lax.broadcasted_iota(i32, (q, q), 1)

    u = u_ref[...]
    ubuf[16:16 + q, :] = u
    for g, w in enumerate(POOL_WINDOWS):
        sl = slice(g * POOL_GC, (g + 1) * POOL_GC)
        acc = u[:, sl]
        for k in range(1, w):
            acc = acc + ubuf[16 - k:16 - k + q, sl]
        d = acc / jnp.minimum(pos + 1, w).astype(f32) - u[:, sl]
        po_ref[:, sl] = (_mm(d.astype(bf16), pw_ref[g]) * ps_ref[:, sl]).astype(bf16)
    ubuf[0:16, :] = ubuf[q:q + 16, :]

    xbuf[8:8 + q, :] = xbc_ref[...]
    conv = cb_ref[...]
    for j in range(CONV_W):
        off = 8 - (CONV_W - 1) + j
        conv = conv + xbuf[off:off + q, :] * cw_ref[j:j + 1, :]
    xbuf[0:8, :] = xbuf[q:q + 8, :]
    act = _silu(conv)
    xs = act[:, 0:D_INNER]

    dt = _softplus(dt_ref[...] + dtb_ref[...])
    a = dt * (-jnp.exp(alog_ref[...]))
    cs = _cumsum_rows(tri_ref[...], a)
    cs_t = jnp.transpose(cs)
    dt_t = jnp.transpose(dt)
    cs_last = cs[q - 1:q, :]
    w_end = jnp.exp(cs_last - cs) * dt
    ecs = jnp.exp(cs)
    hpg = SSM_H // SSM_G
    for g in range(SSM_G):
        bm = act[:, D_INNER + g * SSM_N:D_INNER + (g + 1) * SSM_N].astype(bf16)
        cm = act[:, D_INNER + (SSM_G + g) * SSM_N:D_INNER + (SSM_G + g + 1) * SSM_N].astype(bf16)
        cb = _mm_nt(cm, bm)
        for hh in range(hpg):
            h = g * hpg + hh
            psl = slice(h * SSM_P, (h + 1) * SSM_P)
            seg = cs[:, h:h + 1] - cs_t[h:h + 1, :]
            lm = jnp.exp(jnp.where(causal, seg, NEG_BIG))
            sc = cb * lm * dt_t[h:h + 1, :]
            xh = xs[:, psl]
            hs = h_ref[h]
            yh = _mm(sc.astype(bf16), xh.astype(bf16)) + _mm_nt(cm, hs.astype(bf16)) * ecs[:, h:h + 1]
            h_ref[h] = hs * jnp.exp(cs_last[:, h:h + 1]) + _mm_tn((xh * w_end[:, h:h + 1]).astype(bf16), bm)
            ybuf[:, psl] = yh
    y = (ybuf[...] + dsk_ref[...] * xs) * _silu(z_ref[...])
    gw = D_INNER // SSM_G
    for g in range(SSM_G):
        sl = slice(g * gw, (g + 1) * gw)
        yg = y[:, sl]
        y_ref[:, sl] = (yg * lax.rsqrt(jnp.mean(yg * yg, -1, keepdims=True) + EPS) * nrm_ref[:, sl]).astype(bf16)


def _pad_lanes(v):
    return jnp.pad(v.astype(f32), (0, LANE - v.shape[0]))[None, :]


def _odd_mixer(u, z, xbc, dt, pool_prev, conv_prev, h0, layer_in, prm, pos0, prev, layer_out, n_layers):
    pool_w, pool_scale, conv_w, conv_b, dt_bias, a_log, d_skip, ssm_norm = prm
    nb, t, _ = u.shape
    q = math.gcd(t, SSD_CHUNK)
    nc = t // q
    grp = _seq_group(nb)
    row = lambda c: pl.BlockSpec((grp, q, c), lambda b, i: (b, i, 0))
    lead = (nb, t)
    tri = jnp.tril(jnp.ones((q, q), bf16))
    tail = (SSM_H, SSM_P, SSM_N)
    in_specs = [row(POOL_DIM), row(D_INNER), row(CONV_DIM), row(LANE),
                _layer_block((POOL_PAST, POOL_DIM), layer_in, grp),
                _layer_block((CONV_W - 1, CONV_DIM), layer_in, grp), _layer_block(tail, layer_in, grp),
                _const_spec((POOL_GROUPS, POOL_GC, POOL_GC)), _const_spec((1, POOL_DIM)),
                _const_spec((CONV_W, CONV_DIM)), _const_spec((1, CONV_DIM)),
                _const_spec((1, LANE)), _const_spec((1, LANE)),
                _const_spec((1, D_INNER)), _const_spec((1, D_INNER)), _const_spec((q, q))]
    state_shape, extra, extra_specs, aliases = _stacked_out(prev, n_layers, nb, tail, len(in_specs), 2)
    return pl.pallas_call(
        functools.partial(_odd_mixer_kernel, q=q, pos0=pos0),
        grid=(nb // grp, nc),
        in_specs=in_specs + extra_specs,
        out_specs=[row(POOL_DIM), row(D_INNER), _layer_block(tail, layer_out, grp)],
        out_shape=[jax.ShapeDtypeStruct(lead + (POOL_DIM,), bf16), jax.ShapeDtypeStruct(lead + (D_INNER,), bf16),
                   state_shape],
        input_output_aliases=aliases,
        scratch_shapes=[pltpu.VMEM((grp, q + 16, POOL_DIM), f32), pltpu.VMEM((grp, q + 8, CONV_DIM), f32),
                        pltpu.VMEM((grp, q, D_INNER), f32)],
        compiler_params=_cparams(("parallel", "arbitrary")),
        name="odd_mixer_t%d" % t,
    )(u, z, xbc, dt, pool_prev, conv_prev, h0, pool_w.astype(bf16), pool_scale[None, :], conv_w, conv_b[None, :],
      _pad_lanes(dt_bias), _pad_lanes(a_log), jnp.repeat(d_skip, SSM_P)[None, :], ssm_norm[None, :], tri, *extra)


def _dsa_sample_score_kernel(pt_ref, qi_ref, kiwi_ref, *rest, npg):
    pages, o_ref = rest[:npg], rest[npg]
    ts = qi_ref.shape[0]
    kiwi = kiwi_ref[...]
    wi = kiwi[:, IDX_DIM:IDX_DIM + IDX_H] * IDX_H ** -0.5
    ki_new = jnp.transpose(jnp.concatenate([kiwi, jnp.zeros((PAGE - ts, LANE), f32)], axis=0))[0:IDX_DIM, :]
    ki = jnp.concatenate([p[...] for p in pages] + [ki_new], axis=1).astype(bf16)
    nk = ki.shape[1]
    acc = jnp.zeros((ts, nk), f32)
    for h in range(IDX_H):
        s = _mm(qi_ref[:, h * IDX_DIM:(h + 1) * IDX_DIM], ki)
        acc = acc + jnp.maximum(s, 0.0) * wi[:, h:h + 1]
    col = lax.broadcasted_iota(i32, (ts, nk), 1)
    row = lax.broadcasted_iota(i32, (ts, nk), 0)
    o_ref[...] = jnp.where(col <= npg * PAGE + row, _sort_key(acc), INT_MIN)


def _select_kernel(k_ref, o_ref, key_ref, *, topk, idx_bits):
    key_ref[...] = k_ref[...]
    t = _select_topk(key_ref, 1, key_ref.shape[0], topk, idx_bits)
    o_ref[...] = jnp.where(key_ref[...] >= t, 1.0, 0.0)


def _dsa_sample_attn_kernel(pt_ref, q_ref, kn_ref, vn_ref, sel_ref, bias_ref, spread_ref, *rest, npg):
    kpages, vpages, o_ref = rest[:npg], rest[npg:2 * npg], rest[2 * npg]
    pad = jnp.zeros((PAGE * ATT_H - kn_ref.shape[0], ATT_DH), f32)
    kx = jnp.concatenate([p[...] for p in kpages] + [kn_ref[...], pad], axis=0).astype(bf16)
    vx = jnp.concatenate([p[...] for p in vpages] + [vn_ref[...], pad], axis=0).astype(bf16)
    selx = jnp.concatenate([_mm(sel_ref[:, j * PAGE:(j + 1) * PAGE], spread_ref[...]) for j in range(npg + 1)],
                           axis=1)
    s = _mm_nt(q_ref[...], kx) * ATT_DH ** -0.5 + jnp.where(selx > 0.5, bias_ref[...], NEG_BIG)
    p = jnp.exp(s - jnp.max(s, axis=-1, keepdims=True))
    o_ref[...] = (_mm(p.astype(bf16), vx) / jnp.sum(p, axis=-1, keepdims=True)).astype(bf16)


def _dsa_sample(qa, ka, va, qi, kiwi, cache_k, cache_v, cache_ki, layer, page_table, rel_bias):
    db, ts = qa.shape[:2]
    npg = page_table.shape[1]
    n_past = npg * PAGE
    nk = n_past + PAGE
    nq = db * ts
    topk = min(TOPK_MAX, (n_past + ts) // 4)
    hd = ATT_H * ATT_DH
    ki_page = lambda j: pl.BlockSpec((None, None, IDX_DIM, PAGE), lambda b, pt: (layer, pt[b, j], 0, 0))
    seq = lambda r, c: pl.BlockSpec((None, r, c), lambda b, pt: (b, 0, 0))

    keys = pl.pallas_call(
        functools.partial(_dsa_sample_score_kernel, npg=npg),
        grid_spec=pltpu.PrefetchScalarGridSpec(
            num_scalar_prefetch=1, grid=(db,),
            in_specs=[seq(ts, IDX_H * IDX_DIM), seq(ts, LANE)] + [ki_page(j) for j in range(npg)],
            out_specs=seq(ts, nk)),
        out_shape=jax.ShapeDtypeStruct((db, ts, nk), i32),
        compiler_params=_cparams(("parallel",)),
        name="dsa_sample_score",
    )(page_table, qi, kiwi, *([jnp.swapaxes(cache_ki, 2, 3)] * npg))

    col = pl.BlockSpec((nk, LANE), lambda i: (0, i))
    sel = pl.pallas_call(
        functools.partial(_select_kernel, topk=topk, idx_bits=max(1, (nk - 1).bit_length())),
        grid=(nq // LANE,),
        in_specs=[col],
        out_specs=col,
        out_shape=jax.ShapeDtypeStruct((nk, nq), f32),
        scratch_shapes=[pltpu.VMEM((nk, LANE), i32)],
        compiler_params=_cparams(("parallel",)),
        name="dsa_sample_select",
    )(keys.reshape(nq, nk).T)
    sel = jnp.repeat(sel.T.reshape(db, ts, nk), ATT_H, axis=1).astype(bf16)
    rel = n_past + jnp.arange(ts)[:, None] - jnp.arange(nk)[None, :]
    bias = jnp.repeat(jnp.moveaxis(_bias_lookup(rel_bias, rel), -1, 1), ATT_H, axis=-1)
    same_head = jnp.arange(nk * ATT_H)[None, :] % ATT_H == jnp.arange(ATT_H)[:, None]
    bias = jnp.where(same_head[None], bias, NEG_BIG).reshape(ts * ATT_H, nk * ATT_H)
    spread = (jnp.arange(PAGE * ATT_H)[None, :] // ATT_H == jnp.arange(PAGE)[:, None]).astype(bf16)

    n_pages = cache_k.shape[1]
    rows = PAGE * ATT_H
    kv_page = lambda j: pl.BlockSpec((rows, ATT_DH), lambda b, pt: (layer * n_pages + pt[b, j], 0))
    const = lambda a: pl.BlockSpec(a.shape, lambda b, pt: (0, 0))
    as_rows = lambda a: a.reshape(db, ts * ATT_H, ATT_DH)
    out = pl.pallas_call(
        functools.partial(_dsa_sample_attn_kernel, npg=npg),
        grid_spec=pltpu.PrefetchScalarGridSpec(
            num_scalar_prefetch=1, grid=(db,),
            in_specs=[seq(ts * ATT_H, ATT_DH)] * 3 + [seq(ts * ATT_H, nk), const(bias), const(spread)]
                     + [kv_page(j) for j in range(npg)] * 2,
            out_specs=seq(ts * ATT_H, ATT_DH)),
        out_shape=jax.ShapeDtypeStruct((db, ts * ATT_H, ATT_DH), bf16),
        compiler_params=_cparams(("parallel",)),
        name="dsa_sample_attn",
    )(page_table, as_rows(qa), as_rows(ka), as_rows(va), sel, bias, spread,
      *([cache_k.reshape(-1, ATT_DH)] * npg), *([cache_v.reshape(-1, ATT_DH)] * npg))
    return out.reshape(db, ts, hd)


def kernel(x_prompt, x_sample, cache_k, cache_v, cache_kidx, state_ret, state_pool, state_conv, state_ssm,
           page_table, norm_mix, norm_ffn, w_in_even, w_out_even, q_norm, k_norm, rel_bias,
           w_in_odd, w_out_odd, pool_w, pool_scale, conv_w, conv_b, dt_bias, a_log, d_skip, ssm_norm,
           w_gate, w_up, w_down):
    bp, sp, d = x_prompt.shape
    db, ts, _ = x_sample.shape
    n_p, n_s = bp * sp, db * ts
    assert d == D_MODEL and n_p % TM == 0 and n_s % TM == 0 and sp % LANE == 0 and sp >= POOL_PAST
    n_past = page_table.shape[1] * PAGE
    n_even, n_odd = (DEPTH + 1) // 2, DEPTH // 2
    hd = ATT_H * ATT_DH
    xp, xs = x_prompt.reshape(n_p, d), x_sample.reshape(n_s, d)
    seqs = lambda a: a.reshape(db, ts, a.shape[-1])
    pseqs = lambda a: a.reshape(bp, sp, a.shape[-1])
    last = lambda a, n: a.reshape(bp, sp, a.shape[-1])[:, sp - n:]
    tail = lambda prev, cur, n: jnp.concatenate([prev.astype(f32), cur], axis=1)[:, -n:]
    zeros = lambda *s: jnp.zeros((1, bp) + s, f32)
    outs = [[] for _ in range(14)]
    rstate_p = rstate_s = h_p = h_s = kv_p = kv_s = None
    w_in_e = jnp.pad(w_in_even, ((0, 0), (0, 0), (0, EVEN_PROJ_PAD - EVEN_PROJ))).astype(bf16)
    w_in_o = jnp.pad(w_in_odd, ((0, 0), (0, 0), (0, ODD_PROJ_PAD - ODD_PROJ))).astype(bf16)
    w_out_e, w_out_o = w_out_even.astype(bf16), w_out_odd.astype(bf16)
    w_ffn = (w_gate.astype(bf16), w_up.astype(bf16), w_down.astype(bf16))
    for l in range(DEPTH):
        if l % 2 == 0:
            i = l // 2
            w_out, mix_layer = w_out_e, i
            prm = (norm_mix[l][None], w_in_e, q_norm[i][None], k_norm[i][None])
            ret_p, qa_p, ka_p, kab_p, va_p, kiwi_p, qit_p, vt_p = _even_proj(xp, *prm, i, n_even, kv_p, prompt_len=sp)
            ret_s, qa_s, ka_s, _, va_s, kiwi_s, qi_s = _even_proj(xs, *prm, i, n_even, kv_s)
            kv_p, kv_s = (ka_p, va_p), (ka_s, va_s)
            mix_a_p, rstate_p = _retention(pseqs(ret_p), zeros(RET_H, RET_DK, RET_DV), 0, 0, rstate_p, i, n_even)
            mix_a_s, rstate_s = _retention(seqs(ret_s), state_ret, i, n_past, rstate_s, i, n_even)
            mix_b_p = _dsa_prompt(qa_p, kab_p, vt_p, qit_p, kiwi_p, rel_bias, bp, sp)
            mix_b_s = _dsa_sample(seqs(qa_s), ka_s[i], va_s[i], seqs(qi_s), seqs(kiwi_s),
                                  cache_k, cache_v, cache_kidx, i, page_table, rel_bias)
            new = [None, None, kiwi_p.reshape(bp, sp, LANE)[..., :IDX_DIM], None, None, None, None,
                   None, None, seqs(kiwi_s)[..., :IDX_DIM], None, None, None, None]
        else:
            j = l // 2
            w_out, mix_layer = w_out_o, j
            u_p, z_p, xbc_p, dt_p = _odd_proj(xp, norm_mix[l][None], w_in_o, j)
            u_s, z_s, xbc_s, dt_s = _odd_proj(xs, norm_mix[l][None], w_in_o, j)
            prm = (pool_w[j], pool_scale[j], conv_w[j], conv_b[j], dt_bias[j], a_log[j], d_skip[j], ssm_norm[j])
            mix_a_p, mix_b_p, h_p = _odd_mixer(
                pseqs(u_p), pseqs(z_p), pseqs(xbc_p), pseqs(dt_p), zeros(POOL_PAST, POOL_DIM),
                zeros(CONV_W - 1, CONV_DIM), zeros(SSM_H, SSM_P, SSM_N), 0, prm, 0, h_p, j, n_odd)
            mix_a_s, mix_b_s, h_s = _odd_mixer(
                seqs(u_s), seqs(z_s), seqs(xbc_s), seqs(dt_s), state_pool, state_conv, state_ssm, j,
                prm, n_past, h_s, j, n_odd)
            new = [None, None, None, None, last(u_p, POOL_PAST), last(xbc_p, CONV_W - 1), None,
                   None, None, None, None,
                   tail(state_pool[j], seqs(u_s), POOL_PAST), tail(state_conv[j], seqs(xbc_s), CONV_W - 1), None]
        for acc, leaf in zip(outs, new):
            if leaf is not None:
                acc.append(leaf)
        ffn = (w_out, mix_layer, norm_ffn[l][None], *w_ffn, l)
        xp = _out_ffn(xp, mix_a_p.reshape(n_p, -1), mix_b_p.reshape(n_p, -1), *ffn)
        xs = _out_ffn(xs, mix_a_s.reshape(n_s, -1), mix_b_s.reshape(n_s, -1), *ffn)
    leaves = [jnp.stack(a) if a else None for a in outs]
    leaves[3], leaves[6], leaves[10], leaves[13] = rstate_p, h_p, rstate_s, h_s
    leaves[0], leaves[1] = (a.reshape(n_even, bp, sp, ATT_H, ATT_DH) for a in kv_p)
    leaves[7], leaves[8] = (a.reshape(n_even, db, ts, ATT_H, ATT_DH) for a in kv_s)
    return (xp.reshape(bp, sp, d), xs.reshape(db, ts, d)) + tuple(leaves)
```

```python
import functools
import math

import jax
import jax.numpy as jnp
import numpy as np
from jax import lax
from jax.experimental import pallas as pl
from jax.experimental.pallas import tpu as pltpu

f32 = jnp.float32
bf16 = jnp.bfloat16
i32 = jnp.int32

D_MODEL = 1024
DEPTH = 4
PAGE = 128
RET_H, RET_DK, RET_DV, RET_CHUNK = 4, 128, 128, 128
ROPE_BASE = 10000.0
ATT_H, ATT_DH, ATT_BLOCK = 4, 128, 128
IDX_H, IDX_DIM = 8, 64
TOPK_MAX = 256
REL_BUCKETS, REL_MAX_DIST = 32, 128
POOL_WINDOWS = (2, 4, 8, 16)
POOL_GROUPS = 4
POOL_DIM = D_MODEL // 2
POOL_GC = POOL_DIM // POOL_GROUPS
POOL_PAST = 15
D_INNER = D_MODEL // 2
SSM_P = 64
SSM_H = D_INNER // SSM_P
SSM_G = 2
SSM_N = 128
CONV_W = 4
CONV_DIM = D_INNER + 2 * SSM_G * SSM_N
SSD_CHUNK = 128
FF_DIM = -(-8 * D_MODEL // (3 * 256)) * 256
EPS = 1e-6

EVEN_PROJ = 4 * 512 + 3 * 512 + 512 + IDX_DIM + IDX_H
EVEN_PROJ_PAD = 4224
ODD_PROJ = POOL_DIM + D_INNER + CONV_DIM + SSM_H
ODD_PROJ_PAD = 2176

LANE = 128
INT_MIN = -(2 ** 31)
NEG_BIG = -1e30
LOG2E = 1.4426950408889634
VMEM_LIMIT = 56 * 1024 * 1024
TM = 256
FF_SPLIT = 2
DSA_CHUNK = 512
SEQ_GROUP = 8


def _cparams(sem):
    return pltpu.CompilerParams(dimension_semantics=sem, vmem_limit_bytes=VMEM_LIMIT)


def _mm(a, b):
    return jnp.dot(a, b, preferred_element_type=f32)


def _mm_nt(a, b):
    return lax.dot_general(a, b, (((1,), (1,)), ((), ())), preferred_element_type=f32)


def _mm_tn(a, b):
    return lax.dot_general(a, b, (((0,), (0,)), ((), ())), preferred_element_type=f32)


def _rms(x, g):
    return x * lax.rsqrt(jnp.mean(x * x, -1, keepdims=True) + EPS) * g


def _silu(x):
    return x / (1.0 + jnp.exp(-x))


def _const_spec(shape):
    nd = len(shape)
    return pl.BlockSpec(shape, lambda *a: (0,) * nd)


def _even_proj_kernel(x_ref, g_ref, w_ref, qg_ref, kg_ref, *refs, for_prompt, n_aliased):
    ret_ref, qa_ref, ka_ref, kab_ref, va_ref, kiwi_ref, *idx_refs = refs[n_aliased:]
    xb = _rms(x_ref[...], g_ref[...]).astype(bf16)
    ret_ref[...] = _mm(xb, w_ref[:, 0:2048])
    qa = _mm(xb, w_ref[:, 2048:2560])
    ka = _mm(xb, w_ref[:, 2560:3072])
    va = _mm(xb, w_ref[:, 3072:3584])
    for h in range(ATT_H):
        sl = slice(h * ATT_DH, (h + 1) * ATT_DH)
        qa_ref[:, sl] = _rms(qa[:, sl], qg_ref[...]).astype(bf16)
        kn = _rms(ka[:, sl], kg_ref[...])
        ka_ref[:, h, :] = kn
        kab_ref[:, sl] = kn.astype(bf16)
        va_ref[:, h, :] = va[:, sl]
    qi = _mm(xb, w_ref[:, 3584:4096]) * IDX_DIM ** -0.5
    kiwi_ref[...] = _mm(xb, w_ref[:, 4096:4224])
    if not for_prompt:
        idx_refs[0][...] = qi.astype(bf16)
        return
    qit_ref, vt_ref = idx_refs
    for blk in range(TM // LANE):
        for h in range(IDX_H):
            part = qi[blk * LANE:(blk + 1) * LANE, h * IDX_DIM:(h + 1) * IDX_DIM]
            qit_ref[blk, :, h * LANE:(h + 1) * LANE] = jnp.transpose(part).astype(bf16)
    for pr in range(ATT_H // 2):
        vt_ref[pr] = jnp.transpose(va[:, pr * 2 * ATT_DH:(pr + 1) * 2 * ATT_DH]).astype(bf16)


def _even_proj(x, g, w, qg, kg, layer, n_layers, prev_kv, prompt_len=None):
    n = x.shape[0]
    row = lambda c: pl.BlockSpec((TM, c), lambda i: (i, 0))
    heads = pl.BlockSpec((None, TM, ATT_H, ATT_DH), lambda i: (layer, i, 0, 0))
    stacked = jax.ShapeDtypeStruct((n_layers, n, ATT_H, ATT_DH), f32)
    out_specs = [row(2048), row(512), heads, row(512), heads, row(LANE)]
    out_shape = [jax.ShapeDtypeStruct((n, 2048), f32), jax.ShapeDtypeStruct((n, 512), bf16), stacked,
                 jax.ShapeDtypeStruct((n, 512), bf16), stacked, jax.ShapeDtypeStruct((n, LANE), f32)]
    extra = [] if prev_kv is None else list(prev_kv)
    aliases = {} if prev_kv is None else {5: 2, 6: 4}
    if prompt_len is None:
        out_specs.append(row(512))
        out_shape.append(jax.ShapeDtypeStruct((n, 512), bf16))
    else:
        ck = math.gcd(prompt_len, DSA_CHUNK)
        assert ck % TM == 0 and prompt_len % ck == 0
        per_chunk, per_seq, npair = ck // TM, prompt_len // TM, ATT_H // 2
        out_specs += [pl.BlockSpec((TM // LANE, IDX_DIM, IDX_H * LANE), lambda i: (i, 0, 0)),
                      pl.BlockSpec((None, npair, None, 2 * ATT_DH, TM),
                                   lambda i: (i // per_seq, 0, (i % per_seq) // per_chunk, 0, i % per_chunk))]
        out_shape += [jax.ShapeDtypeStruct((n // LANE, IDX_DIM, IDX_H * LANE), bf16),
                      jax.ShapeDtypeStruct((n // prompt_len, npair, prompt_len // ck, 2 * ATT_DH, ck), bf16)]
    return pl.pallas_call(
        functools.partial(_even_proj_kernel, for_prompt=prompt_len is not None, n_aliased=len(extra)),
        grid=(n // TM,),
        in_specs=[row(D_MODEL), _const_spec((1, D_MODEL)), _layer_weight(w, layer),
                  _const_spec((1, ATT_DH)), _const_spec((1, ATT_DH))]
                 + [pl.BlockSpec(memory_space=pl.ANY)] * len(extra),
        out_specs=out_specs,
        out_shape=out_shape,
        input_output_aliases=aliases,
        compiler_params=_cparams(("parallel",)),
        name="even_proj",
    )(x, g, w, qg, kg, *extra)


def _odd_proj_kernel(x_ref, g_ref, w_ref, u_ref, z_ref, xbc_ref, dt_ref):
    xb = _rms(x_ref[...], g_ref[...]).astype(bf16)
    u_ref[...] = _mm(xb, w_ref[:, 0:512])
    z_ref[...] = _mm(xb, w_ref[:, 512:1024])
    xbc_ref[...] = _mm(xb, w_ref[:, 1024:2048])
    dt_ref[...] = _mm(xb, w_ref[:, 2048:2176])


def _odd_proj(x, g, w, layer):
    n = x.shape[0]
    row = lambda c: pl.BlockSpec((TM, c), lambda i: (i, 0))
    outs = [512, 512, 1024, LANE]
    return pl.pallas_call(
        _odd_proj_kernel,
        grid=(n // TM,),
        in_specs=[row(D_MODEL), _const_spec((1, D_MODEL)), _layer_weight(w, layer)],
        out_specs=[row(c) for c in outs],
        out_shape=[jax.ShapeDtypeStruct((n, c), f32) for c in outs],
        compiler_params=_cparams(("parallel",)),
        name="odd_proj",
    )(x, g, w)


def _out_ffn_kernel(x_ref, a_ref, b_ref, wo_ref, g_ref, wg_ref, wu_ref, wd_ref, o_ref):
    half = wo_ref.shape[0] // 2
    x = x_ref[...] + _mm(a_ref[...], wo_ref[0:half, :]) + _mm(b_ref[...], wo_ref[half:, :])
    hb = _rms(x, g_ref[...]).astype(bf16)
    fc = FF_DIM // FF_SPLIT
    ff = None
    for c in range(FF_SPLIT):
        sl = slice(c * fc, (c + 1) * fc)
        act = (_silu(_mm(hb, wg_ref[:, sl])) * _mm(hb, wu_ref[:, sl])).astype(bf16)
        down = _mm(act, wd_ref[sl, :])
        ff = down if ff is None else ff + down
    o_ref[...] = x + ff


def _layer_weight(w, layer, single_buffer=False):
    mode = dict(pipeline_mode=pl.Buffered(1)) if single_buffer else {}
    return pl.BlockSpec((None,) + w.shape[1:], lambda i: (layer,) + (0,) * (w.ndim - 1), **mode)


def _out_ffn(x, a, b, wo, mix_layer, g, wg, wu, wd, layer):
    n = x.shape[0]
    row = lambda c: pl.BlockSpec((TM, c), lambda i: (i, 0))
    return pl.pallas_call(
        _out_ffn_kernel,
        grid=(n // TM,),
        in_specs=[row(D_MODEL), row(a.shape[1]), row(b.shape[1]), _layer_weight(wo, mix_layer, True),
                  _const_spec((1, D_MODEL)), _layer_weight(wg, layer, True), _layer_weight(wu, layer, True),
                  _layer_weight(wd, layer, True)],
        out_specs=row(D_MODEL),
        out_shape=jax.ShapeDtypeStruct((n, D_MODEL), f32),
        compiler_params=_cparams(("parallel",)),
        name="out_ffn",
    )(x, a, b, wo, g, wg, wu, wd)


def _rope_tables(pos):
    half = RET_DK // 2
    inv = ROPE_BASE ** (-jnp.linspace(0.0, 1.0, half, dtype=f32))
    ang = pos.astype(f32)[:, None] * inv[None, :]
    cos, sin = jnp.cos(ang), jnp.sin(ang)
    return jnp.concatenate([cos, cos], -1), jnp.concatenate([-sin, sin], -1)


def _ret_decay(q):
    log_g = jnp.log1p(-jnp.exp2(-5.0 - jnp.arange(RET_H, dtype=f32)))
    idx = jnp.arange(q, dtype=f32)
    diff = idx[:, None] - idx[None, :]
    dmask = jnp.where(diff[None] >= 0, jnp.exp(log_g[:, None, None] * jnp.maximum(diff, 0.0)[None]), 0.0)
    xi = jnp.exp(log_g[:, None] * (idx + 1.0)[None])
    zeta = jnp.exp(log_g[:, None] * (q - 1.0 - idx)[None])
    g_chunk = jnp.exp(log_g * q)
    return dmask, xi, zeta, g_chunk


def _t5_bucket(rel):
    n = jnp.maximum(rel, 0)
    max_exact = REL_BUCKETS // 2
    nf = jnp.maximum(n, 1).astype(f32)
    large = max_exact + (jnp.log(nf / max_exact) / math.log(REL_MAX_DIST / max_exact)
                         * (REL_BUCKETS - max_exact)).astype(i32)
    large = jnp.minimum(large, REL_BUCKETS - 1)
    return jnp.where(n < max_exact, n, large)


def _bias_lookup(rel_bias, rel):
    onehot = jax.nn.one_hot(_t5_bucket(rel), REL_BUCKETS, dtype=f32)
    return jnp.einsum("...b,bh->...h", onehot, rel_bias.astype(f32), precision=lax.Precision.HIGHEST)


def _rotary(x, c, s):
    return x * c + pltpu.roll(x, RET_DK // 2, 1) * s


def _retention_kernel(q_ref, k_ref, v_ref, g_ref, r0_ref, c_ref, s_ref, dm_ref, xi_ref, zt_ref, gc_ref,
                      *rest):
    o_ref, r_ref = rest[-2:]

    @pl.when(pl.program_id(1) == 0)
    def _():
        r_ref[...] = r0_ref[...]

    cos, sin = c_ref[...], s_ref[...]
    for b in range(q_ref.shape[0]):
        for h in range(RET_H):
            sl = slice(h * RET_DK, (h + 1) * RET_DK)
            qr = _rotary(q_ref[b, :, sl], cos, sin).astype(bf16)
            kr = _rotary(k_ref[b, :, sl], cos, sin) * RET_DK ** -0.5
            vb = v_ref[b, :, sl].astype(bf16)
            r = r_ref[b, h]
            s = _mm_nt(qr, kr.astype(bf16)) * dm_ref[h]
            o = _mm(s.astype(bf16), vb) + _mm(qr, r.astype(bf16)) * xi_ref[h]
            r_ref[b, h] = r * gc_ref[h, 0:1, :] + _mm_tn((kr * zt_ref[h]).astype(bf16), vb)
            o = o * lax.rsqrt(jnp.mean(o * o, -1, keepdims=True) + EPS)
            o_ref[b, :, sl] = (_silu(g_ref[b, :, sl]) * o).astype(bf16)


def _seq_group(nb):
    return math.gcd(nb, SEQ_GROUP)


def _layer_block(tail, layer, group):
    return pl.BlockSpec((None, group) + tail, lambda b, c: (layer, b) + (0,) * len(tail))


def _stacked_out(prev, n_layers, nb, tail, n_inputs, out_index):
    shape = jax.ShapeDtypeStruct((n_layers, nb) + tail, f32)
    if prev is None:
        return shape, [], [], {}
    return shape, [prev], [pl.BlockSpec(memory_space=pl.ANY)], {n_inputs: out_index}


def _retention(ret, r0, layer_in, pos0, prev, layer_out, n_layers):
    nb, t, _ = ret.shape
    q = math.gcd(t, RET_CHUNK)
    grp = _seq_group(nb)
    cos, sin = _rope_tables(pos0 + jnp.arange(t))
    dmask, xi, zeta, g_chunk = _ret_decay(q)
    bcast = lambda a: jnp.broadcast_to(a[:, :, None], (RET_H, a.shape[1], LANE))
    col = lambda j: pl.BlockSpec((grp, q, 512), lambda b, c: (b, c, j))
    tab = pl.BlockSpec((q, LANE), lambda b, c: (c, 0))
    tail = (RET_H, RET_DK, RET_DV)
    in_specs = [col(0), col(1), col(2), col(3), _layer_block(tail, layer_in, grp), tab, tab,
                _const_spec((RET_H, q, q)), _const_spec((RET_H, q, LANE)), _const_spec((RET_H, q, LANE)),
                _const_spec((RET_H, 8, LANE))]
    state_shape, extra, extra_specs, aliases = _stacked_out(prev, n_layers, nb, tail, len(in_specs), 1)
    return pl.pallas_call(
        _retention_kernel,
        grid=(nb // grp, t // q),
        in_specs=in_specs + extra_specs,
        out_specs=[col(0), _layer_block(tail, layer_out, grp)],
        out_shape=[jax.ShapeDtypeStruct((nb, t, 512), bf16), state_shape],
        input_output_aliases=aliases,
        compiler_params=_cparams(("parallel", "arbitrary")),
        name="retention_t%d" % t,
    )(ret, ret, ret, ret, r0, cos, sin, dmask, bcast(xi), bcast(zeta),
      jnp.broadcast_to(g_chunk[:, None, None], (RET_H, 8, LANE)), *extra)


def _sort_key(score):
    bits = lax.bitcast_convert_type(score, i32)
    bits = jnp.where(bits == INT_MIN, 0, bits)
    return jnp.where(bits < 0, bits ^ 0x7FFFFFFF, bits)


def _count(key_ref, nchunk, ck, pred):
    def body(c, acc):
        r0 = pl.multiple_of(c * ck, ck)
        hit = jnp.where(pred(key_ref[pl.ds(r0, ck), :], r0), 1, 0).astype(i32)
        return acc + jnp.sum(hit.reshape(ck // 8, 8, LANE), axis=0)
    acc = lax.fori_loop(0, nchunk, body, jnp.zeros((8, LANE), i32))
    return jnp.sum(acc, axis=0, keepdims=True)


def _cut_ties(key_ref, nchunk, ck, t, surplus, keep, idx_bits):
    rows = lax.broadcasted_iota(i32, (ck, LANE), 0)

    @pl.when(jnp.max(jnp.where(surplus, 1, 0)) > 0)
    def _():
        want = jnp.where(surplus, keep, jnp.int32(2 ** 30))

        def idx_step(it, x):
            cand = x + jnp.left_shift(jnp.int32(1), idx_bits - 1 - it)
            cnt = _count(key_ref, nchunk, ck, lambda blk, r0: jnp.where(blk == t, rows + r0, cand) < cand)
            return jnp.where(cnt < want, cand, x)

        last = lax.fori_loop(0, idx_bits, idx_step, jnp.zeros((1, LANE), i32))

        def demote(c, carry):
            r0 = pl.multiple_of(c * ck, ck)
            blk = key_ref[pl.ds(r0, ck), :]
            drop = jnp.where(blk == t, rows + r0, last) > last
            key_ref[pl.ds(r0, ck), :] = jnp.where(drop, INT_MIN, blk)
            return carry

        lax.fori_loop(0, nchunk, demote, 0)


PLANE_ROWS = 256
_SWAP_STEPS = ((16, 0x0000FFFF), (8, 0x00FF00FF), (4, 0x0F0F0F0F), (2, 0x33333333), (1, 0x55555555))


def _bit_planes(words):
    a = list(words)
    for j, m in _SWAP_STEPS:
        for k in range(32):
            if k & j == 0:
                t = ((a[k] >> j) ^ a[k + j]) & m
                a[k] = a[k] ^ (t << j)
                a[k + j] = a[k + j] ^ t
    return a


def _select_topk_planes(key_ref, planes_ref, nchunk, ck, topk, idx_bits):
    u32 = jnp.uint32
    per_chunk = ck // PLANE_ROWS
    ngroups = nchunk * per_chunk

    def to_planes(g, carry):
        r0 = pl.multiple_of(g * PLANE_ROWS, PLANE_ROWS)
        words = [lax.bitcast_convert_type(key_ref[pl.ds(r0 + 8 * i, 8), :], u32) for i in range(32)]
        planes = _bit_planes(words)
        planes[31] = ~planes[31]
        for b in range(32):
            planes_ref[g, b] = lax.bitcast_convert_type(planes[b], i32)
        return carry

    lax.fori_loop(0, ngroups, to_planes, 0)

    gmax = planes_ref.shape[0]
    full = jnp.full((8, LANE), 0xFFFFFFFF, u32)
    none = jnp.zeros((8, LANE), u32)

    def radix_select(ng):
        def run():
            alive0 = tuple(jnp.where(g < ngroups, full, none) for g in range(ng))

            def bit_step(it, carry):
                t, need, alive = carry
                b = 31 - it
                ones = [a & lax.bitcast_convert_type(planes_ref[g, b], u32) for g, a in enumerate(alive)]
                cnt = functools.reduce(lambda x, y: x + y, [lax.population_count(o) for o in ones])
                c = jnp.sum(cnt.astype(i32), axis=0, keepdims=True)
                take = c >= need
                t = t | jnp.where(take, jnp.left_shift(jnp.int32(1), b), 0)
                need = jnp.where(take, need, need - c)
                return t, need, tuple(jnp.where(take, o, a ^ o) for o, a in zip(ones, alive))

            init = (jnp.zeros((1, LANE), i32), jnp.full((1, LANE), topk, i32), alive0)
            t, need, alive = lax.fori_loop(0, 32, bit_step, init)
            equal = functools.reduce(lambda x, y: x + y, [lax.population_count(a) for a in alive])
            return t, need, jnp.sum(equal.astype(i32), axis=0, keepdims=True)
        return run

    sizes = sorted({max(1, gmax // 4), max(1, gmax // 2), gmax})
    if isinstance(ngroups, int):
        t, need, n_equal = radix_select(min(s for s in sizes if s >= ngroups))()
    else:
        which = sum((ngroups > s).astype(i32) for s in sizes[:-1]) if len(sizes) > 1 else jnp.int32(0)
        t, need, n_equal = lax.switch(which, [radix_select(s) for s in sizes])
    t = t ^ INT_MIN
    surplus = (n_equal > need) & (t > INT_MIN)
    t = jnp.maximum(t, INT_MIN + 1)
    _cut_ties(key_ref, nchunk, ck, t, surplus, need, idx_bits)
    return t


def _dsa_prompt_kernel(qa_ref, qit_ref, kiwiq_ref, k_ref, vt_ref, kiwik_ref, bias_ref, o_ref, key_ref, acc_ref,
                       qbd_ref, planes_ref, *, topk, idx_bits, nch, ck):
    qb = pl.program_id(1)
    cb = ck // LANE
    nchunk = (qb + cb) // cb
    wit = jnp.transpose(kiwiq_ref[...])[IDX_DIM:IDX_DIM + IDX_H, :] * IDX_H ** -0.5
    qpos = qb * LANE + lax.broadcasted_iota(i32, (ck, LANE), 1)
    rows = lax.broadcasted_iota(i32, (ck, LANE), 0)

    def score_chunk(c, carry):
        r0 = pl.multiple_of(c * ck, ck)
        kic = kiwik_ref[pl.ds(r0, ck), 0:IDX_DIM].astype(bf16)
        acc = jnp.zeros((ck, LANE), f32)
        for h in range(IDX_H):
            s = _mm(kic, qit_ref[:, h * LANE:(h + 1) * LANE])
            acc = acc + jnp.maximum(s, 0.0) * wit[h:h + 1, :]
        key_ref[pl.ds(r0, ck), :] = jnp.where(rows + r0 <= qpos, _sort_key(acc), INT_MIN)
        return carry

    lax.fori_loop(0, nchunk, score_chunk, 0)
    t = _select_topk_planes(key_ref, planes_ref, nchunk, ck, topk, idx_bits)

    npair = ATT_H // 2
    for pr in range(npair):
        qbd_ref[pr] = jnp.zeros((2 * ATT_DH, 2 * LANE), bf16)
        for hh in range(2):
            sl = slice((2 * pr + hh) * ATT_DH, (2 * pr + hh + 1) * ATT_DH)
            qbd_ref[pr, hh * ATT_DH:(hh + 1) * ATT_DH, hh * LANE:(hh + 1) * LANE] = (
                jnp.transpose(qa_ref[:, sl].astype(f32)).astype(bf16))
    acc_ref[...] = jnp.zeros_like(acc_ref)
    nfar = jnp.maximum(qb - 1, 0) // cb

    pairs = range(npair)

    def att_chunks(near, unroll):
        def body(i, carry):
            ms, ls = list(carry[0]), list(carry[1])
            cs = [i * unroll + u for u in range(unroll)]
            r0s = [pl.multiple_of(c * ck, ck) for c in cs]
            qks = [[_mm(k_ref[pl.ds(r0, ck), pr * 2 * ATT_DH:(pr + 1) * 2 * ATT_DH], qbd_ref[pr]) for pr in pairs]
                   for r0 in r0s]
            for c, r0, qk in zip(cs, r0s, qks):
                neg1 = jnp.where(key_ref[pl.ds(r0, ck), :] >= t, 0.0, NEG_BIG)
                negm = jnp.concatenate([neg1, neg1], axis=1)
                alphas, ps = [], []
                for pr in pairs:
                    s = qk[pr] * (ATT_DH ** -0.5 * LOG2E) + negm
                    if near:
                        s = s + jnp.concatenate(
                            [bias_ref[jnp.clip(qb - (c * cb + j), 0, 2), pr] for j in range(cb)], axis=0)
                    m_new = jnp.maximum(ms[pr], jnp.max(s, axis=0, keepdims=True))
                    alpha = jnp.exp2(ms[pr] - m_new)
                    p = jnp.exp2(s - m_new)
                    ls[pr] = ls[pr] * alpha + jnp.sum(p, axis=0, keepdims=True)
                    ms[pr] = m_new
                    alphas.append(alpha)
                    ps.append(p.astype(bf16))
                pvs = [_mm(vt_ref[pr * nch + c], ps[pr]) for pr in pairs]
                for pr in pairs:
                    for hh in range(2):
                        d = slice(hh * LANE, (hh + 1) * LANE)
                        acc_ref[2 * pr + hh] = acc_ref[2 * pr + hh] * alphas[pr][:, d] + pvs[pr][d, d]
            return tuple(ms), tuple(ls)
        return body

    carry = ((jnp.full((1, 2 * LANE), NEG_BIG, f32),) * npair, (jnp.zeros((1, 2 * LANE), f32),) * npair)
    carry = lax.fori_loop(0, nfar // 2, att_chunks(False, 2), carry)
    carry = lax.fori_loop(nfar // 2 * 2, nfar, att_chunks(False, 1), carry)
    _, ls = lax.fori_loop(nfar, nchunk, att_chunks(True, 1), carry)
    for h in range(ATT_H):
        l = ls[h // 2][:, (h % 2) * LANE:(h % 2 + 1) * LANE]
        o_ref[:, h * ATT_DH:(h + 1) * ATT_DH] = jnp.transpose(acc_ref[h] / l).astype(bf16)


def _bias_tiles(rel_bias, nd):
    j = jnp.arange(LANE)[:, None]
    i = jnp.arange(LANE)[None, :]
    rel = jnp.arange(nd)[:, None, None] * LANE + (i - j)[None]
    return jnp.moveaxis(_bias_lookup(rel_bias, rel), -1, 1)


def _dsa_prompt(qa, kab, vt, qit, kiwi, rel_bias, nb, t):
    nqb = t // LANE
    topk = min(TOPK_MAX, t // 4)
    assert REL_MAX_DIST <= LANE + 1
    bias = _bias_tiles(rel_bias, 3)
    bias = (bias - bias[2:3]) * LOG2E
    npair = ATT_H // 2
    bias = bias.reshape(3, npair, 2, LANE, LANE).transpose(0, 1, 3, 2, 4).reshape(3, npair, LANE, 2 * LANE)
    ck = math.gcd(t, DSA_CHUNK)
    nch = t // ck
    vt = vt.reshape(nb * npair * nch, 2 * ATT_DH, ck)
    qrow = lambda c: pl.BlockSpec((LANE, c), lambda b, q: (b * nqb + q, 0))
    seq = lambda c: pl.BlockSpec((t, c), lambda b, q: (b, 0))
    kern = functools.partial(_dsa_prompt_kernel, topk=topk, idx_bits=max(1, (t - 1).bit_length()), nch=nch, ck=ck)
    return pl.pallas_call(
        kern,
        grid=(nb, nqb),
        in_specs=[qrow(512), pl.BlockSpec((None, IDX_DIM, IDX_H * LANE), lambda b, q: (b * nqb + q, 0, 0)),
                  qrow(LANE), seq(512),
                  pl.BlockSpec((npair * nch, 2 * ATT_DH, ck), lambda b, q: (b, 0, 0)), seq(LANE),
                  _const_spec((3, npair, LANE, 2 * LANE))],
        out_specs=qrow(512),
        out_shape=jax.ShapeDtypeStruct((nb * t, 512), bf16),
        scratch_shapes=[pltpu.VMEM((t, LANE), i32), pltpu.VMEM((ATT_H, ATT_DH, LANE), f32),
                        pltpu.VMEM((npair, 2 * ATT_DH, 2 * LANE), bf16),
                        pltpu.VMEM((t // PLANE_ROWS, 32, 8, LANE), i32)],
        compiler_params=_cparams(("parallel", "arbitrary")),
        name="dsa_prompt",
    )(qa, qit, kiwi, kab, vt, kiwi, bias)


def _softplus(x):
    return jnp.maximum(x, 0.0) + jnp.log1p(jnp.exp(-jnp.abs(x)))


def _cumsum_rows(tri, a):
    hi = a.astype(bf16)
    r1 = a - hi.astype(f32)
    mid = r1.astype(bf16)
    lo = (r1 - mid.astype(f32)).astype(bf16)
    return _mm(tri, hi) + _mm(tri, mid) + _mm(tri, lo)


def _odd_mixer_kernel(*refs, q, pos0):
    seq_in, shared, seq_out = refs[:7], refs[7:16], refs[-6:]
    for b in range(seq_in[0].shape[0]):
        _odd_mixer_seq(*(r.at[b] for r in seq_in), *shared, *(r.at[b] for r in seq_out), q=q, pos0=pos0)


def _odd_mixer_seq(u_ref, z_ref, xbc_ref, dt_ref, pp_ref, cp_ref, h0_ref, pw_ref, ps_ref, cw_ref, cb_ref,
                   dtb_ref, alog_ref, dsk_ref, nrm_ref, tri_ref, po_ref, y_ref, h_ref, ubuf, xbuf, ybuf, *, q, pos0):
    c = pl.program_id(1)

    @pl.when(c == 0)
    def _():
        ubuf[0:1, :] = jnp.zeros((1, POOL_DIM), f32)
        ubuf[1:16, :] = pp_ref[...]
        xbuf[0:8 - (CONV_W - 1), :] = jnp.zeros((8 - (CONV_W - 1), CONV_DIM), f32)
        xbuf[8 - (CONV_W - 1):8, :] = cp_ref[...]
        h_ref[...] = h0_ref[...]

    pos = pos0 + c * q + lax.broadcasted_iota(i32, (q, LANE), 0)
    causal = lax.broadcasted_iota(i32, (q, q), 0) >= lax.broadcasted_iota(i32, (q, q), 1)

    u = u_ref[...]
    ubuf[16:16 + q, :] = u
    for g, w in enumerate(POOL_WINDOWS):
        sl = slice(g * POOL_GC, (g + 1) * POOL_GC)
        acc = u[:, sl]
        for k in range(1, w):
            acc = acc + ubuf[16 - k:16 - k + q, sl]
        d = acc / jnp.minimum(pos + 1, w).astype(f32) - u[:, sl]
        po_ref[:, sl] = (_mm(d.astype(bf16), pw_ref[g]) * ps_ref[:, sl]).astype(bf16)
    ubuf[0:16, :] = ubuf[q:q + 16, :]

    xbuf[8:8 + q, :] = xbc_ref[...]
    conv = cb_ref[...]
    for j in range(CONV_W):
        off = 8 - (CONV_W - 1) + j
        conv = conv + xbuf[off:off + q, :] * cw_ref[j:j + 1, :]
    xbuf[0:8, :] = xbuf[q:q + 8, :]
    act = _silu(conv)
    xs = act[:, 0:D_INNER]

    dt = _softplus(dt_ref[...] + dtb_ref[...])
    a = dt * (-jnp.exp(alog_ref[...]))
    cs = _cumsum_rows(tri_ref[...], a)
    cs_t = jnp.transpose(cs)
    dt_t = jnp.transpose(dt)
    cs_last = cs[q - 1:q, :]
    w_end = jnp.exp(cs_last - cs) * dt
    ecs = jnp.exp(cs)
    hpg = SSM_H // SSM_G
    for g in range(SSM_G):
        bm = act[:, D_INNER + g * SSM_N:D_INNER + (g + 1) * SSM_N].astype(bf16)
        cm = act[:, D_INNER + (SSM_G + g) * SSM_N:D_INNER + (SSM_G + g + 1) * SSM_N].astype(bf16)
        cb = _mm_nt(cm, bm)
        for hh in range(hpg):
            h = g * hpg + hh
            psl = slice(h * SSM_P, (h + 1) * SSM_P)
            seg = cs[:, h:h + 1] - cs_t[h:h + 1, :]
            lm = jnp.exp(jnp.where(causal, seg, NEG_BIG))
            sc = cb * lm * dt_t[h:h + 1, :]
            xh = xs[:, psl]
            hs = h_ref[h]
            yh = _mm(sc.astype(bf16), xh.astype(bf16)) + _mm_nt(cm, hs.astype(bf16)) * ecs[:, h:h + 1]
            h_ref[h] = hs * jnp.exp(cs_last[:, h:h + 1]) + _mm_tn((xh * w_end[:, h:h + 1]).astype(bf16), bm)
            ybuf[:, psl] = yh
    y = (ybuf[...] + dsk_ref[...] * xs) * _silu(z_ref[...])
    gw = D_INNER // SSM_G
    for g in range(SSM_G):
        sl = slice(g * gw, (g + 1) * gw)
        yg = y[:, sl]
        y_ref[:, sl] = (yg * lax.rsqrt(jnp.mean(yg * yg, -1, keepdims=True) + EPS) * nrm_ref[:, sl]).astype(bf16)


def _pad_lanes(v):
    return jnp.pad(v.astype(f32), (0, LANE - v.shape[0]))[None, :]


def _odd_mixer(u, z, xbc, dt, pool_prev, conv_prev, h0, layer_in, prm, pos0, prev, layer_out, n_layers):
    pool_w, pool_scale, conv_w, conv_b, dt_bias, a_log, d_skip, ssm_norm = prm
    nb, t, _ = u.shape
    q = math.gcd(t, SSD_CHUNK)
    nc = t // q
    grp = _seq_group(nb)
    row = lambda c: pl.BlockSpec((grp, q, c), lambda b, i: (b, i, 0))
    lead = (nb, t)
    tri = jnp.tril(jnp.ones((q, q), bf16))
    tail = (SSM_H, SSM_P, SSM_N)
    in_specs = [row(POOL_DIM), row(D_INNER), row(CONV_DIM), row(LANE),
                _layer_block((POOL_PAST, POOL_DIM), layer_in, grp),
                _layer_block((CONV_W - 1, CONV_DIM), layer_in, grp), _layer_block(tail, layer_in, grp),
                _const_spec((POOL_GROUPS, POOL_GC, POOL_GC)), _const_spec((1, POOL_DIM)),
                _const_spec((CONV_W, CONV_DIM)), _const_spec((1, CONV_DIM)),
                _const_spec((1, LANE)), _const_spec((1, LANE)),
                _const_spec((1, D_INNER)), _const_spec((1, D_INNER)), _const_spec((q, q))]
    state_shape, extra, extra_specs, aliases = _stacked_out(prev, n_layers, nb, tail, len(in_specs), 2)
    return pl.pallas_call(
        functools.partial(_odd_mixer_kernel, q=q, pos0=pos0),
        grid=(nb // grp, nc),
        in_specs=in_specs + extra_specs,
        out_specs=[row(POOL_DIM), row(D_INNER), _layer_block(tail, layer_out, grp)],
        out_shape=[jax.ShapeDtypeStruct(lead + (POOL_DIM,), bf16), jax.ShapeDtypeStruct(lead + (D_INNER,), bf16),
                   state_shape],
        input_output_aliases=aliases,
        scratch_shapes=[pltpu.VMEM((grp, q + 16, POOL_DIM), f32), pltpu.VMEM((grp, q + 8, CONV_DIM), f32),
                        pltpu.VMEM((grp, q, D_INNER), f32)],
        compiler_params=_cparams(("parallel", "arbitrary")),
        name="odd_mixer_t%d" % t,
    )(u, z, xbc, dt, pool_prev, conv_prev, h0, pool_w.astype(bf16), pool_scale[None, :], conv_w, conv_b[None, :],
      _pad_lanes(dt_bias), _pad_lanes(a_log), jnp.repeat(d_skip, SSM_P)[None, :], ssm_norm[None, :], tri, *extra)


def _dsa_sample_score_kernel(pt_ref, qi_ref, wi_ref, kiwi_ref, *rest, npg):
    pages, o_ref = rest[:npg], rest[npg]
    ts = kiwi_ref.shape[0]
    ki_new = jnp.transpose(jnp.concatenate([kiwi_ref[...], jnp.zeros((PAGE - ts, LANE), f32)], axis=0))[0:IDX_DIM, :]
    ki = jnp.concatenate([p[...] for p in pages] + [ki_new], axis=1).astype(bf16)
    nk = ki.shape[1]
    s = jnp.maximum(_mm(qi_ref[...], ki), 0.0) * (wi_ref[...] * IDX_H ** -0.5)
    acc = jnp.sum(s.reshape(ts, IDX_H, nk), axis=1)
    col = lax.broadcasted_iota(i32, (ts, nk), 1)
    row = lax.broadcasted_iota(i32, (ts, nk), 0)
    o_ref[...] = jnp.where(col <= npg * PAGE + row, _sort_key(acc), INT_MIN)


def _select_kernel(k_ref, o_ref, key_ref, planes_ref, *, topk, idx_bits):
    nk, rows = k_ref.shape[0], key_ref.shape[0]
    key_ref[0:nk, :] = k_ref[...]
    if rows > nk:
        key_ref[nk:rows, :] = jnp.full((rows - nk, LANE), INT_MIN, i32)
    t = _select_topk_planes(key_ref, planes_ref, 1, rows, topk, idx_bits)
    o_ref[...] = jnp.where(key_ref[0:nk, :] >= t, 1.0, 0.0)


def _dsa_sample_attn_kernel(pt_ref, q_ref, kn_ref, vn_ref, sel_ref, bias_ref, spread_ref, *rest, npg):
    kpages, vpages, o_ref = rest[:npg], rest[npg:2 * npg], rest[2 * npg]
    pad = jnp.zeros((PAGE * ATT_H - kn_ref.shape[0], ATT_DH), f32)
    kx = jnp.concatenate([p[...] for p in kpages] + [kn_ref[...], pad], axis=0).astype(bf16)
    vx = jnp.concatenate([p[...] for p in vpages] + [vn_ref[...], pad], axis=0).astype(bf16)
    selx = jnp.concatenate([_mm(sel_ref[:, j * PAGE:(j + 1) * PAGE], spread_ref[...]) for j in range(npg + 1)],
                           axis=1)
    s = _mm_nt(q_ref[...], kx) * ATT_DH ** -0.5 + jnp.where(selx > 0.5, bias_ref[...], NEG_BIG)
    p = jnp.exp(s - jnp.max(s, axis=-1, keepdims=True))
    o_ref[...] = (_mm(p.astype(bf16), vx) / jnp.sum(p, axis=-1, keepdims=True)).astype(bf16)


def _dsa_sample(qa, ka, va, qi, kiwi, cache_k, cache_v, cache_ki, layer, page_table, rel_bias):
    db, ts = qa.shape[:2]
    npg = page_table.shape[1]
    n_past = npg * PAGE
    nk = n_past + PAGE
    nq = db * ts
    topk = min(TOPK_MAX, (n_past + ts) // 4)
    hd = ATT_H * ATT_DH
    ki_page = lambda j: pl.BlockSpec((None, None, IDX_DIM, PAGE), lambda b, pt: (layer, pt[b, j], 0, 0))
    seq = lambda r, c: pl.BlockSpec((None, r, c), lambda b, pt: (b, 0, 0))

    keys = pl.pallas_call(
        functools.partial(_dsa_sample_score_kernel, npg=npg),
        grid_spec=pltpu.PrefetchScalarGridSpec(
            num_scalar_prefetch=1, grid=(db,),
            in_specs=[seq(ts * IDX_H, IDX_DIM), seq(ts * IDX_H, 1), seq(ts, LANE)] + [ki_page(j) for j in range(npg)],
            out_specs=seq(ts, nk)),
        out_shape=jax.ShapeDtypeStruct((db, ts, nk), i32),
        compiler_params=_cparams(("parallel",)),
        name="dsa_sample_score",
    )(page_table, qi.reshape(db, ts * IDX_H, IDX_DIM),
      kiwi[..., IDX_DIM:IDX_DIM + IDX_H].reshape(db, ts * IDX_H, 1), kiwi,
      *([jnp.swapaxes(cache_ki, 2, 3)] * npg))

    col = pl.BlockSpec((nk, LANE), lambda i: (0, i))
    sel_rows = -(-nk // PLANE_ROWS) * PLANE_ROWS
    sel = pl.pallas_call(
        functools.partial(_select_kernel, topk=topk, idx_bits=max(1, (sel_rows - 1).bit_length())),
        grid=(nq // LANE,),
        in_specs=[col],
        out_specs=col,
        out_shape=jax.ShapeDtypeStruct((nk, nq), f32),
        scratch_shapes=[pltpu.VMEM((sel_rows, LANE), i32),
                        pltpu.VMEM((sel_rows // PLANE_ROWS, 32, 8, LANE), i32)],
        compiler_params=_cparams(("parallel",)),
        name="dsa_sample_select",
    )(keys.reshape(nq, nk).T)
    sel = jnp.repeat(sel.T.reshape(db, ts, nk), ATT_H, axis=1).astype(bf16)
    rel = n_past + jnp.arange(ts)[:, None] - jnp.arange(nk)[None, :]
    bias = jnp.repeat(jnp.moveaxis(_bias_lookup(rel_bias, rel), -1, 1), ATT_H, axis=-1)
    same_head = jnp.arange(nk * ATT_H)[None, :] % ATT_H == jnp.arange(ATT_H)[:, None]
    bias = jnp.where(same_head[None], bias, NEG_BIG).reshape(ts * ATT_H, nk * ATT_H)
    spread = (jnp.arange(PAGE * ATT_H)[None, :] // ATT_H == jnp.arange(PAGE)[:, None]).astype(bf16)

    n_pages = cache_k.shape[1]
    rows = PAGE * ATT_H
    kv_page = lambda j: pl.BlockSpec((rows, ATT_DH), lambda b, pt: (layer * n_pages + pt[b, j], 0))
    const = lambda a: pl.BlockSpec(a.shape, lambda b, pt: (0, 0))
    as_rows = lambda a: a.reshape(db, ts * ATT_H, ATT_DH)
    out = pl.pallas_call(
        functools.partial(_dsa_sample_attn_kernel, npg=npg),
        grid_spec=pltpu.PrefetchScalarGridSpec(
            num_scalar_prefetch=1, grid=(db,),
            in_specs=[seq(ts * ATT_H, ATT_DH)] * 3 + [seq(ts * ATT_H, nk), const(bias), const(spread)]
                     + [kv_page(j) for j in range(npg)] * 2,
            out_specs=seq(ts * ATT_H, ATT_DH)),
        out_shape=jax.ShapeDtypeStruct((db, ts * ATT_H, ATT_DH), bf16),
        compiler_params=_cparams(("parallel",)),
        name="dsa_sample_attn",
    )(page_table, as_rows(qa), as_rows(ka), as_rows(va), sel, bias, spread,
      *([cache_k.reshape(-1, ATT_DH)] * npg), *([cache_v.reshape(-1, ATT_DH)] * npg))
    return out.reshape(db, ts, hd)


def kernel(x_prompt, x_sample, cache_k, cache_v, cache_kidx, state_ret, state_pool, state_conv, state_ssm,
           page_table, norm_mix, norm_ffn, w_in_even, w_out_even, q_norm, k_norm, rel_bias,
           w_in_odd, w_out_odd, pool_w, pool_scale, conv_w, conv_b, dt_bias, a_log, d_skip, ssm_norm,
           w_gate, w_up, w_down):
    bp, sp, d = x_prompt.shape
    db, ts, _ = x_sample.shape
    n_p, n_s = bp * sp, db * ts
    assert d == D_MODEL and n_p % TM == 0 and n_s % TM == 0 and sp % LANE == 0 and sp >= POOL_PAST
    n_past = page_table.shape[1] * PAGE
    n_even, n_odd = (DEPTH + 1) // 2, DEPTH // 2
    hd = ATT_H * ATT_DH
    xp, xs = x_prompt.reshape(n_p, d), x_sample.reshape(n_s, d)
    seqs = lambda a: a.reshape(db, ts, a.shape[-1])
    pseqs = lambda a: a.reshape(bp, sp, a.shape[-1])
    last = lambda a, n: a.reshape(bp, sp, a.shape[-1])[:, sp - n:]
    tail = lambda prev, cur, n: jnp.concatenate([prev.astype(f32), cur], axis=1)[:, -n:]
    zeros = lambda *s: jnp.zeros((1, bp) + s, f32)
    outs = [[] for _ in range(14)]
    rstate_p = rstate_s = h_p = h_s = kv_p = kv_s = None
    w_in_e = jnp.pad(w_in_even, ((0, 0), (0, 0), (0, EVEN_PROJ_PAD - EVEN_PROJ))).astype(bf16)
    w_in_o = jnp.pad(w_in_odd, ((0, 0), (0, 0), (0, ODD_PROJ_PAD - ODD_PROJ))).astype(bf16)
    w_out_e, w_out_o = w_out_even.astype(bf16), w_out_odd.astype(bf16)
    w_ffn = (w_gate.astype(bf16), w_up.astype(bf16), w_down.astype(bf16))
    for l in range(DEPTH):
        if l % 2 == 0:
            i = l // 2
            w_out, mix_layer = w_out_e, i
            prm = (norm_mix[l][None], w_in_e, q_norm[i][None], k_norm[i][None])
            ret_p, qa_p, ka_p, kab_p, va_p, kiwi_p, qit_p, vt_p = _even_proj(xp, *prm, i, n_even, kv_p, prompt_len=sp)
            ret_s, qa_s, ka_s, _, va_s, kiwi_s, qi_s = _even_proj(xs, *prm, i, n_even, kv_s)
            kv_p, kv_s = (ka_p, va_p), (ka_s, va_s)
            mix_a_p, rstate_p = _retention(pseqs(ret_p), zeros(RET_H, RET_DK, RET_DV), 0, 0, rstate_p, i, n_even)
            mix_a_s, rstate_s = _retention(seqs(ret_s), state_ret, i, n_past, rstate_s, i, n_even)
            mix_b_p = _dsa_prompt(qa_p, kab_p, vt_p, qit_p, kiwi_p, rel_bias, bp, sp)
            mix_b_s = _dsa_sample(seqs(qa_s), ka_s[i], va_s[i], seqs(qi_s), seqs(kiwi_s),
                                  cache_k, cache_v, cache_kidx, i, page_table, rel_bias)
            new = [None, None, kiwi_p.reshape(bp, sp, LANE)[..., :IDX_DIM], None, None, None, None,
                   None, None, seqs(kiwi_s)[..., :IDX_DIM], None, None, None, None]
        else:
            j = l // 2
            w_out, mix_layer = w_out_o, j
            u_p, z_p, xbc_p, dt_p = _odd_proj(xp, norm_mix[l][None], w_in_o, j)
            u_s, z_s, xbc_s, dt_s = _odd_proj(xs, norm_mix[l][None], w_in_o, j)
            prm = (pool_w[j], pool_scale[j], conv_w[j], conv_b[j], dt_bias[j], a_log[j], d_skip[j], ssm_norm[j])
            mix_a_p, mix_b_p, h_p = _odd_mixer(
                pseqs(u_p), pseqs(z_p), pseqs(xbc_p), pseqs(dt_p), zeros(POOL_PAST, POOL_DIM),
                zeros(CONV_W - 1, CONV_DIM), zeros(SSM_H, SSM_P, SSM_N), 0, prm, 0, h_p, j, n_odd)
            mix_a_s, mix_b_s, h_s = _odd_mixer(
                seqs(u_s), seqs(z_s), seqs(xbc_s), seqs(dt_s), state_pool, state_conv, state_ssm, j,
                prm, n_past, h_s, j, n_odd)
            new = [None, None, None, None, last(u_p, POOL_PAST), last(xbc_p, CONV_W - 1), None,
                   None, None, None, None,
                   tail(state_pool[j], seqs(u_s), POOL_PAST), tail(state_conv[j], seqs(xbc_s), CONV_W - 1), None]
        for acc, leaf in zip(outs, new):
            if leaf is not None:
                acc.append(leaf)
        ffn = (w_out, mix_layer, norm_ffn[l][None], *w_ffn, l)
        xp = _out_ffn(xp, mix_a_p.reshape(n_p, -1), mix_b_p.reshape(n_p, -1), *ffn)
        xs = _out_ffn(xs, mix_a_s.reshape(n_s, -1), mix_b_s.reshape(n_s, -1), *ffn)
    leaves = [jnp.stack(a) if a else None for a in outs]
    leaves[3], leaves[6], leaves[10], leaves[13] = rstate_p, h_p, rstate_s, h_s
    leaves[0], leaves[1] = (a.reshape(n_even, bp, sp, ATT_H, ATT_DH) for a in kv_p)
    leaves[7], leaves[8] = (a.reshape(n_even, db, ts, ATT_H, ATT_DH) for a in kv_s)
    return (xp.reshape(bp, sp, d), xs.reshape(db, ts, d)) + tuple(leaves)
```

```python
import functools
import math

import jax
import jax.numpy as jnp
import numpy as np
from jax import lax
from jax.experimental import pallas as pl
from jax.experimental.pallas import tpu as pltpu

f32 = jnp.float32
bf16 = jnp.bfloat16
i32 = jnp.int32

D_MODEL = 1024
DEPTH = 4
PAGE = 128
RET_H, RET_DK, RET_DV, RET_CHUNK = 4, 128, 128, 128
ROPE_BASE = 10000.0
ATT_H, ATT_DH, ATT_BLOCK = 4, 128, 128
IDX_H, IDX_DIM = 8, 64
TOPK_MAX = 256
REL_BUCKETS, REL_MAX_DIST = 32, 128
POOL_WINDOWS = (2, 4, 8, 16)
POOL_GROUPS = 4
POOL_DIM = D_MODEL // 2
POOL_GC = POOL_DIM // POOL_GROUPS
POOL_PAST = 15
D_INNER = D_MODEL // 2
SSM_P = 64
SSM_H = D_INNER // SSM_P
SSM_G = 2
SSM_N = 128
CONV_W = 4
CONV_DIM = D_INNER + 2 * SSM_G * SSM_N
SSD_CHUNK = 128
FF_DIM = -(-8 * D_MODEL // (3 * 256)) * 256
EPS = 1e-6

EVEN_PROJ = 4 * 512 + 3 * 512 + 512 + IDX_DIM + IDX_H
EVEN_PROJ_PAD = 4224
ODD_PROJ = POOL_DIM + D_INNER + CONV_DIM + SSM_H
ODD_PROJ_PAD = 2176

LANE = 128
INT_MIN = -(2 ** 31)
NEG_BIG = -1e30
LOG2E = 1.4426950408889634
VMEM_LIMIT = 56 * 1024 * 1024
TM = 256
FF_SPLIT = 2
DSA_CHUNK = 512
SCORE_ROWS = 256
SEQ_GROUP = 8


def _cparams(sem):
    return pltpu.CompilerParams(dimension_semantics=sem, vmem_limit_bytes=VMEM_LIMIT)


def _mm(a, b):
    return jnp.dot(a, b, preferred_element_type=f32)


def _mm_nt(a, b):
    return lax.dot_general(a, b, (((1,), (1,)), ((), ())), preferred_element_type=f32)


def _mm_tn(a, b):
    return lax.dot_general(a, b, (((0,), (0,)), ((), ())), preferred_element_type=f32)


def _rms(x, g):
    return x * lax.rsqrt(jnp.mean(x * x, -1, keepdims=True) + EPS) * g


def _silu(x):
    return x / (1.0 + jnp.exp(-x))


def _const_spec(shape):
    nd = len(shape)
    return pl.BlockSpec(shape, lambda *a: (0,) * nd)


def _even_proj_kernel(x_ref, g_ref, w_ref, qg_ref, kg_ref, *refs, for_prompt, n_aliased):
    ret_ref, qa_ref, ka_ref, kab_ref, va_ref, kiwi_ref, *idx_refs = refs[n_aliased:]
    xb = _rms(x_ref[...], g_ref[...]).astype(bf16)
    qa = _mm(xb, w_ref[:, 2048:2560])
    ka = _mm(xb, w_ref[:, 2560:3072])
    va = _mm(xb, w_ref[:, 3072:3584])
    qi = _mm(xb, w_ref[:, 3584:4096]) * IDX_DIM ** -0.5
    kiwi_ref[...] = _mm(xb, w_ref[:, 4096:4224])
    for h in range(ATT_H):
        sl = slice(h * ATT_DH, (h + 1) * ATT_DH)
        qa_ref[:, sl] = _rms(qa[:, sl], qg_ref[...]).astype(bf16)
        kn = _rms(ka[:, sl], kg_ref[...])
        ka_ref[:, h, :] = kn
        kab_ref[:, sl] = kn.astype(bf16)
        va_ref[:, h, :] = va[:, sl]
    if for_prompt:
        qit_ref, vt_ref = idx_refs
        for blk in range(TM // LANE):
            for h in range(IDX_H):
                part = qi[blk * LANE:(blk + 1) * LANE, h * IDX_DIM:(h + 1) * IDX_DIM]
                qit_ref[blk, :, h * LANE:(h + 1) * LANE] = jnp.transpose(part).astype(bf16)
        for pr in range(ATT_H // 2):
            vt_ref[pr] = jnp.transpose(va[:, pr * 2 * ATT_DH:(pr + 1) * 2 * ATT_DH]).astype(bf16)
    else:
        idx_refs[0][...] = qi.astype(bf16)
    ret_ref[...] = _mm(xb, w_ref[:, 0:2048])


def _even_proj(x, g, w, qg, kg, layer, n_layers, prev_kv, prompt_len=None):
    n = x.shape[0]
    row = lambda c: pl.BlockSpec((TM, c), lambda i: (i, 0))
    heads = pl.BlockSpec((None, TM, ATT_H, ATT_DH), lambda i: (layer, i, 0, 0))
    stacked = jax.ShapeDtypeStruct((n_layers, n, ATT_H, ATT_DH), f32)
    out_specs = [row(2048), row(512), heads, row(512), heads, row(LANE)]
    out_shape = [jax.ShapeDtypeStruct((n, 2048), f32), jax.ShapeDtypeStruct((n, 512), bf16), stacked,
                 jax.ShapeDtypeStruct((n, 512), bf16), stacked, jax.ShapeDtypeStruct((n, LANE), f32)]
    extra = [] if prev_kv is None else list(prev_kv)
    aliases = {} if prev_kv is None else {5: 2, 6: 4}
    if prompt_len is None:
        out_specs.append(row(512))
        out_shape.append(jax.ShapeDtypeStruct((n, 512), bf16))
    else:
        ck = math.gcd(prompt_len, DSA_CHUNK)
        assert ck % TM == 0 and prompt_len % ck == 0
        per_chunk, per_seq, npair = ck // TM, prompt_len // TM, ATT_H // 2
        out_specs += [pl.BlockSpec((TM // LANE, IDX_DIM, IDX_H * LANE), lambda i: (i, 0, 0)),
                      pl.BlockSpec((None, npair, None, 2 * ATT_DH, TM),
                                   lambda i: (i // per_seq, 0, (i % per_seq) // per_chunk, 0, i % per_chunk))]
        out_shape += [jax.ShapeDtypeStruct((n // LANE, IDX_DIM, IDX_H * LANE), bf16),
                      jax.ShapeDtypeStruct((n // prompt_len, npair, prompt_len // ck, 2 * ATT_DH, ck), bf16)]
    return pl.pallas_call(
        functools.partial(_even_proj_kernel, for_prompt=prompt_len is not None, n_aliased=len(extra)),
        grid=(n // TM,),
        in_specs=[row(D_MODEL), _const_spec((1, D_MODEL)), _layer_weight(w, layer),
                  _const_spec((1, ATT_DH)), _const_spec((1, ATT_DH))]
                 + [pl.BlockSpec(memory_space=pl.ANY)] * len(extra),
        out_specs=out_specs,
        out_shape=out_shape,
        input_output_aliases=aliases,
        compiler_params=_cparams(("parallel",)),
        name="even_proj",
    )(x, g, w, qg, kg, *extra)


def _odd_proj_kernel(x_ref, g_ref, w_ref, u_ref, z_ref, xbc_ref, dt_ref):
    xb = _rms(x_ref[...], g_ref[...]).astype(bf16)
    u_ref[...] = _mm(xb, w_ref[:, 0:512])
    z_ref[...] = _mm(xb, w_ref[:, 512:1024])
    xbc_ref[...] = _mm(xb, w_ref[:, 1024:2048])
    dt_ref[...] = _mm(xb, w_ref[:, 2048:2176])


def _odd_proj(x, g, w, layer):
    n = x.shape[0]
    row = lambda c: pl.BlockSpec((TM, c), lambda i: (i, 0))
    outs = [512, 512, 1024, LANE]
    return pl.pallas_call(
        _odd_proj_kernel,
        grid=(n // TM,),
        in_specs=[row(D_MODEL), _const_spec((1, D_MODEL)), _layer_weight(w, layer)],
        out_specs=[row(c) for c in outs],
        out_shape=[jax.ShapeDtypeStruct((n, c), f32) for c in outs],
        compiler_params=_cparams(("parallel",)),
        name="odd_proj",
    )(x, g, w)


def _out_ffn_kernel(x_ref, a_ref, b_ref, wo_ref, g_ref, wg_ref, wu_ref, wd_ref, o_ref):
    half = wo_ref.shape[0] // 2
    x = x_ref[...] + _mm(a_ref[...], wo_ref[0:half, :]) + _mm(b_ref[...], wo_ref[half:, :])
    hb = _rms(x, g_ref[...]).astype(bf16)
    fc = FF_DIM // FF_SPLIT
    ff = None
    for c in range(FF_SPLIT):
        sl = slice(c * fc, (c + 1) * fc)
        act = (_silu(_mm(hb, wg_ref[:, sl])) * _mm(hb, wu_ref[:, sl])).astype(bf16)
        down = _mm(act, wd_ref[sl, :])
        ff = down if ff is None else ff + down
    o_ref[...] = x + ff


def _layer_weight(w, layer, single_buffer=False):
    mode = dict(pipeline_mode=pl.Buffered(1)) if single_buffer else {}
    return pl.BlockSpec((None,) + w.shape[1:], lambda i: (layer,) + (0,) * (w.ndim - 1), **mode)


def _out_ffn(x, a, b, wo, mix_layer, g, wg, wu, wd, layer):
    n = x.shape[0]
    row = lambda c: pl.BlockSpec((TM, c), lambda i: (i, 0))
    return pl.pallas_call(
        _out_ffn_kernel,
        grid=(n // TM,),
        in_specs=[row(D_MODEL), row(a.shape[1]), row(b.shape[1]), _layer_weight(wo, mix_layer, True),
                  _const_spec((1, D_MODEL)), _layer_weight(wg, layer, True), _layer_weight(wu, layer, True),
                  _layer_weight(wd, layer, True)],
        out_specs=row(D_MODEL),
        out_shape=jax.ShapeDtypeStruct((n, D_MODEL), f32),
        compiler_params=_cparams(("parallel",)),
        name="out_ffn",
    )(x, a, b, wo, g, wg, wu, wd)


def _rope_tables(pos):
    half = RET_DK // 2
    inv = ROPE_BASE ** (-jnp.linspace(0.0, 1.0, half, dtype=f32))
    ang = pos.astype(f32)[:, None] * inv[None, :]
    cos, sin = jnp.cos(ang), jnp.sin(ang)
    return jnp.concatenate([cos, cos], -1), jnp.concatenate([-sin, sin], -1)


def _ret_decay(q):
    log_g = jnp.log1p(-jnp.exp2(-5.0 - jnp.arange(RET_H, dtype=f32)))
    idx = jnp.arange(q, dtype=f32)
    diff = idx[:, None] - idx[None, :]
    dmask = jnp.where(diff[None] >= 0, jnp.exp(log_g[:, None, None] * jnp.maximum(diff, 0.0)[None]), 0.0)
    xi = jnp.exp(log_g[:, None] * (idx + 1.0)[None])
    zeta = jnp.exp(log_g[:, None] * (q - 1.0 - idx)[None])
    g_chunk = jnp.exp(log_g * q)
    return dmask, xi, zeta, g_chunk


def _t5_bucket(rel):
    n = jnp.maximum(rel, 0)
    max_exact = REL_BUCKETS // 2
    nf = jnp.maximum(n, 1).astype(f32)
    large = max_exact + (jnp.log(nf / max_exact) / math.log(REL_MAX_DIST / max_exact)
                         * (REL_BUCKETS - max_exact)).astype(i32)
    large = jnp.minimum(large, REL_BUCKETS - 1)
    return jnp.where(n < max_exact, n, large)


def _bias_lookup(rel_bias, rel):
    onehot = jax.nn.one_hot(_t5_bucket(rel), REL_BUCKETS, dtype=f32)
    return jnp.einsum("...b,bh->...h", onehot, rel_bias.astype(f32), precision=lax.Precision.HIGHEST)


def _rotary(x, c, s):
    return x * c + pltpu.roll(x, RET_DK // 2, 1) * s


def _retention_kernel(q_ref, k_ref, v_ref, g_ref, r0_ref, c_ref, s_ref, dm_ref, xi_ref, zt_ref, gc_ref,
                      *rest):
    o_ref, r_ref = rest[-2:]

    @pl.when(pl.program_id(1) == 0)
    def _():
        r_ref[...] = r0_ref[...]

    cos, sin = c_ref[...], s_ref[...]
    for b in range(q_ref.shape[0]):
        for h in range(RET_H):
            sl = slice(h * RET_DK, (h + 1) * RET_DK)
            qr = _rotary(q_ref[b, :, sl], cos, sin).astype(bf16)
            kr = _rotary(k_ref[b, :, sl], cos, sin) * RET_DK ** -0.5
            vb = v_ref[b, :, sl].astype(bf16)
            r = r_ref[b, h]
            s = _mm_nt(qr, kr.astype(bf16)) * dm_ref[h]
            o = _mm(s.astype(bf16), vb) + _mm(qr, r.astype(bf16)) * xi_ref[h]
            r_ref[b, h] = r * gc_ref[h, 0:1, :] + _mm_tn((kr * zt_ref[h]).astype(bf16), vb)
            o = o * lax.rsqrt(jnp.mean(o * o, -1, keepdims=True) + EPS)
            o_ref[b, :, sl] = (_silu(g_ref[b, :, sl]) * o).astype(bf16)


def _seq_group(nb):
    return math.gcd(nb, SEQ_GROUP)


def _layer_block(tail, layer, group):
    return pl.BlockSpec((None, group) + tail, lambda b, c: (layer, b) + (0,) * len(tail))


def _stacked_out(prev, n_layers, nb, tail, n_inputs, out_index):
    shape = jax.ShapeDtypeStruct((n_layers, nb) + tail, f32)
    if prev is None:
        return shape, [], [], {}
    return shape, [prev], [pl.BlockSpec(memory_space=pl.ANY)], {n_inputs: out_index}


def _retention(ret, r0, layer_in, pos0, prev, layer_out, n_layers):
    nb, t, _ = ret.shape
    q = math.gcd(t, RET_CHUNK)
    grp = _seq_group(nb)
    cos, sin = _rope_tables(pos0 + jnp.arange(t))
    dmask, xi, zeta, g_chunk = _ret_decay(q)
    bcast = lambda a: jnp.broadcast_to(a[:, :, None], (RET_H, a.shape[1], LANE))
    col = lambda j: pl.BlockSpec((grp, q, 512), lambda b, c: (b, c, j))
    tab = pl.BlockSpec((q, LANE), lambda b, c: (c, 0))
    tail = (RET_H, RET_DK, RET_DV)
    in_specs = [col(0), col(1), col(2), col(3), _layer_block(tail, layer_in, grp), tab, tab,
                _const_spec((RET_H, q, q)), _const_spec((RET_H, q, LANE)), _const_spec((RET_H, q, LANE)),
                _const_spec((RET_H, 8, LANE))]
    state_shape, extra, extra_specs, aliases = _stacked_out(prev, n_layers, nb, tail, len(in_specs), 1)
    return pl.pallas_call(
        _retention_kernel,
        grid=(nb // grp, t // q),
        in_specs=in_specs + extra_specs,
        out_specs=[col(0), _layer_block(tail, layer_out, grp)],
        out_shape=[jax.ShapeDtypeStruct((nb, t, 512), bf16), state_shape],
        input_output_aliases=aliases,
        compiler_params=_cparams(("parallel", "arbitrary")),
        name="retention_t%d" % t,
    )(ret, ret, ret, ret, r0, cos, sin, dmask, bcast(xi), bcast(zeta),
      jnp.broadcast_to(g_chunk[:, None, None], (RET_H, 8, LANE)), *extra)


def _sort_key(score):
    bits = lax.bitcast_convert_type(score, i32)
    bits = jnp.where(bits == INT_MIN, 0, bits)
    return jnp.where(bits < 0, bits ^ 0x7FFFFFFF, bits)


def _count(key_ref, nchunk, ck, pred):
    def body(c, acc):
        r0 = pl.multiple_of(c * ck, ck)
        hit = jnp.where(pred(key_ref[pl.ds(r0, ck), :], r0), 1, 0).astype(i32)
        return acc + jnp.sum(hit.reshape(ck // 8, 8, LANE), axis=0)
    acc = lax.fori_loop(0, nchunk, body, jnp.zeros((8, LANE), i32))
    return jnp.sum(acc, axis=0, keepdims=True)


def _cut_ties(key_ref, nchunk, ck, t, surplus, keep, idx_bits):
    rows = lax.broadcasted_iota(i32, (ck, LANE), 0)

    @pl.when(jnp.max(jnp.where(surplus, 1, 0)) > 0)
    def _():
        want = jnp.where(surplus, keep, jnp.int32(2 ** 30))

        def idx_step(it, x):
            cand = x + jnp.left_shift(jnp.int32(1), idx_bits - 1 - it)
            cnt = _count(key_ref, nchunk, ck, lambda blk, r0: jnp.where(blk == t, rows + r0, cand) < cand)
            return jnp.where(cnt < want, cand, x)

        last = lax.fori_loop(0, idx_bits, idx_step, jnp.zeros((1, LANE), i32))

        def demote(c, carry):
            r0 = pl.multiple_of(c * ck, ck)
            blk = key_ref[pl.ds(r0, ck), :]
            drop = jnp.where(blk == t, rows + r0, last) > last
            key_ref[pl.ds(r0, ck), :] = jnp.where(drop, INT_MIN, blk)
            return carry

        lax.fori_loop(0, nchunk, demote, 0)


PLANE_ROWS = 256
_SWAP_STEPS = ((16, 0x0000FFFF), (8, 0x00FF00FF), (4, 0x0F0F0F0F), (2, 0x33333333), (1, 0x55555555))


def _bit_planes(words):
    a = list(words)
    for j, m in _SWAP_STEPS:
        for k in range(32):
            if k & j == 0:
                t = ((a[k] >> j) ^ a[k + j]) & m
                a[k] = a[k] ^ (t << j)
                a[k + j] = a[k + j] ^ t
    return a


def _select_topk_planes(key_ref, planes_ref, nchunk, ck, topk, idx_bits):
    u32 = jnp.uint32
    per_chunk = ck // PLANE_ROWS
    ngroups = nchunk * per_chunk

    def to_planes(g, carry):
        r0 = pl.multiple_of(g * PLANE_ROWS, PLANE_ROWS)
        words = [lax.bitcast_convert_type(key_ref[pl.ds(r0 + 8 * i, 8), :], u32) for i in range(32)]
        planes = _bit_planes(words)
        planes[31] = ~planes[31]
        for b in range(32):
            planes_ref[g, b] = lax.bitcast_convert_type(planes[b], i32)
        return carry

    lax.fori_loop(0, ngroups, to_planes, 0)

    gmax = planes_ref.shape[0]
    full = jnp.full((8, LANE), 0xFFFFFFFF, u32)
    none = jnp.zeros((8, LANE), u32)

    def radix_select(ng):
        def run():
            alive0 = tuple(jnp.where(g < ngroups, full, none) for g in range(ng))

            def bit_step(it, carry):
                t, need, alive = carry
                b = 31 - it
                ones = [a & lax.bitcast_convert_type(planes_ref[g, b], u32) for g, a in enumerate(alive)]
                cnt = functools.reduce(lambda x, y: x + y, [lax.population_count(o) for o in ones])
                c = jnp.sum(cnt.astype(i32), axis=0, keepdims=True)
                take = c >= need
                t = t | jnp.where(take, jnp.left_shift(jnp.int32(1), b), 0)
                need = jnp.where(take, need, need - c)
                return t, need, tuple(jnp.where(take, o, a ^ o) for o, a in zip(ones, alive))

            init = (jnp.zeros((1, LANE), i32), jnp.full((1, LANE), topk, i32), alive0)
            t, need, alive = lax.fori_loop(0, 32, bit_step, init)
            equal = functools.reduce(lambda x, y: x + y, [lax.population_count(a) for a in alive])
            return t, need, jnp.sum(equal.astype(i32), axis=0, keepdims=True)
        return run

    sizes = sorted({max(1, gmax // 4), max(1, gmax // 2), gmax})
    if isinstance(ngroups, int):
        t, need, n_equal = radix_select(min(s for s in sizes if s >= ngroups))()
    else:
        which = sum((ngroups > s).astype(i32) for s in sizes[:-1]) if len(sizes) > 1 else jnp.int32(0)
        t, need, n_equal = lax.switch(which, [radix_select(s) for s in sizes])
    t = t ^ INT_MIN
    surplus = (n_equal > need) & (t > INT_MIN)
    t = jnp.maximum(t, INT_MIN + 1)
    _cut_ties(key_ref, nchunk, ck, t, surplus, need, idx_bits)
    return t


def _dsa_prompt_kernel(qa_ref, qit_ref, kiwiq_ref, k_ref, vt_ref, kiwik_ref, bias_ref, o_ref, key_ref, acc_ref,
                       qbd_ref, planes_ref, *, topk, idx_bits, nch, ck):
    qb = pl.program_id(1)
    cb = ck // LANE
    nchunk = (qb + cb) // cb
    wit = jnp.transpose(kiwiq_ref[...])[IDX_DIM:IDX_DIM + IDX_H, :] * IDX_H ** -0.5
    qpos = qb * LANE + lax.broadcasted_iota(i32, (SCORE_ROWS, LANE), 1)
    rows = lax.broadcasted_iota(i32, (SCORE_ROWS, LANE), 0)

    def score_chunk(c, carry):
        for sub in range(ck // SCORE_ROWS):
            r0 = pl.multiple_of(c * ck, ck) + sub * SCORE_ROWS
            kic = kiwik_ref[pl.ds(r0, SCORE_ROWS), 0:IDX_DIM].astype(bf16)
            acc = jnp.zeros((SCORE_ROWS, LANE), f32)
            for h in range(0, IDX_H, 2):
                s = _mm(kic, qit_ref[:, h * LANE:(h + 2) * LANE])
                acc = acc + jnp.maximum(s[:, :LANE], 0.0) * wit[h:h + 1, :]
                acc = acc + jnp.maximum(s[:, LANE:], 0.0) * wit[h + 1:h + 2, :]
            valid = rows + r0 <= qpos
            key_ref[pl.ds(r0, SCORE_ROWS), :] = jnp.where(valid, _sort_key(acc), INT_MIN)
        return carry

    lax.fori_loop(0, nchunk, score_chunk, 0)
    t = _select_topk_planes(key_ref, planes_ref, nchunk, ck, topk, idx_bits)

    npair = ATT_H // 2
    for pr in range(npair):
        qbd_ref[pr] = jnp.zeros((2 * ATT_DH, 2 * LANE), bf16)
        for hh in range(2):
            sl = slice((2 * pr + hh) * ATT_DH, (2 * pr + hh + 1) * ATT_DH)
            qbd_ref[pr, hh * ATT_DH:(hh + 1) * ATT_DH, hh * LANE:(hh + 1) * LANE] = (
                jnp.transpose(qa_ref[:, sl].astype(f32)).astype(bf16))
    acc_ref[...] = jnp.zeros_like(acc_ref)
    nfar = jnp.maximum(qb - 1, 0) // cb

    pairs = range(npair)

    def att_chunks(near, unroll):
        def body(i, carry):
            ms, ls = list(carry[0]), list(carry[1])
            cs = [i * unroll + u for u in range(unroll)]
            r0s = [pl.multiple_of(c * ck, ck) for c in cs]
            qks = [[_mm(k_ref[pl.ds(r0, ck), pr * 2 * ATT_DH:(pr + 1) * 2 * ATT_DH], qbd_ref[pr]) for pr in pairs]
                   for r0 in r0s]
            for c, r0, qk in zip(cs, r0s, qks):
                neg1 = jnp.where(key_ref[pl.ds(r0, ck), :] >= t, 0.0, NEG_BIG)
                negm = jnp.concatenate([neg1, neg1], axis=1)
                alphas, ps = [], []
                for pr in pairs:
                    s = qk[pr] * (ATT_DH ** -0.5 * LOG2E) + negm
                    if near:
                        s = s + jnp.concatenate(
                            [bias_ref[jnp.clip(qb - (c * cb + j), 0, 2), pr] for j in range(cb)], axis=0)
                    m_new = jnp.maximum(ms[pr], jnp.max(s, axis=0, keepdims=True))
                    alpha = jnp.exp2(ms[pr] - m_new)
                    p = jnp.exp2(s - m_new)
                    ls[pr] = ls[pr] * alpha + jnp.sum(p, axis=0, keepdims=True)
                    ms[pr] = m_new
                    alphas.append(alpha)
                    ps.append(p.astype(bf16))
                pvs = [_mm(vt_ref[pr * nch + c], ps[pr]) for pr in pairs]
                for pr in pairs:
                    for hh in range(2):
                        d = slice(hh * LANE, (hh + 1) * LANE)
                        acc_ref[2 * pr + hh] = acc_ref[2 * pr + hh] * alphas[pr][:, d] + pvs[pr][d, d]
            return tuple(ms), tuple(ls)
        return body

    carry = ((jnp.full((1, 2 * LANE), NEG_BIG, f32),) * npair, (jnp.zeros((1, 2 * LANE), f32),) * npair)
    carry = lax.fori_loop(0, nfar // 2, att_chunks(False, 2), carry)
    carry = lax.fori_loop(nfar // 2 * 2, nfar, att_chunks(False, 1), carry)
    _, ls = lax.fori_loop(nfar, nchunk, att_chunks(True, 1), carry)
    for h in range(ATT_H):
        l = ls[h // 2][:, (h % 2) * LANE:(h % 2 + 1) * LANE]
        o_ref[:, h * ATT_DH:(h + 1) * ATT_DH] = jnp.transpose(acc_ref[h] / l).astype(bf16)


def _bias_tiles(rel_bias, nd):
    j = jnp.arange(LANE)[:, None]
    i = jnp.arange(LANE)[None, :]
    rel = jnp.arange(nd)[:, None, None] * LANE + (i - j)[None]
    return jnp.moveaxis(_bias_lookup(rel_bias, rel), -1, 1)


def _dsa_prompt(qa, kab, vt, qit, kiwi, rel_bias, nb, t):
    nqb = t // LANE
    topk = min(TOPK_MAX, t // 4)
    assert REL_MAX_DIST <= LANE + 1
    bias = _bias_tiles(rel_bias, 3)
    bias = (bias - bias[2:3]) * LOG2E
    npair = ATT_H // 2
    bias = bias.reshape(3, npair, 2, LANE, LANE).transpose(0, 1, 3, 2, 4).reshape(3, npair, LANE, 2 * LANE)
    ck = math.gcd(t, DSA_CHUNK)
    nch = t // ck
    vt = vt.reshape(nb * npair * nch, 2 * ATT_DH, ck)
    qrow = lambda c: pl.BlockSpec((LANE, c), lambda b, q: (b * nqb + q, 0))
    seq = lambda c: pl.BlockSpec((t, c), lambda b, q: (b, 0))
    kern = functools.partial(_dsa_prompt_kernel, topk=topk, idx_bits=max(1, (t - 1).bit_length()), nch=nch, ck=ck)
    return pl.pallas_call(
        kern,
        grid=(nb, nqb),
        in_specs=[qrow(512), pl.BlockSpec((None, IDX_DIM, IDX_H * LANE), lambda b, q: (b * nqb + q, 0, 0)),
                  qrow(LANE), seq(512),
                  pl.BlockSpec((npair * nch, 2 * ATT_DH, ck), lambda b, q: (b, 0, 0)), seq(LANE),
                  _const_spec((3, npair, LANE, 2 * LANE))],
        out_specs=qrow(512),
        out_shape=jax.ShapeDtypeStruct((nb * t, 512), bf16),
        scratch_shapes=[pltpu.VMEM((t, LANE), i32), pltpu.VMEM((ATT_H, ATT_DH, LANE), f32),
                        pltpu.VMEM((npair, 2 * ATT_DH, 2 * LANE), bf16),
                        pltpu.VMEM((t // PLANE_ROWS, 32, 8, LANE), i32)],
        compiler_params=_cparams(("parallel", "arbitrary")),
        name="dsa_prompt",
    )(qa, qit, kiwi, kab, vt, kiwi, bias)


def _softplus(x):
    return jnp.maximum(x, 0.0) + jnp.log1p(jnp.exp(-jnp.abs(x)))


def _cumsum_rows(tri, a):
    hi = a.astype(bf16)
    r1 = a - hi.astype(f32)
    mid = r1.astype(bf16)
    lo = (r1 - mid.astype(f32)).astype(bf16)
    return _mm(tri, hi) + _mm(tri, mid) + _mm(tri, lo)


def _odd_mixer_kernel(*refs, q, pos0):
    seq_in, shared, seq_out = refs[:7], refs[7:16], refs[-6:]
    for b in range(seq_in[0].shape[0]):
        _odd_mixer_seq(*(r.at[b] for r in seq_in), *shared, *(r.at[b] for r in seq_out), q=q, pos0=pos0)


def _odd_mixer_seq(u_ref, z_ref, xbc_ref, dt_ref, pp_ref, cp_ref, h0_ref, pw_ref, ps_ref, cw_ref, cb_ref,
                   dtb_ref, alog_ref, dsk_ref, nrm_ref, tri_ref, po_ref, y_ref, h_ref, ubuf, xbuf, ybuf, *, q, pos0):
    c = pl.program_id(1)

    @pl.when(c == 0)
    def _():
        ubuf[0:1, :] = jnp.zeros((1, POOL_DIM), f32)
        ubuf[1:16, :] = pp_ref[...]
        xbuf[0:8 - (CONV_W - 1), :] = jnp.zeros((8 - (CONV_W - 1), CONV_DIM), f32)
        xbuf[8 - (CONV_W - 1):8, :] = cp_ref[...]
        h_ref[...] = h0_ref[...]

    pos = pos0 + c * q + lax.broadcasted_iota(i32, (q, LANE), 0)
    causal = lax.broadcasted_iota(i32, (q, q), 0) >= lax.broadcasted_iota(i32, (q, q), 1)

    u = u_ref[...]
    ubuf[16:16 + q, :] = u
    for g, w in enumerate(POOL_WINDOWS):
        sl = slice(g * POOL_GC, (g + 1) * POOL_GC)
        acc = u[:, sl]
        for k in range(1, w):
            acc = acc + ubuf[16 - k:16 - k + q, sl]
        d = acc / jnp.minimum(pos + 1, w).astype(f32) - u[:, sl]
        po_ref[:, sl] = (_mm(d.astype(bf16), pw_ref[g]) * ps_ref[:, sl]).astype(bf16)
    ubuf[0:16, :] = ubuf[q:q + 16, :]

    xbuf[8:8 + q, :] = xbc_ref[...]
    conv = cb_ref[...]
    for j in range(CONV_W):
        off = 8 - (CONV_W - 1) + j
        conv = conv + xbuf[off:off + q, :] * cw_ref[j:j + 1, :]
    xbuf[0:8, :] = xbuf[q:q + 8, :]
    act = _silu(conv)
    xs = act[:, 0:D_INNER]

    dt = _softplus(dt_ref[...] + dtb_ref[...])
    a = dt * (-jnp.exp(alog_ref[...]))
    cs = _cumsum_rows(tri_ref[...], a)
    cs_t = jnp.transpose(cs)
    dt_t = jnp.transpose(dt)
    cs_last = cs[q - 1:q, :]
    w_end = jnp.exp(cs_last - cs) * dt
    ecs = jnp.exp(cs)
    hpg = SSM_H // SSM_G
    for g in range(SSM_G):
        bm = act[:, D_INNER + g * SSM_N:D_INNER + (g + 1) * SSM_N].astype(bf16)
        cm = act[:, D_INNER + (SSM_G + g) * SSM_N:D_INNER + (SSM_G + g + 1) * SSM_N].astype(bf16)
        cb = _mm_nt(cm, bm)
        for hh in range(hpg):
            h = g * hpg + hh
            psl = slice(h * SSM_P, (h + 1) * SSM_P)
            seg = cs[:, h:h + 1] - cs_t[h:h + 1, :]
            lm = jnp.exp(jnp.where(causal, seg, NEG_BIG))
            sc = cb * lm * dt_t[h:h + 1, :]
            xh = xs[:, psl]
            hs = h_ref[h]
            yh = _mm(sc.astype(bf16), xh.astype(bf16)) + _mm_nt(cm, hs.astype(bf16)) * ecs[:, h:h + 1]
            h_ref[h] = hs * jnp.exp(cs_last[:, h:h + 1]) + _mm_tn((xh * w_end[:, h:h + 1]).astype(bf16), bm)
            ybuf[:, psl] = yh
    y = (ybuf[...] + dsk_ref[...] * xs) * _silu(z_ref[...])
    gw = D_INNER // SSM_G
    for g in range(SSM_G):
        sl = slice(g * gw, (g + 1) * gw)
        yg = y[:, sl]
        y_ref[:, sl] = (yg * lax.rsqrt(jnp.mean(yg * yg, -1, keepdims=True) + EPS) * nrm_ref[:, sl]).astype(bf16)


def _pad_lanes(v):
    return jnp.pad(v.astype(f32), (0, LANE - v.shape[0]))[None, :]


def _odd_mixer(u, z, xbc, dt, pool_prev, conv_prev, h0, layer_in, prm, pos0, prev, layer_out, n_layers):
    pool_w, pool_scale, conv_w, conv_b, dt_bias, a_log, d_skip, ssm_norm = prm
    nb, t, _ = u.shape
    q = math.gcd(t, SSD_CHUNK)
    nc = t // q
    grp = _seq_group(nb)
    row = lambda c: pl.BlockSpec((grp, q, c), lambda b, i: (b, i, 0))
    lead = (nb, t)
    tri = jnp.tril(jnp.ones((q, q), bf16))
    tail = (SSM_H, SSM_P, SSM_N)
    in_specs = [row(POOL_DIM), row(D_INNER), row(CONV_DIM), row(LANE),
                _layer_block((POOL_PAST, POOL_DIM), layer_in, grp),
                _layer_block((CONV_W - 1, CONV_DIM), layer_in, grp), _layer_block(tail, layer_in, grp),
                _const_spec((POOL_GROUPS, POOL_GC, POOL_GC)), _const_spec((1, POOL_DIM)),
                _const_spec((CONV_W, CONV_DIM)), _const_spec((1, CONV_DIM)),
                _const_spec((1, LANE)), _const_spec((1, LANE)),
                _const_spec((1, D_INNER)), _const_spec((1, D_INNER)), _const_spec((q, q))]
    state_shape, extra, extra_specs, aliases = _stacked_out(prev, n_layers, nb, tail, len(in_specs), 2)
    return pl.pallas_call(
        functools.partial(_odd_mixer_kernel, q=q, pos0=pos0),
        grid=(nb // grp, nc),
        in_specs=in_specs + extra_specs,
        out_specs=[row(POOL_DIM), row(D_INNER), _layer_block(tail, layer_out, grp)],
        out_shape=[jax.ShapeDtypeStruct(lead + (POOL_DIM,), bf16), jax.ShapeDtypeStruct(lead + (D_INNER,), bf16),
                   state_shape],
        input_output_aliases=aliases,
        scratch_shapes=[pltpu.VMEM((grp, q + 16, POOL_DIM), f32), pltpu.VMEM((grp, q + 8, CONV_DIM), f32),
                        pltpu.VMEM((grp, q, D_INNER), f32)],
        compiler_params=_cparams(("parallel", "arbitrary")),
        name="odd_mixer_t%d" % t,
    )(u, z, xbc, dt, pool_prev, conv_prev, h0, pool_w.astype(bf16), pool_scale[None, :], conv_w, conv_b[None, :],
      _pad_lanes(dt_bias), _pad_lanes(a_log), jnp.repeat(d_skip, SSM_P)[None, :], ssm_norm[None, :], tri, *extra)


def _dsa_sample_score_kernel(pt_ref, qi_ref, wi_ref, kiwi_ref, *rest, npg):
    pages, o_ref = rest[:npg], rest[npg]
    ts = kiwi_ref.shape[0]
    ki_new = jnp.transpose(jnp.concatenate([kiwi_ref[...], jnp.zeros((PAGE - ts, LANE), f32)], axis=0))[0:IDX_DIM, :]
    ki = jnp.concatenate([p[...] for p in pages] + [ki_new], axis=1).astype(bf16)
    nk = ki.shape[1]
    s = jnp.maximum(_mm(qi_ref[...], ki), 0.0) * (wi_ref[...] * IDX_H ** -0.5)
    acc = jnp.sum(s.reshape(ts, IDX_H, nk), axis=1)
    col = lax.broadcasted_iota(i32, (ts, nk), 1)
    row = lax.broadcasted_iota(i32, (ts, nk), 0)
    o_ref[...] = jnp.where(col <= npg * PAGE + row, _sort_key(acc), INT_MIN)


def _select_kernel(k_ref, o_ref, key_ref, planes_ref, *, topk, idx_bits):
    nk, rows = k_ref.shape[0], key_ref.shape[0]
    key_ref[0:nk, :] = k_ref[...]
    if rows > nk:
        key_ref[nk:rows, :] = jnp.full((rows - nk, LANE), INT_MIN, i32)
    t = _select_topk_planes(key_ref, planes_ref, 1, rows, topk, idx_bits)
    o_ref[...] = jnp.where(key_ref[0:nk, :] >= t, 1.0, 0.0)


def _dsa_sample_attn_kernel(pt_ref, q_ref, kn_ref, vn_ref, sel_ref, bias_ref, spread_ref, *rest, npg):
    kpages, vpages, o_ref = rest[:npg], rest[npg:2 * npg], rest[2 * npg]
    pad = jnp.zeros((PAGE * ATT_H - kn_ref.shape[0], ATT_DH), f32)
    kx = jnp.concatenate([p[...] for p in kpages] + [kn_ref[...], pad], axis=0).astype(bf16)
    vx = jnp.concatenate([p[...] for p in vpages] + [vn_ref[...], pad], axis=0).astype(bf16)
    selx = jnp.concatenate([_mm(sel_ref[:, j * PAGE:(j + 1) * PAGE], spread_ref[...]) for j in range(npg + 1)],
                           axis=1)
    s = _mm_nt(q_ref[...], kx) * ATT_DH ** -0.5 + jnp.where(selx > 0.5, bias_ref[...], NEG_BIG)
    p = jnp.exp(s - jnp.max(s, axis=-1, keepdims=True))
    o_ref[...] = (_mm(p.astype(bf16), vx) / jnp.sum(p, axis=-1, keepdims=True)).astype(bf16)


def _dsa_sample(qa, ka, va, qi, kiwi, cache_k, cache_v, cache_ki, layer, page_table, rel_bias):
    db, ts = qa.shape[:2]
    npg = page_table.shape[1]
    n_past = npg * PAGE
    nk = n_past + PAGE
    nq = db * ts
    topk = min(TOPK_MAX, (n_past + ts) // 4)
    hd = ATT_H * ATT_DH
    ki_page = lambda j: pl.BlockSpec((None, None, IDX_DIM, PAGE), lambda b, pt: (layer, pt[b, j], 0, 0))
    seq = lambda r, c: pl.BlockSpec((None, r, c), lambda b, pt: (b, 0, 0))

    keys = pl.pallas_call(
        functools.partial(_dsa_sample_score_kernel, npg=npg),
        grid_spec=pltpu.PrefetchScalarGridSpec(
            num_scalar_prefetch=1, grid=(db,),
            in_specs=[seq(ts * IDX_H, IDX_DIM), seq(ts * IDX_H, 1), seq(ts, LANE)] + [ki_page(j) for j in range(npg)],
            out_specs=seq(ts, nk)),
        out_shape=jax.ShapeDtypeStruct((db, ts, nk), i32),
        compiler_params=_cparams(("parallel",)),
        name="dsa_sample_score",
    )(page_table, qi.reshape(db, ts * IDX_H, IDX_DIM),
      kiwi[..., IDX_DIM:IDX_DIM + IDX_H].reshape(db, ts * IDX_H, 1), kiwi,
      *([jnp.swapaxes(cache_ki, 2, 3)] * npg))

    col = pl.BlockSpec((nk, LANE), lambda i: (0, i))
    sel_rows = -(-nk // PLANE_ROWS) * PLANE_ROWS
    sel = pl.pallas_call(
        functools.partial(_select_kernel, topk=topk, idx_bits=max(1, (sel_rows - 1).bit_length())),
        grid=(nq // LANE,),
        in_specs=[col],
        out_specs=col,
        out_shape=jax.ShapeDtypeStruct((nk, nq), f32),
        scratch_shapes=[pltpu.VMEM((sel_rows, LANE), i32),
                        pltpu.VMEM((sel_rows // PLANE_ROWS, 32, 8, LANE), i32)],
        compiler_params=_cparams(("parallel",)),
        name="dsa_sample_select",
    )(keys.reshape(nq, nk).T)
    sel = jnp.repeat(sel.T.reshape(db, ts, nk), ATT_H, axis=1).astype(bf16)
    rel = n_past + jnp.arange(ts)[:, None] - jnp.arange(nk)[None, :]
    bias = jnp.repeat(jnp.moveaxis(_bias_lookup(rel_bias, rel), -1, 1), ATT_H, axis=-1)
    same_head = jnp.arange(nk * ATT_H)[None, :] % ATT_H == jnp.arange(ATT_H)[:, None]
    bias = jnp.where(same_head[None], bias, NEG_BIG).reshape(ts * ATT_H, nk * ATT_H)
    spread = (jnp.arange(PAGE * ATT_H)[None, :] // ATT_H == jnp.arange(PAGE)[:, None]).astype(bf16)

    n_pages = cache_k.shape[1]
    rows = PAGE * ATT_H
    kv_page = lambda j: pl.BlockSpec((rows, ATT_DH), lambda b, pt: (layer * n_pages + pt[b, j], 0))
    const = lambda a: pl.BlockSpec(a.shape, lambda b, pt: (0, 0))
    as_rows = lambda a: a.reshape(db, ts * ATT_H, ATT_DH)
    out = pl.pallas_call(
        functools.partial(_dsa_sample_attn_kernel, npg=npg),
        grid_spec=pltpu.PrefetchScalarGridSpec(
            num_scalar_prefetch=1, grid=(db,),
            in_specs=[seq(ts * ATT_H, ATT_DH)] * 3 + [seq(ts * ATT_H, nk), const(bias), const(spread)]
                     + [kv_page(j) for j in range(npg)] * 2,
            out_specs=seq(ts * ATT_H, ATT_DH)),
        out_shape=jax.ShapeDtypeStruct((db, ts * ATT_H, ATT_DH), bf16),
        compiler_params=_cparams(("parallel",)),
        name="dsa_sample_attn",
    )(page_table, as_rows(qa), as_rows(ka), as_rows(va), sel, bias, spread,
      *([cache_k.reshape(-1, ATT_DH)] * npg), *([cache_v.reshape(-1, ATT_DH)] * npg))
    return out.reshape(db, ts, hd)


def kernel(x_prompt, x_sample, cache_k, cache_v, cache_kidx, state_ret, state_pool, state_conv, state_ssm,
           page_table, norm_mix, norm_ffn, w_in_even, w_out_even, q_norm, k_norm, rel_bias,
           w_in_odd, w_out_odd, pool_w, pool_scale, conv_w, conv_b, dt_bias, a_log, d_skip, ssm_norm,
           w_gate, w_up, w_down):
    bp, sp, d = x_prompt.shape
    db, ts, _ = x_sample.shape
    n_p, n_s = bp * sp, db * ts
    assert d == D_MODEL and n_p % TM == 0 and n_s % TM == 0 and sp % LANE == 0 and sp >= POOL_PAST
    n_past = page_table.shape[1] * PAGE
    n_even, n_odd = (DEPTH + 1) // 2, DEPTH // 2
    hd = ATT_H * ATT_DH
    xp, xs = x_prompt.reshape(n_p, d), x_sample.reshape(n_s, d)
    seqs = lambda a: a.reshape(db, ts, a.shape[-1])
    pseqs = lambda a: a.reshape(bp, sp, a.shape[-1])
    last = lambda a, n: a.reshape(bp, sp, a.shape[-1])[:, sp - n:]
    tail = lambda prev, cur, n: jnp.concatenate([prev.astype(f32), cur], axis=1)[:, -n:]
    zeros = lambda *s: jnp.zeros((1, bp) + s, f32)
    outs = [[] for _ in range(14)]
    rstate_p = rstate_s = h_p = h_s = kv_p = kv_s = None
    w_in_e = jnp.pad(w_in_even, ((0, 0), (0, 0), (0, EVEN_PROJ_PAD - EVEN_PROJ))).astype(bf16)
    w_in_o = jnp.pad(w_in_odd, ((0, 0), (0, 0), (0, ODD_PROJ_PAD - ODD_PROJ))).astype(bf16)
    w_out_e, w_out_o = w_out_even.astype(bf16), w_out_odd.astype(bf16)
    w_ffn = (w_gate.astype(bf16), w_up.astype(bf16), w_down.astype(bf16))
    for l in range(DEPTH):
        if l % 2 == 0:
            i = l // 2
            w_out, mix_layer = w_out_e, i
            prm = (norm_mix[l][None], w_in_e, q_norm[i][None], k_norm[i][None])
            ret_p, qa_p, ka_p, kab_p, va_p, kiwi_p, qit_p, vt_p = _even_proj(xp, *prm, i, n_even, kv_p, prompt_len=sp)
            ret_s, qa_s, ka_s, _, va_s, kiwi_s, qi_s = _even_proj(xs, *prm, i, n_even, kv_s)
            kv_p, kv_s = (ka_p, va_p), (ka_s, va_s)
            mix_a_p, rstate_p = _retention(pseqs(ret_p), zeros(RET_H, RET_DK, RET_DV), 0, 0, rstate_p, i, n_even)
            mix_a_s, rstate_s = _retention(seqs(ret_s), state_ret, i, n_past, rstate_s, i, n_even)
            mix_b_p = _dsa_prompt(qa_p, kab_p, vt_p, qit_p, kiwi_p, rel_bias, bp, sp)
            mix_b_s = _dsa_sample(seqs(qa_s), ka_s[i], va_s[i], seqs(qi_s), seqs(kiwi_s),
                                  cache_k, cache_v, cache_kidx, i, page_table, rel_bias)
            new = [None, None, kiwi_p.reshape(bp, sp, LANE)[..., :IDX_DIM], None, None, None, None,
                   None, None, seqs(kiwi_s)[..., :IDX_DIM], None, None, None, None]
        else:
            j = l // 2
            w_out, mix_layer = w_out_o, j
            u_p, z_p, xbc_p, dt_p = _odd_proj(xp, norm_mix[l][None], w_in_o, j)
            u_s, z_s, xbc_s, dt_s = _odd_proj(xs, norm_mix[l][None], w_in_o, j)
            prm = (pool_w[j], pool_scale[j], conv_w[j], conv_b[j], dt_bias[j], a_log[j], d_skip[j], ssm_norm[j])
            mix_a_p, mix_b_p, h_p = _odd_mixer(
                pseqs(u_p), pseqs(z_p), pseqs(xbc_p), pseqs(dt_p), zeros(POOL_PAST, POOL_DIM),
                zeros(CONV_W - 1, CONV_DIM), zeros(SSM_H, SSM_P, SSM_N), 0, prm, 0, h_p, j, n_odd)
            mix_a_s, mix_b_s, h_s = _odd_mixer(
                seqs(u_s), seqs(z_s), seqs(xbc_s), seqs(dt_s), state_pool, state_conv, state_ssm, j,
                prm, n_past, h_s, j, n_odd)
            new = [None, None, None, None, last(u_p, POOL_PAST), last(xbc_p, CONV_W - 1), None,
                   None, None, None, None,
                   tail(state_pool[j], seqs(u_s), POOL_PAST), tail(state_conv[j], seqs(xbc_s), CONV_W - 1), None]
        for acc, leaf in zip(outs, new):
            if leaf is not None:
                acc.append(leaf)
        ffn = (w_out, mix_layer, norm_ffn[l][None], *w_ffn, l)
        xp = _out_ffn(xp, mix_a_p.reshape(n_p, -1), mix_b_p.reshape(n_p, -1), *ffn)
        xs = _out_ffn(xs, mix_a_s.reshape(n_s, -1), mix_b_s.reshape(n_s, -1), *ffn)
    leaves = [jnp.stack(a) if a else None for a in outs]
    leaves[3], leaves[6], leaves[10], leaves[13] = rstate_p, h_p, rstate_s, h_s
    leaves[0], leaves[1] = (a.reshape(n_even, bp, sp, ATT_H, ATT_DH) for a in kv_p)
    leaves[7], leaves[8] = (a.reshape(n_even, db, ts, ATT_H, ATT_DH) for a in kv_s)
    return (xp.reshape(bp, sp, d), xs.reshape(db, ts, d)) + tuple(leaves)
```

```python
import functools
import math

import jax
import jax.numpy as jnp
import numpy as np
from jax import lax
from jax.experimental import pallas as pl
from jax.experimental.pallas import tpu as pltpu

f32 = jnp.float32
bf16 = jnp.bfloat16
i32 = jnp.int32

D_MODEL = 1024
DEPTH = 4
PAGE = 128
RET_H, RET_DK, RET_DV, RET_CHUNK = 4, 128, 128, 128
ROPE_BASE = 10000.0
ATT_H, ATT_DH, ATT_BLOCK = 4, 128, 128
IDX_H, IDX_DIM = 8, 64
TOPK_MAX = 256
REL_BUCKETS, REL_MAX_DIST = 32, 128
POOL_WINDOWS = (2, 4, 8, 16)
POOL_GROUPS = 4
POOL_DIM = D_MODEL // 2
POOL_GC = POOL_DIM // POOL_GROUPS
POOL_PAST = 15
D_INNER = D_MODEL // 2
SSM_P = 64
SSM_H = D_INNER // SSM_P
SSM_G = 2
SSM_N = 128
CONV_W = 4
CONV_DIM = D_INNER + 2 * SSM_G * SSM_N
SSD_CHUNK = 128
FF_DIM = -(-8 * D_MODEL // (3 * 256)) * 256
EPS = 1e-6

EVEN_PROJ = 4 * 512 + 3 * 512 + 512 + IDX_DIM + IDX_H
EVEN_PROJ_PAD = 4224
ODD_PROJ = POOL_DIM + D_INNER + CONV_DIM + SSM_H
ODD_PROJ_PAD = 2176

LANE = 128
INT_MIN = -(2 ** 31)
NEG_BIG = -1e30
LOG2E = 1.4426950408889634
VMEM_LIMIT = 56 * 1024 * 1024
TM = 512
FF_SPLIT = 2
DSA_CHUNK = 512
SCORE_ROWS = 256
SEQ_GROUP = 8


def _cparams(sem):
    return pltpu.CompilerParams(dimension_semantics=sem, vmem_limit_bytes=VMEM_LIMIT)


def _mm(a, b):
    return jnp.dot(a, b, preferred_element_type=f32)


def _mm_nt(a, b):
    return lax.dot_general(a, b, (((1,), (1,)), ((), ())), preferred_element_type=f32)


def _mm_tn(a, b):
    return lax.dot_general(a, b, (((0,), (0,)), ((), ())), preferred_element_type=f32)


def _rms(x, g):
    return x * lax.rsqrt(jnp.mean(x * x, -1, keepdims=True) + EPS) * g


def _silu(x):
    return x / (1.0 + jnp.exp(-x))


def _const_spec(shape):
    nd = len(shape)
    return pl.BlockSpec(shape, lambda *a: (0,) * nd)


def _even_proj_kernel(x_ref, g_ref, w_ref, qg_ref, kg_ref, *refs, for_prompt, n_aliased):
    ret_ref, qa_ref, ka_ref, kab_ref, va_ref, kiwi_ref, *idx_refs = refs[n_aliased:]
    xb = _rms(x_ref[...], g_ref[...]).astype(bf16)
    qa = _mm(xb, w_ref[:, 2048:2560])
    ka = _mm(xb, w_ref[:, 2560:3072])
    va = _mm(xb, w_ref[:, 3072:3584])
    qi = _mm(xb, w_ref[:, 3584:4096]) * IDX_DIM ** -0.5
    kiwi_ref[...] = _mm(xb, w_ref[:, 4096:4224])
    for h in range(ATT_H):
        sl = slice(h * ATT_DH, (h + 1) * ATT_DH)
        qa_ref[:, sl] = _rms(qa[:, sl], qg_ref[...]).astype(bf16)
        kn = _rms(ka[:, sl], kg_ref[...])
        ka_ref[:, h, :] = kn
        kab_ref[:, sl] = kn.astype(bf16)
        va_ref[:, h, :] = va[:, sl]
    if for_prompt:
        qit_ref, vt_ref = idx_refs
        for blk in range(TM // LANE):
            for h in range(IDX_H):
                part = qi[blk * LANE:(blk + 1) * LANE, h * IDX_DIM:(h + 1) * IDX_DIM]
                qit_ref[blk, :, h * LANE:(h + 1) * LANE] = jnp.transpose(part).astype(bf16)
        for pr in range(ATT_H // 2):
            vt_ref[pr] = jnp.transpose(va[:, pr * 2 * ATT_DH:(pr + 1) * 2 * ATT_DH]).astype(bf16)
    else:
        idx_refs[0][...] = qi.astype(bf16)
    ret_ref[...] = _mm(xb, w_ref[:, 0:2048])


def _even_proj(x, g, w, qg, kg, layer, n_layers, prev_kv, prompt_len=None):
    n = x.shape[0]
    row = lambda c: pl.BlockSpec((TM, c), lambda i: (i, 0))
    heads = pl.BlockSpec((None, TM, ATT_H, ATT_DH), lambda i: (layer, i, 0, 0))
    stacked = jax.ShapeDtypeStruct((n_layers, n, ATT_H, ATT_DH), f32)
    out_specs = [row(2048), row(512), heads, row(512), heads, row(LANE)]
    out_shape = [jax.ShapeDtypeStruct((n, 2048), f32), jax.ShapeDtypeStruct((n, 512), bf16), stacked,
                 jax.ShapeDtypeStruct((n, 512), bf16), stacked, jax.ShapeDtypeStruct((n, LANE), f32)]
    extra = [] if prev_kv is None else list(prev_kv)
    aliases = {} if prev_kv is None else {5: 2, 6: 4}
    if prompt_len is None:
        out_specs.append(row(512))
        out_shape.append(jax.ShapeDtypeStruct((n, 512), bf16))
    else:
        ck = math.gcd(prompt_len, DSA_CHUNK)
        assert ck % TM == 0 and prompt_len % ck == 0
        per_chunk, per_seq, npair = ck // TM, prompt_len // TM, ATT_H // 2
        out_specs += [pl.BlockSpec((TM // LANE, IDX_DIM, IDX_H * LANE), lambda i: (i, 0, 0)),
                      pl.BlockSpec((None, npair, None, 2 * ATT_DH, TM),
                                   lambda i: (i // per_seq, 0, (i % per_seq) // per_chunk, 0, i % per_chunk))]
        out_shape += [jax.ShapeDtypeStruct((n // LANE, IDX_DIM, IDX_H * LANE), bf16),
                      jax.ShapeDtypeStruct((n // prompt_len, npair, prompt_len // ck, 2 * ATT_DH, ck), bf16)]
    return pl.pallas_call(
        functools.partial(_even_proj_kernel, for_prompt=prompt_len is not None, n_aliased=len(extra)),
        grid=(n // TM,),
        in_specs=[row(D_MODEL), _const_spec((1, D_MODEL)), _layer_weight(w, layer),
                  _const_spec((1, ATT_DH)), _const_spec((1, ATT_DH))]
                 + [pl.BlockSpec(memory_space=pl.ANY)] * len(extra),
        out_specs=out_specs,
        out_shape=out_shape,
        input_output_aliases=aliases,
        compiler_params=_cparams(("parallel",)),
        name="even_proj",
    )(x, g, w, qg, kg, *extra)


def _odd_proj_kernel(x_ref, g_ref, w_ref, u_ref, z_ref, xbc_ref, dt_ref):
    xb = _rms(x_ref[...], g_ref[...]).astype(bf16)
    u_ref[...] = _mm(xb, w_ref[:, 0:512])
    z_ref[...] = _mm(xb, w_ref[:, 512:1024])
    xbc_ref[...] = _mm(xb, w_ref[:, 1024:2048])
    dt_ref[...] = _mm(xb, w_ref[:, 2048:2176])


def _odd_proj(x, g, w, layer):
    n = x.shape[0]
    row = lambda c: pl.BlockSpec((TM, c), lambda i: (i, 0))
    outs = [512, 512, 1024, LANE]
    return pl.pallas_call(
        _odd_proj_kernel,
        grid=(n // TM,),
        in_specs=[row(D_MODEL), _const_spec((1, D_MODEL)), _layer_weight(w, layer)],
        out_specs=[row(c) for c in outs],
        out_shape=[jax.ShapeDtypeStruct((n, c), f32) for c in outs],
        compiler_params=_cparams(("parallel",)),
        name="odd_proj",
    )(x, g, w)


def _out_ffn_kernel(x_ref, a_ref, b_ref, wo_ref, g_ref, wg_ref, wu_ref, wd_ref, o_ref):
    half = wo_ref.shape[0] // 2
    x = x_ref[...] + _mm(a_ref[...], wo_ref[0:half, :]) + _mm(b_ref[...], wo_ref[half:, :])
    hb = _rms(x, g_ref[...]).astype(bf16)
    fc = FF_DIM // FF_SPLIT
    ff = None
    for c in range(FF_SPLIT):
        sl = slice(c * fc, (c + 1) * fc)
        act = (_silu(_mm(hb, wg_ref[:, sl])) * _mm(hb, wu_ref[:, sl])).astype(bf16)
        down = _mm(act, wd_ref[sl, :])
        ff = down if ff is None else ff + down
    o_ref[...] = x + ff


def _layer_weight(w, layer, single_buffer=False):
    mode = dict(pipeline_mode=pl.Buffered(1)) if single_buffer else {}
    return pl.BlockSpec((None,) + w.shape[1:], lambda i: (layer,) + (0,) * (w.ndim - 1), **mode)


def _out_ffn(x, a, b, wo, mix_layer, g, wg, wu, wd, layer):
    n = x.shape[0]
    row = lambda c: pl.BlockSpec((TM, c), lambda i: (i, 0))
    return pl.pallas_call(
        _out_ffn_kernel,
        grid=(n // TM,),
        in_specs=[row(D_MODEL), row(a.shape[1]), row(b.shape[1]), _layer_weight(wo, mix_layer, True),
                  _const_spec((1, D_MODEL)), _layer_weight(wg, layer, True), _layer_weight(wu, layer, True),
                  _layer_weight(wd, layer, True)],
        out_specs=row(D_MODEL),
        out_shape=jax.ShapeDtypeStruct((n, D_MODEL), f32),
        compiler_params=_cparams(("parallel",)),
        name="out_ffn",
    )(x, a, b, wo, g, wg, wu, wd)


def _rope_tables(pos):
    half = RET_DK // 2
    inv = ROPE_BASE ** (-jnp.linspace(0.0, 1.0, half, dtype=f32))
    ang = pos.astype(f32)[:, None] * inv[None, :]
    cos, sin = jnp.cos(ang), jnp.sin(ang)
    return jnp.concatenate([cos, cos], -1), jnp.concatenate([-sin, sin], -1)


def _ret_decay(q):
    log_g = jnp.log1p(-jnp.exp2(-5.0 - jnp.arange(RET_H, dtype=f32)))
    idx = jnp.arange(q, dtype=f32)
    diff = idx[:, None] - idx[None, :]
    dmask = jnp.where(diff[None] >= 0, jnp.exp(log_g[:, None, None] * jnp.maximum(diff, 0.0)[None]), 0.0)
    xi = jnp.exp(log_g[:, None] * (idx + 1.0)[None])
    zeta = jnp.exp(log_g[:, None] * (q - 1.0 - idx)[None])
    g_chunk = jnp.exp(log_g * q)
    return dmask, xi, zeta, g_chunk


def _t5_bucket(rel):
    n = jnp.maximum(rel, 0)
    max_exact = REL_BUCKETS // 2
    nf = jnp.maximum(n, 1).astype(f32)
    large = max_exact + (jnp.log(nf / max_exact) / math.log(REL_MAX_DIST / max_exact)
                         * (REL_BUCKETS - max_exact)).astype(i32)
    large = jnp.minimum(large, REL_BUCKETS - 1)
    return jnp.where(n < max_exact, n, large)


def _bias_lookup(rel_bias, rel):
    onehot = jax.nn.one_hot(_t5_bucket(rel), REL_BUCKETS, dtype=f32)
    return jnp.einsum("...b,bh->...h", onehot, rel_bias.astype(f32), precision=lax.Precision.HIGHEST)


def _rotary(x, c, s):
    return x * c + pltpu.roll(x, RET_DK // 2, 1) * s


def _retention_kernel(q_ref, k_ref, v_ref, g_ref, r0_ref, c_ref, s_ref, dm_ref, xi_ref, zt_ref, gc_ref,
                      *rest):
    o_ref, r_ref = rest[-2:]

    @pl.when(pl.program_id(1) == 0)
    def _():
        r_ref[...] = r0_ref[...]

    cos, sin = c_ref[...], s_ref[...]
    for b in range(q_ref.shape[0]):
        for h in range(RET_H):
            sl = slice(h * RET_DK, (h + 1) * RET_DK)
            qr = _rotary(q_ref[b, :, sl], cos, sin).astype(bf16)
            kr = _rotary(k_ref[b, :, sl], cos, sin) * RET_DK ** -0.5
            vb = v_ref[b, :, sl].astype(bf16)
            r = r_ref[b, h]
            s = _mm_nt(qr, kr.astype(bf16)) * dm_ref[h]
            o = _mm(s.astype(bf16), vb) + _mm(qr, r.astype(bf16)) * xi_ref[h]
            r_ref[b, h] = r * gc_ref[h, 0:1, :] + _mm_tn((kr * zt_ref[h]).astype(bf16), vb)
            o = o * lax.rsqrt(jnp.mean(o * o, -1, keepdims=True) + EPS)
            o_ref[b, :, sl] = (_silu(g_ref[b, :, sl]) * o).astype(bf16)


def _seq_group(nb):
    return math.gcd(nb, SEQ_GROUP)


def _layer_block(tail, layer, group):
    return pl.BlockSpec((None, group) + tail, lambda b, c: (layer, b) + (0,) * len(tail))


def _stacked_out(prev, n_layers, nb, tail, n_inputs, out_index):
    shape = jax.ShapeDtypeStruct((n_layers, nb) + tail, f32)
    if prev is None:
        return shape, [], [], {}
    return shape, [prev], [pl.BlockSpec(memory_space=pl.ANY)], {n_inputs: out_index}


def _retention(ret, r0, layer_in, pos0, prev, layer_out, n_layers):
    nb, t, _ = ret.shape
    q = math.gcd(t, RET_CHUNK)
    grp = _seq_group(nb)
    cos, sin = _rope_tables(pos0 + jnp.arange(t))
    dmask, xi, zeta, g_chunk = _ret_decay(q)
    bcast = lambda a: jnp.broadcast_to(a[:, :, None], (RET_H, a.shape[1], LANE))
    col = lambda j: pl.BlockSpec((grp, q, 512), lambda b, c: (b, c, j))
    tab = pl.BlockSpec((q, LANE), lambda b, c: (c, 0))
    tail = (RET_H, RET_DK, RET_DV)
    in_specs = [col(0), col(1), col(2), col(3), _layer_block(tail, layer_in, grp), tab, tab,
                _const_spec((RET_H, q, q)), _const_spec((RET_H, q, LANE)), _const_spec((RET_H, q, LANE)),
                _const_spec((RET_H, 8, LANE))]
    state_shape, extra, extra_specs, aliases = _stacked_out(prev, n_layers, nb, tail, len(in_specs), 1)
    return pl.pallas_call(
        _retention_kernel,
        grid=(nb // grp, t // q),
        in_specs=in_specs + extra_specs,
        out_specs=[col(0), _layer_block(tail, layer_out, grp)],
        out_shape=[jax.ShapeDtypeStruct((nb, t, 512), bf16), state_shape],
        input_output_aliases=aliases,
        compiler_params=_cparams(("parallel", "arbitrary")),
        name="retention_t%d" % t,
    )(ret, ret, ret, ret, r0, cos, sin, dmask, bcast(xi), bcast(zeta),
      jnp.broadcast_to(g_chunk[:, None, None], (RET_H, 8, LANE)), *extra)


def _sort_key(score):
    bits = lax.bitcast_convert_type(score, i32)
    bits = jnp.where(bits == INT_MIN, 0, bits)
    return jnp.where(bits < 0, bits ^ 0x7FFFFFFF, bits)


def _count(key_ref, nchunk, ck, pred):
    def body(c, acc):
        r0 = pl.multiple_of(c * ck, ck)
        hit = jnp.where(pred(key_ref[pl.ds(r0, ck), :], r0), 1, 0).astype(i32)
        return acc + jnp.sum(hit.reshape(ck // 8, 8, LANE), axis=0)
    acc = lax.fori_loop(0, nchunk, body, jnp.zeros((8, LANE), i32))
    return jnp.sum(acc, axis=0, keepdims=True)


def _cut_ties(key_ref, nchunk, ck, t, surplus, keep, idx_bits):
    rows = lax.broadcasted_iota(i32, (ck, LANE), 0)

    @pl.when(jnp.max(jnp.where(surplus, 1, 0)) > 0)
    def _():
        want = jnp.where(surplus, keep, jnp.int32(2 ** 30))

        def idx_step(it, x):
            cand = x + jnp.left_shift(jnp.int32(1), idx_bits - 1 - it)
            cnt = _count(key_ref, nchunk, ck, lambda blk, r0: jnp.where(blk == t, rows + r0, cand) < cand)
            return jnp.where(cnt < want, cand, x)

        last = lax.fori_loop(0, idx_bits, idx_step, jnp.zeros((1, LANE), i32))

        def demote(c, carry):
            r0 = pl.multiple_of(c * ck, ck)
            blk = key_ref[pl.ds(r0, ck), :]
            drop = jnp.where(blk == t, rows + r0, last) > last
            key_ref[pl.ds(r0, ck), :] = jnp.where(drop, INT_MIN, blk)
            return carry

        lax.fori_loop(0, nchunk, demote, 0)


PLANE_ROWS = 256
_SWAP_STEPS = ((16, 0x0000FFFF), (8, 0x00FF00FF), (4, 0x0F0F0F0F), (2, 0x33333333), (1, 0x55555555))


def _bit_planes(words):
    a = list(words)
    for j, m in _SWAP_STEPS:
        for k in range(32):
            if k & j == 0:
                t = ((a[k] >> j) ^ a[k + j]) & m
                a[k] = a[k] ^ (t << j)
                a[k + j] = a[k + j] ^ t
    return a


def _select_topk_planes(key_ref, planes_ref, nchunk, ck, topk, idx_bits):
    u32 = jnp.uint32
    per_chunk = ck // PLANE_ROWS
    ngroups = nchunk * per_chunk

    def to_planes(g, carry):
        r0 = pl.multiple_of(g * PLANE_ROWS, PLANE_ROWS)
        words = [lax.bitcast_convert_type(key_ref[pl.ds(r0 + 8 * i, 8), :], u32) for i in range(32)]
        planes = _bit_planes(words)
        planes[31] = ~planes[31]
        for b in range(32):
            planes_ref[g, b] = lax.bitcast_convert_type(planes[b], i32)
        return carry

    lax.fori_loop(0, ngroups, to_planes, 0)

    gmax = planes_ref.shape[0]
    full = jnp.full((8, LANE), 0xFFFFFFFF, u32)
    none = jnp.zeros((8, LANE), u32)

    def radix_select(ng):
        def run():
            alive0 = tuple(jnp.where(g < ngroups, full, none) for g in range(ng))

            def bit_step(it, carry):
                t, need, alive = carry
                b = 31 - it
                ones = [a & lax.bitcast_convert_type(planes_ref[g, b], u32) for g, a in enumerate(alive)]
                cnt = functools.reduce(lambda x, y: x + y, [lax.population_count(o) for o in ones])
                c = jnp.sum(cnt.astype(i32), axis=0, keepdims=True)
                take = c >= need
                t = t | jnp.where(take, jnp.left_shift(jnp.int32(1), b), 0)
                need = jnp.where(take, need, need - c)
                return t, need, tuple(jnp.where(take, o, a ^ o) for o, a in zip(ones, alive))

            init = (jnp.zeros((1, LANE), i32), jnp.full((1, LANE), topk, i32), alive0)
            t, need, alive = lax.fori_loop(0, 32, bit_step, init)
            equal = functools.reduce(lambda x, y: x + y, [lax.population_count(a) for a in alive])
            return t, need, jnp.sum(equal.astype(i32), axis=0, keepdims=True)
        return run

    sizes = sorted({max(1, gmax // 4), max(1, gmax // 2), gmax})
    if isinstance(ngroups, int):
        t, need, n_equal = radix_select(min(s for s in sizes if s >= ngroups))()
    else:
        which = sum((ngroups > s).astype(i32) for s in sizes[:-1]) if len(sizes) > 1 else jnp.int32(0)
        t, need, n_equal = lax.switch(which, [radix_select(s) for s in sizes])
    t = t ^ INT_MIN
    surplus = (n_equal > need) & (t > INT_MIN)
    t = jnp.maximum(t, INT_MIN + 1)
    _cut_ties(key_ref, nchunk, ck, t, surplus, need, idx_bits)
    return t


def _dsa_prompt_kernel(qa_ref, qit_ref, kiwiq_ref, k_ref, vt_ref, kiwik_ref, bias_ref, o_ref, key_ref, acc_ref,
                       qbd_ref, planes_ref, *, topk, idx_bits, nch, ck):
    qb = pl.program_id(1)
    cb = ck // LANE
    nchunk = (qb + cb) // cb
    wit = jnp.transpose(kiwiq_ref[...])[IDX_DIM:IDX_DIM + IDX_H, :] * IDX_H ** -0.5
    qpos = qb * LANE + lax.broadcasted_iota(i32, (SCORE_ROWS, LANE), 1)
    rows = lax.broadcasted_iota(i32, (SCORE_ROWS, LANE), 0)

    def score_chunk(c, carry):
        for sub in range(ck // SCORE_ROWS):
            r0 = pl.multiple_of(c * ck, ck) + sub * SCORE_ROWS
            kic = kiwik_ref[pl.ds(r0, SCORE_ROWS), 0:IDX_DIM].astype(bf16)
            acc = jnp.zeros((SCORE_ROWS, LANE), f32)
            for h in range(0, IDX_H, 2):
                s = _mm(kic, qit_ref[:, h * LANE:(h + 2) * LANE])
                acc = acc + jnp.maximum(s[:, :LANE], 0.0) * wit[h:h + 1, :]
                acc = acc + jnp.maximum(s[:, LANE:], 0.0) * wit[h + 1:h + 2, :]
            valid = rows + r0 <= qpos
            key_ref[pl.ds(r0, SCORE_ROWS), :] = jnp.where(valid, _sort_key(acc), INT_MIN)
        return carry

    lax.fori_loop(0, nchunk, score_chunk, 0)
    t = _select_topk_planes(key_ref, planes_ref, nchunk, ck, topk, idx_bits)

    npair = ATT_H // 2
    for pr in range(npair):
        qbd_ref[pr] = jnp.zeros((2 * ATT_DH, 2 * LANE), bf16)
        for hh in range(2):
            sl = slice((2 * pr + hh) * ATT_DH, (2 * pr + hh + 1) * ATT_DH)
            qbd_ref[pr, hh * ATT_DH:(hh + 1) * ATT_DH, hh * LANE:(hh + 1) * LANE] = (
                jnp.transpose(qa_ref[:, sl].astype(f32)).astype(bf16))
    acc_ref[...] = jnp.zeros_like(acc_ref)
    nfar = jnp.maximum(qb - 1, 0) // cb

    pairs = range(npair)

    def att_chunks(near, unroll):
        def body(i, carry):
            ms, ls = list(carry[0]), list(carry[1])
            cs = [i * unroll + u for u in range(unroll)]
            r0s = [pl.multiple_of(c * ck, ck) for c in cs]
            qks = [[_mm(k_ref[pl.ds(r0, ck), pr * 2 * ATT_DH:(pr + 1) * 2 * ATT_DH], qbd_ref[pr]) for pr in pairs]
                   for r0 in r0s]
            for c, r0, qk in zip(cs, r0s, qks):
                neg1 = jnp.where(key_ref[pl.ds(r0, ck), :] >= t, 0.0, NEG_BIG)
                negm = jnp.concatenate([neg1, neg1], axis=1)
                alphas, ps = [], []
                for pr in pairs:
                    s = qk[pr] * (ATT_DH ** -0.5 * LOG2E) + negm
                    if near:
                        s = s + jnp.concatenate(
                            [bias_ref[jnp.clip(qb - (c * cb + j), 0, 2), pr] for j in range(cb)], axis=0)
                    m_new = jnp.maximum(ms[pr], jnp.max(s, axis=0, keepdims=True))
                    alpha = jnp.exp2(ms[pr] - m_new)
                    p = jnp.exp2(s - m_new)
                    ls[pr] = ls[pr] * alpha + jnp.sum(p, axis=0, keepdims=True)
                    ms[pr] = m_new
                    alphas.append(alpha)
                    ps.append(p.astype(bf16))
                pvs = [_mm(vt_ref[pr * nch + c], ps[pr]) for pr in pairs]
                for pr in pairs:
                    for hh in range(2):
                        d = slice(hh * LANE, (hh + 1) * LANE)
                        acc_ref[2 * pr + hh] = acc_ref[2 * pr + hh] * alphas[pr][:, d] + pvs[pr][d, d]
            return tuple(ms), tuple(ls)
        return body

    carry = ((jnp.full((1, 2 * LANE), NEG_BIG, f32),) * npair, (jnp.zeros((1, 2 * LANE), f32),) * npair)
    carry = lax.fori_loop(0, nfar // 2, att_chunks(False, 2), carry)
    carry = lax.fori_loop(nfar // 2 * 2, nfar, att_chunks(False, 1), carry)
    _, ls = lax.fori_loop(nfar, nchunk, att_chunks(True, 1), carry)
    for h in range(ATT_H):
        l = ls[h // 2][:, (h % 2) * LANE:(h % 2 + 1) * LANE]
        o_ref[:, h * ATT_DH:(h + 1) * ATT_DH] = jnp.transpose(acc_ref[h] / l).astype(bf16)


def _bias_tiles(rel_bias, nd):
    j = jnp.arange(LANE)[:, None]
    i = jnp.arange(LANE)[None, :]
    rel = jnp.arange(nd)[:, None, None] * LANE + (i - j)[None]
    return jnp.moveaxis(_bias_lookup(rel_bias, rel), -1, 1)


def _dsa_prompt(qa, kab, vt, qit, kiwi, rel_bias, nb, t):
    nqb = t // LANE
    topk = min(TOPK_MAX, t // 4)
    assert REL_MAX_DIST <= LANE + 1
    bias = _bias_tiles(rel_bias, 3)
    bias = (bias - bias[2:3]) * LOG2E
    npair = ATT_H // 2
    bias = bias.reshape(3, npair, 2, LANE, LANE).transpose(0, 1, 3, 2, 4).reshape(3, npair, LANE, 2 * LANE)
    ck = math.gcd(t, DSA_CHUNK)
    nch = t // ck
    vt = vt.reshape(nb * npair * nch, 2 * ATT_DH, ck)
    qrow = lambda c: pl.BlockSpec((LANE, c), lambda b, q: (b * nqb + q, 0))
    seq = lambda c: pl.BlockSpec((t, c), lambda b, q: (b, 0))
    kern = functools.partial(_dsa_prompt_kernel, topk=topk, idx_bits=max(1, (t - 1).bit_length()), nch=nch, ck=ck)
    return pl.pallas_call(
        kern,
        grid=(nb, nqb),
        in_specs=[qrow(512), pl.BlockSpec((None, IDX_DIM, IDX_H * LANE), lambda b, q: (b * nqb + q, 0, 0)),
                  qrow(LANE), seq(512),
                  pl.BlockSpec((npair * nch, 2 * ATT_DH, ck), lambda b, q: (b, 0, 0)), seq(LANE),
                  _const_spec((3, npair, LANE, 2 * LANE))],
        out_specs=qrow(512),
        out_shape=jax.ShapeDtypeStruct((nb * t, 512), bf16),
        scratch_shapes=[pltpu.VMEM((t, LANE), i32), pltpu.VMEM((ATT_H, ATT_DH, LANE), f32),
                        pltpu.VMEM((npair, 2 * ATT_DH, 2 * LANE), bf16),
                        pltpu.VMEM((t // PLANE_ROWS, 32, 8, LANE), i32)],
        compiler_params=_cparams(("parallel", "arbitrary")),
        name="dsa_prompt",
    )(qa, qit, kiwi, kab, vt, kiwi, bias)


def _softplus(x):
    return jnp.maximum(x, 0.0) + jnp.log1p(jnp.exp(-jnp.abs(x)))


def _cumsum_rows(tri, a):
    hi = a.astype(bf16)
    r1 = a - hi.astype(f32)
    mid = r1.astype(bf16)
    lo = (r1 - mid.astype(f32)).astype(bf16)
    return _mm(tri, hi) + _mm(tri, mid) + _mm(tri, lo)


def _odd_mixer_kernel(*refs, q, pos0):
    seq_in, shared, seq_out = refs[:7], refs[7:16], refs[-6:]
    for b in range(seq_in[0].shape[0]):
        _odd_mixer_seq(*(r.at[b] for r in seq_in), *shared, *(r.at[b] for r in seq_out), q=q, pos0=pos0)


def _odd_mixer_seq(u_ref, z_ref, xbc_ref, dt_ref, pp_ref, cp_ref, h0_ref, pw_ref, ps_ref, cw_ref, cb_ref,
                   dtb_ref, alog_ref, dsk_ref, nrm_ref, tri_ref, po_ref, y_ref, h_ref, ubuf, xbuf, ybuf, *, q, pos0):
    c = pl.program_id(1)

    @pl.when(c == 0)
    def _():
        ubuf[0:1, :] = jnp.zeros((1, POOL_DIM), f32)
        ubuf[1:16, :] = pp_ref[...]
        xbuf[0:8 - (CONV_W - 1), :] = jnp.zeros((8 - (CONV_W - 1), CONV_DIM), f32)
        xbuf[8 - (CONV_W - 1):8, :] = cp_ref[...]
        h_ref[...] = h0_ref[...]

    pos = pos0 + c * q + lax.broadcasted_iota(i32, (q, LANE), 0)
    causal = lax.broadcasted_iota(i32, (q, q), 0) >= lax.broadcasted_iota(i32, (q, q), 1)

    u = u_ref[...]
    ubuf[16:16 + q, :] = u
    for g, w in enumerate(POOL_WINDOWS):
        sl = slice(g * POOL_GC, (g + 1) * POOL_GC)
        acc = u[:, sl]
        for k in range(1, w):
            acc = acc + ubuf[16 - k:16 - k + q, sl]
        d = acc / jnp.minimum(pos + 1, w).astype(f32) - u[:, sl]
        po_ref[:, sl] = (_mm(d.astype(bf16), pw_ref[g]) * ps_ref[:, sl]).astype(bf16)
    ubuf[0:16, :] = ubuf[q:q + 16, :]

    xbuf[8:8 + q, :] = xbc_ref[...]
    conv = cb_ref[...]
    for j in range(CONV_W):
        off = 8 - (CONV_W - 1) + j
        conv = conv + xbuf[off:off + q, :] * cw_ref[j:j + 1, :]
    xbuf[0:8, :] = xbuf[q:q + 8, :]
    act = _silu(conv)
    xs = act[:, 0:D_INNER]

    dt = _softplus(dt_ref[...] + dtb_ref[...])
    a = dt * (-jnp.exp(alog_ref[...]))
    cs = _cumsum_rows(tri_ref[...], a)
    cs_t = jnp.transpose(cs)
    dt_t = jnp.transpose(dt)
    cs_last = cs[q - 1:q, :]
    w_end = jnp.exp(cs_last - cs) * dt
    ecs = jnp.exp(cs)
    hpg = SSM_H // SSM_G
    for g in range(SSM_G):
        bm = act[:, D_INNER + g * SSM_N:D_INNER + (g + 1) * SSM_N].astype(bf16)
        cm = act[:, D_INNER + (SSM_G + g) * SSM_N:D_INNER + (SSM_G + g + 1) * SSM_N].astype(bf16)
        cb = _mm_nt(cm, bm)
        for hh in range(hpg):
            h = g * hpg + hh
            psl = slice(h * SSM_P, (h + 1) * SSM_P)
            seg = cs[:, h:h + 1] - cs_t[h:h + 1, :]
            lm = jnp.exp(jnp.where(causal, seg, NEG_BIG))
            sc = cb * lm * dt_t[h:h + 1, :]
            xh = xs[:, psl]
            hs = h_ref[h]
            yh = _mm(sc.astype(bf16), xh.astype(bf16)) + _mm_nt(cm, hs.astype(bf16)) * ecs[:, h:h + 1]
            h_ref[h] = hs * jnp.exp(cs_last[:, h:h + 1]) + _mm_tn((xh * w_end[:, h:h + 1]).astype(bf16), bm)
            ybuf[:, psl] = yh
    y = (ybuf[...] + dsk_ref[...] * xs) * _silu(z_ref[...])
    gw = D_INNER // SSM_G
    for g in range(SSM_G):
        sl = slice(g * gw, (g + 1) * gw)
        yg = y[:, sl]
        y_ref[:, sl] = (yg * lax.rsqrt(jnp.mean(yg * yg, -1, keepdims=True) + EPS) * nrm_ref[:, sl]).astype(bf16)


def _pad_lanes(v):
    return jnp.pad(v.astype(f32), (0, LANE - v.shape[0]))[None, :]


def _odd_mixer(u, z, xbc, dt, pool_prev, conv_prev, h0, layer_in, prm, pos0, prev, layer_out, n_layers):
    pool_w, pool_scale, conv_w, conv_b, dt_bias, a_log, d_skip, ssm_norm = prm
    nb, t, _ = u.shape
    q = math.gcd(t, SSD_CHUNK)
    nc = t // q
    grp = _seq_group(nb)
    row = lambda c: pl.BlockSpec((grp, q, c), lambda b, i: (b, i, 0))
    lead = (nb, t)
    tri = jnp.tril(jnp.ones((q, q), bf16))
    tail = (SSM_H, SSM_P, SSM_N)
    in_specs = [row(POOL_DIM), row(D_INNER), row(CONV_DIM), row(LANE),
                _layer_block((POOL_PAST, POOL_DIM), layer_in, grp),
                _layer_block((CONV_W - 1, CONV_DIM), layer_in, grp), _layer_block(tail, layer_in, grp),
                _const_spec((POOL_GROUPS, POOL_GC, POOL_GC)), _const_spec((1, POOL_DIM)),
                _const_spec((CONV_W, CONV_DIM)), _const_spec((1, CONV_DIM)),
                _const_spec((1, LANE)), _const_spec((1, LANE)),
                _const_spec((1, D_INNER)), _const_spec((1, D_INNER)), _const_spec((q, q))]
    state_shape, extra, extra_specs, aliases = _stacked_out(prev, n_layers, nb, tail, len(in_specs), 2)
    return pl.pallas_call(
        functools.partial(_odd_mixer_kernel, q=q, pos0=pos0),
        grid=(nb // grp, nc),
        in_specs=in_specs + extra_specs,
        out_specs=[row(POOL_DIM), row(D_INNER), _layer_block(tail, layer_out, grp)],
        out_shape=[jax.ShapeDtypeStruct(lead + (POOL_DIM,), bf16), jax.ShapeDtypeStruct(lead + (D_INNER,), bf16),
                   state_shape],
        input_output_aliases=aliases,
        scratch_shapes=[pltpu.VMEM((grp, q + 16, POOL_DIM), f32), pltpu.VMEM((grp, q + 8, CONV_DIM), f32),
                        pltpu.VMEM((grp, q, D_INNER), f32)],
        compiler_params=_cparams(("parallel", "arbitrary")),
        name="odd_mixer_t%d" % t,
    )(u, z, xbc, dt, pool_prev, conv_prev, h0, pool_w.astype(bf16), pool_scale[None, :], conv_w, conv_b[None, :],
      _pad_lanes(dt_bias), _pad_lanes(a_log), jnp.repeat(d_skip, SSM_P)[None, :], ssm_norm[None, :], tri, *extra)


def _dsa_sample_score_kernel(pt_ref, qi_ref, wi_ref, kiwi_ref, *rest, npg):
    pages, o_ref = rest[:npg], rest[npg]
    ts = kiwi_ref.shape[0]
    ki_new = jnp.transpose(jnp.concatenate([kiwi_ref[...], jnp.zeros((PAGE - ts, LANE), f32)], axis=0))[0:IDX_DIM, :]
    ki = jnp.concatenate([p[...] for p in pages] + [ki_new], axis=1).astype(bf16)
    nk = ki.shape[1]
    s = jnp.maximum(_mm(qi_ref[...], ki), 0.0) * (wi_ref[...] * IDX_H ** -0.5)
    acc = jnp.sum(s.reshape(ts, IDX_H, nk), axis=1)
    col = lax.broadcasted_iota(i32, (ts, nk), 1)
    row = lax.broadcasted_iota(i32, (ts, nk), 0)
    o_ref[...] = jnp.where(col <= npg * PAGE + row, _sort_key(acc), INT_MIN)


def _select_kernel(k_ref, o_ref, key_ref, planes_ref, *, topk, idx_bits):
    nk, rows = k_ref.shape[0], key_ref.shape[0]
    key_ref[0:nk, :] = k_ref[...]
    if rows > nk:
        key_ref[nk:rows, :] = jnp.full((rows - nk, LANE), INT_MIN, i32)
    t = _select_topk_planes(key_ref, planes_ref, 1, rows, topk, idx_bits)
    o_ref[...] = jnp.where(key_ref[0:nk, :] >= t, 1.0, 0.0)


def _dsa_sample_attn_kernel(pt_ref, q_ref, kn_ref, vn_ref, sel_ref, bias_ref, spread_ref, *rest, npg):
    kpages, vpages, o_ref = rest[:npg], rest[npg:2 * npg], rest[2 * npg]
    pad = jnp.zeros((PAGE * ATT_H - kn_ref.shape[0], ATT_DH), f32)
    kx = jnp.concatenate([p[...] for p in kpages] + [kn_ref[...], pad], axis=0).astype(bf16)
    vx = jnp.concatenate([p[...] for p in vpages] + [vn_ref[...], pad], axis=0).astype(bf16)
    selx = jnp.concatenate([_mm(sel_ref[:, j * PAGE:(j + 1) * PAGE], spread_ref[...]) for j in range(npg + 1)],
                           axis=1)
    s = _mm_nt(q_ref[...], kx) * ATT_DH ** -0.5 + jnp.where(selx > 0.5, bias_ref[...], NEG_BIG)
    p = jnp.exp(s - jnp.max(s, axis=-1, keepdims=True))
    o_ref[...] = (_mm(p.astype(bf16), vx) / jnp.sum(p, axis=-1, keepdims=True)).astype(bf16)


def _dsa_sample(qa, ka, va, qi, kiwi, cache_k, cache_v, cache_ki, layer, page_table, rel_bias):
    db, ts = qa.shape[:2]
    npg = page_table.shape[1]
    n_past = npg * PAGE
    nk = n_past + PAGE
    nq = db * ts
    topk = min(TOPK_MAX, (n_past + ts) // 4)
    hd = ATT_H * ATT_DH
    ki_page = lambda j: pl.BlockSpec((None, None, IDX_DIM, PAGE), lambda b, pt: (layer, pt[b, j], 0, 0))
    seq = lambda r, c: pl.BlockSpec((None, r, c), lambda b, pt: (b, 0, 0))

    keys = pl.pallas_call(
        functools.partial(_dsa_sample_score_kernel, npg=npg),
        grid_spec=pltpu.PrefetchScalarGridSpec(
            num_scalar_prefetch=1, grid=(db,),
            in_specs=[seq(ts * IDX_H, IDX_DIM), seq(ts * IDX_H, 1), seq(ts, LANE)] + [ki_page(j) for j in range(npg)],
            out_specs=seq(ts, nk)),
        out_shape=jax.ShapeDtypeStruct((db, ts, nk), i32),
        compiler_params=_cparams(("parallel",)),
        name="dsa_sample_score",
    )(page_table, qi.reshape(db, ts * IDX_H, IDX_DIM),
      kiwi[..., IDX_DIM:IDX_DIM + IDX_H].reshape(db, ts * IDX_H, 1), kiwi,
      *([jnp.swapaxes(cache_ki, 2, 3)] * npg))

    col = pl.BlockSpec((nk, LANE), lambda i: (0, i))
    sel_rows = -(-nk // PLANE_ROWS) * PLANE_ROWS
    sel = pl.pallas_call(
        functools.partial(_select_kernel, topk=topk, idx_bits=max(1, (sel_rows - 1).bit_length())),
        grid=(nq // LANE,),
        in_specs=[col],
        out_specs=col,
        out_shape=jax.ShapeDtypeStruct((nk, nq), f32),
        scratch_shapes=[pltpu.VMEM((sel_rows, LANE), i32),
                        pltpu.VMEM((sel_rows // PLANE_ROWS, 32, 8, LANE), i32)],
        compiler_params=_cparams(("parallel",)),
        name="dsa_sample_select",
    )(keys.reshape(nq, nk).T)
    sel = jnp.repeat(sel.T.reshape(db, ts, nk), ATT_H, axis=1).astype(bf16)
    rel = n_past + jnp.arange(ts)[:, None] - jnp.arange(nk)[None, :]
    bias = jnp.repeat(jnp.moveaxis(_bias_lookup(rel_bias, rel), -1, 1), ATT_H, axis=-1)
    same_head = jnp.arange(nk * ATT_H)[None, :] % ATT_H == jnp.arange(ATT_H)[:, None]
    bias = jnp.where(same_head[None], bias, NEG_BIG).reshape(ts * ATT_H, nk * ATT_H)
    spread = (jnp.arange(PAGE * ATT_H)[None, :] // ATT_H == jnp.arange(PAGE)[:, None]).astype(bf16)

    n_pages = cache_k.shape[1]
    rows = PAGE * ATT_H
    kv_page = lambda j: pl.BlockSpec((rows, ATT_DH), lambda b, pt: (layer * n_pages + pt[b, j], 0))
    const = lambda a: pl.BlockSpec(a.shape, lambda b, pt: (0, 0))
    as_rows = lambda a: a.reshape(db, ts * ATT_H, ATT_DH)
    out = pl.pallas_call(
        functools.partial(_dsa_sample_attn_kernel, npg=npg),
        grid_spec=pltpu.PrefetchScalarGridSpec(
            num_scalar_prefetch=1, grid=(db,),
            in_specs=[seq(ts * ATT_H, ATT_DH)] * 3 + [seq(ts * ATT_H, nk), const(bias), const(spread)]
                     + [kv_page(j) for j in range(npg)] * 2,
            out_specs=seq(ts * ATT_H, ATT_DH)),
        out_shape=jax.ShapeDtypeStruct((db, ts * ATT_H, ATT_DH), bf16),
        compiler_params=_cparams(("parallel",)),
        name="dsa_sample_attn",
    )(page_table, as_rows(qa), as_rows(ka), as_rows(va), sel, bias, spread,
      *([cache_k.reshape(-1, ATT_DH)] * npg), *([cache_v.reshape(-1, ATT_DH)] * npg))
    return out.reshape(db, ts, hd)


def kernel(x_prompt, x_sample, cache_k, cache_v, cache_kidx, state_ret, state_pool, state_conv, state_ssm,
           page_table, norm_mix, norm_ffn, w_in_even, w_out_even, q_norm, k_norm, rel_bias,
           w_in_odd, w_out_odd, pool_w, pool_scale, conv_w, conv_b, dt_bias, a_log, d_skip, ssm_norm,
           w_gate, w_up, w_down):
    bp, sp, d = x_prompt.shape
    db, ts, _ = x_sample.shape
    n_p, n_s = bp * sp, db * ts
    assert d == D_MODEL and n_p % TM == 0 and n_s % TM == 0 and sp % LANE == 0 and sp >= POOL_PAST
    n_past = page_table.shape[1] * PAGE
    n_even, n_odd = (DEPTH + 1) // 2, DEPTH // 2
    hd = ATT_H * ATT_DH
    xp, xs = x_prompt.reshape(n_p, d), x_sample.reshape(n_s, d)
    seqs = lambda a: a.reshape(db, ts, a.shape[-1])
    pseqs = lambda a: a.reshape(bp, sp, a.shape[-1])
    last = lambda a, n: a.reshape(bp, sp, a.shape[-1])[:, sp - n:]
    tail = lambda prev, cur, n: jnp.concatenate([prev.astype(f32), cur], axis=1)[:, -n:]
    zeros = lambda *s: jnp.zeros((1, bp) + s, f32)
    outs = [[] for _ in range(14)]
    rstate_p = rstate_s = h_p = h_s = kv_p = kv_s = None
    w_in_e = jnp.pad(w_in_even, ((0, 0), (0, 0), (0, EVEN_PROJ_PAD - EVEN_PROJ))).astype(bf16)
    w_in_o = jnp.pad(w_in_odd, ((0, 0), (0, 0), (0, ODD_PROJ_PAD - ODD_PROJ))).astype(bf16)
    w_out_e, w_out_o = w_out_even.astype(bf16), w_out_odd.astype(bf16)
    w_ffn = (w_gate.astype(bf16), w_up.astype(bf16), w_down.astype(bf16))
    for l in range(DEPTH):
        if l % 2 == 0:
            i = l // 2
            w_out, mix_layer = w_out_e, i
            prm = (norm_mix[l][None], w_in_e, q_norm[i][None], k_norm[i][None])
            ret_p, qa_p, ka_p, kab_p, va_p, kiwi_p, qit_p, vt_p = _even_proj(xp, *prm, i, n_even, kv_p, prompt_len=sp)
            ret_s, qa_s, ka_s, _, va_s, kiwi_s, qi_s = _even_proj(xs, *prm, i, n_even, kv_s)
            kv_p, kv_s = (ka_p, va_p), (ka_s, va_s)
            mix_a_p, rstate_p = _retention(pseqs(ret_p), zeros(RET_H, RET_DK, RET_DV), 0, 0, rstate_p, i, n_even)
            mix_a_s, rstate_s = _retention(seqs(ret_s), state_ret, i, n_past, rstate_s, i, n_even)
            mix_b_p = _dsa_prompt(qa_p, kab_p, vt_p, qit_p, kiwi_p, rel_bias, bp, sp)
            mix_b_s = _dsa_sample(seqs(qa_s), ka_s[i], va_s[i], seqs(qi_s), seqs(kiwi_s),
                                  cache_k, cache_v, cache_kidx, i, page_table, rel_bias)
            new = [None, None, kiwi_p.reshape(bp, sp, LANE)[..., :IDX_DIM], None, None, None, None,
                   None, None, seqs(kiwi_s)[..., :IDX_DIM], None, None, None, None]
        else:
            j = l // 2
            w_out, mix_layer = w_out_o, j
            u_p, z_p, xbc_p, dt_p = _odd_proj(xp, norm_mix[l][None], w_in_o, j)
            u_s, z_s, xbc_s, dt_s = _odd_proj(xs, norm_mix[l][None], w_in_o, j)
            prm = (pool_w[j], pool_scale[j], conv_w[j], conv_b[j], dt_bias[j], a_log[j], d_skip[j], ssm_norm[j])
            mix_a_p, mix_b_p, h_p = _odd_mixer(
                pseqs(u_p), pseqs(z_p), pseqs(xbc_p), pseqs(dt_p), zeros(POOL_PAST, POOL_DIM),
                zeros(CONV_W - 1, CONV_DIM), zeros(SSM_H, SSM_P, SSM_N), 0, prm, 0, h_p, j, n_odd)
            mix_a_s, mix_b_s, h_s = _odd_mixer(
                seqs(u_s), seqs(z_s), seqs(xbc_s), seqs(dt_s), state_pool, state_conv, state_ssm, j,
                prm, n_past, h_s, j, n_odd)
            new = [None, None, None, None, last(u_p, POOL_PAST), last(xbc_p, CONV_W - 1), None,
                   None, None, None, None,
                   tail(state_pool[j], seqs(u_s), POOL_PAST), tail(state_conv[j], seqs(xbc_s), CONV_W - 1), None]
        for acc, leaf in zip(outs, new):
            if leaf is not None:
                acc.append(leaf)
        ffn = (w_out, mix_layer, norm_ffn[l][None], *w_ffn, l)
        xp = _out_ffn(xp, mix_a_p.reshape(n_p, -1), mix_b_p.reshape(n_p, -1), *ffn)
        xs = _out_ffn(xs, mix_a_s.reshape(n_s, -1), mix_b_s.reshape(n_s, -1), *ffn)
    leaves = [jnp.stack(a) if a else None for a in outs]
    leaves[3], leaves[6], leaves[10], leaves[13] = rstate_p, h_p, rstate_s, h_s
    leaves[0], leaves[1] = (a.reshape(n_even, bp, sp, ATT_H, ATT_DH) for a in kv_p)
    leaves[7], leaves[8] = (a.reshape(n_even, db, ts, ATT_H, ATT_DH) for a in kv_s)
    return (xp.reshape(bp, sp, d), xs.reshape(db, ts, d)) + tuple(leaves)
```

```python
import functools
import math

import jax
import jax.numpy as jnp
from jax import lax
from jax.experimental import pallas as pl
from jax.experimental.pallas import tpu as pltpu

f32 = jnp.float32
bf16 = jnp.bfloat16
i32 = jnp.int32

D_MODEL = 1024
DEPTH = 4
PAGE = 128
RET_H, RET_DK, RET_DV, RET_CHUNK = 4, 128, 128, 128
ROPE_BASE = 10000.0
ATT_H, ATT_DH = 4, 128
IDX_H, IDX_DIM = 8, 64
TOPK_MAX = 256
REL_BUCKETS, REL_MAX_DIST = 32, 128
POOL_WINDOWS = (2, 4, 8, 16)
POOL_GROUPS = 4
POOL_DIM = D_MODEL // 2
POOL_GC = POOL_DIM // POOL_GROUPS
POOL_PAST = 15
D_INNER = D_MODEL // 2
SSM_P = 64
SSM_H = D_INNER // SSM_P
SSM_G = 2
SSM_N = 128
CONV_W = 4
CONV_DIM = D_INNER + 2 * SSM_G * SSM_N
SSD_CHUNK = 128
FF_DIM = -(-8 * D_MODEL // (3 * 256)) * 256
EPS = 1e-6

EVEN_PROJ = 4 * 512 + 3 * 512 + 512 + IDX_DIM + IDX_H
EVEN_PROJ_PAD = 4224
ODD_PROJ = POOL_DIM + D_INNER + CONV_DIM + SSM_H
ODD_PROJ_PAD = 2176

LANE = 128
INT_MIN = -(2 ** 31)
NEG_BIG = -1e30
LOG2E = 1.4426950408889634
VMEM_LIMIT = 56 * 1024 * 1024
TM = 512
FF_SPLIT = 2
DSA_CHUNK = 512
SCORE_ROWS = 256
SEQ_GROUP = 8


def _cparams(sem):
    return pltpu.CompilerParams(dimension_semantics=sem, vmem_limit_bytes=VMEM_LIMIT)


def _mm(a, b):
    return jnp.dot(a, b, preferred_element_type=f32)


def _mm_nt(a, b):
    return lax.dot_general(a, b, (((1,), (1,)), ((), ())), preferred_element_type=f32)


def _mm_tn(a, b):
    return lax.dot_general(a, b, (((0,), (0,)), ((), ())), preferred_element_type=f32)


def _rms(x, g):
    return x * lax.rsqrt(jnp.mean(x * x, -1, keepdims=True) + EPS) * g


def _silu(x):
    return x / (1.0 + jnp.exp(-x))


def _const_spec(shape):
    nd = len(shape)
    return pl.BlockSpec(shape, lambda *a: (0,) * nd)


def _even_proj_kernel(x_ref, g_ref, w_ref, qg_ref, kg_ref, *refs, for_prompt, n_aliased):
    ret_ref, qa_ref, ka_ref, kab_ref, va_ref, kiwi_ref, *idx_refs = refs[n_aliased:]
    xb = _rms(x_ref[...], g_ref[...]).astype(bf16)
    qa = _mm(xb, w_ref[:, 2048:2560])
    ka = _mm(xb, w_ref[:, 2560:3072])
    va = _mm(xb, w_ref[:, 3072:3584])
    qi = _mm(xb, w_ref[:, 3584:4096]) * IDX_DIM ** -0.5
    kiwi_ref[...] = _mm(xb, w_ref[:, 4096:4224])
    for h in range(ATT_H):
        sl = slice(h * ATT_DH, (h + 1) * ATT_DH)
        qa_ref[:, sl] = _rms(qa[:, sl], qg_ref[...]).astype(bf16)
        kn = _rms(ka[:, sl], kg_ref[...])
        ka_ref[:, h, :] = kn
        kab_ref[:, sl] = kn.astype(bf16)
        va_ref[:, h, :] = va[:, sl]
    if for_prompt:
        qit_ref, vt_ref = idx_refs
        for blk in range(TM // LANE):
            for h in range(IDX_H):
                part = qi[blk * LANE:(blk + 1) * LANE, h * IDX_DIM:(h + 1) * IDX_DIM]
                qit_ref[blk, :, h * LANE:(h + 1) * LANE] = jnp.transpose(part).astype(bf16)
        for pr in range(ATT_H // 2):
            vt_ref[pr] = jnp.transpose(va[:, pr * 2 * ATT_DH:(pr + 1) * 2 * ATT_DH]).astype(bf16)
    else:
        idx_refs[0][...] = qi.astype(bf16)
    ret_ref[...] = _mm(xb, w_ref[:, 0:2048])


def _even_proj(x, g, w, qg, kg, layer, n_layers, prev_kv, prompt_len=None):
    n = x.shape[0]
    row = lambda c: pl.BlockSpec((TM, c), lambda i: (i, 0))
    heads = pl.BlockSpec((None, TM, ATT_H, ATT_DH), lambda i: (layer, i, 0, 0))
    stacked = jax.ShapeDtypeStruct((n_layers, n, ATT_H, ATT_DH), f32)
    out_specs = [row(2048), row(512), heads, row(512), heads, row(LANE)]
    out_shape = [jax.ShapeDtypeStruct((n, 2048), f32), jax.ShapeDtypeStruct((n, 512), bf16), stacked,
                 jax.ShapeDtypeStruct((n, 512), bf16), stacked, jax.ShapeDtypeStruct((n, LANE), f32)]
    extra = [] if prev_kv is None else list(prev_kv)
    aliases = {} if prev_kv is None else {5: 2, 6: 4}
    if prompt_len is None:
        out_specs.append(row(512))
        out_shape.append(jax.ShapeDtypeStruct((n, 512), bf16))
    else:
        ck = math.gcd(prompt_len, DSA_CHUNK)
        assert ck % TM == 0 and prompt_len % ck == 0
        per_chunk, per_seq, npair = ck // TM, prompt_len // TM, ATT_H // 2
        out_specs += [pl.BlockSpec((TM // LANE, IDX_DIM, IDX_H * LANE), lambda i: (i, 0, 0)),
                      pl.BlockSpec((None, npair, None, 2 * ATT_DH, TM),
                                   lambda i: (i // per_seq, 0, (i % per_seq) // per_chunk, 0, i % per_chunk))]
        out_shape += [jax.ShapeDtypeStruct((n // LANE, IDX_DIM, IDX_H * LANE), bf16),
                      jax.ShapeDtypeStruct((n // prompt_len, npair, prompt_len // ck, 2 * ATT_DH, ck), bf16)]
    return pl.pallas_call(
        functools.partial(_even_proj_kernel, for_prompt=prompt_len is not None, n_aliased=len(extra)),
        grid=(n // TM,),
        in_specs=[row(D_MODEL), _const_spec((1, D_MODEL)), _layer_weight(w, layer),
                  _const_spec((1, ATT_DH)), _const_spec((1, ATT_DH))]
                 + [pl.BlockSpec(memory_space=pl.ANY)] * len(extra),
        out_specs=out_specs,
        out_shape=out_shape,
        input_output_aliases=aliases,
        compiler_params=_cparams(("parallel",)),
        name="even_proj",
    )(x, g, w, qg, kg, *extra)


def _odd_proj_kernel(x_ref, g_ref, w_ref, u_ref, z_ref, xbc_ref, dt_ref):
    xb = _rms(x_ref[...], g_ref[...]).astype(bf16)
    u_ref[...] = _mm(xb, w_ref[:, 0:512])
    z_ref[...] = _mm(xb, w_ref[:, 512:1024])
    xbc_ref[...] = _mm(xb, w_ref[:, 1024:2048])
    dt_ref[...] = _mm(xb, w_ref[:, 2048:2176])


def _odd_proj(x, g, w, layer):
    n = x.shape[0]
    row = lambda c: pl.BlockSpec((TM, c), lambda i: (i, 0))
    outs = [512, 512, 1024, LANE]
    return pl.pallas_call(
        _odd_proj_kernel,
        grid=(n // TM,),
        in_specs=[row(D_MODEL), _const_spec((1, D_MODEL)), _layer_weight(w, layer)],
        out_specs=[row(c) for c in outs],
        out_shape=[jax.ShapeDtypeStruct((n, c), f32) for c in outs],
        compiler_params=_cparams(("parallel",)),
        name="odd_proj",
    )(x, g, w)


def _out_ffn_kernel(x_ref, a_ref, b_ref, wo_ref, g_ref, wg_ref, wu_ref, wd_ref, o_ref):
    half = wo_ref.shape[0] // 2
    x = x_ref[...] + _mm(a_ref[...], wo_ref[0:half, :]) + _mm(b_ref[...], wo_ref[half:, :])
    hb = _rms(x, g_ref[...]).astype(bf16)
    fc = FF_DIM // FF_SPLIT
    ff = None
    for c in range(FF_SPLIT):
        sl = slice(c * fc, (c + 1) * fc)
        act = (_silu(_mm(hb, wg_ref[:, sl])) * _mm(hb, wu_ref[:, sl])).astype(bf16)
        down = _mm(act, wd_ref[sl, :])
        ff = down if ff is None else ff + down
    o_ref[...] = x + ff


def _layer_weight(w, layer, single_buffer=False):
    mode = dict(pipeline_mode=pl.Buffered(1)) if single_buffer else {}
    return pl.BlockSpec((None,) + w.shape[1:], lambda i: (layer,) + (0,) * (w.ndim - 1), **mode)


def _out_ffn(x, a, b, wo, mix_layer, g, wg, wu, wd, layer):
    n = x.shape[0]
    row = lambda c: pl.BlockSpec((TM, c), lambda i: (i, 0))
    return pl.pallas_call(
        _out_ffn_kernel,
        grid=(n // TM,),
        in_specs=[row(D_MODEL), row(a.shape[1]), row(b.shape[1]), _layer_weight(wo, mix_layer, True),
                  _const_spec((1, D_MODEL)), _layer_weight(wg, layer, True), _layer_weight(wu, layer, True),
                  _layer_weight(wd, layer, True)],
        out_specs=row(D_MODEL),
        out_shape=jax.ShapeDtypeStruct((n, D_MODEL), f32),
        compiler_params=_cparams(("parallel",)),
        name="out_ffn",
    )(x, a, b, wo, g, wg, wu, wd)


def _rope_tables(pos):
    half = RET_DK // 2
    inv = ROPE_BASE ** (-jnp.linspace(0.0, 1.0, half, dtype=f32))
    ang = pos.astype(f32)[:, None] * inv[None, :]
    cos, sin = jnp.cos(ang), jnp.sin(ang)
    return jnp.concatenate([cos, cos], -1), jnp.concatenate([-sin, sin], -1)


def _ret_decay(q):
    log_g = jnp.log1p(-jnp.exp2(-5.0 - jnp.arange(RET_H, dtype=f32)))
    idx = jnp.arange(q, dtype=f32)
    diff = idx[:, None] - idx[None, :]
    dmask = jnp.where(diff[None] >= 0, jnp.exp(log_g[:, None, None] * jnp.maximum(diff, 0.0)[None]), 0.0)
    xi = jnp.exp(log_g[:, None] * (idx + 1.0)[None])
    zeta = jnp.exp(log_g[:, None] * (q - 1.0 - idx)[None])
    g_chunk = jnp.exp(log_g * q)
    return dmask, xi, zeta, g_chunk


def _t5_bucket(rel):
    n = jnp.maximum(rel, 0)
    max_exact = REL_BUCKETS // 2
    nf = jnp.maximum(n, 1).astype(f32)
    large = max_exact + (jnp.log(nf / max_exact) / math.log(REL_MAX_DIST / max_exact)
                         * (REL_BUCKETS - max_exact)).astype(i32)
    large = jnp.minimum(large, REL_BUCKETS - 1)
    return jnp.where(n < max_exact, n, large)


def _bias_lookup(rel_bias, rel):
    onehot = jax.nn.one_hot(_t5_bucket(rel), REL_BUCKETS, dtype=f32)
    return jnp.einsum("...b,bh->...h", onehot, rel_bias.astype(f32), precision=lax.Precision.HIGHEST)


def _rotary(x, c, s):
    return x * c + pltpu.roll(x, RET_DK // 2, 1) * s


def _retention_kernel(q_ref, k_ref, v_ref, g_ref, r0_ref, c_ref, s_ref, dm_ref, xi_ref, zt_ref, gc_ref,
                      *rest):
    o_ref, r_ref = rest[-2:]

    @pl.when(pl.program_id(1) == 0)
    def _():
        r_ref[...] = r0_ref[...]

    cos, sin = c_ref[...], s_ref[...]
    for b in range(q_ref.shape[0]):
        for h in range(RET_H):
            sl = slice(h * RET_DK, (h + 1) * RET_DK)
            qr = _rotary(q_ref[b, :, sl], cos, sin).astype(bf16)
            kr = _rotary(k_ref[b, :, sl], cos, sin) * RET_DK ** -0.5
            vb = v_ref[b, :, sl].astype(bf16)
            r = r_ref[b, h]
            s = _mm_nt(qr, kr.astype(bf16)) * dm_ref[h]
            o = _mm(s.astype(bf16), vb) + _mm(qr, r.astype(bf16)) * xi_ref[h]
            r_ref[b, h] = r * gc_ref[h, 0:1, :] + _mm_tn((kr * zt_ref[h]).astype(bf16), vb)
            o = o * lax.rsqrt(jnp.mean(o * o, -1, keepdims=True) + EPS)
            o_ref[b, :, sl] = (_silu(g_ref[b, :, sl]) * o).astype(bf16)


def _seq_group(nb):
    return math.gcd(nb, SEQ_GROUP)


def _layer_block(tail, layer, group):
    return pl.BlockSpec((None, group) + tail, lambda b, c: (layer, b) + (0,) * len(tail))


def _stacked_out(prev, n_layers, nb, tail, n_inputs, out_index):
    shape = jax.ShapeDtypeStruct((n_layers, nb) + tail, f32)
    if prev is None:
        return shape, [], [], {}
    return shape, [prev], [pl.BlockSpec(memory_space=pl.ANY)], {n_inputs: out_index}


def _retention(ret, r0, layer_in, pos0, prev, layer_out, n_layers):
    nb, t, _ = ret.shape
    q = math.gcd(t, RET_CHUNK)
    grp = _seq_group(nb)
    cos, sin = _rope_tables(pos0 + jnp.arange(t))
    dmask, xi, zeta, g_chunk = _ret_decay(q)
    bcast = lambda a: jnp.broadcast_to(a[:, :, None], (RET_H, a.shape[1], LANE))
    col = lambda j: pl.BlockSpec((grp, q, 512), lambda b, c: (b, c, j))
    tab = pl.BlockSpec((q, LANE), lambda b, c: (c, 0))
    tail = (RET_H, RET_DK, RET_DV)
    in_specs = [col(0), col(1), col(2), col(3), _layer_block(tail, layer_in, grp), tab, tab,
                _const_spec((RET_H, q, q)), _const_spec((RET_H, q, LANE)), _const_spec((RET_H, q, LANE)),
                _const_spec((RET_H, 8, LANE))]
    state_shape, extra, extra_specs, aliases = _stacked_out(prev, n_layers, nb, tail, len(in_specs), 1)
    return pl.pallas_call(
        _retention_kernel,
        grid=(nb // grp, t // q),
        in_specs=in_specs + extra_specs,
        out_specs=[col(0), _layer_block(tail, layer_out, grp)],
        out_shape=[jax.ShapeDtypeStruct((nb, t, 512), bf16), state_shape],
        input_output_aliases=aliases,
        compiler_params=_cparams(("parallel", "arbitrary")),
        name="retention_t%d" % t,
    )(ret, ret, ret, ret, r0, cos, sin, dmask, bcast(xi), bcast(zeta),
      jnp.broadcast_to(g_chunk[:, None, None], (RET_H, 8, LANE)), *extra)


def _sort_key(score):
    bits = lax.bitcast_convert_type(score, i32)
    bits = jnp.where(bits == INT_MIN, 0, bits)
    return jnp.where(bits < 0, bits ^ 0x7FFFFFFF, bits)


def _count(key_ref, nchunk, ck, pred):
    def body(c, acc):
        r0 = pl.multiple_of(c * ck, ck)
        hit = jnp.where(pred(key_ref[pl.ds(r0, ck), :], r0), 1, 0).astype(i32)
        return acc + jnp.sum(hit.reshape(ck // 8, 8, LANE), axis=0)
    acc = lax.fori_loop(0, nchunk, body, jnp.zeros((8, LANE), i32))
    return jnp.sum(acc, axis=0, keepdims=True)


def _cut_ties(key_ref, nchunk, ck, t, surplus, keep, idx_bits):
    rows = lax.broadcasted_iota(i32, (ck, LANE), 0)

    @pl.when(jnp.max(jnp.where(surplus, 1, 0)) > 0)
    def _():
        want = jnp.where(surplus, keep, jnp.int32(2 ** 30))

        def idx_step(it, x):
            cand = x + jnp.left_shift(jnp.int32(1), idx_bits - 1 - it)
            cnt = _count(key_ref, nchunk, ck, lambda blk, r0: jnp.where(blk == t, rows + r0, cand) < cand)
            return jnp.where(cnt < want, cand, x)

        last = lax.fori_loop(0, idx_bits, idx_step, jnp.zeros((1, LANE), i32))

        def demote(c, carry):
            r0 = pl.multiple_of(c * ck, ck)
            blk = key_ref[pl.ds(r0, ck), :]
            drop = jnp.where(blk == t, rows + r0, last) > last
            key_ref[pl.ds(r0, ck), :] = jnp.where(drop, INT_MIN, blk)
            return carry

        lax.fori_loop(0, nchunk, demote, 0)


PLANE_ROWS = 256
_SWAP_STEPS = ((16, 0x0000FFFF), (8, 0x00FF00FF), (4, 0x0F0F0F0F), (2, 0x33333333), (1, 0x55555555))


def _bit_planes(words):
    a = list(words)
    for j, m in _SWAP_STEPS:
        for k in range(32):
            if k & j == 0:
                t = ((a[k] >> j) ^ a[k + j]) & m
                a[k] = a[k] ^ (t << j)
                a[k + j] = a[k + j] ^ t
    return a


def _select_topk_planes(key_ref, planes_ref, nchunk, ck, topk, idx_bits):
    u32 = jnp.uint32
    per_chunk = ck // PLANE_ROWS
    ngroups = nchunk * per_chunk

    def to_planes(g, carry):
        r0 = pl.multiple_of(g * PLANE_ROWS, PLANE_ROWS)
        words = [lax.bitcast_convert_type(key_ref[pl.ds(r0 + 8 * i, 8), :], u32) for i in range(32)]
        planes = _bit_planes(words)
        planes[31] = ~planes[31]
        for b in range(32):
            planes_ref[g, b] = lax.bitcast_convert_type(planes[b], i32)
        return carry

    lax.fori_loop(0, ngroups, to_planes, 0)

    gmax = planes_ref.shape[0]
    full = jnp.full((8, LANE), 0xFFFFFFFF, u32)
    none = jnp.zeros((8, LANE), u32)

    def radix_select(ng):
        def run():
            alive0 = tuple(jnp.where(g < ngroups, full, none) for g in range(ng))

            def bit_step(it, carry):
                t, need, alive = carry
                b = 31 - it
                ones = [a & lax.bitcast_convert_type(planes_ref[g, b], u32) for g, a in enumerate(alive)]
                cnt = functools.reduce(lambda x, y: x + y, [lax.population_count(o) for o in ones])
                c = jnp.sum(cnt.astype(i32), axis=0, keepdims=True)
                take = c >= need
                t = t | jnp.where(take, jnp.left_shift(jnp.int32(1), b), 0)
                need = jnp.where(take, need, need - c)
                return t, need, tuple(jnp.where(take, o, a ^ o) for o, a in zip(ones, alive))

            init = (jnp.zeros((1, LANE), i32), jnp.full((1, LANE), topk, i32), alive0)
            t, need, alive = lax.fori_loop(0, 32, bit_step, init)
            equal = functools.reduce(lambda x, y: x + y, [lax.population_count(a) for a in alive])
            return t, need, jnp.sum(equal.astype(i32), axis=0, keepdims=True)
        return run

    sizes = sorted({max(1, gmax // 4), max(1, gmax // 2), gmax})
    if isinstance(ngroups, int):
        t, need, n_equal = radix_select(min(s for s in sizes if s >= ngroups))()
    else:
        which = sum((ngroups > s).astype(i32) for s in sizes[:-1]) if len(sizes) > 1 else jnp.int32(0)
        t, need, n_equal = lax.switch(which, [radix_select(s) for s in sizes])
    t = t ^ INT_MIN
    surplus = (n_equal > need) & (t > INT_MIN)
    t = jnp.maximum(t, INT_MIN + 1)
    _cut_ties(key_ref, nchunk, ck, t, surplus, need, idx_bits)
    return t


def _dsa_prompt_kernel(qa_ref, qit_ref, kiwiq_ref, k_ref, vt_ref, kiwik_ref, bias_ref, o_ref, key_ref, acc_ref,
                       qbd_ref, planes_ref, *, topk, idx_bits, nch, ck):
    qb = pl.program_id(1)
    cb = ck // LANE
    nchunk = (qb + cb) // cb
    wit = jnp.transpose(kiwiq_ref[...])[IDX_DIM:IDX_DIM + IDX_H, :] * IDX_H ** -0.5
    qpos = qb * LANE + lax.broadcasted_iota(i32, (SCORE_ROWS, LANE), 1)
    rows = lax.broadcasted_iota(i32, (SCORE_ROWS, LANE), 0)

    def score_chunk(c, carry):
        for sub in range(ck // SCORE_ROWS):
            r0 = pl.multiple_of(c * ck, ck) + sub * SCORE_ROWS
            kic = kiwik_ref[pl.ds(r0, SCORE_ROWS), 0:IDX_DIM].astype(bf16)
            acc = jnp.zeros((SCORE_ROWS, LANE), f32)
            for h in range(0, IDX_H, 2):
                s = _mm(kic, qit_ref[:, h * LANE:(h + 2) * LANE])
                acc = acc + jnp.maximum(s[:, :LANE], 0.0) * wit[h:h + 1, :]
                acc = acc + jnp.maximum(s[:, LANE:], 0.0) * wit[h + 1:h + 2, :]
            valid = rows + r0 <= qpos
            key_ref[pl.ds(r0, SCORE_ROWS), :] = jnp.where(valid, _sort_key(acc), INT_MIN)
        return carry

    lax.fori_loop(0, nchunk, score_chunk, 0)
    t = _select_topk_planes(key_ref, planes_ref, nchunk, ck, topk, idx_bits)

    npair = ATT_H // 2
    @pl.when(qb == 0)
    def _():
        qbd_ref[...] = jnp.zeros_like(qbd_ref)

    for pr in range(npair):
        for hh in range(2):
            sl = slice((2 * pr + hh) * ATT_DH, (2 * pr + hh + 1) * ATT_DH)
            qbd_ref[pr, hh * ATT_DH:(hh + 1) * ATT_DH, hh * LANE:(hh + 1) * LANE] = (
                jnp.transpose(qa_ref[:, sl].astype(f32)).astype(bf16))
    acc_ref[...] = jnp.zeros_like(acc_ref)
    nfar = jnp.maximum(qb - 1, 0) // cb

    pairs = range(npair)

    def att_chunks(near, unroll, first=0):
        def body(i, carry):
            ms, ls = list(carry[0]), list(carry[1])
            cs = [first + i * unroll + u for u in range(unroll)]
            r0s = [pl.multiple_of(c * ck, ck) for c in cs]
            qks = [[_mm(k_ref[pl.ds(r0, ck), pr * 2 * ATT_DH:(pr + 1) * 2 * ATT_DH], qbd_ref[pr]) for pr in pairs]
                   for r0 in r0s]
            for c, r0, qk in zip(cs, r0s, qks):
                neg1 = jnp.where(key_ref[pl.ds(r0, ck), :] >= t, 0.0, NEG_BIG)
                negm = jnp.concatenate([neg1, neg1], axis=1)
                alphas, ps = [], []
                for pr in pairs:
                    s = qk[pr] * (ATT_DH ** -0.5 * LOG2E) + negm
                    if near:
                        s = s + jnp.concatenate(
                            [bias_ref[jnp.clip(qb - (c * cb + j), 0, 2), pr] for j in range(cb)], axis=0)
                    m_new = jnp.maximum(ms[pr], jnp.max(s, axis=0, keepdims=True))
                    alpha = jnp.exp2(ms[pr] - m_new)
                    p = jnp.exp2(s - m_new)
                    ls[pr] = ls[pr] * alpha + jnp.sum(p, axis=0, keepdims=True)
                    ms[pr] = m_new
                    alphas.append(alpha)
                    ps.append(p.astype(bf16))
                pvs = [_mm(vt_ref[pr * nch + c], ps[pr]) for pr in pairs]
                for pr in pairs:
                    for hh in range(2):
                        d = slice(hh * LANE, (hh + 1) * LANE)
                        acc_ref[2 * pr + hh] = acc_ref[2 * pr + hh] * alphas[pr][:, d] + pvs[pr][d, d]
            return tuple(ms), tuple(ls)
        return body

    carry = ((jnp.full((1, 2 * LANE), NEG_BIG, f32),) * npair, (jnp.zeros((1, 2 * LANE), f32),) * npair)
    carry = lax.fori_loop(0, nfar // 2, att_chunks(False, 2), carry)
    tail = nfar // 2 * 2
    tail_pairs = (nchunk - tail) // 2
    carry = lax.fori_loop(0, tail_pairs, att_chunks(True, 2, tail), carry)
    _, ls = lax.fori_loop(tail + 2 * tail_pairs, nchunk, att_chunks(True, 1), carry)
    for h in range(ATT_H):
        l = ls[h // 2][:, (h % 2) * LANE:(h % 2 + 1) * LANE]
        o_ref[:, h * ATT_DH:(h + 1) * ATT_DH] = jnp.transpose(acc_ref[h] / l).astype(bf16)


def _bias_tiles(rel_bias, nd):
    j = jnp.arange(LANE)[:, None]
    i = jnp.arange(LANE)[None, :]
    rel = jnp.arange(nd)[:, None, None] * LANE + (i - j)[None]
    return jnp.moveaxis(_bias_lookup(rel_bias, rel), -1, 1)


def _dsa_prompt(qa, kab, vt, qit, kiwi, rel_bias, nb, t):
    nqb = t // LANE
    topk = min(TOPK_MAX, t // 4)
    assert REL_MAX_DIST <= LANE + 1
    bias = _bias_tiles(rel_bias, 3)
    bias = (bias - bias[2:3]) * LOG2E
    npair = ATT_H // 2
    bias = bias.reshape(3, npair, 2, LANE, LANE).transpose(0, 1, 3, 2, 4).reshape(3, npair, LANE, 2 * LANE)
    ck = math.gcd(t, DSA_CHUNK)
    nch = t // ck
    vt = vt.reshape(nb * npair * nch, 2 * ATT_DH, ck)
    qrow = lambda c: pl.BlockSpec((LANE, c), lambda b, q: (b * nqb + q, 0))
    seq = lambda c: pl.BlockSpec((t, c), lambda b, q: (b, 0))
    kern = functools.partial(_dsa_prompt_kernel, topk=topk, idx_bits=max(1, (t - 1).bit_length()), nch=nch, ck=ck)
    return pl.pallas_call(
        kern,
        grid=(nb, nqb),
        in_specs=[qrow(512), pl.BlockSpec((None, IDX_DIM, IDX_H * LANE), lambda b, q: (b * nqb + q, 0, 0)),
                  qrow(LANE), seq(512),
                  pl.BlockSpec((npair * nch, 2 * ATT_DH, ck), lambda b, q: (b, 0, 0)), seq(LANE),
                  _const_spec((3, npair, LANE, 2 * LANE))],
        out_specs=qrow(512),
        out_shape=jax.ShapeDtypeStruct((nb * t, 512), bf16),
        scratch_shapes=[pltpu.VMEM((t, LANE), i32), pltpu.VMEM((ATT_H, ATT_DH, LANE), f32),
                        pltpu.VMEM((npair, 2 * ATT_DH, 2 * LANE), bf16),
                        pltpu.VMEM((t // PLANE_ROWS, 32, 8, LANE), i32)],
        compiler_params=_cparams(("parallel", "arbitrary")),
        name="dsa_prompt",
    )(qa, qit, kiwi, kab, vt, kiwi, bias)


def _softplus(x):
    return jnp.maximum(x, 0.0) + jnp.log1p(jnp.exp(-jnp.abs(x)))


def _cumsum_rows(tri, a):
    hi = a.astype(bf16)
    r1 = a - hi.astype(f32)
    mid = r1.astype(bf16)
    lo = (r1 - mid.astype(f32)).astype(bf16)
    return _mm(tri, hi) + _mm(tri, mid) + _mm(tri, lo)


def _odd_mixer_kernel(*refs, q, pos0):
    seq_in, shared, seq_out = refs[:7], refs[7:16], refs[-6:]
    for b in range(seq_in[0].shape[0]):
        _odd_mixer_seq(*(r.at[b] for r in seq_in), *shared, *(r.at[b] for r in seq_out), q=q, pos0=pos0)


def _odd_mixer_seq(u_ref, z_ref, xbc_ref, dt_ref, pp_ref, cp_ref, h0_ref, pw_ref, ps_ref, cw_ref, cb_ref,
                   dtb_ref, alog_ref, dsk_ref, nrm_ref, tri_ref, po_ref, y_ref, h_ref, ubuf, xbuf, ybuf, *, q, pos0):
    c = pl.program_id(1)

    @pl.when(c == 0)
    def _():
        ubuf[0:1, :] = jnp.zeros((1, POOL_DIM), f32)
        ubuf[1:16, :] = pp_ref[...]
        xbuf[0:8 - (CONV_W - 1), :] = jnp.zeros((8 - (CONV_W - 1), CONV_DIM), f32)
        xbuf[8 - (CONV_W - 1):8, :] = cp_ref[...]
        h_ref[...] = h0_ref[...]

    pos = pos0 + c * q + lax.broadcasted_iota(i32, (q, LANE), 0)
    causal = lax.broadcasted_iota(i32, (q, q), 0) >= lax.broadcasted_iota(i32, (q, q), 1)

    u = u_ref[...]
    ubuf[16:16 + q, :] = u
    for g, w in enumerate(POOL_WINDOWS):
        sl = slice(g * POOL_GC, (g + 1) * POOL_GC)
        acc = u[:, sl]
        for k in range(1, w):
            acc = acc + ubuf[16 - k:16 - k + q, sl]
        d = acc / jnp.minimum(pos + 1, w).astype(f32) - u[:, sl]
        po_ref[:, sl] = (_mm(d.astype(bf16), pw_ref[g]) * ps_ref[:, sl]).astype(bf16)
    ubuf[0:16, :] = ubuf[q:q + 16, :]

    xbuf[8:8 + q, :] = xbc_ref[...]
    conv = cb_ref[...]
    for j in range(CONV_W):
        off = 8 - (CONV_W - 1) + j
        conv = conv + xbuf[off:off + q, :] * cw_ref[j:j + 1, :]
    xbuf[0:8, :] = xbuf[q:q + 8, :]
    act = _silu(conv)
    xs = act[:, 0:D_INNER]

    dt = _softplus(dt_ref[...] + dtb_ref[...])
    a = dt * (-jnp.exp(alog_ref[...]))
    cs = _cumsum_rows(tri_ref[...], a)
    cs_t = jnp.transpose(cs)
    dt_t = jnp.transpose(dt)
    cs_last = cs[q - 1:q, :]
    w_end = jnp.exp(cs_last - cs) * dt
    ecs = jnp.exp(cs)
    hpg = SSM_H // SSM_G
    for g in range(SSM_G):
        bm = act[:, D_INNER + g * SSM_N:D_INNER + (g + 1) * SSM_N].astype(bf16)
        cm = act[:, D_INNER + (SSM_G + g) * SSM_N:D_INNER + (SSM_G + g + 1) * SSM_N].astype(bf16)
        cb = _mm_nt(cm, bm)
        for hh in range(hpg):
            h = g * hpg + hh
            psl = slice(h * SSM_P, (h + 1) * SSM_P)
            seg = cs[:, h:h + 1] - cs_t[h:h + 1, :]
            lm = jnp.exp(jnp.where(causal, seg, NEG_BIG))
            sc = cb * lm * dt_t[h:h + 1, :]
            xh = xs[:, psl]
            hs = h_ref[h]
            yh = _mm(sc.astype(bf16), xh.astype(bf16)) + _mm_nt(cm, hs.astype(bf16)) * ecs[:, h:h + 1]
            h_ref[h] = hs * jnp.exp(cs_last[:, h:h + 1]) + _mm_tn((xh * w_end[:, h:h + 1]).astype(bf16), bm)
            ybuf[:, psl] = yh
    y = (ybuf[...] + dsk_ref[...] * xs) * _silu(z_ref[...])
    gw = D_INNER // SSM_G
    for g in range(SSM_G):
        sl = slice(g * gw, (g + 1) * gw)
        yg = y[:, sl]
        y_ref[:, sl] = (yg * lax.rsqrt(jnp.mean(yg * yg, -1, keepdims=True) + EPS) * nrm_ref[:, sl]).astype(bf16)


def _pad_lanes(v):
    return jnp.pad(v.astype(f32), (0, LANE - v.shape[0]))[None, :]


def _odd_mixer(u, z, xbc, dt, pool_prev, conv_prev, h0, layer_in, prm, pos0, prev, layer_out, n_layers):
    pool_w, pool_scale, conv_w, conv_b, dt_bias, a_log, d_skip, ssm_norm = prm
    nb, t, _ = u.shape
    q = math.gcd(t, SSD_CHUNK)
    nc = t // q
    grp = _seq_group(nb)
    row = lambda c: pl.BlockSpec((grp, q, c), lambda b, i: (b, i, 0))
    lead = (nb, t)
    tri = jnp.tril(jnp.ones((q, q), bf16))
    tail = (SSM_H, SSM_P, SSM_N)
    in_specs = [row(POOL_DIM), row(D_INNER), row(CONV_DIM), row(LANE),
                _layer_block((POOL_PAST, POOL_DIM), layer_in, grp),
                _layer_block((CONV_W - 1, CONV_DIM), layer_in, grp), _layer_block(tail, layer_in, grp),
                _const_spec((POOL_GROUPS, POOL_GC, POOL_GC)), _const_spec((1, POOL_DIM)),
                _const_spec((CONV_W, CONV_DIM)), _const_spec((1, CONV_DIM)),
                _const_spec((1, LANE)), _const_spec((1, LANE)),
                _const_spec((1, D_INNER)), _const_spec((1, D_INNER)), _const_spec((q, q))]
    state_shape, extra, extra_specs, aliases = _stacked_out(prev, n_layers, nb, tail, len(in_specs), 2)
    return pl.pallas_call(
        functools.partial(_odd_mixer_kernel, q=q, pos0=pos0),
        grid=(nb // grp, nc),
        in_specs=in_specs + extra_specs,
        out_specs=[row(POOL_DIM), row(D_INNER), _layer_block(tail, layer_out, grp)],
        out_shape=[jax.ShapeDtypeStruct(lead + (POOL_DIM,), bf16), jax.ShapeDtypeStruct(lead + (D_INNER,), bf16),
                   state_shape],
        input_output_aliases=aliases,
        scratch_shapes=[pltpu.VMEM((grp, q + 16, POOL_DIM), f32), pltpu.VMEM((grp, q + 8, CONV_DIM), f32),
                        pltpu.VMEM((grp, q, D_INNER), f32)],
        compiler_params=_cparams(("parallel", "arbitrary")),
        name="odd_mixer_t%d" % t,
    )(u, z, xbc, dt, pool_prev, conv_prev, h0, pool_w.astype(bf16), pool_scale[None, :], conv_w, conv_b[None, :],
      _pad_lanes(dt_bias), _pad_lanes(a_log), jnp.repeat(d_skip, SSM_P)[None, :], ssm_norm[None, :], tri, *extra)


def _dsa_sample_score_kernel(pt_ref, qi_ref, wi_ref, kiwi_ref, *rest, npg):
    pages, o_ref = rest[:npg], rest[npg]
    ts = kiwi_ref.shape[0]
    ki_new = jnp.transpose(jnp.concatenate([kiwi_ref[...], jnp.zeros((PAGE - ts, LANE), f32)], axis=0))[0:IDX_DIM, :]
    ki = jnp.concatenate([p[...] for p in pages] + [ki_new], axis=1).astype(bf16)
    nk = ki.shape[1]
    s = jnp.maximum(_mm(qi_ref[...], ki), 0.0) * (wi_ref[...] * IDX_H ** -0.5)
    acc = jnp.sum(s.reshape(ts, IDX_H, nk), axis=1)
    col = lax.broadcasted_iota(i32, (ts, nk), 1)
    row = lax.broadcasted_iota(i32, (ts, nk), 0)
    o_ref[...] = jnp.where(col <= npg * PAGE + row, _sort_key(acc), INT_MIN)


def _select_kernel(k_ref, o_ref, key_ref, planes_ref, *, topk, idx_bits):
    nk, rows = k_ref.shape[0], key_ref.shape[0]
    key_ref[0:nk, :] = k_ref[...]
    if rows > nk:
        key_ref[nk:rows, :] = jnp.full((rows - nk, LANE), INT_MIN, i32)
    t = _select_topk_planes(key_ref, planes_ref, 1, rows, topk, idx_bits)
    o_ref[...] = jnp.where(key_ref[0:nk, :] >= t, 1.0, 0.0)


def _dsa_sample_attn_kernel(pt_ref, q_ref, kn_ref, vn_ref, sel_ref, bias_ref, spread_ref, *rest, npg):
    kpages, vpages, o_ref = rest[:npg], rest[npg:2 * npg], rest[2 * npg]
    pad = jnp.zeros((PAGE * ATT_H - kn_ref.shape[0], ATT_DH), f32)
    kx = jnp.concatenate([p[...] for p in kpages] + [kn_ref[...], pad], axis=0).astype(bf16)
    vx = jnp.concatenate([p[...] for p in vpages] + [vn_ref[...], pad], axis=0).astype(bf16)
    selx = jnp.concatenate([_mm(sel_ref[:, j * PAGE:(j + 1) * PAGE], spread_ref[...]) for j in range(npg + 1)],
                           axis=1)
    s = _mm_nt(q_ref[...], kx) * ATT_DH ** -0.5 + jnp.where(selx > 0.5, bias_ref[...], NEG_BIG)
    p = jnp.exp(s - jnp.max(s, axis=-1, keepdims=True))
    o_ref[...] = (_mm(p.astype(bf16), vx) / jnp.sum(p, axis=-1, keepdims=True)).astype(bf16)


def _dsa_sample(qa, ka, va, qi, kiwi, cache_k, cache_v, cache_ki, layer, page_table, rel_bias):
    db, ts = qa.shape[:2]
    npg = page_table.shape[1]
    n_past = npg * PAGE
    nk = n_past + PAGE
    nq = db * ts
    topk = min(TOPK_MAX, (n_past + ts) // 4)
    hd = ATT_H * ATT_DH
    ki_page = lambda j: pl.BlockSpec((None, None, IDX_DIM, PAGE), lambda b, pt: (layer, pt[b, j], 0, 0))
    seq = lambda r, c: pl.BlockSpec((None, r, c), lambda b, pt: (b, 0, 0))

    keys = pl.pallas_call(
        functools.partial(_dsa_sample_score_kernel, npg=npg),
        grid_spec=pltpu.PrefetchScalarGridSpec(
            num_scalar_prefetch=1, grid=(db,),
            in_specs=[seq(ts * IDX_H, IDX_DIM), seq(ts * IDX_H, 1), seq(ts, LANE)] + [ki_page(j) for j in range(npg)],
            out_specs=seq(ts, nk)),
        out_shape=jax.ShapeDtypeStruct((db, ts, nk), i32),
        compiler_params=_cparams(("parallel",)),
        name="dsa_sample_score",
    )(page_table, qi.reshape(db, ts * IDX_H, IDX_DIM),
      kiwi[..., IDX_DIM:IDX_DIM + IDX_H].reshape(db, ts * IDX_H, 1), kiwi,
      *([jnp.swapaxes(cache_ki, 2, 3)] * npg))

    col = pl.BlockSpec((nk, LANE), lambda i: (0, i))
    sel_rows = -(-nk // PLANE_ROWS) * PLANE_ROWS
    sel = pl.pallas_call(
        functools.partial(_select_kernel, topk=topk, idx_bits=max(1, (sel_rows - 1).bit_length())),
        grid=(nq // LANE,),
        in_specs=[col],
        out_specs=col,
        out_shape=jax.ShapeDtypeStruct((nk, nq), f32),
        scratch_shapes=[pltpu.VMEM((sel_rows, LANE), i32),
                        pltpu.VMEM((sel_rows // PLANE_ROWS, 32, 8, LANE), i32)],
        compiler_params=_cparams(("parallel",)),
        name="dsa_sample_select",
    )(keys.reshape(nq, nk).T)
    sel = jnp.repeat(sel.T.reshape(db, ts, nk), ATT_H, axis=1).astype(bf16)
    rel = n_past + jnp.arange(ts)[:, None] - jnp.arange(nk)[None, :]
    bias = jnp.repeat(jnp.moveaxis(_bias_lookup(rel_bias, rel), -1, 1), ATT_H, axis=-1)
    same_head = jnp.arange(nk * ATT_H)[None, :] % ATT_H == jnp.arange(ATT_H)[:, None]
    bias = jnp.where(same_head[None], bias, NEG_BIG).reshape(ts * ATT_H, nk * ATT_H)
    spread = (jnp.arange(PAGE * ATT_H)[None, :] // ATT_H == jnp.arange(PAGE)[:, None]).astype(bf16)

    n_pages = cache_k.shape[1]
    rows = PAGE * ATT_H
    kv_page = lambda j: pl.BlockSpec((rows, ATT_DH), lambda b, pt: (layer * n_pages + pt[b, j], 0))
    const = lambda a: pl.BlockSpec(a.shape, lambda b, pt: (0, 0))
    as_rows = lambda a: a.reshape(db, ts * ATT_H, ATT_DH)
    out = pl.pallas_call(
        functools.partial(_dsa_sample_attn_kernel, npg=npg),
        grid_spec=pltpu.PrefetchScalarGridSpec(
            num_scalar_prefetch=1, grid=(db,),
            in_specs=[seq(ts * ATT_H, ATT_DH)] * 3 + [seq(ts * ATT_H, nk), const(bias), const(spread)]
                     + [kv_page(j) for j in range(npg)] * 2,
            out_specs=seq(ts * ATT_H, ATT_DH)),
        out_shape=jax.ShapeDtypeStruct((db, ts * ATT_H, ATT_DH), bf16),
        compiler_params=_cparams(("parallel",)),
        name="dsa_sample_attn",
    )(page_table, as_rows(qa), as_rows(ka), as_rows(va), sel, bias, spread,
      *([cache_k.reshape(-1, ATT_DH)] * npg), *([cache_v.reshape(-1, ATT_DH)] * npg))
    return out.reshape(db, ts, hd)


def kernel(x_prompt, x_sample, cache_k, cache_v, cache_kidx, state_ret, state_pool, state_conv, state_ssm,
           page_table, norm_mix, norm_ffn, w_in_even, w_out_even, q_norm, k_norm, rel_bias,
           w_in_odd, w_out_odd, pool_w, pool_scale, conv_w, conv_b, dt_bias, a_log, d_skip, ssm_norm,
           w_gate, w_up, w_down):
    bp, sp, d = x_prompt.shape
    db, ts, _ = x_sample.shape
    n_p, n_s = bp * sp, db * ts
    assert d == D_MODEL and n_p % TM == 0 and n_s % TM == 0 and sp % LANE == 0 and sp >= POOL_PAST
    n_past = page_table.shape[1] * PAGE
    n_even, n_odd = (DEPTH + 1) // 2, DEPTH // 2
    xp, xs = x_prompt.reshape(n_p, d), x_sample.reshape(n_s, d)
    seqs = lambda a: a.reshape(db, ts, a.shape[-1])
    pseqs = lambda a: a.reshape(bp, sp, a.shape[-1])
    last = lambda a, n: a.reshape(bp, sp, a.shape[-1])[:, sp - n:]
    tail = lambda prev, cur, n: jnp.concatenate([prev.astype(f32), cur], axis=1)[:, -n:]
    zeros = lambda *s: jnp.zeros((1, bp) + s, f32)
    outs = [[] for _ in range(14)]
    rstate_p = rstate_s = h_p = h_s = kv_p = kv_s = None
    w_in_e = jnp.pad(w_in_even, ((0, 0), (0, 0), (0, EVEN_PROJ_PAD - EVEN_PROJ))).astype(bf16)
    w_in_o = jnp.pad(w_in_odd, ((0, 0), (0, 0), (0, ODD_PROJ_PAD - ODD_PROJ))).astype(bf16)
    w_out_e, w_out_o = w_out_even.astype(bf16), w_out_odd.astype(bf16)
    w_ffn = (w_gate.astype(bf16), w_up.astype(bf16), w_down.astype(bf16))
    for l in range(DEPTH):
        if l % 2 == 0:
            i = l // 2
            w_out, mix_layer = w_out_e, i
            prm = (norm_mix[l][None], w_in_e, q_norm[i][None], k_norm[i][None])
            ret_p, qa_p, ka_p, kab_p, va_p, kiwi_p, qit_p, vt_p = _even_proj(xp, *prm, i, n_even, kv_p, prompt_len=sp)
            ret_s, qa_s, ka_s, _, va_s, kiwi_s, qi_s = _even_proj(xs, *prm, i, n_even, kv_s)
            kv_p, kv_s = (ka_p, va_p), (ka_s, va_s)
            mix_a_p, rstate_p = _retention(pseqs(ret_p), zeros(RET_H, RET_DK, RET_DV), 0, 0, rstate_p, i, n_even)
            mix_a_s, rstate_s = _retention(seqs(ret_s), state_ret, i, n_past, rstate_s, i, n_even)
            mix_b_p = _dsa_prompt(qa_p, kab_p, vt_p, qit_p, kiwi_p, rel_bias, bp, sp)
            mix_b_s = _dsa_sample(seqs(qa_s), ka_s[i], va_s[i], seqs(qi_s), seqs(kiwi_s),
                                  cache_k, cache_v, cache_kidx, i, page_table, rel_bias)
            new = [None, None, kiwi_p.reshape(bp, sp, LANE)[..., :IDX_DIM], None, None, None, None,
                   None, None, seqs(kiwi_s)[..., :IDX_DIM], None, None, None, None]
        else:
            j = l // 2
            w_out, mix_layer = w_out_o, j
            u_p, z_p, xbc_p, dt_p = _odd_proj(xp, norm_mix[l][None], w_in_o, j)
            u_s, z_s, xbc_s, dt_s = _odd_proj(xs, norm_mix[l][None], w_in_o, j)
            prm = (pool_w[j], pool_scale[j], conv_w[j], conv_b[j], dt_bias[j], a_log[j], d_skip[j], ssm_norm[j])
            mix_a_p, mix_b_p, h_p = _odd_mixer(
                pseqs(u_p), pseqs(z_p), pseqs(xbc_p), pseqs(dt_p), zeros(POOL_PAST, POOL_DIM),
                zeros(CONV_W - 1, CONV_DIM), zeros(SSM_H, SSM_P, SSM_N), 0, prm, 0, h_p, j, n_odd)
            mix_a_s, mix_b_s, h_s = _odd_mixer(
                seqs(u_s), seqs(z_s), seqs(xbc_s), seqs(dt_s), state_pool, state_conv, state_ssm, j,
                prm, n_past, h_s, j, n_odd)
            new = [None, None, None, None, last(u_p, POOL_PAST), last(xbc_p, CONV_W - 1), None,
                   None, None, None, None,
                   tail(state_pool[j], seqs(u_s), POOL_PAST), tail(state_conv[j], seqs(xbc_s), CONV_W - 1), None]
        for acc, leaf in zip(outs, new):
            if leaf is not None:
                acc.append(leaf)
        ffn = (w_out, mix_layer, norm_ffn[l][None], *w_ffn, l)
        xp = _out_ffn(xp, mix_a_p.reshape(n_p, -1), mix_b_p.reshape(n_p, -1), *ffn)
        xs = _out_ffn(xs, mix_a_s.reshape(n_s, -1), mix_b_s.reshape(n_s, -1), *ffn)
    leaves = [jnp.stack(a) if a else None for a in outs]
    leaves[3], leaves[6], leaves[10], leaves[13] = rstate_p, h_p, rstate_s, h_s
    leaves[0], leaves[1] = (a.reshape(n_even, bp, sp, ATT_H, ATT_DH) for a in kv_p)
    leaves[7], leaves[8] = (a.reshape(n_even, db, ts, ATT_H, ATT_DH) for a in kv_s)
    return (xp.reshape(bp, sp, d), xs.reshape(db, ts, d)) + tuple(leaves)
```

```python
import functools
import math

import jax
import jax.numpy as jnp
from jax import lax
from jax.experimental import pallas as pl
from jax.experimental.pallas import tpu as pltpu

f32 = jnp.float32
bf16 = jnp.bfloat16
i32 = jnp.int32

D_MODEL = 1024
DEPTH = 4
PAGE = 128
RET_H, RET_DK, RET_DV, RET_CHUNK = 4, 128, 128, 128
ROPE_BASE = 10000.0
ATT_H, ATT_DH = 4, 128
IDX_H, IDX_DIM = 8, 64
TOPK_MAX = 256
REL_BUCKETS, REL_MAX_DIST = 32, 128
POOL_WINDOWS = (2, 4, 8, 16)
POOL_GROUPS = 4
POOL_DIM = D_MODEL // 2
POOL_GC = POOL_DIM // POOL_GROUPS
POOL_PAST = 15
D_INNER = D_MODEL // 2
SSM_P = 64
SSM_H = D_INNER // SSM_P
SSM_G = 2
SSM_N = 128
CONV_W = 4
CONV_DIM = D_INNER + 2 * SSM_G * SSM_N
SSD_CHUNK = 128
FF_DIM = -(-8 * D_MODEL // (3 * 256)) * 256
EPS = 1e-6

EVEN_PROJ = 4 * 512 + 3 * 512 + 512 + IDX_DIM + IDX_H
EVEN_PROJ_PAD = 4224
ODD_PROJ = POOL_DIM + D_INNER + CONV_DIM + SSM_H
ODD_PROJ_PAD = 2176

LANE = 128
INT_MIN = -(2 ** 31)
NEG_BIG = -1e30
LOG2E = 1.4426950408889634
VMEM_LIMIT = 56 * 1024 * 1024
TM = 512
FF_SPLIT = 1
DSA_CHUNK = 512
SCORE_ROWS = 256
SEQ_GROUP = 8


def _cparams(sem):
    return pltpu.CompilerParams(dimension_semantics=sem, vmem_limit_bytes=VMEM_LIMIT)


def _mm(a, b):
    return jnp.dot(a, b, preferred_element_type=f32)


def _mm_nt(a, b):
    return lax.dot_general(a, b, (((1,), (1,)), ((), ())), preferred_element_type=f32)


def _mm_tn(a, b):
    return lax.dot_general(a, b, (((0,), (0,)), ((), ())), preferred_element_type=f32)


def _rms(x, g):
    return x * lax.rsqrt(jnp.mean(x * x, -1, keepdims=True) + EPS) * g


def _silu(x):
    return x / (1.0 + jnp.exp(-x))


def _const_spec(shape):
    nd = len(shape)
    return pl.BlockSpec(shape, lambda *a: (0,) * nd)


def _even_proj_kernel(x_ref, g_ref, w_ref, qg_ref, kg_ref, *refs, for_prompt, n_aliased):
    ret_ref, qa_ref, ka_ref, kab_ref, va_ref, kiwi_ref, *idx_refs = refs[n_aliased:]
    xb = _rms(x_ref[...], g_ref[...]).astype(bf16)
    qa = _mm(xb, w_ref[:, 2048:2560])
    ka = _mm(xb, w_ref[:, 2560:3072])
    va = _mm(xb, w_ref[:, 3072:3584])
    qi = _mm(xb, w_ref[:, 3584:4096]) * IDX_DIM ** -0.5
    kiwi_ref[...] = _mm(xb, w_ref[:, 4096:4224])
    for h in range(ATT_H):
        sl = slice(h * ATT_DH, (h + 1) * ATT_DH)
        qa_ref[:, sl] = _rms(qa[:, sl], qg_ref[...]).astype(bf16)
        kn = _rms(ka[:, sl], kg_ref[...])
        ka_ref[:, h, :] = kn
        kab_ref[:, sl] = kn.astype(bf16)
        va_ref[:, h, :] = va[:, sl]
    if for_prompt:
        qit_ref, vt_ref = idx_refs
        for blk in range(TM // LANE):
            for h in range(IDX_H):
                part = qi[blk * LANE:(blk + 1) * LANE, h * IDX_DIM:(h + 1) * IDX_DIM]
                qit_ref[blk, :, h * LANE:(h + 1) * LANE] = jnp.transpose(part).astype(bf16)
        for pr in range(ATT_H // 2):
            vt_ref[pr] = jnp.transpose(va[:, pr * 2 * ATT_DH:(pr + 1) * 2 * ATT_DH]).astype(bf16)
    else:
        idx_refs[0][...] = qi.astype(bf16)
    ret_ref[...] = _mm(xb, w_ref[:, 0:2048])


def _even_proj(x, g, w, qg, kg, layer, n_layers, prev_kv, prompt_len=None):
    n = x.shape[0]
    row = lambda c: pl.BlockSpec((TM, c), lambda i: (i, 0))
    heads = pl.BlockSpec((None, TM, ATT_H, ATT_DH), lambda i: (layer, i, 0, 0))
    stacked = jax.ShapeDtypeStruct((n_layers, n, ATT_H, ATT_DH), f32)
    out_specs = [row(2048), row(512), heads, row(512), heads, row(LANE)]
    out_shape = [jax.ShapeDtypeStruct((n, 2048), f32), jax.ShapeDtypeStruct((n, 512), bf16), stacked,
                 jax.ShapeDtypeStruct((n, 512), bf16), stacked, jax.ShapeDtypeStruct((n, LANE), f32)]
    extra = [] if prev_kv is None else list(prev_kv)
    aliases = {} if prev_kv is None else {5: 2, 6: 4}
    if prompt_len is None:
        out_specs.append(row(512))
        out_shape.append(jax.ShapeDtypeStruct((n, 512), bf16))
    else:
        ck = math.gcd(prompt_len, DSA_CHUNK)
        assert ck % TM == 0 and prompt_len % ck == 0
        per_chunk, per_seq, npair = ck // TM, prompt_len // TM, ATT_H // 2
        out_specs += [pl.BlockSpec((TM // LANE, IDX_DIM, IDX_H * LANE), lambda i: (i, 0, 0)),
                      pl.BlockSpec((None, npair, None, 2 * ATT_DH, TM),
                                   lambda i: (i // per_seq, 0, (i % per_seq) // per_chunk, 0, i % per_chunk))]
        out_shape += [jax.ShapeDtypeStruct((n // LANE, IDX_DIM, IDX_H * LANE), bf16),
                      jax.ShapeDtypeStruct((n // prompt_len, npair, prompt_len // ck, 2 * ATT_DH, ck), bf16)]
    return pl.pallas_call(
        functools.partial(_even_proj_kernel, for_prompt=prompt_len is not None, n_aliased=len(extra)),
        grid=(n // TM,),
        in_specs=[row(D_MODEL), _const_spec((1, D_MODEL)), _layer_weight(w, layer),
                  _const_spec((1, ATT_DH)), _const_spec((1, ATT_DH))]
                 + [pl.BlockSpec(memory_space=pl.ANY)] * len(extra),
        out_specs=out_specs,
        out_shape=out_shape,
        input_output_aliases=aliases,
        compiler_params=_cparams(("parallel",)),
        name="even_proj",
    )(x, g, w, qg, kg, *extra)


def _odd_proj_kernel(x_ref, g_ref, w_ref, u_ref, z_ref, xbc_ref, dt_ref):
    xb = _rms(x_ref[...], g_ref[...]).astype(bf16)
    u_ref[...] = _mm(xb, w_ref[:, 0:512])
    z_ref[...] = _mm(xb, w_ref[:, 512:1024])
    xbc_ref[...] = _mm(xb, w_ref[:, 1024:2048])
    dt_ref[...] = _mm(xb, w_ref[:, 2048:2176])


def _odd_proj(x, g, w, layer):
    n = x.shape[0]
    row = lambda c: pl.BlockSpec((TM, c), lambda i: (i, 0))
    outs = [512, 512, 1024, LANE]
    return pl.pallas_call(
        _odd_proj_kernel,
        grid=(n // TM,),
        in_specs=[row(D_MODEL), _const_spec((1, D_MODEL)), _layer_weight(w, layer)],
        out_specs=[row(c) for c in outs],
        out_shape=[jax.ShapeDtypeStruct((n, c), f32) for c in outs],
        compiler_params=_cparams(("parallel",)),
        name="odd_proj",
    )(x, g, w)


def _out_ffn_kernel(x_ref, a_ref, b_ref, wo_ref, g_ref, wg_ref, wu_ref, wd_ref, o_ref):
    half = wo_ref.shape[0] // 2
    x = x_ref[...] + _mm(a_ref[...], wo_ref[0:half, :]) + _mm(b_ref[...], wo_ref[half:, :])
    hb = _rms(x, g_ref[...]).astype(bf16)
    fc = FF_DIM // FF_SPLIT
    ff = None
    for c in range(FF_SPLIT):
        sl = slice(c * fc, (c + 1) * fc)
        act = (_silu(_mm(hb, wg_ref[:, sl])) * _mm(hb, wu_ref[:, sl])).astype(bf16)
        down = _mm(act, wd_ref[sl, :])
        ff = down if ff is None else ff + down
    o_ref[...] = x + ff


def _layer_weight(w, layer, single_buffer=False):
    mode = dict(pipeline_mode=pl.Buffered(1)) if single_buffer else {}
    return pl.BlockSpec((None,) + w.shape[1:], lambda i: (layer,) + (0,) * (w.ndim - 1), **mode)


def _out_ffn(x, a, b, wo, mix_layer, g, wg, wu, wd, layer):
    n = x.shape[0]
    row = lambda c: pl.BlockSpec((TM, c), lambda i: (i, 0))
    return pl.pallas_call(
        _out_ffn_kernel,
        grid=(n // TM,),
        in_specs=[row(D_MODEL), row(a.shape[1]), row(b.shape[1]), _layer_weight(wo, mix_layer, True),
                  _const_spec((1, D_MODEL)), _layer_weight(wg, layer, True), _layer_weight(wu, layer, True),
                  _layer_weight(wd, layer, True)],
        out_specs=row(D_MODEL),
        out_shape=jax.ShapeDtypeStruct((n, D_MODEL), f32),
        compiler_params=_cparams(("parallel",)),
        name="out_ffn",
    )(x, a, b, wo, g, wg, wu, wd)


def _rope_tables(pos):
    half = RET_DK // 2
    inv = ROPE_BASE ** (-jnp.linspace(0.0, 1.0, half, dtype=f32))
    ang = pos.astype(f32)[:, None] * inv[None, :]
    cos, sin = jnp.cos(ang), jnp.sin(ang)
    return jnp.concatenate([cos, cos], -1), jnp.concatenate([-sin, sin], -1)


def _ret_decay(q):
    log_g = jnp.log1p(-jnp.exp2(-5.0 - jnp.arange(RET_H, dtype=f32)))
    idx = jnp.arange(q, dtype=f32)
    diff = idx[:, None] - idx[None, :]
    dmask = jnp.where(diff[None] >= 0, jnp.exp(log_g[:, None, None] * jnp.maximum(diff, 0.0)[None]), 0.0)
    xi = jnp.exp(log_g[:, None] * (idx + 1.0)[None])
    zeta = jnp.exp(log_g[:, None] * (q - 1.0 - idx)[None])
    g_chunk = jnp.exp(log_g * q)
    return dmask, xi, zeta, g_chunk


def _t5_bucket(rel):
    n = jnp.maximum(rel, 0)
    max_exact = REL_BUCKETS // 2
    nf = jnp.maximum(n, 1).astype(f32)
    large = max_exact + (jnp.log(nf / max_exact) / math.log(REL_MAX_DIST / max_exact)
                         * (REL_BUCKETS - max_exact)).astype(i32)
    large = jnp.minimum(large, REL_BUCKETS - 1)
    return jnp.where(n < max_exact, n, large)


def _bias_lookup(rel_bias, rel):
    onehot = jax.nn.one_hot(_t5_bucket(rel), REL_BUCKETS, dtype=f32)
    return jnp.einsum("...b,bh->...h", onehot, rel_bias.astype(f32), precision=lax.Precision.HIGHEST)


def _rotary(x, c, s):
    return x * c + pltpu.roll(x, RET_DK // 2, 1) * s


def _retention_kernel(q_ref, k_ref, v_ref, g_ref, r0_ref, c_ref, s_ref, dm_ref, xi_ref, zt_ref, gc_ref,
                      *rest):
    o_ref, r_ref = rest[-2:]

    @pl.when(pl.program_id(1) == 0)
    def _():
        r_ref[...] = r0_ref[...]

    cos, sin = c_ref[...], s_ref[...]
    for b in range(q_ref.shape[0]):
        for h in range(RET_H):
            sl = slice(h * RET_DK, (h + 1) * RET_DK)
            qr = _rotary(q_ref[b, :, sl], cos, sin).astype(bf16)
            kr = _rotary(k_ref[b, :, sl], cos, sin) * RET_DK ** -0.5
            vb = v_ref[b, :, sl].astype(bf16)
            r = r_ref[b, h]
            s = _mm_nt(qr, kr.astype(bf16)) * dm_ref[h]
            o = _mm(s.astype(bf16), vb) + _mm(qr, r.astype(bf16)) * xi_ref[h]
            r_ref[b, h] = r * gc_ref[h, 0:1, :] + _mm_tn((kr * zt_ref[h]).astype(bf16), vb)
            o = o * lax.rsqrt(jnp.mean(o * o, -1, keepdims=True) + EPS)
            o_ref[b, :, sl] = (_silu(g_ref[b, :, sl]) * o).astype(bf16)


def _seq_group(nb):
    return math.gcd(nb, SEQ_GROUP)


def _layer_block(tail, layer, group):
    return pl.BlockSpec((None, group) + tail, lambda b, c: (layer, b) + (0,) * len(tail))


def _stacked_out(prev, n_layers, nb, tail, n_inputs, out_index):
    shape = jax.ShapeDtypeStruct((n_layers, nb) + tail, f32)
    if prev is None:
        return shape, [], [], {}
    return shape, [prev], [pl.BlockSpec(memory_space=pl.ANY)], {n_inputs: out_index}


def _retention(ret, r0, layer_in, pos0, prev, layer_out, n_layers):
    nb, t, _ = ret.shape
    q = math.gcd(t, RET_CHUNK)
    grp = _seq_group(nb)
    cos, sin = _rope_tables(pos0 + jnp.arange(t))
    dmask, xi, zeta, g_chunk = _ret_decay(q)
    bcast = lambda a: jnp.broadcast_to(a[:, :, None], (RET_H, a.shape[1], LANE))
    col = lambda j: pl.BlockSpec((grp, q, 512), lambda b, c: (b, c, j))
    tab = pl.BlockSpec((q, LANE), lambda b, c: (c, 0))
    tail = (RET_H, RET_DK, RET_DV)
    in_specs = [col(0), col(1), col(2), col(3), _layer_block(tail, layer_in, grp), tab, tab,
                _const_spec((RET_H, q, q)), _const_spec((RET_H, q, LANE)), _const_spec((RET_H, q, LANE)),
                _const_spec((RET_H, 8, LANE))]
    state_shape, extra, extra_specs, aliases = _stacked_out(prev, n_layers, nb, tail, len(in_specs), 1)
    return pl.pallas_call(
        _retention_kernel,
        grid=(nb // grp, t // q),
        in_specs=in_specs + extra_specs,
        out_specs=[col(0), _layer_block(tail, layer_out, grp)],
        out_shape=[jax.ShapeDtypeStruct((nb, t, 512), bf16), state_shape],
        input_output_aliases=aliases,
        compiler_params=_cparams(("parallel", "arbitrary")),
        name="retention_t%d" % t,
    )(ret, ret, ret, ret, r0, cos, sin, dmask, bcast(xi), bcast(zeta),
      jnp.broadcast_to(g_chunk[:, None, None], (RET_H, 8, LANE)), *extra)


def _sort_key(score):
    bits = lax.bitcast_convert_type(score, i32)
    bits = jnp.where(bits == INT_MIN, 0, bits)
    return jnp.where(bits < 0, bits ^ 0x7FFFFFFF, bits)


def _count(key_ref, nchunk, ck, pred):
    def body(c, acc):
        r0 = pl.multiple_of(c * ck, ck)
        hit = jnp.where(pred(key_ref[pl.ds(r0, ck), :], r0), 1, 0).astype(i32)
        return acc + jnp.sum(hit.reshape(ck // 8, 8, LANE), axis=0)
    acc = lax.fori_loop(0, nchunk, body, jnp.zeros((8, LANE), i32))
    return jnp.sum(acc, axis=0, keepdims=True)


def _cut_ties(key_ref, nchunk, ck, t, surplus, keep, idx_bits):
    rows = lax.broadcasted_iota(i32, (ck, LANE), 0)

    @pl.when(jnp.max(jnp.where(surplus, 1, 0)) > 0)
    def _():
        want = jnp.where(surplus, keep, jnp.int32(2 ** 30))

        def idx_step(it, x):
            cand = x + jnp.left_shift(jnp.int32(1), idx_bits - 1 - it)
            cnt = _count(key_ref, nchunk, ck, lambda blk, r0: jnp.where(blk == t, rows + r0, cand) < cand)
            return jnp.where(cnt < want, cand, x)

        last = lax.fori_loop(0, idx_bits, idx_step, jnp.zeros((1, LANE), i32))

        def demote(c, carry):
            r0 = pl.multiple_of(c * ck, ck)
            blk = key_ref[pl.ds(r0, ck), :]
            drop = jnp.where(blk == t, rows + r0, last) > last
            key_ref[pl.ds(r0, ck), :] = jnp.where(drop, INT_MIN, blk)
            return carry

        lax.fori_loop(0, nchunk, demote, 0)


PLANE_ROWS = 256
_SWAP_STEPS = ((16, 0x0000FFFF), (8, 0x00FF00FF), (4, 0x0F0F0F0F), (2, 0x33333333), (1, 0x55555555))


def _bit_planes(words):
    a = list(words)
    for j, m in _SWAP_STEPS:
        for k in range(32):
            if k & j == 0:
                t = ((a[k] >> j) ^ a[k + j]) & m
                a[k] = a[k] ^ (t << j)
                a[k + j] = a[k + j] ^ t
    return a


def _select_topk_planes(key_ref, planes_ref, nchunk, ck, topk, idx_bits):
    u32 = jnp.uint32
    per_chunk = ck // PLANE_ROWS
    ngroups = nchunk * per_chunk

    def to_planes(g, carry):
        r0 = pl.multiple_of(g * PLANE_ROWS, PLANE_ROWS)
        words = [lax.bitcast_convert_type(key_ref[pl.ds(r0 + 8 * i, 8), :], u32) for i in range(32)]
        planes = _bit_planes(words)
        planes[31] = ~planes[31]
        for b in range(32):
            planes_ref[g, b] = lax.bitcast_convert_type(planes[b], i32)
        return carry

    lax.fori_loop(0, ngroups, to_planes, 0)

    gmax = planes_ref.shape[0]
    full = jnp.full((8, LANE), 0xFFFFFFFF, u32)
    none = jnp.zeros((8, LANE), u32)

    def radix_select(ng):
        def run():
            alive0 = tuple(jnp.where(g < ngroups, full, none) for g in range(ng))

            def bit_step(it, carry):
                t, need, alive = carry
                b = 31 - it
                ones = [a & lax.bitcast_convert_type(planes_ref[g, b], u32) for g, a in enumerate(alive)]
                cnt = functools.reduce(lambda x, y: x + y, [lax.population_count(o) for o in ones])
                c = jnp.sum(cnt.astype(i32), axis=0, keepdims=True)
                take = c >= need
                t = t | jnp.where(take, jnp.left_shift(jnp.int32(1), b), 0)
                need = jnp.where(take, need, need - c)
                return t, need, tuple(jnp.where(take, o, a ^ o) for o, a in zip(ones, alive))

            init = (jnp.zeros((1, LANE), i32), jnp.full((1, LANE), topk, i32), alive0)
            t, need, alive = lax.fori_loop(0, 32, bit_step, init)
            equal = functools.reduce(lambda x, y: x + y, [lax.population_count(a) for a in alive])
            return t, need, jnp.sum(equal.astype(i32), axis=0, keepdims=True)
        return run

    sizes = sorted({max(1, gmax // 4), max(1, gmax // 2), gmax})
    if isinstance(ngroups, int):
        t, need, n_equal = radix_select(min(s for s in sizes if s >= ngroups))()
    else:
        which = sum((ngroups > s).astype(i32) for s in sizes[:-1]) if len(sizes) > 1 else jnp.int32(0)
        t, need, n_equal = lax.switch(which, [radix_select(s) for s in sizes])
    t = t ^ INT_MIN
    surplus = (n_equal > need) & (t > INT_MIN)
    t = jnp.maximum(t, INT_MIN + 1)
    _cut_ties(key_ref, nchunk, ck, t, surplus, need, idx_bits)
    return t


def _dsa_prompt_kernel(qa_ref, qit_ref, kiwiq_ref, k_ref, vt_ref, kiwik_ref, bias_ref, o_ref, key_ref, acc_ref,
                       qbd_ref, planes_ref, *, topk, idx_bits, nch, ck):
    qb = pl.program_id(1)
    cb = ck // LANE
    nchunk = (qb + cb) // cb
    wit = jnp.transpose(kiwiq_ref[...])[IDX_DIM:IDX_DIM + IDX_H, :] * IDX_H ** -0.5
    qpos = qb * LANE + lax.broadcasted_iota(i32, (SCORE_ROWS, LANE), 1)
    rows = lax.broadcasted_iota(i32, (SCORE_ROWS, LANE), 0)

    def score_chunk(c, carry):
        for sub in range(ck // SCORE_ROWS):
            r0 = pl.multiple_of(c * ck, ck) + sub * SCORE_ROWS
            kic = kiwik_ref[pl.ds(r0, SCORE_ROWS), 0:IDX_DIM].astype(bf16)
            acc = jnp.zeros((SCORE_ROWS, LANE), f32)
            for h in range(0, IDX_H, 2):
                s = _mm(kic, qit_ref[:, h * LANE:(h + 2) * LANE])
                acc = acc + jnp.maximum(s[:, :LANE], 0.0) * wit[h:h + 1, :]
                acc = acc + jnp.maximum(s[:, LANE:], 0.0) * wit[h + 1:h + 2, :]
            valid = rows + r0 <= qpos
            key_ref[pl.ds(r0, SCORE_ROWS), :] = jnp.where(valid, _sort_key(acc), INT_MIN)
        return carry

    lax.fori_loop(0, nchunk, score_chunk, 0)
    t = _select_topk_planes(key_ref, planes_ref, nchunk, ck, topk, idx_bits)

    npair = ATT_H // 2
    @pl.when(qb == 0)
    def _():
        qbd_ref[...] = jnp.zeros_like(qbd_ref)

    for pr in range(npair):
        for hh in range(2):
            sl = slice((2 * pr + hh) * ATT_DH, (2 * pr + hh + 1) * ATT_DH)
            qbd_ref[pr, hh * ATT_DH:(hh + 1) * ATT_DH, hh * LANE:(hh + 1) * LANE] = (
                jnp.transpose(qa_ref[:, sl].astype(f32)).astype(bf16))
    acc_ref[...] = jnp.zeros_like(acc_ref)
    nfar = jnp.maximum(qb - 1, 0) // cb

    pairs = range(npair)

    def att_chunks(near, unroll, first=0):
        def body(i, carry):
            ms, ls = list(carry[0]), list(carry[1])
            cs = [first + i * unroll + u for u in range(unroll)]
            r0s = [pl.multiple_of(c * ck, ck) for c in cs]
            qks = [[_mm(k_ref[pl.ds(r0, ck), pr * 2 * ATT_DH:(pr + 1) * 2 * ATT_DH], qbd_ref[pr]) for pr in pairs]
                   for r0 in r0s]
            for c, r0, qk in zip(cs, r0s, qks):
                neg1 = jnp.where(key_ref[pl.ds(r0, ck), :] >= t, 0.0, NEG_BIG)
                negm = jnp.concatenate([neg1, neg1], axis=1)
                alphas, ps = [], []
                for pr in pairs:
                    s = qk[pr] * (ATT_DH ** -0.5 * LOG2E) + negm
                    if near:
                        s = s + jnp.concatenate(
                            [bias_ref[jnp.clip(qb - (c * cb + j), 0, 2), pr] for j in range(cb)], axis=0)
                    m_new = jnp.maximum(ms[pr], jnp.max(s, axis=0, keepdims=True))
                    alpha = jnp.exp2(ms[pr] - m_new)
                    p = jnp.exp2(s - m_new)
                    ls[pr] = ls[pr] * alpha + jnp.sum(p, axis=0, keepdims=True)
                    ms[pr] = m_new
                    alphas.append(alpha)
                    ps.append(p.astype(bf16))
                pvs = [_mm(vt_ref[pr * nch + c], ps[pr]) for pr in pairs]
                for pr in pairs:
                    for hh in range(2):
                        d = slice(hh * LANE, (hh + 1) * LANE)
                        acc_ref[2 * pr + hh] = acc_ref[2 * pr + hh] * alphas[pr][:, d] + pvs[pr][d, d]
            return tuple(ms), tuple(ls)
        return body

    carry = ((jnp.full((1, 2 * LANE), NEG_BIG, f32),) * npair, (jnp.zeros((1, 2 * LANE), f32),) * npair)
    carry = lax.fori_loop(0, nfar // 2, att_chunks(False, 2), carry)
    tail = nfar // 2 * 2
    tail_pairs = (nchunk - tail) // 2
    carry = lax.fori_loop(0, tail_pairs, att_chunks(True, 2, tail), carry)
    _, ls = lax.fori_loop(tail + 2 * tail_pairs, nchunk, att_chunks(True, 1), carry)
    for h in range(ATT_H):
        l = ls[h // 2][:, (h % 2) * LANE:(h % 2 + 1) * LANE]
        o_ref[:, h * ATT_DH:(h + 1) * ATT_DH] = jnp.transpose(acc_ref[h] / l).astype(bf16)


def _bias_tiles(rel_bias, nd):
    j = jnp.arange(LANE)[:, None]
    i = jnp.arange(LANE)[None, :]
    rel = jnp.arange(nd)[:, None, None] * LANE + (i - j)[None]
    return jnp.moveaxis(_bias_lookup(rel_bias, rel), -1, 1)


def _dsa_prompt(qa, kab, vt, qit, kiwi, rel_bias, nb, t):
    nqb = t // LANE
    topk = min(TOPK_MAX, t // 4)
    assert REL_MAX_DIST <= LANE + 1
    bias = _bias_tiles(rel_bias, 3)
    bias = (bias - bias[2:3]) * LOG2E
    npair = ATT_H // 2
    bias = bias.reshape(3, npair, 2, LANE, LANE).transpose(0, 1, 3, 2, 4).reshape(3, npair, LANE, 2 * LANE)
    ck = math.gcd(t, DSA_CHUNK)
    nch = t // ck
    vt = vt.reshape(nb * npair * nch, 2 * ATT_DH, ck)
    qrow = lambda c: pl.BlockSpec((LANE, c), lambda b, q: (b * nqb + q, 0))
    seq = lambda c: pl.BlockSpec((t, c), lambda b, q: (b, 0))
    kern = functools.partial(_dsa_prompt_kernel, topk=topk, idx_bits=max(1, (t - 1).bit_length()), nch=nch, ck=ck)
    return pl.pallas_call(
        kern,
        grid=(nb, nqb),
        in_specs=[qrow(512), pl.BlockSpec((None, IDX_DIM, IDX_H * LANE), lambda b, q: (b * nqb + q, 0, 0)),
                  qrow(LANE), seq(512),
                  pl.BlockSpec((npair * nch, 2 * ATT_DH, ck), lambda b, q: (b, 0, 0)), seq(LANE),
                  _const_spec((3, npair, LANE, 2 * LANE))],
        out_specs=qrow(512),
        out_shape=jax.ShapeDtypeStruct((nb * t, 512), bf16),
        scratch_shapes=[pltpu.VMEM((t, LANE), i32), pltpu.VMEM((ATT_H, ATT_DH, LANE), f32),
                        pltpu.VMEM((npair, 2 * ATT_DH, 2 * LANE), bf16),
                        pltpu.VMEM((t // PLANE_ROWS, 32, 8, LANE), i32)],
        compiler_params=_cparams(("parallel", "arbitrary")),
        name="dsa_prompt",
    )(qa, qit, kiwi, kab, vt, kiwi, bias)


def _softplus(x):
    return jnp.maximum(x, 0.0) + jnp.log1p(jnp.exp(-jnp.abs(x)))


def _cumsum_rows(tri, a):
    hi = a.astype(bf16)
    r1 = a - hi.astype(f32)
    mid = r1.astype(bf16)
    lo = (r1 - mid.astype(f32)).astype(bf16)
    return _mm(tri, hi) + _mm(tri, mid) + _mm(tri, lo)


def _odd_mixer_kernel(*refs, q, pos0):
    seq_in, shared, seq_out = refs[:7], refs[7:16], refs[-6:]
    for b in range(seq_in[0].shape[0]):
        _odd_mixer_seq(*(r.at[b] for r in seq_in), *shared, *(r.at[b] for r in seq_out), q=q, pos0=pos0)


def _odd_mixer_seq(u_ref, z_ref, xbc_ref, dt_ref, pp_ref, cp_ref, h0_ref, pw_ref, ps_ref, cw_ref, cb_ref,
                   dtb_ref, alog_ref, dsk_ref, nrm_ref, tri_ref, po_ref, y_ref, h_ref, ubuf, xbuf, ybuf, *, q, pos0):
    c = pl.program_id(1)

    @pl.when(c == 0)
    def _():
        ubuf[0:1, :] = jnp.zeros((1, POOL_DIM), f32)
        ubuf[1:16, :] = pp_ref[...]
        xbuf[0:8 - (CONV_W - 1), :] = jnp.zeros((8 - (CONV_W - 1), CONV_DIM), f32)
        xbuf[8 - (CONV_W - 1):8, :] = cp_ref[...]
        h_ref[...] = h0_ref[...]

    pos = pos0 + c * q + lax.broadcasted_iota(i32, (q, LANE), 0)
    causal = lax.broadcasted_iota(i32, (q, q), 0) >= lax.broadcasted_iota(i32, (q, q), 1)

    u = u_ref[...]
    ubuf[16:16 + q, :] = u
    for g, w in enumerate(POOL_WINDOWS):
        sl = slice(g * POOL_GC, (g + 1) * POOL_GC)
        acc = u[:, sl]
        for k in range(1, w):
            acc = acc + ubuf[16 - k:16 - k + q, sl]
        d = acc / jnp.minimum(pos + 1, w).astype(f32) - u[:, sl]
        po_ref[:, sl] = (_mm(d.astype(bf16), pw_ref[g]) * ps_ref[:, sl]).astype(bf16)
    ubuf[0:16, :] = ubuf[q:q + 16, :]

    xbuf[8:8 + q, :] = xbc_ref[...]
    conv = cb_ref[...]
    for j in range(CONV_W):
        off = 8 - (CONV_W - 1) + j
        conv = conv + xbuf[off:off + q, :] * cw_ref[j:j + 1, :]
    xbuf[0:8, :] = xbuf[q:q + 8, :]
    act = _silu(conv)
    xs = act[:, 0:D_INNER]

    dt = _softplus(dt_ref[...] + dtb_ref[...])
    a = dt * (-jnp.exp(alog_ref[...]))
    cs = _cumsum_rows(tri_ref[...], a)
    cs_t = jnp.transpose(cs)
    dt_t = jnp.transpose(dt)
    cs_last = cs[q - 1:q, :]
    w_end = jnp.exp(cs_last - cs) * dt
    ecs = jnp.exp(cs)
    hpg = SSM_H // SSM_G
    for g in range(SSM_G):
        bm = act[:, D_INNER + g * SSM_N:D_INNER + (g + 1) * SSM_N].astype(bf16)
        cm = act[:, D_INNER + (SSM_G + g) * SSM_N:D_INNER + (SSM_G + g + 1) * SSM_N].astype(bf16)
        cb = _mm_nt(cm, bm)
        for hh in range(hpg):
            h = g * hpg + hh
            psl = slice(h * SSM_P, (h + 1) * SSM_P)
            seg = cs[:, h:h + 1] - cs_t[h:h + 1, :]
            lm = jnp.exp(jnp.where(causal, seg, NEG_BIG))
            sc = cb * lm * dt_t[h:h + 1, :]
            xh = xs[:, psl]
            hs = h_ref[h]
            yh = _mm(sc.astype(bf16), xh.astype(bf16)) + _mm_nt(cm, hs.astype(bf16)) * ecs[:, h:h + 1]
            h_ref[h] = hs * jnp.exp(cs_last[:, h:h + 1]) + _mm_tn((xh * w_end[:, h:h + 1]).astype(bf16), bm)
            ybuf[:, psl] = yh
    y = (ybuf[...] + dsk_ref[...] * xs) * _silu(z_ref[...])
    gw = D_INNER // SSM_G
    for g in range(SSM_G):
        sl = slice(g * gw, (g + 1) * gw)
        yg = y[:, sl]
        y_ref[:, sl] = (yg * lax.rsqrt(jnp.mean(yg * yg, -1, keepdims=True) + EPS) * nrm_ref[:, sl]).astype(bf16)


def _pad_lanes(v):
    return jnp.pad(v.astype(f32), (0, LANE - v.shape[0]))[None, :]


def _odd_mixer(u, z, xbc, dt, pool_prev, conv_prev, h0, layer_in, prm, pos0, prev, layer_out, n_layers):
    pool_w, pool_scale, conv_w, conv_b, dt_bias, a_log, d_skip, ssm_norm = prm
    nb, t, _ = u.shape
    q = math.gcd(t, SSD_CHUNK)
    nc = t // q
    grp = _seq_group(nb)
    row = lambda c: pl.BlockSpec((grp, q, c), lambda b, i: (b, i, 0))
    lead = (nb, t)
    tri = jnp.tril(jnp.ones((q, q), bf16))
    tail = (SSM_H, SSM_P, SSM_N)
    in_specs = [row(POOL_DIM), row(D_INNER), row(CONV_DIM), row(LANE),
                _layer_block((POOL_PAST, POOL_DIM), layer_in, grp),
                _layer_block((CONV_W - 1, CONV_DIM), layer_in, grp), _layer_block(tail, layer_in, grp),
                _const_spec((POOL_GROUPS, POOL_GC, POOL_GC)), _const_spec((1, POOL_DIM)),
                _const_spec((CONV_W, CONV_DIM)), _const_spec((1, CONV_DIM)),
                _const_spec((1, LANE)), _const_spec((1, LANE)),
                _const_spec((1, D_INNER)), _const_spec((1, D_INNER)), _const_spec((q, q))]
    state_shape, extra, extra_specs, aliases = _stacked_out(prev, n_layers, nb, tail, len(in_specs), 2)
    return pl.pallas_call(
        functools.partial(_odd_mixer_kernel, q=q, pos0=pos0),
        grid=(nb // grp, nc),
        in_specs=in_specs + extra_specs,
        out_specs=[row(POOL_DIM), row(D_INNER), _layer_block(tail, layer_out, grp)],
        out_shape=[jax.ShapeDtypeStruct(lead + (POOL_DIM,), bf16), jax.ShapeDtypeStruct(lead + (D_INNER,), bf16),
                   state_shape],
        input_output_aliases=aliases,
        scratch_shapes=[pltpu.VMEM((grp, q + 16, POOL_DIM), f32), pltpu.VMEM((grp, q + 8, CONV_DIM), f32),
                        pltpu.VMEM((grp, q, D_INNER), f32)],
        compiler_params=_cparams(("parallel", "arbitrary")),
        name="odd_mixer_t%d" % t,
    )(u, z, xbc, dt, pool_prev, conv_prev, h0, pool_w.astype(bf16), pool_scale[None, :], conv_w, conv_b[None, :],
      _pad_lanes(dt_bias), _pad_lanes(a_log), jnp.repeat(d_skip, SSM_P)[None, :], ssm_norm[None, :], tri, *extra)


def _dsa_sample_score_kernel(pt_ref, qi_ref, wi_ref, kiwi_ref, *rest, npg):
    pages, o_ref = rest[:npg], rest[npg]
    ts = kiwi_ref.shape[0]
    ki_new = jnp.transpose(jnp.concatenate([kiwi_ref[...], jnp.zeros((PAGE - ts, LANE), f32)], axis=0))[0:IDX_DIM, :]
    ki = jnp.concatenate([p[...] for p in pages] + [ki_new], axis=1).astype(bf16)
    nk = ki.shape[1]
    s = jnp.maximum(_mm(qi_ref[...], ki), 0.0) * (wi_ref[...] * IDX_H ** -0.5)
    acc = jnp.sum(s.reshape(ts, IDX_H, nk), axis=1)
    col = lax.broadcasted_iota(i32, (ts, nk), 1)
    row = lax.broadcasted_iota(i32, (ts, nk), 0)
    o_ref[...] = jnp.where(col <= npg * PAGE + row, _sort_key(acc), INT_MIN)


def _select_kernel(k_ref, o_ref, key_ref, planes_ref, *, topk, idx_bits):
    nk, rows = k_ref.shape[0], key_ref.shape[0]
    key_ref[0:nk, :] = k_ref[...]
    if rows > nk:
        key_ref[nk:rows, :] = jnp.full((rows - nk, LANE), INT_MIN, i32)
    t = _select_topk_planes(key_ref, planes_ref, 1, rows, topk, idx_bits)
    o_ref[...] = jnp.where(key_ref[0:nk, :] >= t, 1.0, 0.0)


def _dsa_sample_attn_kernel(pt_ref, q_ref, kn_ref, vn_ref, sel_ref, bias_ref, spread_ref, *rest, npg):
    kpages, vpages, o_ref = rest[:npg], rest[npg:2 * npg], rest[2 * npg]
    pad = jnp.zeros((PAGE * ATT_H - kn_ref.shape[0], ATT_DH), f32)
    kx = jnp.concatenate([p[...] for p in kpages] + [kn_ref[...], pad], axis=0).astype(bf16)
    vx = jnp.concatenate([p[...] for p in vpages] + [vn_ref[...], pad], axis=0).astype(bf16)
    selx = jnp.concatenate([_mm(sel_ref[:, j * PAGE:(j + 1) * PAGE], spread_ref[...]) for j in range(npg + 1)],
                           axis=1)
    s = _mm_nt(q_ref[...], kx) * ATT_DH ** -0.5 + jnp.where(selx > 0.5, bias_ref[...], NEG_BIG)
    p = jnp.exp(s - jnp.max(s, axis=-1, keepdims=True))
    o_ref[...] = (_mm(p.astype(bf16), vx) / jnp.sum(p, axis=-1, keepdims=True)).astype(bf16)


def _dsa_sample(qa, ka, va, qi, kiwi, cache_k, cache_v, cache_ki, layer, page_table, rel_bias):
    db, ts = qa.shape[:2]
    npg = page_table.shape[1]
    n_past = npg * PAGE
    nk = n_past + PAGE
    nq = db * ts
    topk = min(TOPK_MAX, (n_past + ts) // 4)
    hd = ATT_H * ATT_DH
    ki_page = lambda j: pl.BlockSpec((None, None, IDX_DIM, PAGE), lambda b, pt: (layer, pt[b, j], 0, 0))
    seq = lambda r, c: pl.BlockSpec((None, r, c), lambda b, pt: (b, 0, 0))

    keys = pl.pallas_call(
        functools.partial(_dsa_sample_score_kernel, npg=npg),
        grid_spec=pltpu.PrefetchScalarGridSpec(
            num_scalar_prefetch=1, grid=(db,),
            in_specs=[seq(ts * IDX_H, IDX_DIM), seq(ts * IDX_H, 1), seq(ts, LANE)] + [ki_page(j) for j in range(npg)],
            out_specs=seq(ts, nk)),
        out_shape=jax.ShapeDtypeStruct((db, ts, nk), i32),
        compiler_params=_cparams(("parallel",)),
        name="dsa_sample_score",
    )(page_table, qi.reshape(db, ts * IDX_H, IDX_DIM),
      kiwi[..., IDX_DIM:IDX_DIM + IDX_H].reshape(db, ts * IDX_H, 1), kiwi,
      *([jnp.swapaxes(cache_ki, 2, 3)] * npg))

    col = pl.BlockSpec((nk, LANE), lambda i: (0, i))
    sel_rows = -(-nk // PLANE_ROWS) * PLANE_ROWS
    sel = pl.pallas_call(
        functools.partial(_select_kernel, topk=topk, idx_bits=max(1, (sel_rows - 1).bit_length())),
        grid=(nq // LANE,),
        in_specs=[col],
        out_specs=col,
        out_shape=jax.ShapeDtypeStruct((nk, nq), f32),
        scratch_shapes=[pltpu.VMEM((sel_rows, LANE), i32),
                        pltpu.VMEM((sel_rows // PLANE_ROWS, 32, 8, LANE), i32)],
        compiler_params=_cparams(("parallel",)),
        name="dsa_sample_select",
    )(keys.reshape(nq, nk).T)
    sel = jnp.repeat(sel.T.reshape(db, ts, nk), ATT_H, axis=1).astype(bf16)
    rel = n_past + jnp.arange(ts)[:, None] - jnp.arange(nk)[None, :]
    bias = jnp.repeat(jnp.moveaxis(_bias_lookup(rel_bias, rel), -1, 1), ATT_H, axis=-1)
    same_head = jnp.arange(nk * ATT_H)[None, :] % ATT_H == jnp.arange(ATT_H)[:, None]
    bias = jnp.where(same_head[None], bias, NEG_BIG).reshape(ts * ATT_H, nk * ATT_H)
    spread = (jnp.arange(PAGE * ATT_H)[None, :] // ATT_H == jnp.arange(PAGE)[:, None]).astype(bf16)

    n_pages = cache_k.shape[1]
    rows = PAGE * ATT_H
    kv_page = lambda j: pl.BlockSpec((rows, ATT_DH), lambda b, pt: (layer * n_pages + pt[b, j], 0))
    const = lambda a: pl.BlockSpec(a.shape, lambda b, pt: (0, 0))
    as_rows = lambda a: a.reshape(db, ts * ATT_H, ATT_DH)
    out = pl.pallas_call(
        functools.partial(_dsa_sample_attn_kernel, npg=npg),
        grid_spec=pltpu.PrefetchScalarGridSpec(
            num_scalar_prefetch=1, grid=(db,),
            in_specs=[seq(ts * ATT_H, ATT_DH)] * 3 + [seq(ts * ATT_H, nk), const(bias), const(spread)]
                     + [kv_page(j) for j in range(npg)] * 2,
            out_specs=seq(ts * ATT_H, ATT_DH)),
        out_shape=jax.ShapeDtypeStruct((db, ts * ATT_H, ATT_DH), bf16),
        compiler_params=_cparams(("parallel",)),
        name="dsa_sample_attn",
    )(page_table, as_rows(qa), as_rows(ka), as_rows(va), sel, bias, spread,
      *([cache_k.reshape(-1, ATT_DH)] * npg), *([cache_v.reshape(-1, ATT_DH)] * npg))
    return out.reshape(db, ts, hd)


def kernel(x_prompt, x_sample, cache_k, cache_v, cache_kidx, state_ret, state_pool, state_conv, state_ssm,
           page_table, norm_mix, norm_ffn, w_in_even, w_out_even, q_norm, k_norm, rel_bias,
           w_in_odd, w_out_odd, pool_w, pool_scale, conv_w, conv_b, dt_bias, a_log, d_skip, ssm_norm,
           w_gate, w_up, w_down):
    bp, sp, d = x_prompt.shape
    db, ts, _ = x_sample.shape
    n_p, n_s = bp * sp, db * ts
    assert d == D_MODEL and n_p % TM == 0 and n_s % TM == 0 and sp % LANE == 0 and sp >= POOL_PAST
    n_past = page_table.shape[1] * PAGE
    n_even, n_odd = (DEPTH + 1) // 2, DEPTH // 2
    xp, xs = x_prompt.reshape(n_p, d), x_sample.reshape(n_s, d)
    seqs = lambda a: a.reshape(db, ts, a.shape[-1])
    pseqs = lambda a: a.reshape(bp, sp, a.shape[-1])
    last = lambda a, n: a.reshape(bp, sp, a.shape[-1])[:, sp - n:]
    tail = lambda prev, cur, n: jnp.concatenate([prev.astype(f32), cur], axis=1)[:, -n:]
    zeros = lambda *s: jnp.zeros((1, bp) + s, f32)
    outs = [[] for _ in range(14)]
    rstate_p = rstate_s = h_p = h_s = kv_p = kv_s = None
    w_in_e = jnp.pad(w_in_even, ((0, 0), (0, 0), (0, EVEN_PROJ_PAD - EVEN_PROJ))).astype(bf16)
    w_in_o = jnp.pad(w_in_odd, ((0, 0), (0, 0), (0, ODD_PROJ_PAD - ODD_PROJ))).astype(bf16)
    w_out_e, w_out_o = w_out_even.astype(bf16), w_out_odd.astype(bf16)
    w_ffn = (w_gate.astype(bf16), w_up.astype(bf16), w_down.astype(bf16))
    for l in range(DEPTH):
        if l % 2 == 0:
            i = l // 2
            w_out, mix_layer = w_out_e, i
            prm = (norm_mix[l][None], w_in_e, q_norm[i][None], k_norm[i][None])
            ret_p, qa_p, ka_p, kab_p, va_p, kiwi_p, qit_p, vt_p = _even_proj(xp, *prm, i, n_even, kv_p, prompt_len=sp)
            ret_s, qa_s, ka_s, _, va_s, kiwi_s, qi_s = _even_proj(xs, *prm, i, n_even, kv_s)
            kv_p, kv_s = (ka_p, va_p), (ka_s, va_s)
            mix_a_p, rstate_p = _retention(pseqs(ret_p), zeros(RET_H, RET_DK, RET_DV), 0, 0, rstate_p, i, n_even)
            mix_a_s, rstate_s = _retention(seqs(ret_s), state_ret, i, n_past, rstate_s, i, n_even)
            mix_b_p = _dsa_prompt(qa_p, kab_p, vt_p, qit_p, kiwi_p, rel_bias, bp, sp)
            mix_b_s = _dsa_sample(seqs(qa_s), ka_s[i], va_s[i], seqs(qi_s), seqs(kiwi_s),
                                  cache_k, cache_v, cache_kidx, i, page_table, rel_bias)
            new = [None, None, kiwi_p.reshape(bp, sp, LANE)[..., :IDX_DIM], None, None, None, None,
                   None, None, seqs(kiwi_s)[..., :IDX_DIM], None, None, None, None]
        else:
            j = l // 2
            w_out, mix_layer = w_out_o, j
            u_p, z_p, xbc_p, dt_p = _odd_proj(xp, norm_mix[l][None], w_in_o, j)
            u_s, z_s, xbc_s, dt_s = _odd_proj(xs, norm_mix[l][None], w_in_o, j)
            prm = (pool_w[j], pool_scale[j], conv_w[j], conv_b[j], dt_bias[j], a_log[j], d_skip[j], ssm_norm[j])
            mix_a_p, mix_b_p, h_p = _odd_mixer(
                pseqs(u_p), pseqs(z_p), pseqs(xbc_p), pseqs(dt_p), zeros(POOL_PAST, POOL_DIM),
                zeros(CONV_W - 1, CONV_DIM), zeros(SSM_H, SSM_P, SSM_N), 0, prm, 0, h_p, j, n_odd)
            mix_a_s, mix_b_s, h_s = _odd_mixer(
                seqs(u_s), seqs(z_s), seqs(xbc_s), seqs(dt_s), state_pool, state_conv, state_ssm, j,
                prm, n_past, h_s, j, n_odd)
            new = [None, None, None, None, last(u_p, POOL_PAST), last(xbc_p, CONV_W - 1), None,
                   None, None, None, None,
                   tail(state_pool[j], seqs(u_s), POOL_PAST), tail(state_conv[j], seqs(xbc_s), CONV_W - 1), None]
        for acc, leaf in zip(outs, new):
            if leaf is not None:
                acc.append(leaf)
        ffn = (w_out, mix_layer, norm_ffn[l][None], *w_ffn, l)
        xp = _out_ffn(xp, mix_a_p.reshape(n_p, -1), mix_b_p.reshape(n_p, -1), *ffn)
        xs = _out_ffn(xs, mix_a_s.reshape(n_s, -1), mix_b_s.reshape(n_s, -1), *ffn)
    leaves = [jnp.stack(a) if a else None for a in outs]
    leaves[3], leaves[6], leaves[10], leaves[13] = rstate_p, h_p, rstate_s, h_s
    leaves[0], leaves[1] = (a.reshape(n_even, bp, sp, ATT_H, ATT_DH) for a in kv_p)
    leaves[7], leaves[8] = (a.reshape(n_even, db, ts, ATT_H, ATT_DH) for a in kv_s)
    return (xp.reshape(bp, sp, d), xs.reshape(db, ts, d)) + tuple(leaves)
```
